```python
import math
import jax, jax.numpy as jnp
from jax import lax
import numpy as np

D_MODEL = 1024
BATCH = 8
SEQ = 4096
DEPTH = 2

HEAD_DIM = 64
RMS_EPS = 1e-6
DSWA_PATTERNS = ((128, 1), (512, 4), (2048, 16))
DSWA_HEADS_PER_GROUP = 4
DSWA_HEADS = len(DSWA_PATTERNS) * DSWA_HEADS_PER_GROUP
DSWA_BLOCK = 128
A_WIDTH = DSWA_HEADS * HEAD_DIM
GLA_HEADS = 4
GLA_DK = D_MODEL // 2 // GLA_HEADS
GLA_DV = D_MODEL // GLA_HEADS
GLA_GATE_RANK = 16
GLA_TAU = 16.0
GLA_CHUNK = 64
DSA_HEADS = 12
IDX_HEADS = 8
IDX_DIM = 64
IDX_TOPK = 256
DSA_QBLOCK = 128
REL_BUCKETS = 32
REL_MAX_DIST = 2048
N_BIAS_HEADS = DSWA_HEADS + DSA_HEADS
D_FF = -(-8 * D_MODEL // (3 * 256)) * 256
PLE_DIM = 256
_IN_WIDTHS = (
    A_WIDTH, A_WIDTH, A_WIDTH,
    GLA_HEADS * GLA_DK, GLA_HEADS * GLA_DK,
    GLA_HEADS * GLA_DV, GLA_HEADS * GLA_DV, GLA_GATE_RANK,
    DSA_HEADS * HEAD_DIM, HEAD_DIM, HEAD_DIM,
    IDX_HEADS * IDX_DIM, IDX_DIM, IDX_HEADS,
)
N_IN = sum(_IN_WIDTHS)
_IN_OFFSETS = tuple(sum(_IN_WIDTHS[:i + 1]) for i in range(len(_IN_WIDTHS) - 1))

kernel_name = "hybrid_dswa_gla_dsa_gated_block"


def rms_norm(x, g):
    xf = x.astype(jnp.float32)
    y = xf * lax.rsqrt(jnp.mean(xf * xf, axis=-1, keepdims=True) + RMS_EPS)
    return (y * g.astype(jnp.float32)).astype(x.dtype)


def rel_bucket(dist):
    max_exact = REL_BUCKETS // 2
    d = jnp.maximum(dist, 0)
    df = jnp.maximum(d, 1).astype(jnp.float32)
    large = max_exact + (jnp.log(df / max_exact) / math.log(REL_MAX_DIST / max_exact)
                         * (REL_BUCKETS - max_exact)).astype(jnp.int32)
    large = jnp.minimum(large, REL_BUCKETS - 1)
    return jnp.where(d < max_exact, d, large)


def dilated_window_group(q, k, v, bias_tab, window, dilation):
    B_, S_, H, Dh = q.shape
    r = dilation
    n = window // dilation
    blk = DSWA_BLOCK
    L = S_ // r
    nblk = -(-L // blk)
    Lp = nblk * blk

    def to_res(t):
        return t.reshape(B_, L, r, H, Dh).transpose(0, 2, 1, 3, 4)

    qr = jnp.pad(to_res(q), ((0, 0), (0, 0), (0, Lp - L), (0, 0), (0, 0)))
    kv_pad = ((0, 0), (0, 0), (blk, Lp - L), (0, 0), (0, 0))
    kr = jnp.pad(to_res(k), kv_pad)
    vr = jnp.pad(to_res(v), kv_pad)
    qb = qr.reshape(B_, r, nblk, blk, H, Dh)

    def band(t):
        tb = t.reshape(B_, r, nblk + 1, blk, H, Dh)
        return jnp.concatenate([tb[:, :, :-1], tb[:, :, 1:]], axis=3)

    kb, vb = band(kr), band(vr)
    iq = jnp.arange(blk)[:, None]
    jk = jnp.arange(2 * blk)[None, :]
    delta = iq - jk + blk
    key_idx = jnp.arange(nblk)[:, None, None] * blk + jk[None] - blk
    valid = (delta >= 0) & (delta <= n) & (key_idx >= 0)
    bias = bias_tab[rel_bucket(delta * r)].transpose(2, 0, 1)
    s = jnp.einsum('brnqhd,brnkhd->brnhqk', qb, kb).astype(jnp.float32) + bias.astype(jnp.float32)
    s = jnp.where(valid[:, None], s, -jnp.inf)
    m = jnp.max(s, axis=-1)
    pexp = jnp.exp(s - m[..., None])
    l = jnp.sum(pexp, axis=-1)
    m = m.transpose(0, 1, 2, 4, 3)
    l = l.transpose(0, 1, 2, 4, 3)
    o = jnp.einsum('brnhqk,brnkhd->brnqhd', pexp, vb.astype(jnp.float32)) / l[..., None]

    def from_res(t):
        rest = t.shape[4:]
        t = t.reshape((B_, r, Lp) + rest)[:, :, :L]
        return jnp.swapaxes(t, 1, 2).reshape((B_, S_) + rest)

    return from_res(o), from_res(m), from_res(l)


def mixer_dilated(q, k, v, bias_tab):
    outs, ms, ls = [], [], []
    for g, (window, dilation) in enumerate(DSWA_PATTERNS):
        sl = slice(g * DSWA_HEADS_PER_GROUP, (g + 1) * DSWA_HEADS_PER_GROUP)
        o, m, l = dilated_window_group(q[:, :, sl], k[:, :, sl], v[:, :, sl], bias_tab[:, sl], window, dilation)
        outs.append(o)
        ms.append(m)
        ls.append(l)
    o = jnp.stack(outs)
    m = jnp.stack(ms)
    l = jnp.stack(ls)
    wts = l * jnp.exp(m - jnp.max(m, axis=0, keepdims=True))
    wts = wts / jnp.sum(wts, axis=0, keepdims=True)
    return jnp.sum(wts[..., None] * o, axis=0)


def mixer_gla(q, k, v, a_low, w_a2, b_a):
    B_, S_, H, dk = q.shape
    dv = v.shape[-1]
    C = GLA_CHUNK
    N = S_ // C
    log_a = jax.nn.log_sigmoid((a_low @ w_a2 + b_a).astype(jnp.float32)) / GLA_TAU
    bcum = jnp.cumsum(log_a.reshape(B_, N, C, H, dk), axis=2)
    qf = q.astype(jnp.float32).reshape(B_, N, C, H, dk) * dk ** -0.5
    kf = k.astype(jnp.float32).reshape(B_, N, C, H, dk)
    vf = v.astype(jnp.float32).reshape(B_, N, C, H, dv)
    b_last = bcum[:, :, -1:]
    q_in = qf * jnp.exp(bcum)
    k_in = kf * jnp.exp(-bcum)
    k_end = kf * jnp.exp(b_last - bcum)
    causal = jnp.tril(jnp.ones((C, C), dtype=bool))
    att = jnp.einsum('bnqhd,bnkhd->bnhqk', q_in, k_in)
    att = jnp.where(causal, att, 0.0)
    o_intra = jnp.einsum('bnhqk,bnkhe->bnqhe', att, vf)

    def step(state, xs):
        qc, kc, vc, dec = xs
        o = jnp.einsum('bqhd,bhde->bqhe', qc, state)
        state = state * dec[..., None] + jnp.einsum('bkhd,bkhe->bhde', kc, vc)
        return state, o

    xs = (jnp.moveaxis(q_in, 1, 0), jnp.moveaxis(k_end, 1, 0), jnp.moveaxis(vf, 1, 0),
          jnp.moveaxis(jnp.exp(b_last[:, :, 0]), 1, 0))
    state0 = jnp.zeros((B_, H, dk, dv), jnp.float32)
    _, o_inter = lax.scan(step, state0, xs)
    o = o_intra + jnp.moveaxis(o_inter, 0, 1)
    return o.reshape(B_, S_, H, dv)


def mixer_dsa(q, k, v, iq, ik, iw, bias_tab):
    B_, S_, H, Dh = q.shape
    topk = min(IDX_TOPK, S_ // 4)
    nb = S_ // DSA_QBLOCK
    pos = jnp.arange(S_)
    ikf = ik.astype(jnp.float32)
    idx_scale = (IDX_HEADS ** -0.5) * (IDX_DIM ** -0.5)

    def blocks(t):
        return jnp.swapaxes(t.reshape((B_, nb, DSA_QBLOCK) + t.shape[2:]), 0, 1)

    def one_block(args):
        qb, iqb, iwb, tq = args
        sc = jax.nn.relu(jnp.einsum('bqhd,bsd->bqhs', iqb.astype(jnp.float32), ikf))
        score = jnp.einsum('bqh,bqhs->bqs', iwb.astype(jnp.float32) * idx_scale, sc)
        visible = pos[None, :] <= tq[:, None]
        score = jnp.where(visible[None], score, -jnp.inf)
        top_s, idx = lax.top_k(score, topk)
        sel_ok = top_s > -jnp.inf
        kg = jax.vmap(lambda kk, ii: kk[ii])(k, idx)
        vg = jax.vmap(lambda vv, ii: vv[ii])(v, idx)
        logits = jnp.einsum('bqhd,bqkd->bhqk', qb, kg).astype(jnp.float32)
        bias = bias_tab[rel_bucket(tq[None, :, None] - idx)].astype(jnp.float32)
        logits = logits + bias.transpose(0, 3, 1, 2)
        logits = jnp.where(sel_ok[:, None], logits, -jnp.inf)
        prob = jax.nn.softmax(logits, axis=-1)
        return jnp.einsum('bhqk,bqkd->bqhd', prob, vg.astype(jnp.float32))

    out = lax.map(one_block, (blocks(q), blocks(iq), blocks(iw), pos.reshape(nb, DSA_QBLOCK)))
    return jnp.swapaxes(out, 0, 1).reshape(B_, S_, H, Dh)


def hybrid_layer(h, p_i, rel_bias, norm_mix, w_in, qn_a, kn_a, qn_c, kn_c, w_alpha2, b_alpha,
                 gla_norm, w_out_a, w_out_b, w_out_c, w_gate, b_gate, w_o, norm_ffn,
                 w_ffn_gate, w_ffn_up, w_ffn_down, norm_ple, w_ple_gate, w_ple_proj):
    B_, S_, _ = h.shape
    dt = h.dtype
    u = rms_norm(h, norm_mix)
    parts = jnp.split(u @ w_in, _IN_OFFSETS, axis=-1)
    a_q, a_k, a_v, b_q, b_k, b_v, b_r, b_al, c_q, c_k, c_v, i_q, i_k, i_w = parts

    def heads(t, nh):
        return t.reshape(B_, S_, nh, -1)

    qa = rms_norm(heads(a_q, DSWA_HEADS), qn_a) * HEAD_DIM ** -0.5
    ka = rms_norm(heads(a_k, DSWA_HEADS), kn_a)
    o_a = mixer_dilated(qa, ka, heads(a_v, DSWA_HEADS), rel_bias[:, :DSWA_HEADS])
    y_a = o_a.reshape(B_, S_, -1).astype(dt) @ w_out_a
    o_b = mixer_gla(heads(b_q, GLA_HEADS), heads(b_k, GLA_HEADS), heads(b_v, GLA_HEADS), b_al, w_alpha2, b_alpha)
    o_b = rms_norm(o_b, gla_norm) * jax.nn.silu(heads(b_r, GLA_HEADS).astype(jnp.float32))
    y_b = o_b.reshape(B_, S_, -1).astype(dt) @ w_out_b
    qc = rms_norm(heads(c_q, DSA_HEADS), qn_c) * HEAD_DIM ** -0.5
    kc = rms_norm(c_k, kn_c)
    o_c = mixer_dsa(qc, kc, c_v, heads(i_q, IDX_HEADS), i_k, i_w, rel_bias[:, DSWA_HEADS:])
    y_c = o_c.reshape(B_, S_, -1).astype(dt) @ w_out_c
    g_a, g_b, g_c = jnp.split(jax.nn.sigmoid(u @ w_gate + b_gate), 3, axis=-1)
    h = h + (g_a * y_a + g_b * y_b + g_c * y_c) @ w_o
    f = rms_norm(h, norm_ffn)
    h = h + (jax.nn.silu(f @ w_ffn_gate) * (f @ w_ffn_up)) @ w_ffn_down
    e = rms_norm(h, norm_ple)
    h = h + jax.nn.sigmoid(e @ w_ple_gate) * (p_i @ w_ple_proj)
    return h


def setup_inputs(seed: int = 0) -> dict:
    key = jax.random.key(seed)
    ks = iter(jax.random.split(key, 32))

    def nrm(shape, scale):
        return jax.random.normal(next(ks), shape, jnp.float32) * scale

    def gain(shape):
        return 1.0 + nrm(shape, 0.05)

    D = D_MODEL
    return {
        "x": nrm((BATCH, SEQ, D), 1.0),
        "p": nrm((DEPTH, BATCH, SEQ, PLE_DIM), 1.0),
        "rel_bias": nrm((REL_BUCKETS, N_BIAS_HEADS), 0.2),
        "norm_mix": gain((DEPTH, D)),
        "w_in": nrm((DEPTH, D, N_IN), D ** -0.5),
        "qn_a": gain((DEPTH, HEAD_DIM)),
        "kn_a": gain((DEPTH, HEAD_DIM)),
        "qn_c": gain((DEPTH, HEAD_DIM)),
        "kn_c": gain((DEPTH, HEAD_DIM)),
        "w_alpha2": nrm((DEPTH, GLA_GATE_RANK, GLA_HEADS * GLA_DK), GLA_GATE_RANK ** -0.5),
        "b_alpha": nrm((DEPTH, GLA_HEADS * GLA_DK), 0.1),
        "gla_norm": gain((DEPTH, GLA_DV)),
        "w_out_a": nrm((DEPTH, DSWA_HEADS_PER_GROUP * HEAD_DIM, D), (DSWA_HEADS_PER_GROUP * HEAD_DIM) ** -0.5),
        "w_out_b": nrm((DEPTH, GLA_HEADS * GLA_DV, D), (GLA_HEADS * GLA_DV) ** -0.5),
        "w_out_c": nrm((DEPTH, DSA_HEADS * HEAD_DIM, D), (DSA_HEADS * HEAD_DIM) ** -0.5),
        "w_gate": nrm((DEPTH, D, 3 * D), D ** -0.5),
        "b_gate": nrm((DEPTH, 3 * D), 0.1),
        "w_o": nrm((DEPTH, D, D), D ** -0.5),
        "norm_ffn": gain((DEPTH, D)),
        "w_ffn_gate": nrm((DEPTH, D, D_FF), D ** -0.5),
        "w_ffn_up": nrm((DEPTH, D, D_FF), D ** -0.5),
        "w_ffn_down": nrm((DEPTH, D_FF, D), D_FF ** -0.5),
        "norm_ple": gain((DEPTH, D)),
        "w_ple_gate": nrm((DEPTH, D, D), D ** -0.5),
        "w_ple_proj": nrm((DEPTH, PLE_DIM, D), PLE_DIM ** -0.5),
    }


def reference(x, p, rel_bias, norm_mix, w_in, qn_a, kn_a, qn_c, kn_c, w_alpha2, b_alpha, gla_norm,
              w_out_a, w_out_b, w_out_c, w_gate, b_gate, w_o, norm_ffn, w_ffn_gate, w_ffn_up,
              w_ffn_down, norm_ple, w_ple_gate, w_ple_proj):
    h = x
    for i in range(DEPTH):
        h = hybrid_layer(h, p[i], rel_bias, norm_mix[i], w_in[i], qn_a[i], kn_a[i], qn_c[i], kn_c[i],
                         w_alpha2[i], b_alpha[i], gla_norm[i], w_out_a[i], w_out_b[i], w_out_c[i],
                         w_gate[i], b_gate[i], w_o[i], norm_ffn[i], w_ffn_gate[i], w_ffn_up[i],
                         w_ffn_down[i], norm_ple[i], w_ple_gate[i], w_ple_proj[i])
    return h
```

```python
import functools
import math

import numpy as np
import jax
import jax.numpy as jnp
from jax import lax
from jax.experimental import pallas as pl
from jax.experimental.pallas import tpu as pltpu

F32 = jnp.float32
BF16 = jnp.bfloat16
I32 = jnp.int32

D_MODEL = 1024
HEAD_DIM = 64
RMS_EPS = 1e-6
DSWA_PATTERNS = ((128, 1), (512, 4), (2048, 16))
DSWA_HPG = 4
DSWA_HEADS = 12
DSWA_BLOCK = 128
GLA_HEADS = 4
GLA_DK = 128
GLA_DV = 256
GLA_RANK = 16
GLA_TAU = 16.0
GLA_CHUNK = 64
DSA_HEADS = 12
IDX_HEADS = 8
IDX_DIM = 64
IDX_TOPK = 256
REL_BUCKETS = 32
REL_MAX_DIST = 2048
D_FF = 2816
PLE_DIM = 256
IN_WIDTHS = (768, 768, 768, 512, 512, 1024, 1024, 16, 768, 64, 64, 512, 64, 8)

LANES = 128
MXU_N = 256
VMEM_LIMIT = 56 * 1024 * 1024

W1 = 2560
P1_AQ, P1_AK, P1_CQ, P1_CK = 0, 768, 1536, 2304
W2 = 5120
P2_BV, P2_BR, P2_BQ, P2_BK, P2_IQ, P2_AV = 0, 1024, 2048, 2560, 3072, 3584
P2_BAL, P2_CV, P2_IK, P2_IW = 4352, 4480, 4608, 4736

PROJ_TM = 1024
PROJ_TN = 512

DSA_TQ = 128
DSA_CH = 256
DSA_BIAS_CONST_FROM = 1512
DSA_Z0 = 1792
DSA_Z = DSA_Z0 + DSA_CH
INT_MIN = -(2 ** 31)
NEG_BIG = -1e30


def _cparams(sem):
    return pltpu.CompilerParams(dimension_semantics=sem, vmem_limit_bytes=VMEM_LIMIT)


def _proj_kernel(h_ref, g_ref, w_ref, e_ref, *rest, mode):
    if mode == "qk":
        bd_ref, o_ref, u_ref = rest
    else:
        o_ref, u_ref = rest

    @pl.when(pl.program_id(1) == 0)
    def _():
        x = h_ref[...]
        ms = jnp.mean(x * x, axis=-1, keepdims=True)
        u_ref[...] = (x * lax.rsqrt(ms + RMS_EPS) * g_ref[...]).astype(BF16)

    acc = jnp.dot(u_ref[...], w_ref[...], preferred_element_type=F32)
    if mode == "plain":
        out = acc
    elif mode == "gate":
        out = jax.nn.sigmoid(acc + e_ref[...])
    else:
        sq = acc * acc
        hi = sq.astype(BF16)
        lo = (sq - hi.astype(F32)).astype(BF16)
        ss = (jnp.dot(hi, bd_ref[...], preferred_element_type=F32)
              + jnp.dot(lo, bd_ref[...], preferred_element_type=F32))
        out = acc * lax.rsqrt(ss * (1.0 / HEAD_DIM) + RMS_EPS) * e_ref[...]
    o_ref[...] = out.astype(o_ref.dtype)


def _proj(h, g, w, e, mode):
    m, d = h.shape
    n = w.shape[1]
    tm, tn = PROJ_TM, PROJ_TN
    in_specs = [
        pl.BlockSpec((tm, d), lambda i, j: (i, 0)),
        pl.BlockSpec((1, d), lambda i, j: (0, 0)),
        pl.BlockSpec((d, tn), lambda i, j: (0, j)),
        pl.BlockSpec((1, tn), lambda i, j: (0, j)),
    ]
    args = [h, g, w, e]
    if mode == "qk":
        r = np.arange(tn) // HEAD_DIM
        bd = jnp.asarray((r[:, None] == r[None, :]).astype(np.float32), dtype=BF16)
        in_specs.append(pl.BlockSpec((tn, tn), lambda i, j: (0, 0)))
        args.append(bd)
    return pl.pallas_call(
        functools.partial(_proj_kernel, mode=mode),
        grid=(m // tm, n // tn),
        in_specs=in_specs,
        out_specs=pl.BlockSpec((tm, tn), lambda i, j: (i, j)),
        out_shape=jax.ShapeDtypeStruct((m, n), BF16),
        scratch_shapes=[pltpu.VMEM((tm, d), BF16)],
        compiler_params=_cparams(("parallel", "arbitrary")),
        name="proj_" + mode,
    )(*args)


def _dswa_kernel(q_ref, kp_ref, kc_ref, vp_ref, vc_ref, bias_ref, o_ref, st_ref):
    i = pl.program_id(2)
    blk = DSWA_BLOCK
    row = lax.broadcasted_iota(I32, (blk, 2 * blk), 0)
    col = lax.broadcasted_iota(I32, (blk, 2 * blk), 1)
    prev_off = jnp.where(i > 0, 0, 4 * blk)
    valid = ((col < blk) & (col >= row + prev_off)) | ((col >= blk) & ((col - blk) <= row))
    q = q_ref[...]
    k = jnp.concatenate([kp_ref[...], kc_ref[...]], axis=0)
    v = jnp.concatenate([vp_ref[...], vc_ref[...]], axis=0)
    lane = lax.broadcasted_iota(I32, (blk, LANES), 1)
    stats = jnp.zeros((blk, LANES), F32)
    for h in range(DSWA_HPG):
        sl = slice(h * HEAD_DIM, (h + 1) * HEAD_DIM)
        s = lax.dot_general(q[:, sl], k[:, sl], (((1,), (1,)), ((), ())), preferred_element_type=F32)
        s = jnp.where(valid, s + bias_ref[h], -jnp.inf)
        m = jnp.max(s, axis=-1, keepdims=True)
        p = jnp.exp(s - m)
        l = jnp.sum(p, axis=-1, keepdims=True)
        o = jnp.dot(p.astype(BF16), v[:, sl], preferred_element_type=F32) / l
        o_ref[:, sl] = o
        stats = jnp.where(lane == h, m, stats)
        stats = jnp.where(lane == DSWA_HPG + h, l, stats)
    st_ref[...] = stats


def _dswa(p1, p2, bias, g, r, bsz, seq):
    blk = DSWA_BLOCK
    L = seq // r
    nblk = L // blk
    gw = DSWA_HPG * HEAD_DIM
    p1v = p1.reshape(bsz, L, r * W1)
    p2v = p2.reshape(bsz, L, r * W2)
    q_col = lambda c: c * (W1 // gw) + (P1_AQ // gw) + g
    k_col = lambda c: c * (W1 // gw) + (P1_AK // gw) + g
    v_col = lambda c: c * (W2 // gw) + (P2_AV // gw) + g
    in_specs = [
        pl.BlockSpec((None, blk, gw), lambda b, c, i: (b, i, q_col(c))),
        pl.BlockSpec((None, blk, gw), lambda b, c, i: (b, jnp.maximum(i - 1, 0), k_col(c))),
        pl.BlockSpec((None, blk, gw), lambda b, c, i: (b, i, k_col(c))),
        pl.BlockSpec((None, blk, gw), lambda b, c, i: (b, jnp.maximum(i - 1, 0), v_col(c))),
        pl.BlockSpec((None, blk, gw), lambda b, c, i: (b, i, v_col(c))),
        pl.BlockSpec((DSWA_HPG, blk, 2 * blk), lambda b, c, i: (0, 0, 0)),
    ]
    out_specs = [
        pl.BlockSpec((None, blk, gw), lambda b, c, i: (b, i, c)),
        pl.BlockSpec((None, blk, LANES), lambda b, c, i: (b, i, c)),
    ]
    o, st = pl.pallas_call(
        _dswa_kernel,
        grid=(bsz, r, nblk),
        in_specs=in_specs,
        out_specs=out_specs,
        out_shape=[jax.ShapeDtypeStruct((bsz, L, r * gw), F32),
                   jax.ShapeDtypeStruct((bsz, L, r * LANES), F32)],
        compiler_params=_cparams(("parallel", "parallel", "arbitrary")),
        name="dswa_g%d" % g,
    )(p1v, p1v, p1v, p2v, p2v, bias)
    return o.reshape(bsz * seq, gw), st.reshape(bsz * seq, LANES)


GLA_TC = 256


def _split3(x):
    a1 = x.astype(BF16)
    r1 = x - a1.astype(F32)
    a2 = r1.astype(BF16)
    a3 = (r1 - a2.astype(F32)).astype(BF16)
    return a1, a2, a3


def _gla_kernel(v_ref, r_ref, q_ref, k_ref, al_ref, wa_ref, ba_ref, gn_ref, o_ref, st_ref):
    C = GLA_CHUNK

    @pl.when(pl.program_id(1) == 0)
    def _():
        st_ref[...] = jnp.zeros_like(st_ref)

    row = lax.broadcasted_iota(I32, (C, C), 0)
    col = lax.broadcasted_iota(I32, (C, C), 1)
    tri = row >= col
    tri_bf = jnp.where(tri, 1.0, 0.0).astype(BF16)
    nt = (((1,), (1,)), ((), ()))
    tn = (((0,), (0,)), ((), ()))
    for c in range(GLA_TC // C):
        rs = slice(c * C, (c + 1) * C)
        z = jnp.dot(al_ref[rs, :], wa_ref[...], preferred_element_type=F32) + ba_ref[...]
        la = (jnp.minimum(z, 0.0) - jnp.log(1.0 + jnp.exp(-jnp.abs(z)))) * (1.0 / GLA_TAU)
        a1, a2, a3 = _split3(la)
        bcum = (jnp.dot(tri_bf, a1, preferred_element_type=F32)
                + jnp.dot(tri_bf, a2, preferred_element_type=F32)
                + jnp.dot(tri_bf, a3, preferred_element_type=F32))
        blast = bcum[C - 1:C, :]
        qf = q_ref[rs, :].astype(F32) * (GLA_DK ** -0.5)
        kf = k_ref[rs, :].astype(F32)
        q_in = (qf * jnp.exp(bcum)).astype(BF16)
        k_in = (kf * jnp.exp(-bcum)).astype(BF16)
        k_end = (kf * jnp.exp(blast - bcum)).astype(BF16)
        dec = jnp.exp(blast)
        for h in range(GLA_HEADS):
            ks = slice(h * GLA_DK, (h + 1) * GLA_DK)
            vs = slice(h * GLA_DV, (h + 1) * GLA_DV)
            vh = v_ref[rs, vs]
            att = lax.dot_general(q_in[:, ks], k_in[:, ks], nt, preferred_element_type=F32)
            att = jnp.where(tri, att, 0.0)
            o = jnp.dot(att.astype(BF16), vh, preferred_element_type=F32)
            st = st_ref[h]
            o = o + lax.dot_general(q_in[:, ks], st.astype(BF16), nt, preferred_element_type=F32)
            st_ref[h] = st * dec[:, ks] + lax.dot_general(vh, k_end[:, ks], tn, preferred_element_type=F32)
            ms = jnp.mean(o * o, axis=-1, keepdims=True)
            y = o * lax.rsqrt(ms + RMS_EPS) * gn_ref[...]
            rg = r_ref[rs, vs].astype(F32)
            y = y * (rg * jax.nn.sigmoid(rg))
            o_ref[rs, vs] = y.astype(o_ref.dtype)


def _gla(p2, wa, ba, gn, bsz, seq):
    tc = GLA_TC
    p2v = p2.reshape(bsz, seq, W2)
    in_specs = [
        pl.BlockSpec((None, tc, 1024), lambda b, t: (b, t, P2_BV // 1024)),
        pl.BlockSpec((None, tc, 1024), lambda b, t: (b, t, P2_BR // 1024)),
        pl.BlockSpec((None, tc, 512), lambda b, t: (b, t, P2_BQ // 512)),
        pl.BlockSpec((None, tc, 512), lambda b, t: (b, t, P2_BK // 512)),
        pl.BlockSpec((None, tc, LANES), lambda b, t: (b, t, P2_BAL // LANES)),
        pl.BlockSpec((LANES, 512), lambda b, t: (0, 0)),
        pl.BlockSpec((1, 512), lambda b, t: (0, 0)),
        pl.BlockSpec((1, GLA_DV), lambda b, t: (0, 0)),
    ]
    out = pl.pallas_call(
        _gla_kernel,
        grid=(bsz, seq // tc),
        in_specs=in_specs,
        out_specs=pl.BlockSpec((None, tc, 1024), lambda b, t: (b, t, 0)),
        out_shape=jax.ShapeDtypeStruct((bsz, seq, 1024), BF16),
        scratch_shapes=[pltpu.VMEM((GLA_HEADS, GLA_DV, GLA_DK), F32)],
        compiler_params=_cparams(("parallel", "arbitrary")),
        name="gla",
    )(p2v, p2v, p2v, p2v, p2v, wa, ba, gn)
    return out.reshape(bsz * seq, 1024)


def _dsa_kernel(q_ref, iq_ref, iw_ref, k_ref, ik_ref, vt_ref, bias_ref, wo_ref, o_ref,
                keys_ref, qall_ref, iqall_ref, acc_ref, *, topk):
    TQ, CH = DSA_TQ, DSA_CH
    qi = pl.program_id(1)
    t0 = qi * TQ
    nch = qi // (CH // TQ) + 1
    nt = (((1,), (1,)), ((), ()))
    tn = (((0,), (0,)), ((), ()))

    for h in range(DSA_HEADS):
        qall_ref[h * TQ:(h + 1) * TQ, :] = q_ref[:, h * HEAD_DIM:(h + 1) * HEAD_DIM]
    for h in range(IDX_HEADS):
        iqall_ref[h * TQ:(h + 1) * TQ, :] = iq_ref[:, h * IDX_DIM:(h + 1) * IDX_DIM]
    idx_scale = (IDX_HEADS ** -0.5) * (IDX_DIM ** -0.5)
    wt = (iw_ref[...].astype(F32) * idx_scale).T
    w_row = jnp.concatenate([wt[h:h + 1, :] for h in range(IDX_HEADS)], axis=1)

    qpos = t0 + lax.broadcasted_iota(I32, (1, TQ), 1)
    krow = lax.broadcasted_iota(I32, (CH, TQ), 0)

    def score_body(c, carry):
        s0 = pl.multiple_of(c * CH, CH)
        ikc = ik_ref[pl.ds(s0, CH), :][:, :IDX_DIM]
        sc = lax.dot_general(ikc, iqall_ref[...], nt, preferred_element_type=F32)
        sc = jnp.maximum(sc, 0.0) * w_row
        s = sc[:, 0:TQ]
        for h in range(1, IDX_HEADS):
            s = s + sc[:, h * TQ:(h + 1) * TQ]
        s = jnp.where(s == 0.0, 0.0, s)
        bits = pltpu.bitcast(s, I32)
        key = bits ^ ((bits >> 31) & 0x7FFFFFFF)
        key = jnp.where(s0 + krow <= qpos, key, INT_MIN)
        keys_ref[pl.ds(s0, CH), :] = key
        return carry

    lax.fori_loop(0, nch, score_body, 0)

    kvec = jnp.minimum(topk, qpos + 1)

    def count(pred_fn):
        def body(c, acc):
            s0 = pl.multiple_of(c * CH, CH)
            kk = keys_ref[pl.ds(s0, CH), :]
            hit = jnp.where(pred_fn(kk, s0), 1, 0).astype(I32)
            return acc + jnp.sum(hit.reshape(CH // 8, 8, TQ), axis=0)
        acc = lax.fori_loop(0, nch, body, jnp.zeros((8, TQ), I32))
        return jnp.sum(acc, axis=0, keepdims=True)

    def count_ge(cand):
        return count(lambda kk, s0: kk >= cand)

    thr = jnp.where(count_ge(jnp.zeros((1, TQ), I32)) >= kvec, 0, INT_MIN).astype(I32)

    def bit_body(b, thr):
        cand = thr | (jnp.int32(1) << (30 - b))
        return jnp.where(count_ge(cand) >= kvec, cand, thr)

    thr = lax.fori_loop(0, 31, bit_body, thr)

    excess = count_ge(thr) - kvec
    has_excess = jnp.max(excess) > 0

    @pl.when(has_excess)
    def _():
        need = kvec - count(lambda kk, s0: kk > thr)

        def tie_lt(cut):
            return count(lambda kk, s0: (kk == thr) & (s0 + krow < cut))

        def cut_body(b, cut):
            cand = cut | (jnp.int32(1) << (12 - b))
            return jnp.where(tie_lt(cand) <= need, cand, cut)

        cut = lax.fori_loop(0, 13, cut_body, jnp.zeros((1, TQ), I32))

        def drop_body(c, carry):
            s0 = pl.multiple_of(c * CH, CH)
            kk = keys_ref[pl.ds(s0, CH), :]
            keys_ref[pl.ds(s0, CH), :] = jnp.where((kk == thr) & (s0 + krow >= cut), INT_MIN, kk)
            return carry

        lax.fori_loop(0, nch, drop_body, 0)

    acc_ref[...] = jnp.zeros_like(acc_ref)
    NQ = DSA_HEADS * TQ

    def att_body(c, carry):
        m, l = carry
        s0 = pl.multiple_of(c * CH, CH)
        kc = k_ref[pl.ds(s0, CH), :][:, :HEAD_DIM]
        st = lax.dot_general(kc, qall_ref[...], nt, preferred_element_type=F32)
        zoff = pl.multiple_of(jnp.maximum(s0 - t0 + DSA_Z0, 0), 8)
        st = st + bias_ref[pl.ds(zoff, CH), :]
        sel = keys_ref[pl.ds(s0, CH), :] >= thr
        sel = jnp.concatenate([sel.astype(I32)] * DSA_HEADS, axis=1) > 0
        st = jnp.where(sel, st, NEG_BIG)
        m_new = jnp.maximum(m, jnp.max(st, axis=0, keepdims=True))
        alpha = jnp.exp(m - m_new)
        p = jnp.exp(st - m_new)
        l = l * alpha + jnp.sum(p, axis=0, keepdims=True)
        acc_ref[...] = acc_ref[...] * alpha + jnp.dot(vt_ref[c], p.astype(BF16), preferred_element_type=F32)
        return m_new, l

    m0 = jnp.full((1, NQ), NEG_BIG, F32)
    l0 = jnp.zeros((1, NQ), F32)
    _, l = lax.fori_loop(0, nch, att_body, (m0, l0))

    ot = (acc_ref[...] / l).astype(BF16)
    y = jnp.zeros((TQ, D_MODEL), F32)
    for h in range(DSA_HEADS):
        y = y + lax.dot_general(ot[:, h * TQ:(h + 1) * TQ], wo_ref[h * HEAD_DIM:(h + 1) * HEAD_DIM, :], tn,
                                preferred_element_type=F32)
    o_ref[...] = y


def _dsa(p1, p2, vt, bias_t, wo, bsz, seq):
    tq = DSA_TQ
    p1v = p1.reshape(bsz, seq, W1)
    p2v = p2.reshape(bsz, seq, W2)
    nq = DSA_HEADS * tq
    in_specs = [
        pl.BlockSpec((None, tq, 768), lambda b, i: (b, i, P1_CQ // 768)),
        pl.BlockSpec((None, tq, 512), lambda b, i: (b, i, P2_IQ // 512)),
        pl.BlockSpec((None, tq, LANES), lambda b, i: (b, i, P2_IW // LANES)),
        pl.BlockSpec((None, seq, LANES), lambda b, i: (b, 0, P1_CK // LANES)),
        pl.BlockSpec((None, seq, LANES), lambda b, i: (b, 0, P2_IK // LANES)),
        pl.BlockSpec((None, seq // DSA_CH, HEAD_DIM, DSA_CH), lambda b, i: (b, 0, 0, 0)),
        pl.BlockSpec((DSA_Z, nq), lambda b, i: (0, 0)),
        pl.BlockSpec((DSA_HEADS * HEAD_DIM, D_MODEL), lambda b, i: (0, 0)),
    ]
    out = pl.pallas_call(
        functools.partial(_dsa_kernel, topk=min(IDX_TOPK, seq // 4)),
        grid=(bsz, seq // tq),
        in_specs=in_specs,
        out_specs=pl.BlockSpec((None, tq, D_MODEL), lambda b, i: (b, i, 0)),
        out_shape=jax.ShapeDtypeStruct((bsz, seq, D_MODEL), F32),
        scratch_shapes=[
            pltpu.VMEM((seq, tq), I32),
            pltpu.VMEM((nq, HEAD_DIM), BF16),
            pltpu.VMEM((IDX_HEADS * tq, IDX_DIM), BF16),
            pltpu.VMEM((HEAD_DIM, nq), F32),
        ],
        compiler_params=_cparams(("parallel", "arbitrary")),
        name="dsa",
    )(p1v, p2v, p2v, p1v, p2v, vt, bias_t, wo)
    return out.reshape(bsz * seq, D_MODEL)


MERGE_TM = 512


def _merge_kernel(h_ref, oa0_ref, oa1_ref, oa2_ref, sa0_ref, sa1_ref, sa2_ref, ob_ref, yc_ref, g_ref,
                  woa_ref, wob_ref, wo_ref, o_ref):
    H = DSWA_HPG
    sts = [sa0_ref[...], sa1_ref[...], sa2_ref[...]]
    oas = [oa0_ref, oa1_ref, oa2_ref]
    ms = [s[:, 0:H] for s in sts]
    ls = [s[:, H:2 * H] for s in sts]
    mmax = jnp.maximum(jnp.maximum(ms[0], ms[1]), ms[2])
    wts = [l * jnp.exp(m - mmax) for m, l in zip(ms, ls)]
    tot = wts[0] + wts[1] + wts[2]
    wts = [w / tot for w in wts]
    parts = []
    for h in range(H):
        sl = slice(h * HEAD_DIM, (h + 1) * HEAD_DIM)
        acc = wts[0][:, h:h + 1] * oas[0][:, sl]
        acc = acc + wts[1][:, h:h + 1] * oas[1][:, sl]
        acc = acc + wts[2][:, h:h + 1] * oas[2][:, sl]
        parts.append(acc)
    oa = jnp.concatenate(parts, axis=1).astype(BF16)
    y_a = jnp.dot(oa, woa_ref[...], preferred_element_type=F32)
    y_b = jnp.dot(ob_ref[...], wob_ref[...], preferred_element_type=F32)
    y_c = yc_ref[...]
    D = D_MODEL
    mix = (g_ref[:, 0:D].astype(F32) * y_a + g_ref[:, D:2 * D].astype(F32) * y_b
           + g_ref[:, 2 * D:3 * D].astype(F32) * y_c)
    o_ref[...] = h_ref[...] + jnp.dot(mix.astype(BF16), wo_ref[...], preferred_element_type=F32)


def _merge(h, oas, sas, ob, yc, gates, woa, wob, wo):
    m = h.shape[0]
    tm = MERGE_TM
    row = lambda w: pl.BlockSpec((tm, w), lambda i: (i, 0))
    full = lambda a: pl.BlockSpec(a.shape, lambda i: (0, 0))
    in_specs = ([row(D_MODEL)] + [row(256)] * 3 + [row(LANES)] * 3 + [row(1024), row(D_MODEL), row(3 * D_MODEL)]
                + [full(woa), full(wob), full(wo)])
    return pl.pallas_call(
        _merge_kernel,
        grid=(m // tm,),
        in_specs=in_specs,
        out_specs=row(D_MODEL),
        out_shape=jax.ShapeDtypeStruct((m, D_MODEL), F32),
        compiler_params=_cparams(("parallel",)),
        name="merge",
    )(h, *oas, *sas, ob, yc, gates, woa, wob, wo)


FFN_TM = 1024
FFN_TF = 256


def _ffn_kernel(h_ref, g_ref, wg_ref, wu_ref, wd_ref, o_ref, u_ref, acc_ref):
    j = pl.program_id(1)

    @pl.when(j == 0)
    def _():
        x = h_ref[...]
        ms = jnp.mean(x * x, axis=-1, keepdims=True)
        u_ref[...] = (x * lax.rsqrt(ms + RMS_EPS) * g_ref[...]).astype(BF16)
        acc_ref[...] = jnp.zeros_like(acc_ref)

    u = u_ref[...]
    a = jnp.dot(u, wg_ref[...], preferred_element_type=F32)
    b = jnp.dot(u, wu_ref[...], preferred_element_type=F32)
    t = (a * jax.nn.sigmoid(a) * b).astype(BF16)
    acc_ref[...] += jnp.dot(t, wd_ref[...], preferred_element_type=F32)

    @pl.when(j == pl.num_programs(1) - 1)
    def _():
        o_ref[...] = h_ref[...] + acc_ref[...]


def _ffn(h, g, wg, wu, wd):
    m, d = h.shape
    f = wg.shape[1]
    tm, tf = FFN_TM, FFN_TF
    return pl.pallas_call(
        _ffn_kernel,
        grid=(m // tm, f // tf),
        in_specs=[
            pl.BlockSpec((tm, d), lambda i, j: (i, 0)),
            pl.BlockSpec((1, d), lambda i, j: (0, 0)),
            pl.BlockSpec((d, tf), lambda i, j: (0, j)),
            pl.BlockSpec((d, tf), lambda i, j: (0, j)),
            pl.BlockSpec((tf, d), lambda i, j: (j, 0)),
        ],
        out_specs=pl.BlockSpec((tm, d), lambda i, j: (i, 0)),
        out_shape=jax.ShapeDtypeStruct((m, d), F32),
        scratch_shapes=[pltpu.VMEM((tm, d), BF16), pltpu.VMEM((tm, d), F32)],
        compiler_params=_cparams(("parallel", "arbitrary")),
        name="ffn",
    )(h, g, wg, wu, wd)


PLE_TM = 512


def _ple_kernel(h_ref, p_ref, g_ref, wg_ref, wp_ref, o_ref):
    x = h_ref[...]
    ms = jnp.mean(x * x, axis=-1, keepdims=True)
    e = (x * lax.rsqrt(ms + RMS_EPS) * g_ref[...]).astype(BF16)
    gate = jax.nn.sigmoid(jnp.dot(e, wg_ref[...], preferred_element_type=F32))
    proj = jnp.dot(p_ref[...].astype(BF16), wp_ref[...], preferred_element_type=F32)
    o_ref[...] = x + gate * proj


def _ple(h, p, g, wg, wp):
    m, d = h.shape
    tm = PLE_TM
    return pl.pallas_call(
        _ple_kernel,
        grid=(m // tm,),
        in_specs=[
            pl.BlockSpec((tm, d), lambda i: (i, 0)),
            pl.BlockSpec((tm, PLE_DIM), lambda i: (i, 0)),
            pl.BlockSpec((1, d), lambda i: (0, 0)),
            pl.BlockSpec((d, d), lambda i: (0, 0)),
            pl.BlockSpec((PLE_DIM, d), lambda i: (0, 0)),
        ],
        out_specs=pl.BlockSpec((tm, d), lambda i: (i, 0)),
        out_shape=jax.ShapeDtypeStruct((m, d), F32),
        compiler_params=_cparams(("parallel",)),
        name="ple",
    )(h, p, g, wg, wp)


def _rel_bucket(dist):
    max_exact = REL_BUCKETS // 2
    d = jnp.maximum(dist, 0)
    df = jnp.maximum(d, 1).astype(F32)
    large = max_exact + (jnp.log(df / max_exact) / math.log(REL_MAX_DIST / max_exact)
                         * (REL_BUCKETS - max_exact)).astype(I32)
    large = jnp.minimum(large, REL_BUCKETS - 1)
    return jnp.where(d < max_exact, d, large)


def _bias_tables(rel_bias):
    blk = DSWA_BLOCK
    iq = np.arange(blk)[:, None]
    jk = np.arange(2 * blk)[None, :]
    delta = iq - jk + blk
    dswa = []
    for g, (_, r) in enumerate(DSWA_PATTERNS):
        b = rel_bias[_rel_bucket(jnp.asarray(delta * r, I32))]
        dswa.append(jnp.transpose(b[:, :, g * DSWA_HPG:(g + 1) * DSWA_HPG], (2, 0, 1)))
    z = np.arange(DSA_Z)[:, None]
    i = np.arange(DSA_TQ)[None, :]
    dist = jnp.asarray(i - z + DSA_Z0, I32)
    bt = rel_bias[_rel_bucket(dist)][:, :, DSWA_HEADS:]
    bias_t = jnp.transpose(bt, (0, 2, 1)).reshape(DSA_Z, DSA_HEADS * DSA_TQ)
    return dswa, bias_t


def _pad_cols(w, width):
    return jnp.pad(w, ((0, 0), (0, width - w.shape[1])))


def _layer_params(w_in, qn_a, kn_a, qn_c, kn_c, w_alpha2, b_alpha):
    offs = np.cumsum((0,) + IN_WIDTHS)
    parts = [w_in[:, offs[i]:offs[i + 1]] for i in range(len(IN_WIDTHS))]
    a_q, a_k, a_v, b_q, b_k, b_v, b_r, b_al, c_q, c_k, c_v, i_q, i_k, i_w = parts
    w1 = _pad_cols(jnp.concatenate([a_q, a_k, c_q, c_k], axis=1), W1).astype(BF16)
    w2 = _pad_cols(jnp.concatenate(
        [b_v, b_r, b_q, b_k, i_q, a_v, _pad_cols(b_al, LANES), _pad_cols(c_v, LANES),
         _pad_cols(i_k, LANES), _pad_cols(i_w, LANES)], axis=1), W2).astype(BF16)
    scale = HEAD_DIM ** -0.5
    gain1 = jnp.concatenate([jnp.tile(qn_a, DSWA_HEADS) * scale, jnp.tile(kn_a, DSWA_HEADS),
                             jnp.tile(qn_c, DSA_HEADS) * scale, kn_c])
    gain1 = jnp.pad(gain1, (0, W1 - gain1.shape[0])).reshape(1, W1)
    wa = jnp.pad(w_alpha2, ((0, LANES - GLA_RANK), (0, 0))).astype(BF16)
    return w1, w2, gain1, wa, b_alpha.reshape(1, -1)


def kernel(x, p, rel_bias, norm_mix, w_in, qn_a, kn_a, qn_c, kn_c, w_alpha2, b_alpha, gla_norm, w_out_a, w_out_b, w_out_c, w_gate, b_gate, w_o, norm_ffn, w_ffn_gate, w_ffn_up, w_ffn_down, norm_ple, w_ple_gate, w_ple_proj):
    bsz, seq, d = x.shape
    depth = p.shape[0]
    m = bsz * seq
    dswa_bias, bias_t = _bias_tables(rel_bias)
    h = x.reshape(m, d)
    zeros_w2 = jnp.zeros((1, W2), F32)
    for i in range(depth):
        w1, w2, gain1, wa, ba = _layer_params(w_in[i], qn_a[i], kn_a[i], qn_c[i], kn_c[i], w_alpha2[i], b_alpha[i])
        gmix = norm_mix[i].reshape(1, d)
        p1 = _proj(h, gmix, w1, gain1, "qk")
        p2 = _proj(h, gmix, w2, zeros_w2, "plain")
        gates = _proj(h, gmix, w_gate[i].astype(BF16), b_gate[i].reshape(1, -1), "gate")
        oas, sas = [], []
        for g, (_, r) in enumerate(DSWA_PATTERNS):
            o, st = _dswa(p1, p2, dswa_bias[g], g, r, bsz, seq)
            oas.append(o)
            sas.append(st)
        ob = _gla(p2, wa, ba, gla_norm[i].reshape(1, -1), bsz, seq)
        cv = p2[:, P2_CV:P2_CV + HEAD_DIM].reshape(bsz, seq // DSA_CH, DSA_CH, HEAD_DIM)
        vt = jnp.transpose(cv, (0, 1, 3, 2))
        yc = _dsa(p1, p2, vt, bias_t, w_out_c[i].astype(BF16), bsz, seq)
        h = _merge(h, oas, sas, ob, yc, gates, w_out_a[i].astype(BF16), w_out_b[i].astype(BF16),
                   w_o[i].astype(BF16))
        h = _ffn(h, norm_ffn[i].reshape(1, d), w_ffn_gate[i].astype(BF16), w_ffn_up[i].astype(BF16),
                 w_ffn_down[i].astype(BF16))
        h = _ple(h, p[i].reshape(m, PLE_DIM), norm_ple[i].reshape(1, d), w_ple_gate[i].astype(BF16),
                 w_ple_proj[i].astype(BF16))
    return h.reshape(bsz, seq, d)
```

```python
import functools
import math

import numpy as np
import jax
import jax.numpy as jnp
from jax import lax
from jax.experimental import pallas as pl
from jax.experimental.pallas import tpu as pltpu

F32 = jnp.float32
BF16 = jnp.bfloat16
I32 = jnp.int32

D_MODEL = 1024
HEAD_DIM = 64
RMS_EPS = 1e-6
DSWA_PATTERNS = ((128, 1), (512, 4), (2048, 16))
DSWA_HPG = 4
DSWA_HEADS = 12
DSWA_BLOCK = 128
GLA_HEADS = 4
GLA_DK = 128
GLA_DV = 256
GLA_RANK = 16
GLA_TAU = 16.0
GLA_CHUNK = 64
DSA_HEADS = 12
IDX_HEADS = 8
IDX_DIM = 64
IDX_TOPK = 256
REL_BUCKETS = 32
REL_MAX_DIST = 2048
D_FF = 2816
PLE_DIM = 256
IN_WIDTHS = (768, 768, 768, 512, 512, 1024, 1024, 16, 768, 64, 64, 512, 64, 8)

LANES = 128
MXU_N = 256
VMEM_LIMIT = 56 * 1024 * 1024

W1 = 2560
P1_AQ, P1_AK, P1_CQ, P1_CK = 0, 768, 1536, 2304
W2 = 5120
P2_BV, P2_BR, P2_BQ, P2_BK, P2_IQ, P2_AV = 0, 1024, 2048, 2560, 3072, 3584
P2_BAL, P2_CV, P2_IK, P2_IW = 4352, 4480, 4608, 4736

PROJ_TM = 1024
PROJ_TN = 512

DSA_TQ = 128
DSA_CH = 256
DSA_BIAS_CONST_FROM = 1512
DSA_Z0 = 1792
DSA_Z = DSA_Z0 + DSA_CH
INT_MIN = -(2 ** 31)
NEG_BIG = -1e30


def _cparams(sem):
    return pltpu.CompilerParams(dimension_semantics=sem, vmem_limit_bytes=VMEM_LIMIT)


def _proj_kernel(h_ref, g_ref, w_ref, e_ref, *rest, mode):
    if mode == "qk":
        bd_ref, o_ref, u_ref = rest
    else:
        o_ref, u_ref = rest

    @pl.when(pl.program_id(1) == 0)
    def _():
        x = h_ref[...]
        ms = jnp.mean(x * x, axis=-1, keepdims=True)
        u_ref[...] = (x * lax.rsqrt(ms + RMS_EPS) * g_ref[...]).astype(BF16)

    acc = jnp.dot(u_ref[...], w_ref[...], preferred_element_type=F32)
    if mode == "plain":
        out = acc
    elif mode == "gate":
        out = jax.nn.sigmoid(acc + e_ref[...])
    else:
        sq = acc * acc
        hi = sq.astype(BF16)
        lo = (sq - hi.astype(F32)).astype(BF16)
        ss = (jnp.dot(hi, bd_ref[...], preferred_element_type=F32)
              + jnp.dot(lo, bd_ref[...], preferred_element_type=F32))
        out = acc * lax.rsqrt(ss * (1.0 / HEAD_DIM) + RMS_EPS) * e_ref[...]
    o_ref[...] = out.astype(o_ref.dtype)


def _proj(h, g, w, e, mode):
    m, d = h.shape
    n = w.shape[1]
    tm, tn = PROJ_TM, PROJ_TN
    in_specs = [
        pl.BlockSpec((tm, d), lambda i, j: (i, 0)),
        pl.BlockSpec((1, d), lambda i, j: (0, 0)),
        pl.BlockSpec((d, tn), lambda i, j: (0, j)),
        pl.BlockSpec((1, tn), lambda i, j: (0, j)),
    ]
    args = [h, g, w, e]
    if mode == "qk":
        r = np.arange(tn) // HEAD_DIM
        bd = jnp.asarray((r[:, None] == r[None, :]).astype(np.float32), dtype=BF16)
        in_specs.append(pl.BlockSpec((tn, tn), lambda i, j: (0, 0)))
        args.append(bd)
    return pl.pallas_call(
        functools.partial(_proj_kernel, mode=mode),
        grid=(m // tm, n // tn),
        in_specs=in_specs,
        out_specs=pl.BlockSpec((tm, tn), lambda i, j: (i, j)),
        out_shape=jax.ShapeDtypeStruct((m, n), BF16),
        scratch_shapes=[pltpu.VMEM((tm, d), BF16)],
        compiler_params=_cparams(("parallel", "arbitrary")),
        name="proj_" + mode,
    )(*args)


def _dswa_kernel(q_ref, kp_ref, kc_ref, vp_ref, vc_ref, bias_ref, o_ref, st_ref):
    i = pl.program_id(2)
    blk = DSWA_BLOCK
    row = lax.broadcasted_iota(I32, (blk, 2 * blk), 0)
    col = lax.broadcasted_iota(I32, (blk, 2 * blk), 1)
    prev_off = jnp.where(i > 0, 0, 4 * blk)
    valid = ((col < blk) & (col >= row + prev_off)) | ((col >= blk) & ((col - blk) <= row))
    q = q_ref[...]
    k = jnp.concatenate([kp_ref[...], kc_ref[...]], axis=0)
    v = jnp.concatenate([vp_ref[...], vc_ref[...]], axis=0)
    lane = lax.broadcasted_iota(I32, (blk, LANES), 1)
    stats = jnp.zeros((blk, LANES), F32)
    for h in range(DSWA_HPG):
        sl = slice(h * HEAD_DIM, (h + 1) * HEAD_DIM)
        s = lax.dot_general(q[:, sl], k[:, sl], (((1,), (1,)), ((), ())), preferred_element_type=F32)
        s = jnp.where(valid, s + bias_ref[h], -jnp.inf)
        m = jnp.max(s, axis=-1, keepdims=True)
        p = jnp.exp(s - m)
        l = jnp.sum(p, axis=-1, keepdims=True)
        o = jnp.dot(p.astype(BF16), v[:, sl], preferred_element_type=F32) / l
        o_ref[:, sl] = o
        stats = jnp.where(lane == h, m, stats)
        stats = jnp.where(lane == DSWA_HPG + h, l, stats)
    st_ref[...] = stats


def _dswa(p1, p2, bias, g, r, bsz, seq):
    blk = DSWA_BLOCK
    L = seq // r
    nblk = L // blk
    gw = DSWA_HPG * HEAD_DIM
    if r == 1:
        p1v = p1.reshape(bsz, L, W1)
        p2v = p2.reshape(bsz, L, W2)
        qv, kv, vv = p1v, p1v, p2v
        q_col = lambda c: (P1_AQ // gw) + g
        k_col = lambda c: (P1_AK // gw) + g
        v_col = lambda c: (P2_AV // gw) + g
    else:
        def regroup(p, off):
            return p[:, off + g * gw:off + (g + 1) * gw].reshape(bsz, L, r * gw)
        qv, kv, vv = regroup(p1, P1_AQ), regroup(p1, P1_AK), regroup(p2, P2_AV)
        q_col = k_col = v_col = lambda c: c
    in_specs = [
        pl.BlockSpec((None, blk, gw), lambda b, c, i: (b, i, q_col(c))),
        pl.BlockSpec((None, blk, gw), lambda b, c, i: (b, jnp.maximum(i - 1, 0), k_col(c))),
        pl.BlockSpec((None, blk, gw), lambda b, c, i: (b, i, k_col(c))),
        pl.BlockSpec((None, blk, gw), lambda b, c, i: (b, jnp.maximum(i - 1, 0), v_col(c))),
        pl.BlockSpec((None, blk, gw), lambda b, c, i: (b, i, v_col(c))),
        pl.BlockSpec((DSWA_HPG, blk, 2 * blk), lambda b, c, i: (0, 0, 0)),
    ]
    out_specs = [
        pl.BlockSpec((None, blk, gw), lambda b, c, i: (b, i, c)),
        pl.BlockSpec((None, blk, LANES), lambda b, c, i: (b, i, c)),
    ]
    o, st = pl.pallas_call(
        _dswa_kernel,
        grid=(bsz, r, nblk),
        in_specs=in_specs,
        out_specs=out_specs,
        out_shape=[jax.ShapeDtypeStruct((bsz, L, r * gw), F32),
                   jax.ShapeDtypeStruct((bsz, L, r * LANES), F32)],
        compiler_params=_cparams(("parallel", "parallel", "arbitrary")),
        name="dswa_g%d" % g,
    )(qv, kv, kv, vv, vv, bias)
    return o.reshape(bsz * seq, gw), st.reshape(bsz * seq, LANES)


GLA_TC = 256


def _split3(x):
    a1 = x.astype(BF16)
    r1 = x - a1.astype(F32)
    a2 = r1.astype(BF16)
    a3 = (r1 - a2.astype(F32)).astype(BF16)
    return a1, a2, a3


def _gla_kernel(v_ref, r_ref, q_ref, k_ref, al_ref, wa_ref, ba_ref, gn_ref, o_ref, st_ref):
    C = GLA_CHUNK

    @pl.when(pl.program_id(1) == 0)
    def _():
        st_ref[...] = jnp.zeros_like(st_ref)

    row = lax.broadcasted_iota(I32, (C, C), 0)
    col = lax.broadcasted_iota(I32, (C, C), 1)
    tri = row >= col
    tri_bf = jnp.where(tri, 1.0, 0.0).astype(BF16)
    nt = (((1,), (1,)), ((), ()))
    tn = (((0,), (0,)), ((), ()))
    for c in range(GLA_TC // C):
        rs = slice(c * C, (c + 1) * C)
        z = jnp.dot(al_ref[rs, :], wa_ref[...], preferred_element_type=F32) + ba_ref[...]
        la = (jnp.minimum(z, 0.0) - jnp.log(1.0 + jnp.exp(-jnp.abs(z)))) * (1.0 / GLA_TAU)
        a1, a2, a3 = _split3(la)
        bcum = (jnp.dot(tri_bf, a1, preferred_element_type=F32)
                + jnp.dot(tri_bf, a2, preferred_element_type=F32)
                + jnp.dot(tri_bf, a3, preferred_element_type=F32))
        blast = bcum[C - 1:C, :]
        qf = q_ref[rs, :].astype(F32) * (GLA_DK ** -0.5)
        kf = k_ref[rs, :].astype(F32)
        q_in = (qf * jnp.exp(bcum)).astype(BF16)
        k_in = (kf * jnp.exp(-bcum)).astype(BF16)
        k_end = (kf * jnp.exp(blast - bcum)).astype(BF16)
        dec = jnp.exp(blast)
        for h in range(GLA_HEADS):
            ks = slice(h * GLA_DK, (h + 1) * GLA_DK)
            vs = slice(h * GLA_DV, (h + 1) * GLA_DV)
            vh = v_ref[rs, vs]
            att = lax.dot_general(q_in[:, ks], k_in[:, ks], nt, preferred_element_type=F32)
            att = jnp.where(tri, att, 0.0)
            o = jnp.dot(att.astype(BF16), vh, preferred_element_type=F32)
            st = st_ref[h]
            o = o + lax.dot_general(q_in[:, ks], st.astype(BF16), nt, preferred_element_type=F32)
            st_ref[h] = st * dec[:, ks] + lax.dot_general(vh, k_end[:, ks], tn, preferred_element_type=F32)
            ms = jnp.mean(o * o, axis=-1, keepdims=True)
            y = o * lax.rsqrt(ms + RMS_EPS) * gn_ref[...]
            rg = r_ref[rs, vs].astype(F32)
            y = y * (rg * jax.nn.sigmoid(rg))
            o_ref[rs, vs] = y.astype(o_ref.dtype)


def _gla(p2, wa, ba, gn, bsz, seq):
    tc = GLA_TC
    p2v = p2.reshape(bsz, seq, W2)
    in_specs = [
        pl.BlockSpec((None, tc, 1024), lambda b, t: (b, t, P2_BV // 1024)),
        pl.BlockSpec((None, tc, 1024), lambda b, t: (b, t, P2_BR // 1024)),
        pl.BlockSpec((None, tc, 512), lambda b, t: (b, t, P2_BQ // 512)),
        pl.BlockSpec((None, tc, 512), lambda b, t: (b, t, P2_BK // 512)),
        pl.BlockSpec((None, tc, LANES), lambda b, t: (b, t, P2_BAL // LANES)),
        pl.BlockSpec((LANES, 512), lambda b, t: (0, 0)),
        pl.BlockSpec((1, 512), lambda b, t: (0, 0)),
        pl.BlockSpec((1, GLA_DV), lambda b, t: (0, 0)),
    ]
    out = pl.pallas_call(
        _gla_kernel,
        grid=(bsz, seq // tc),
        in_specs=in_specs,
        out_specs=pl.BlockSpec((None, tc, 1024), lambda b, t: (b, t, 0)),
        out_shape=jax.ShapeDtypeStruct((bsz, seq, 1024), BF16),
        scratch_shapes=[pltpu.VMEM((GLA_HEADS, GLA_DV, GLA_DK), F32)],
        compiler_params=_cparams(("parallel", "arbitrary")),
        name="gla",
    )(p2v, p2v, p2v, p2v, p2v, wa, ba, gn)
    return out.reshape(bsz * seq, 1024)


def _dsa_kernel(q_ref, iq_ref, iw_ref, k_ref, ik_ref, vt_ref, bias_ref, wot_ref, o_ref,
                keys_ref, qall_ref, iqall_ref, acc_ref, st_ref, pen_ref, *, topk):
    TQ, CH = DSA_TQ, DSA_CH
    CC = 2 * CH
    qi = pl.program_id(1)
    t0 = qi * TQ
    nch = qi // (CH // TQ) + 1
    npair = (nch + 1) // 2
    nt = (((1,), (1,)), ((), ()))

    for h in range(DSA_HEADS):
        qall_ref[h * TQ:(h + 1) * TQ, :] = q_ref[:, h * HEAD_DIM:(h + 1) * HEAD_DIM]
    for h in range(IDX_HEADS):
        iqall_ref[h * TQ:(h + 1) * TQ, :] = iq_ref[:, h * IDX_DIM:(h + 1) * IDX_DIM]
    idx_scale = (IDX_HEADS ** -0.5) * (IDX_DIM ** -0.5)
    wt = (iw_ref[...].astype(F32) * idx_scale).T

    qpos = t0 + lax.broadcasted_iota(I32, (1, TQ), 1)
    RB = 128
    krow1 = lax.broadcasted_iota(I32, (RB, TQ), 0)
    krow = lax.broadcasted_iota(I32, (CH, TQ), 0)
    krow2 = lax.broadcasted_iota(I32, (CC, TQ), 0)

    def score_body(cp, carry):
        for sub in range(CC // RB):
            s0 = pl.multiple_of(cp * CC + sub * RB, RB)
            ikc = ik_ref[pl.ds(s0, RB), :][:, :IDX_DIM]
            s = None
            for hp in range(IDX_HEADS // 2):
                x = lax.dot_general(ikc, iqall_ref[hp * 2 * TQ:(hp + 1) * 2 * TQ, :], nt,
                                    preferred_element_type=F32)
                t = (jnp.maximum(x[:, :TQ], 0.0) * wt[2 * hp:2 * hp + 1, :]
                     + jnp.maximum(x[:, TQ:], 0.0) * wt[2 * hp + 1:2 * hp + 2, :])
                s = t if s is None else s + t
            s = jnp.where(s == 0.0, 0.0, s)
            bits = pltpu.bitcast(s, I32)
            key = bits ^ ((bits >> 31) & 0x7FFFFFFF)
            key = jnp.where(s0 + krow1 <= qpos, key, INT_MIN)
            keys_ref[pl.ds(s0, RB), :] = key
        return carry

    lax.fori_loop(0, npair, score_body, 0)

    kvec = jnp.minimum(topk, qpos + 1)

    def count(pred_fn):
        def body(c, acc):
            s0 = pl.multiple_of(c * CC, CC)
            kk = keys_ref[pl.ds(s0, CC), :]
            hit = jnp.where(pred_fn(kk, s0), 1, 0).astype(I32)
            return acc + jnp.sum(hit.reshape(CC // 8, 8, TQ), axis=0)
        acc = lax.fori_loop(0, npair, body, jnp.zeros((8, TQ), I32))
        return jnp.sum(acc, axis=0, keepdims=True)

    def count_ge(cand):
        return count(lambda kk, s0: kk >= cand)

    thr = jnp.where(count_ge(jnp.zeros((1, TQ), I32)) >= kvec, 0, INT_MIN).astype(I32)

    def bit_body(b, thr):
        cand = thr | (jnp.int32(1) << (30 - b))
        return jnp.where(count_ge(cand) >= kvec, cand, thr)

    thr = lax.fori_loop(0, 31, bit_body, thr)

    excess = count_ge(thr) - kvec
    has_excess = jnp.max(excess) > 0

    @pl.when(has_excess)
    def _():
        need = kvec - count(lambda kk, s0: kk > thr)

        def tie_lt(cut):
            return count(lambda kk, s0: (kk == thr) & (s0 + krow2 < cut))

        def cut_body(b, cut):
            cand = cut | (jnp.int32(1) << (12 - b))
            return jnp.where(tie_lt(cand) <= need, cand, cut)

        cut = lax.fori_loop(0, 13, cut_body, jnp.zeros((1, TQ), I32))

        def drop_body(c, carry):
            s0 = pl.multiple_of(c * CH, CH)
            kk = keys_ref[pl.ds(s0, CH), :]
            keys_ref[pl.ds(s0, CH), :] = jnp.where((kk == thr) & (s0 + krow >= cut), INT_MIN, kk)
            return carry

        lax.fori_loop(0, nch, drop_body, 0)

    acc_ref[...] = jnp.zeros_like(acc_ref)

    NP = DSA_HEADS // 2
    PW = 2 * TQ

    def att_body(c, carry):
        ms, ls = carry
        s0 = pl.multiple_of(c * CH, CH)
        vtc = vt_ref[c]
        zoff = pl.multiple_of(jnp.maximum(s0 - t0 + DSA_Z0, 0), 8)
        NB = CH // RB
        for rb in range(NB):
            rs = pl.ds(s0 + rb * RB, RB)
            pen = jnp.where(keys_ref[rs, :] >= thr, 0.0, NEG_BIG)
            pen_ref[rb * RB:(rb + 1) * RB, :] = jnp.concatenate([pen, pen], axis=1)
        cms = []
        for hp in range(NP):
            ps = slice(hp * PW, (hp + 1) * PW)
            cm = None
            for rb in range(NB):
                kc = k_ref[pl.ds(s0 + rb * RB, RB), :][:, :HEAD_DIM]
                st = lax.dot_general(kc, qall_ref[ps, :], nt, preferred_element_type=F32)
                st = st + bias_ref[pl.ds(zoff + rb * RB, RB), ps] + pen_ref[rb * RB:(rb + 1) * RB, :]
                st_ref[rb * RB:(rb + 1) * RB, ps] = st
                tm = jnp.max(st, axis=0, keepdims=True)
                cm = tm if cm is None else jnp.maximum(cm, tm)
            cms.append(cm)
        new_ms, new_ls = [], []
        for hp in range(NP):
            ps = slice(hp * PW, (hp + 1) * PW)
            m_new = jnp.maximum(ms[hp], cms[hp])
            alpha = jnp.exp2(ms[hp] - m_new)
            l_new = ls[hp] * alpha
            pv = None
            for rb in range(NB):
                p = jnp.exp2(st_ref[rb * RB:(rb + 1) * RB, ps] - m_new)
                l_new = l_new + jnp.sum(p, axis=0, keepdims=True)
                d = jnp.dot(vtc[:, rb * RB:(rb + 1) * RB], p.astype(BF16), preferred_element_type=F32)
                pv = d if pv is None else pv + d
            new_ls.append(l_new)
            new_ms.append(m_new)
            acc_ref[:, ps] = acc_ref[:, ps] * alpha + pv
        return tuple(new_ms), tuple(new_ls)

    m0 = tuple(jnp.full((1, PW), NEG_BIG, F32) for _ in range(NP))
    l0 = tuple(jnp.zeros((1, PW), F32) for _ in range(NP))
    _, ls = lax.fori_loop(0, nch, att_body, (m0, l0))

    ot = jnp.concatenate(
        [(acc_ref[:, h * TQ:(h + 1) * TQ] / ls[h // 2][:, (h % 2) * TQ:(h % 2 + 1) * TQ]).astype(BF16)
         for h in range(DSA_HEADS)], axis=0)
    yt = jnp.dot(wot_ref[...], ot, preferred_element_type=F32)
    o_ref[...] = yt.T


def _dsa(p1, p2, vt, bias_t, wot, bsz, seq):
    tq = DSA_TQ
    p1v = p1.reshape(bsz, seq, W1)
    p2v = p2.reshape(bsz, seq, W2)
    nq = DSA_HEADS * tq
    in_specs = [
        pl.BlockSpec((None, tq, 768), lambda b, i: (b, i, P1_CQ // 768)),
        pl.BlockSpec((None, tq, 512), lambda b, i: (b, i, P2_IQ // 512)),
        pl.BlockSpec((None, tq, LANES), lambda b, i: (b, i, P2_IW // LANES)),
        pl.BlockSpec((None, seq, LANES), lambda b, i: (b, 0, P1_CK // LANES)),
        pl.BlockSpec((None, seq, LANES), lambda b, i: (b, 0, P2_IK // LANES)),
        pl.BlockSpec((None, seq // DSA_CH, HEAD_DIM, DSA_CH), lambda b, i: (b, 0, 0, 0)),
        pl.BlockSpec((DSA_Z, nq), lambda b, i: (0, 0)),
        pl.BlockSpec((D_MODEL, DSA_HEADS * HEAD_DIM), lambda b, i: (0, 0)),
    ]
    out = pl.pallas_call(
        functools.partial(_dsa_kernel, topk=min(IDX_TOPK, seq // 4)),
        grid=(bsz, seq // tq),
        in_specs=in_specs,
        out_specs=pl.BlockSpec((None, tq, D_MODEL), lambda b, i: (b, i, 0)),
        out_shape=jax.ShapeDtypeStruct((bsz, seq, D_MODEL), F32),
        scratch_shapes=[
            pltpu.VMEM((seq, tq), I32),
            pltpu.VMEM((nq, HEAD_DIM), BF16),
            pltpu.VMEM((IDX_HEADS * tq, IDX_DIM), BF16),
            pltpu.VMEM((HEAD_DIM, nq), F32),
            pltpu.VMEM((DSA_CH, nq), F32),
            pltpu.VMEM((DSA_CH, 2 * tq), F32),
        ],
        compiler_params=_cparams(("parallel", "arbitrary")),
        name="dsa",
    )(p1v, p2v, p2v, p1v, p2v, vt, bias_t, wot)
    return out.reshape(bsz * seq, D_MODEL)


MERGE_TM = 512


def _merge_kernel(h_ref, oa0_ref, oa1_ref, oa2_ref, sa0_ref, sa1_ref, sa2_ref, ob_ref, yc_ref, g_ref,
                  woa_ref, wob_ref, wo_ref, o_ref):
    H = DSWA_HPG
    sts = [sa0_ref[...], sa1_ref[...], sa2_ref[...]]
    oas = [oa0_ref, oa1_ref, oa2_ref]
    ms = [s[:, 0:H] for s in sts]
    ls = [s[:, H:2 * H] for s in sts]
    mmax = jnp.maximum(jnp.maximum(ms[0], ms[1]), ms[2])
    wts = [l * jnp.exp(m - mmax) for m, l in zip(ms, ls)]
    tot = wts[0] + wts[1] + wts[2]
    wts = [w / tot for w in wts]
    parts = []
    for h in range(H):
        sl = slice(h * HEAD_DIM, (h + 1) * HEAD_DIM)
        acc = wts[0][:, h:h + 1] * oas[0][:, sl]
        acc = acc + wts[1][:, h:h + 1] * oas[1][:, sl]
        acc = acc + wts[2][:, h:h + 1] * oas[2][:, sl]
        parts.append(acc)
    oa = jnp.concatenate(parts, axis=1).astype(BF16)
    y_a = jnp.dot(oa, woa_ref[...], preferred_element_type=F32)
    y_b = jnp.dot(ob_ref[...], wob_ref[...], preferred_element_type=F32)
    y_c = yc_ref[...]
    D = D_MODEL
    mix = (g_ref[:, 0:D].astype(F32) * y_a + g_ref[:, D:2 * D].astype(F32) * y_b
           + g_ref[:, 2 * D:3 * D].astype(F32) * y_c)
    o_ref[...] = h_ref[...] + jnp.dot(mix.astype(BF16), wo_ref[...], preferred_element_type=F32)


def _merge(h, oas, sas, ob, yc, gates, woa, wob, wo):
    m = h.shape[0]
    tm = MERGE_TM
    row = lambda w: pl.BlockSpec((tm, w), lambda i: (i, 0))
    full = lambda a: pl.BlockSpec(a.shape, lambda i: (0, 0))
    in_specs = ([row(D_MODEL)] + [row(256)] * 3 + [row(LANES)] * 3 + [row(1024), row(D_MODEL), row(3 * D_MODEL)]
                + [full(woa), full(wob), full(wo)])
    return pl.pallas_call(
        _merge_kernel,
        grid=(m // tm,),
        in_specs=in_specs,
        out_specs=row(D_MODEL),
        out_shape=jax.ShapeDtypeStruct((m, D_MODEL), F32),
        compiler_params=_cparams(("parallel",)),
        name="merge",
    )(h, *oas, *sas, ob, yc, gates, woa, wob, wo)


FFN_TM = 1024
FFN_TF = 256


def _ffn_kernel(h_ref, g_ref, wg_ref, wu_ref, wd_ref, o_ref, u_ref, acc_ref):
    j = pl.program_id(1)

    @pl.when(j == 0)
    def _():
        x = h_ref[...]
        ms = jnp.mean(x * x, axis=-1, keepdims=True)
        u_ref[...] = (x * lax.rsqrt(ms + RMS_EPS) * g_ref[...]).astype(BF16)
        acc_ref[...] = jnp.zeros_like(acc_ref)

    u = u_ref[...]
    a = jnp.dot(u, wg_ref[...], preferred_element_type=F32)
    b = jnp.dot(u, wu_ref[...], preferred_element_type=F32)
    t = (a * jax.nn.sigmoid(a) * b).astype(BF16)
    acc_ref[...] += jnp.dot(t, wd_ref[...], preferred_element_type=F32)

    @pl.when(j == pl.num_programs(1) - 1)
    def _():
        o_ref[...] = h_ref[...] + acc_ref[...]


def _ffn(h, g, wg, wu, wd):
    m, d = h.shape
    f = wg.shape[1]
    tm, tf = FFN_TM, FFN_TF
    return pl.pallas_call(
        _ffn_kernel,
        grid=(m // tm, f // tf),
        in_specs=[
            pl.BlockSpec((tm, d), lambda i, j: (i, 0)),
            pl.BlockSpec((1, d), lambda i, j: (0, 0)),
            pl.BlockSpec((d, tf), lambda i, j: (0, j)),
            pl.BlockSpec((d, tf), lambda i, j: (0, j)),
            pl.BlockSpec((tf, d), lambda i, j: (j, 0)),
        ],
        out_specs=pl.BlockSpec((tm, d), lambda i, j: (i, 0)),
        out_shape=jax.ShapeDtypeStruct((m, d), F32),
        scratch_shapes=[pltpu.VMEM((tm, d), BF16), pltpu.VMEM((tm, d), F32)],
        compiler_params=_cparams(("parallel", "arbitrary")),
        name="ffn",
    )(h, g, wg, wu, wd)


PLE_TM = 512


def _ple_kernel(h_ref, p_ref, g_ref, wg_ref, wp_ref, o_ref):
    x = h_ref[...]
    ms = jnp.mean(x * x, axis=-1, keepdims=True)
    e = (x * lax.rsqrt(ms + RMS_EPS) * g_ref[...]).astype(BF16)
    gate = jax.nn.sigmoid(jnp.dot(e, wg_ref[...], preferred_element_type=F32))
    proj = jnp.dot(p_ref[...].astype(BF16), wp_ref[...], preferred_element_type=F32)
    o_ref[...] = x + gate * proj


def _ple(h, p, g, wg, wp):
    m, d = h.shape
    tm = PLE_TM
    return pl.pallas_call(
        _ple_kernel,
        grid=(m // tm,),
        in_specs=[
            pl.BlockSpec((tm, d), lambda i: (i, 0)),
            pl.BlockSpec((tm, PLE_DIM), lambda i: (i, 0)),
            pl.BlockSpec((1, d), lambda i: (0, 0)),
            pl.BlockSpec((d, d), lambda i: (0, 0)),
            pl.BlockSpec((PLE_DIM, d), lambda i: (0, 0)),
        ],
        out_specs=pl.BlockSpec((tm, d), lambda i: (i, 0)),
        out_shape=jax.ShapeDtypeStruct((m, d), F32),
        compiler_params=_cparams(("parallel",)),
        name="ple",
    )(h, p, g, wg, wp)


def _rel_bucket(dist):
    max_exact = REL_BUCKETS // 2
    d = jnp.maximum(dist, 0)
    df = jnp.maximum(d, 1).astype(F32)
    large = max_exact + (jnp.log(df / max_exact) / math.log(REL_MAX_DIST / max_exact)
                         * (REL_BUCKETS - max_exact)).astype(I32)
    large = jnp.minimum(large, REL_BUCKETS - 1)
    return jnp.where(d < max_exact, d, large)


def _bias_tables(rel_bias):
    blk = DSWA_BLOCK
    dswa = []
    for g, (_, r) in enumerate(DSWA_PATTERNS):
        delta = np.arange(3 * blk - 1)[::-1] - (blk - 1)
        rev = rel_bias[_rel_bucket(jnp.asarray(delta * r, I32))][:, g * DSWA_HPG:(g + 1) * DSWA_HPG]
        rows = jnp.stack([rev[blk - 1 - i:blk - 1 - i + 2 * blk] for i in range(blk)], axis=0)
        dswa.append(jnp.transpose(rows, (2, 0, 1)))
    tq = DSA_TQ
    dist = np.arange(DSA_Z + tq - 1)[::-1] - (DSA_Z - 1) + DSA_Z0
    rev = rel_bias[_rel_bucket(jnp.asarray(dist, I32))][:, DSWA_HEADS:] * math.log2(math.e)
    cols = jnp.stack([rev[tq - 1 - i:tq - 1 - i + DSA_Z] for i in range(tq)], axis=2)
    bias_t = cols.reshape(DSA_Z, DSA_HEADS * tq)
    return dswa, bias_t


def _pad_cols(w, width):
    return jnp.pad(w, ((0, 0), (0, width - w.shape[1])))


def _layer_params(w_in, qn_a, kn_a, qn_c, kn_c, w_alpha2, b_alpha):
    offs = np.cumsum((0,) + IN_WIDTHS)
    parts = [w_in[:, offs[i]:offs[i + 1]] for i in range(len(IN_WIDTHS))]
    a_q, a_k, a_v, b_q, b_k, b_v, b_r, b_al, c_q, c_k, c_v, i_q, i_k, i_w = parts
    w1 = _pad_cols(jnp.concatenate([a_q, a_k, c_q, c_k], axis=1), W1).astype(BF16)
    w2 = _pad_cols(jnp.concatenate(
        [b_v, b_r, b_q, b_k, i_q, a_v, _pad_cols(b_al, LANES), _pad_cols(c_v, LANES),
         _pad_cols(i_k, LANES), _pad_cols(i_w, LANES)], axis=1), W2).astype(BF16)
    scale = HEAD_DIM ** -0.5
    gain1 = jnp.concatenate([jnp.tile(qn_a, DSWA_HEADS) * scale, jnp.tile(kn_a, DSWA_HEADS),
                             jnp.tile(qn_c, DSA_HEADS) * (scale * math.log2(math.e)), kn_c])
    gain1 = jnp.pad(gain1, (0, W1 - gain1.shape[0])).reshape(1, W1)
    wa = jnp.pad(w_alpha2, ((0, LANES - GLA_RANK), (0, 0))).astype(BF16)
    return w1, w2, gain1, wa, b_alpha.reshape(1, -1)


def kernel(x, p, rel_bias, norm_mix, w_in, qn_a, kn_a, qn_c, kn_c, w_alpha2, b_alpha, gla_norm, w_out_a, w_out_b, w_out_c, w_gate, b_gate, w_o, norm_ffn, w_ffn_gate, w_ffn_up, w_ffn_down, norm_ple, w_ple_gate, w_ple_proj):
    bsz, seq, d = x.shape
    depth = p.shape[0]
    m = bsz * seq
    dswa_bias, bias_t = _bias_tables(rel_bias)
    h = x.reshape(m, d)
    zeros_w2 = jnp.zeros((1, W2), F32)
    for i in range(depth):
        w1, w2, gain1, wa, ba = _layer_params(w_in[i], qn_a[i], kn_a[i], qn_c[i], kn_c[i], w_alpha2[i], b_alpha[i])
        gmix = norm_mix[i].reshape(1, d)
        p1 = _proj(h, gmix, w1, gain1, "qk")
        p2 = _proj(h, gmix, w2, zeros_w2, "plain")
        gates = _proj(h, gmix, w_gate[i].astype(BF16), b_gate[i].reshape(1, -1), "gate")
        oas, sas = [], []
        for g, (_, r) in enumerate(DSWA_PATTERNS):
            o, st = _dswa(p1, p2, dswa_bias[g], g, r, bsz, seq)
            oas.append(o)
            sas.append(st)
        ob = _gla(p2, wa, ba, gla_norm[i].reshape(1, -1), bsz, seq)
        cv = p2[:, P2_CV:P2_CV + HEAD_DIM].reshape(bsz, seq // DSA_CH, DSA_CH, HEAD_DIM)
        vt = jnp.transpose(cv, (0, 1, 3, 2))
        yc = _dsa(p1, p2, vt, bias_t, w_out_c[i].T.astype(BF16), bsz, seq)
        h = _merge(h, oas, sas, ob, yc, gates, w_out_a[i].astype(BF16), w_out_b[i].astype(BF16),
                   w_o[i].astype(BF16))
        h = _ffn(h, norm_ffn[i].reshape(1, d), w_ffn_gate[i].astype(BF16), w_ffn_up[i].astype(BF16),
                 w_ffn_down[i].astype(BF16))
        h = _ple(h, p[i].reshape(m, PLE_DIM), norm_ple[i].reshape(1, d), w_ple_gate[i].astype(BF16),
                 w_ple_proj[i].astype(BF16))
    return h.reshape(bsz, seq, d)
```

```python
import functools
import math

import numpy as np
import jax
import jax.numpy as jnp
from jax import lax
from jax.experimental import pallas as pl
from jax.experimental.pallas import tpu as pltpu

F32 = jnp.float32
BF16 = jnp.bfloat16
I32 = jnp.int32
I16 = jnp.int16

D_MODEL = 1024
HEAD_DIM = 64
RMS_EPS = 1e-6
DSWA_PATTERNS = ((128, 1), (512, 4), (2048, 16))
DSWA_HPG = 4
DSWA_HEADS = 12
DSWA_BLOCK = 128
GLA_HEADS = 4
GLA_DK = 128
GLA_DV = 256
GLA_RANK = 16
GLA_TAU = 16.0
GLA_CHUNK = 64
DSA_HEADS = 12
IDX_HEADS = 8
IDX_DIM = 64
IDX_TOPK = 256
REL_BUCKETS = 32
REL_MAX_DIST = 2048
D_FF = 2816
PLE_DIM = 256
IN_WIDTHS = (768, 768, 768, 512, 512, 1024, 1024, 16, 768, 64, 64, 512, 64, 8)

LANES = 128
MXU_N = 256
VMEM_LIMIT = 56 * 1024 * 1024

W1 = 2560
P1_AQ, P1_AK, P1_CQ, P1_CK = 0, 768, 1536, 2304
W2 = 5120
P2_BV, P2_BR, P2_BQ, P2_BK, P2_IQ, P2_AV = 0, 1024, 2048, 2560, 3072, 3584
P2_BAL, P2_CV, P2_IK, P2_IW = 4352, 4480, 4608, 4736

PROJ_TM = 1024
PROJ_TN = 512

DSA_TQ = 128
DSA_CH = 256
DSA_BIAS_CONST_FROM = 1512
DSA_Z0 = 1792
DSA_Z = DSA_Z0 + DSA_CH
INT_MIN = -(2 ** 31)
NEG_BIG = -1e30


def _cparams(sem):
    return pltpu.CompilerParams(dimension_semantics=sem, vmem_limit_bytes=VMEM_LIMIT)


def _proj_kernel(h_ref, g_ref, w_ref, e_ref, *rest, mode):
    if mode == "qk":
        bd_ref, o_ref, u_ref = rest
    else:
        o_ref, u_ref = rest

    @pl.when(pl.program_id(1) == 0)
    def _():
        x = h_ref[...]
        ms = jnp.mean(x * x, axis=-1, keepdims=True)
        u_ref[...] = (x * lax.rsqrt(ms + RMS_EPS) * g_ref[...]).astype(BF16)

    acc = jnp.dot(u_ref[...], w_ref[...], preferred_element_type=F32)
    if mode == "plain":
        out = acc
    elif mode == "gate":
        out = jax.nn.sigmoid(acc + e_ref[...])
    else:
        sq = acc * acc
        hi = sq.astype(BF16)
        lo = (sq - hi.astype(F32)).astype(BF16)
        ss = (jnp.dot(hi, bd_ref[...], preferred_element_type=F32)
              + jnp.dot(lo, bd_ref[...], preferred_element_type=F32))
        out = acc * lax.rsqrt(ss * (1.0 / HEAD_DIM) + RMS_EPS) * e_ref[...]
    o_ref[...] = out.astype(o_ref.dtype)


def _proj(h, g, w, e, mode):
    m, d = h.shape
    n = w.shape[1]
    tm, tn = PROJ_TM, PROJ_TN
    in_specs = [
        pl.BlockSpec((tm, d), lambda i, j: (i, 0)),
        pl.BlockSpec((1, d), lambda i, j: (0, 0)),
        pl.BlockSpec((d, tn), lambda i, j: (0, j)),
        pl.BlockSpec((1, tn), lambda i, j: (0, j)),
    ]
    args = [h, g, w, e]
    if mode == "qk":
        r = np.arange(tn) // HEAD_DIM
        bd = jnp.asarray((r[:, None] == r[None, :]).astype(np.float32), dtype=BF16)
        in_specs.append(pl.BlockSpec((tn, tn), lambda i, j: (0, 0)))
        args.append(bd)
    return pl.pallas_call(
        functools.partial(_proj_kernel, mode=mode),
        grid=(m // tm, n // tn),
        in_specs=in_specs,
        out_specs=pl.BlockSpec((tm, tn), lambda i, j: (i, j)),
        out_shape=jax.ShapeDtypeStruct((m, n), BF16),
        scratch_shapes=[pltpu.VMEM((tm, d), BF16)],
        compiler_params=_cparams(("parallel", "arbitrary")),
        name="proj_" + mode,
    )(*args)


def _dswa_kernel(q_ref, kp_ref, kc_ref, vp_ref, vc_ref, bias_ref, o_ref, st_ref):
    i = pl.program_id(2)
    blk = DSWA_BLOCK
    row = lax.broadcasted_iota(I32, (blk, 2 * blk), 0)
    col = lax.broadcasted_iota(I32, (blk, 2 * blk), 1)
    prev_off = jnp.where(i > 0, 0, 4 * blk)
    valid = ((col < blk) & (col >= row + prev_off)) | ((col >= blk) & ((col - blk) <= row))
    q = q_ref[...]
    k = jnp.concatenate([kp_ref[...], kc_ref[...]], axis=0)
    v = jnp.concatenate([vp_ref[...], vc_ref[...]], axis=0)
    lane = lax.broadcasted_iota(I32, (blk, LANES), 1)
    stats = jnp.zeros((blk, LANES), F32)
    for h in range(DSWA_HPG):
        sl = slice(h * HEAD_DIM, (h + 1) * HEAD_DIM)
        s = lax.dot_general(q[:, sl], k[:, sl], (((1,), (1,)), ((), ())), preferred_element_type=F32)
        s = jnp.where(valid, s + bias_ref[h], -jnp.inf)
        m = jnp.max(s, axis=-1, keepdims=True)
        p = jnp.exp(s - m)
        l = jnp.sum(p, axis=-1, keepdims=True)
        o = jnp.dot(p.astype(BF16), v[:, sl], preferred_element_type=F32) / l
        o_ref[:, sl] = o
        stats = jnp.where(lane == h, m, stats)
        stats = jnp.where(lane == DSWA_HPG + h, l, stats)
    st_ref[...] = stats


def _dswa(p1, p2, bias, g, r, bsz, seq):
    blk = DSWA_BLOCK
    L = seq // r
    nblk = L // blk
    gw = DSWA_HPG * HEAD_DIM
    if r == 1:
        p1v = p1.reshape(bsz, L, W1)
        p2v = p2.reshape(bsz, L, W2)
        qv, kv, vv = p1v, p1v, p2v
        q_col = lambda c: (P1_AQ // gw) + g
        k_col = lambda c: (P1_AK // gw) + g
        v_col = lambda c: (P2_AV // gw) + g
    else:
        def regroup(p, off):
            return p[:, off + g * gw:off + (g + 1) * gw].reshape(bsz, L, r * gw)
        qv, kv, vv = regroup(p1, P1_AQ), regroup(p1, P1_AK), regroup(p2, P2_AV)
        q_col = k_col = v_col = lambda c: c
    in_specs = [
        pl.BlockSpec((None, blk, gw), lambda b, c, i: (b, i, q_col(c))),
        pl.BlockSpec((None, blk, gw), lambda b, c, i: (b, jnp.maximum(i - 1, 0), k_col(c))),
        pl.BlockSpec((None, blk, gw), lambda b, c, i: (b, i, k_col(c))),
        pl.BlockSpec((None, blk, gw), lambda b, c, i: (b, jnp.maximum(i - 1, 0), v_col(c))),
        pl.BlockSpec((None, blk, gw), lambda b, c, i: (b, i, v_col(c))),
        pl.BlockSpec((DSWA_HPG, blk, 2 * blk), lambda b, c, i: (0, 0, 0)),
    ]
    out_specs = [
        pl.BlockSpec((None, blk, gw), lambda b, c, i: (b, i, c)),
        pl.BlockSpec((None, blk, LANES), lambda b, c, i: (b, i, c)),
    ]
    o, st = pl.pallas_call(
        _dswa_kernel,
        grid=(bsz, r, nblk),
        in_specs=in_specs,
        out_specs=out_specs,
        out_shape=[jax.ShapeDtypeStruct((bsz, L, r * gw), F32),
                   jax.ShapeDtypeStruct((bsz, L, r * LANES), F32)],
        compiler_params=_cparams(("parallel", "parallel", "arbitrary")),
        name="dswa_g%d" % g,
    )(qv, kv, kv, vv, vv, bias)
    return o.reshape(bsz * seq, gw), st.reshape(bsz * seq, LANES)


GLA_TC = 256


def _split3(x):
    a1 = x.astype(BF16)
    r1 = x - a1.astype(F32)
    a2 = r1.astype(BF16)
    a3 = (r1 - a2.astype(F32)).astype(BF16)
    return a1, a2, a3


def _gla_kernel(v_ref, r_ref, q_ref, k_ref, al_ref, wa_ref, ba_ref, gn_ref, o_ref, st_ref):
    C = GLA_CHUNK

    @pl.when(pl.program_id(1) == 0)
    def _():
        st_ref[...] = jnp.zeros_like(st_ref)

    row = lax.broadcasted_iota(I32, (C, C), 0)
    col = lax.broadcasted_iota(I32, (C, C), 1)
    tri = row >= col
    tri_bf = jnp.where(tri, 1.0, 0.0).astype(BF16)
    nt = (((1,), (1,)), ((), ()))
    tn = (((0,), (0,)), ((), ()))
    for c in range(GLA_TC // C):
        rs = slice(c * C, (c + 1) * C)
        z = jnp.dot(al_ref[rs, :], wa_ref[...], preferred_element_type=F32) + ba_ref[...]
        la = (jnp.minimum(z, 0.0) - jnp.log(1.0 + jnp.exp(-jnp.abs(z)))) * (1.0 / GLA_TAU)
        a1, a2, a3 = _split3(la)
        bcum = (jnp.dot(tri_bf, a1, preferred_element_type=F32)
                + jnp.dot(tri_bf, a2, preferred_element_type=F32)
                + jnp.dot(tri_bf, a3, preferred_element_type=F32))
        blast = bcum[C - 1:C, :]
        qf = q_ref[rs, :].astype(F32) * (GLA_DK ** -0.5)
        kf = k_ref[rs, :].astype(F32)
        q_in = (qf * jnp.exp(bcum)).astype(BF16)
        k_in = (kf * jnp.exp(-bcum)).astype(BF16)
        k_end = (kf * jnp.exp(blast - bcum)).astype(BF16)
        dec = jnp.exp(blast)
        for h in range(GLA_HEADS):
            ks = slice(h * GLA_DK, (h + 1) * GLA_DK)
            vs = slice(h * GLA_DV, (h + 1) * GLA_DV)
            vh = v_ref[rs, vs]
            att = lax.dot_general(q_in[:, ks], k_in[:, ks], nt, preferred_element_type=F32)
            att = jnp.where(tri, att, 0.0)
            o = jnp.dot(att.astype(BF16), vh, preferred_element_type=F32)
            st = st_ref[h]
            o = o + lax.dot_general(q_in[:, ks], st.astype(BF16), nt, preferred_element_type=F32)
            st_ref[h] = st * dec[:, ks] + lax.dot_general(vh, k_end[:, ks], tn, preferred_element_type=F32)
            ms = jnp.mean(o * o, axis=-1, keepdims=True)
            y = o * lax.rsqrt(ms + RMS_EPS) * gn_ref[...]
            rg = r_ref[rs, vs].astype(F32)
            y = y * (rg * jax.nn.sigmoid(rg))
            o_ref[rs, vs] = y.astype(o_ref.dtype)


def _gla(p2, wa, ba, gn, bsz, seq):
    tc = GLA_TC
    p2v = p2.reshape(bsz, seq, W2)
    in_specs = [
        pl.BlockSpec((None, tc, 1024), lambda b, t: (b, t, P2_BV // 1024)),
        pl.BlockSpec((None, tc, 1024), lambda b, t: (b, t, P2_BR // 1024)),
        pl.BlockSpec((None, tc, 512), lambda b, t: (b, t, P2_BQ // 512)),
        pl.BlockSpec((None, tc, 512), lambda b, t: (b, t, P2_BK // 512)),
        pl.BlockSpec((None, tc, LANES), lambda b, t: (b, t, P2_BAL // LANES)),
        pl.BlockSpec((LANES, 512), lambda b, t: (0, 0)),
        pl.BlockSpec((1, 512), lambda b, t: (0, 0)),
        pl.BlockSpec((1, GLA_DV), lambda b, t: (0, 0)),
    ]
    out = pl.pallas_call(
        _gla_kernel,
        grid=(bsz, seq // tc),
        in_specs=in_specs,
        out_specs=pl.BlockSpec((None, tc, 1024), lambda b, t: (b, t, 0)),
        out_shape=jax.ShapeDtypeStruct((bsz, seq, 1024), BF16),
        scratch_shapes=[pltpu.VMEM((GLA_HEADS, GLA_DV, GLA_DK), F32)],
        compiler_params=_cparams(("parallel", "arbitrary")),
        name="gla",
    )(p2v, p2v, p2v, p2v, p2v, wa, ba, gn)
    return out.reshape(bsz * seq, 1024)


def _dsa_kernel(q_ref, iq_ref, iw_ref, k_ref, ik_ref, vt_ref, bias_ref, wot_ref, o_ref,
                keys_ref, hi_ref, lo_ref, qall_ref, iqall_ref, acc_ref, st0_ref, st1_ref, pen0_ref, pen1_ref,
                *, topk):
    TQ, CH = DSA_TQ, DSA_CH
    CC = 2 * CH
    qi = pl.program_id(1)
    t0 = qi * TQ
    nch = qi // (CH // TQ) + 1
    npair = (nch + 1) // 2
    nt = (((1,), (1,)), ((), ()))

    for h in range(DSA_HEADS):
        qall_ref[h * TQ:(h + 1) * TQ, :] = q_ref[:, h * HEAD_DIM:(h + 1) * HEAD_DIM]
    for h in range(IDX_HEADS):
        iqall_ref[h * TQ:(h + 1) * TQ, :] = iq_ref[:, h * IDX_DIM:(h + 1) * IDX_DIM]
    idx_scale = (IDX_HEADS ** -0.5) * (IDX_DIM ** -0.5)
    wt = (iw_ref[...].astype(F32) * idx_scale).T

    qpos = t0 + lax.broadcasted_iota(I32, (1, TQ), 1)
    RB = 128
    krow1 = lax.broadcasted_iota(I32, (RB, TQ), 0)
    krow = lax.broadcasted_iota(I32, (CH, TQ), 0)
    krow2 = lax.broadcasted_iota(I32, (CC, TQ), 0)

    def score_body(cp, carry):
        for sub in range(CC // RB):
            s0 = pl.multiple_of(cp * CC + sub * RB, RB)
            ikc = ik_ref[pl.ds(s0, RB), :][:, :IDX_DIM]
            s = None
            for hp in range(IDX_HEADS // 2):
                x = lax.dot_general(ikc, iqall_ref[hp * 2 * TQ:(hp + 1) * 2 * TQ, :], nt,
                                    preferred_element_type=F32)
                t = (jnp.maximum(x[:, :TQ], 0.0) * wt[2 * hp:2 * hp + 1, :]
                     + jnp.maximum(x[:, TQ:], 0.0) * wt[2 * hp + 1:2 * hp + 2, :])
                s = t if s is None else s + t
            s = jnp.where(s == 0.0, 0.0, s)
            bits = pltpu.bitcast(s, I32)
            key = bits ^ ((bits >> 31) & 0x7FFFFFFF)
            key = jnp.where(s0 + krow1 <= qpos, key, INT_MIN)
            keys_ref[pl.ds(s0, RB), :] = key
            hi_ref[pl.ds(s0, RB), :] = (key >> 16).astype(I16)
        return carry

    lax.fori_loop(0, npair, score_body, 0)

    kvec = jnp.minimum(topk, qpos + 1)

    def count(pred_fn):
        def body(c, acc):
            s0 = pl.multiple_of(c * CC, CC)
            kk = keys_ref[pl.ds(s0, CC), :]
            hit = jnp.where(pred_fn(kk, s0), 1, 0).astype(I32)
            return acc + jnp.sum(hit.reshape(CC // 8, 8, TQ), axis=0)
        acc = lax.fori_loop(0, npair, body, jnp.zeros((8, TQ), I32))
        return jnp.sum(acc, axis=0, keepdims=True)

    PK = 16

    def count16(ref, pred_fn):
        def body(c, acc):
            s0 = pl.multiple_of(c * CC, CC)
            hit = jnp.where(pred_fn(ref[pl.ds(s0, CC), :]), jnp.bfloat16(1), jnp.bfloat16(0))
            parts = [hit[i * PK:(i + 1) * PK, :] for i in range(CC // PK)]
            while len(parts) > 1:
                parts = [parts[i] + parts[i + 1] for i in range(0, len(parts), 2)]
            return acc + parts[0].astype(F32)
        acc = lax.fori_loop(0, npair, body, jnp.zeros((PK, TQ), F32))
        return jnp.sum(acc, axis=0, keepdims=True).astype(I32)

    I16_MIN = -(2 ** 15)

    def kth_largest16(ref, kth):
        def ge(cand):
            return count16(ref, lambda x: x >= cand.astype(I16))
        t = jnp.where(ge(jnp.zeros((1, TQ), I32)) >= kth, 0, I16_MIN).astype(I32)

        def bit_body(b, t):
            cand = t | (jnp.int32(1) << (14 - b))
            return jnp.where(ge(cand) >= kth, cand, t)
        return lax.fori_loop(0, 15, bit_body, t)

    thr_hi = kth_largest16(hi_ref, kvec)
    need_lo = kvec - count16(hi_ref, lambda x: x > thr_hi.astype(I16))

    def lo_body(c, carry):
        s0 = pl.multiple_of(c * CC, CC)
        kk = keys_ref[pl.ds(s0, CC), :]
        lo = jnp.where((kk >> 16) == thr_hi, (kk & 0xFFFF) + I16_MIN, I16_MIN)
        lo_ref[pl.ds(s0, CC), :] = lo.astype(I16)
        return carry

    lax.fori_loop(0, npair, lo_body, 0)
    thr_lo = kth_largest16(lo_ref, need_lo)
    thr = (thr_hi << 16) | ((thr_lo - I16_MIN) & 0xFFFF)

    def count_ge(cand):
        return count(lambda kk, s0: kk >= cand)

    excess = count_ge(thr) - kvec
    has_excess = jnp.max(excess) > 0

    @pl.when(has_excess)
    def _():
        need = kvec - count(lambda kk, s0: kk > thr)

        def tie_lt(cut):
            return count(lambda kk, s0: (kk == thr) & (s0 + krow2 < cut))

        def cut_body(b, cut):
            cand = cut | (jnp.int32(1) << (12 - b))
            return jnp.where(tie_lt(cand) <= need, cand, cut)

        cut = lax.fori_loop(0, 13, cut_body, jnp.zeros((1, TQ), I32))

        def drop_body(c, carry):
            s0 = pl.multiple_of(c * CH, CH)
            kk = keys_ref[pl.ds(s0, CH), :]
            keys_ref[pl.ds(s0, CH), :] = jnp.where((kk == thr) & (s0 + krow >= cut), INT_MIN, kk)
            return carry

        lax.fori_loop(0, nch, drop_body, 0)

    acc_ref[...] = jnp.zeros_like(acc_ref)

    NP = DSA_HEADS // 2
    PW = 2 * TQ

    NB = CH // RB
    last_chunk = keys_ref.shape[0] // CH - 1

    def logits(c, st_ref, pen_ref):
        s0 = pl.multiple_of(c * CH, CH)
        zoff = pl.multiple_of(jnp.clip(s0 - t0 + DSA_Z0, 0, DSA_Z0), 8)
        for rb in range(NB):
            pen = jnp.where(keys_ref[pl.ds(s0 + rb * RB, RB), :] >= thr, 0.0, NEG_BIG)
            pen_ref[rb * RB:(rb + 1) * RB, :] = jnp.concatenate([pen, pen], axis=1)
        cms = []
        for hp in range(NP):
            ps = slice(hp * PW, (hp + 1) * PW)
            cm = None
            for rb in range(NB):
                kc = k_ref[pl.ds(s0 + rb * RB, RB), :][:, :HEAD_DIM]
                st = lax.dot_general(kc, qall_ref[ps, :], nt, preferred_element_type=F32)
                st = st + bias_ref[pl.ds(zoff + rb * RB, RB), ps] + pen_ref[rb * RB:(rb + 1) * RB, :]
                st_ref[rb * RB:(rb + 1) * RB, ps] = st
                tm = jnp.max(st, axis=0, keepdims=True)
                cm = tm if cm is None else jnp.maximum(cm, tm)
            cms.append(cm)
        return tuple(cms)

    def accumulate(c, st_ref, cms, ms, ls):
        vtc = vt_ref[c]
        new_ms, new_ls = [], []
        for hp in range(NP):
            ps = slice(hp * PW, (hp + 1) * PW)
            m_new = jnp.maximum(ms[hp], cms[hp])
            alpha = jnp.exp2(ms[hp] - m_new)
            l_new = ls[hp] * alpha
            pv = None
            for rb in range(NB):
                p = jnp.exp2(st_ref[rb * RB:(rb + 1) * RB, ps] - m_new)
                l_new = l_new + jnp.sum(p, axis=0, keepdims=True)
                d = jnp.dot(vtc[:, rb * RB:(rb + 1) * RB], p.astype(BF16), preferred_element_type=F32)
                pv = d if pv is None else pv + d
            new_ls.append(l_new)
            new_ms.append(m_new)
            acc_ref[:, ps] = acc_ref[:, ps] * alpha + pv
        return tuple(new_ms), tuple(new_ls)

    def att_body(cp, carry):
        ms, ls, cms0 = carry
        c0 = 2 * cp
        cms1 = logits(c0 + 1, st1_ref, pen1_ref)
        ms, ls = accumulate(c0, st0_ref, cms0, ms, ls)
        cms0 = logits(jnp.minimum(c0 + 2, last_chunk), st0_ref, pen0_ref)
        ms, ls = accumulate(c0 + 1, st1_ref, cms1, ms, ls)
        return ms, ls, cms0

    m0 = tuple(jnp.full((1, PW), NEG_BIG, F32) for _ in range(NP))
    l0 = tuple(jnp.zeros((1, PW), F32) for _ in range(NP))
    _, ls, _ = lax.fori_loop(0, npair, att_body, (m0, l0, logits(0, st0_ref, pen0_ref)))

    ot = jnp.concatenate(
        [(acc_ref[:, h * TQ:(h + 1) * TQ] / ls[h // 2][:, (h % 2) * TQ:(h % 2 + 1) * TQ]).astype(BF16)
         for h in range(DSA_HEADS)], axis=0)
    yt = jnp.dot(wot_ref[...], ot, preferred_element_type=F32)
    o_ref[...] = yt.T


def _dsa(p1, p2, vt, bias_t, wot, bsz, seq):
    tq = DSA_TQ
    p1v = p1.reshape(bsz, seq, W1)
    p2v = p2.reshape(bsz, seq, W2)
    nq = DSA_HEADS * tq
    in_specs = [
        pl.BlockSpec((None, tq, 768), lambda b, i: (b, i, P1_CQ // 768)),
        pl.BlockSpec((None, tq, 512), lambda b, i: (b, i, P2_IQ // 512)),
        pl.BlockSpec((None, tq, LANES), lambda b, i: (b, i, P2_IW // LANES)),
        pl.BlockSpec((None, seq, LANES), lambda b, i: (b, 0, P1_CK // LANES)),
        pl.BlockSpec((None, seq, LANES), lambda b, i: (b, 0, P2_IK // LANES)),
        pl.BlockSpec((None, seq // DSA_CH, HEAD_DIM, DSA_CH), lambda b, i: (b, 0, 0, 0)),
        pl.BlockSpec((DSA_Z, nq), lambda b, i: (0, 0)),
        pl.BlockSpec((D_MODEL, DSA_HEADS * HEAD_DIM), lambda b, i: (0, 0)),
    ]
    out = pl.pallas_call(
        functools.partial(_dsa_kernel, topk=min(IDX_TOPK, seq // 4)),
        grid=(bsz, seq // tq),
        in_specs=in_specs,
        out_specs=pl.BlockSpec((None, tq, D_MODEL), lambda b, i: (b, i, 0)),
        out_shape=jax.ShapeDtypeStruct((bsz, seq, D_MODEL), F32),
        scratch_shapes=[
            pltpu.VMEM((seq, tq), I32),
            pltpu.VMEM((seq, tq), I16),
            pltpu.VMEM((seq, tq), I16),
            pltpu.VMEM((nq, HEAD_DIM), BF16),
            pltpu.VMEM((IDX_HEADS * tq, IDX_DIM), BF16),
            pltpu.VMEM((HEAD_DIM, nq), F32),
            pltpu.VMEM((DSA_CH, nq), F32),
            pltpu.VMEM((DSA_CH, nq), F32),
            pltpu.VMEM((DSA_CH, 2 * tq), F32),
            pltpu.VMEM((DSA_CH, 2 * tq), F32),
        ],
        compiler_params=_cparams(("parallel", "arbitrary")),
        name="dsa",
    )(p1v, p2v, p2v, p1v, p2v, vt, bias_t, wot)
    return out.reshape(bsz * seq, D_MODEL)


MERGE_TM = 512


def _merge_kernel(h_ref, oa0_ref, oa1_ref, oa2_ref, sa0_ref, sa1_ref, sa2_ref, ob_ref, yc_ref, g_ref,
                  woa_ref, wob_ref, wo_ref, o_ref):
    H = DSWA_HPG
    sts = [sa0_ref[...], sa1_ref[...], sa2_ref[...]]
    oas = [oa0_ref, oa1_ref, oa2_ref]
    ms = [s[:, 0:H] for s in sts]
    ls = [s[:, H:2 * H] for s in sts]
    mmax = jnp.maximum(jnp.maximum(ms[0], ms[1]), ms[2])
    wts = [l * jnp.exp(m - mmax) for m, l in zip(ms, ls)]
    tot = wts[0] + wts[1] + wts[2]
    wts = [w / tot for w in wts]
    parts = []
    for h in range(H):
        sl = slice(h * HEAD_DIM, (h + 1) * HEAD_DIM)
        acc = wts[0][:, h:h + 1] * oas[0][:, sl]
        acc = acc + wts[1][:, h:h + 1] * oas[1][:, sl]
        acc = acc + wts[2][:, h:h + 1] * oas[2][:, sl]
        parts.append(acc)
    oa = jnp.concatenate(parts, axis=1).astype(BF16)
    y_a = jnp.dot(oa, woa_ref[...], preferred_element_type=F32)
    y_b = jnp.dot(ob_ref[...], wob_ref[...], preferred_element_type=F32)
    y_c = yc_ref[...]
    D = D_MODEL
    mix = (g_ref[:, 0:D].astype(F32) * y_a + g_ref[:, D:2 * D].astype(F32) * y_b
           + g_ref[:, 2 * D:3 * D].astype(F32) * y_c)
    o_ref[...] = h_ref[...] + jnp.dot(mix.astype(BF16), wo_ref[...], preferred_element_type=F32)


def _merge(h, oas, sas, ob, yc, gates, woa, wob, wo):
    m = h.shape[0]
    tm = MERGE_TM
    row = lambda w: pl.BlockSpec((tm, w), lambda i: (i, 0))
    full = lambda a: pl.BlockSpec(a.shape, lambda i: (0, 0))
    in_specs = ([row(D_MODEL)] + [row(256)] * 3 + [row(LANES)] * 3 + [row(1024), row(D_MODEL), row(3 * D_MODEL)]
                + [full(woa), full(wob), full(wo)])
    return pl.pallas_call(
        _merge_kernel,
        grid=(m // tm,),
        in_specs=in_specs,
        out_specs=row(D_MODEL),
        out_shape=jax.ShapeDtypeStruct((m, D_MODEL), F32),
        compiler_params=_cparams(("parallel",)),
        name="merge",
    )(h, *oas, *sas, ob, yc, gates, woa, wob, wo)


FFN_TM = 1024
FFN_TF = 256


def _ffn_kernel(h_ref, g_ref, wg_ref, wu_ref, wd_ref, o_ref, u_ref, acc_ref):
    j = pl.program_id(1)

    @pl.when(j == 0)
    def _():
        x = h_ref[...]
        ms = jnp.mean(x * x, axis=-1, keepdims=True)
        u_ref[...] = (x * lax.rsqrt(ms + RMS_EPS) * g_ref[...]).astype(BF16)
        acc_ref[...] = jnp.zeros_like(acc_ref)

    u = u_ref[...]
    a = jnp.dot(u, wg_ref[...], preferred_element_type=F32)
    b = jnp.dot(u, wu_ref[...], preferred_element_type=F32)
    t = (a * jax.nn.sigmoid(a) * b).astype(BF16)
    acc_ref[...] += jnp.dot(t, wd_ref[...], preferred_element_type=F32)

    @pl.when(j == pl.num_programs(1) - 1)
    def _():
        o_ref[...] = h_ref[...] + acc_ref[...]


def _ffn(h, g, wg, wu, wd):
    m, d = h.shape
    f = wg.shape[1]
    tm, tf = FFN_TM, FFN_TF
    return pl.pallas_call(
        _ffn_kernel,
        grid=(m // tm, f // tf),
        in_specs=[
            pl.BlockSpec((tm, d), lambda i, j: (i, 0)),
            pl.BlockSpec((1, d), lambda i, j: (0, 0)),
            pl.BlockSpec((d, tf), lambda i, j: (0, j)),
            pl.BlockSpec((d, tf), lambda i, j: (0, j)),
            pl.BlockSpec((tf, d), lambda i, j: (j, 0)),
        ],
        out_specs=pl.BlockSpec((tm, d), lambda i, j: (i, 0)),
        out_shape=jax.ShapeDtypeStruct((m, d), F32),
        scratch_shapes=[pltpu.VMEM((tm, d), BF16), pltpu.VMEM((tm, d), F32)],
        compiler_params=_cparams(("parallel", "arbitrary")),
        name="ffn",
    )(h, g, wg, wu, wd)


PLE_TM = 512


def _ple_kernel(h_ref, p_ref, g_ref, wg_ref, wp_ref, o_ref):
    x = h_ref[...]
    ms = jnp.mean(x * x, axis=-1, keepdims=True)
    e = (x * lax.rsqrt(ms + RMS_EPS) * g_ref[...]).astype(BF16)
    gate = jax.nn.sigmoid(jnp.dot(e, wg_ref[...], preferred_element_type=F32))
    proj = jnp.dot(p_ref[...].astype(BF16), wp_ref[...], preferred_element_type=F32)
    o_ref[...] = x + gate * proj


def _ple(h, p, g, wg, wp):
    m, d = h.shape
    tm = PLE_TM
    return pl.pallas_call(
        _ple_kernel,
        grid=(m // tm,),
        in_specs=[
            pl.BlockSpec((tm, d), lambda i: (i, 0)),
            pl.BlockSpec((tm, PLE_DIM), lambda i: (i, 0)),
            pl.BlockSpec((1, d), lambda i: (0, 0)),
            pl.BlockSpec((d, d), lambda i: (0, 0)),
            pl.BlockSpec((PLE_DIM, d), lambda i: (0, 0)),
        ],
        out_specs=pl.BlockSpec((tm, d), lambda i: (i, 0)),
        out_shape=jax.ShapeDtypeStruct((m, d), F32),
        compiler_params=_cparams(("parallel",)),
        name="ple",
    )(h, p, g, wg, wp)


def _rel_bucket(dist):
    max_exact = REL_BUCKETS // 2
    d = jnp.maximum(dist, 0)
    df = jnp.maximum(d, 1).astype(F32)
    large = max_exact + (jnp.log(df / max_exact) / math.log(REL_MAX_DIST / max_exact)
                         * (REL_BUCKETS - max_exact)).astype(I32)
    large = jnp.minimum(large, REL_BUCKETS - 1)
    return jnp.where(d < max_exact, d, large)


def _toeplitz(rev, n_rows, n_cols):
    nh = rev.shape[0]
    lw = rev.shape[1] + 1
    w = jnp.pad(rev, ((0, 0), (0, 1)))
    s = jnp.broadcast_to(w[:, None, :], (nh, n_rows, lw)).reshape(nh, n_rows * lw)
    s = s[:, :n_rows * (lw - 1)].reshape(nh, n_rows, lw - 1)
    return s[:, :, n_rows - 1:n_rows - 1 + n_cols]


def _bias_tables(rel_bias):
    blk = DSWA_BLOCK
    dswa = []
    for g, (_, r) in enumerate(DSWA_PATTERNS):
        delta = np.arange(3 * blk - 1)[::-1] - (blk - 1)
        rev = rel_bias[_rel_bucket(jnp.asarray(delta * r, I32))][:, g * DSWA_HPG:(g + 1) * DSWA_HPG]
        dswa.append(_toeplitz(rev.T, blk, 2 * blk))
    tq = DSA_TQ
    dist = np.arange(DSA_Z + tq - 1)[::-1] - (DSA_Z - 1) + DSA_Z0
    rev = rel_bias[_rel_bucket(jnp.asarray(dist, I32))][:, DSWA_HEADS:] * math.log2(math.e)
    bias_t = jnp.transpose(_toeplitz(rev.T, tq, DSA_Z), (2, 0, 1)).reshape(DSA_Z, DSA_HEADS * tq)
    return dswa, bias_t


def _pad_cols(w, width):
    return jnp.pad(w, ((0, 0), (0, width - w.shape[1])))


def _layer_params(w_in, qn_a, kn_a, qn_c, kn_c, w_alpha2, b_alpha):
    offs = np.cumsum((0,) + IN_WIDTHS)
    parts = [w_in[:, offs[i]:offs[i + 1]] for i in range(len(IN_WIDTHS))]
    a_q, a_k, a_v, b_q, b_k, b_v, b_r, b_al, c_q, c_k, c_v, i_q, i_k, i_w = parts
    w1 = _pad_cols(jnp.concatenate([a_q, a_k, c_q, c_k], axis=1), W1).astype(BF16)
    w2 = _pad_cols(jnp.concatenate(
        [b_v, b_r, b_q, b_k, i_q, a_v, _pad_cols(b_al, LANES), _pad_cols(c_v, LANES),
         _pad_cols(i_k, LANES), _pad_cols(i_w, LANES)], axis=1), W2).astype(BF16)
    scale = HEAD_DIM ** -0.5
    gain1 = jnp.concatenate([jnp.tile(qn_a, DSWA_HEADS) * scale, jnp.tile(kn_a, DSWA_HEADS),
                             jnp.tile(qn_c, DSA_HEADS) * (scale * math.log2(math.e)), kn_c])
    gain1 = jnp.pad(gain1, (0, W1 - gain1.shape[0])).reshape(1, W1)
    wa = jnp.pad(w_alpha2, ((0, LANES - GLA_RANK), (0, 0))).astype(BF16)
    return w1, w2, gain1, wa, b_alpha.reshape(1, -1)


def kernel(x, p, rel_bias, norm_mix, w_in, qn_a, kn_a, qn_c, kn_c, w_alpha2, b_alpha, gla_norm, w_out_a, w_out_b, w_out_c, w_gate, b_gate, w_o, norm_ffn, w_ffn_gate, w_ffn_up, w_ffn_down, norm_ple, w_ple_gate, w_ple_proj):
    bsz, seq, d = x.shape
    depth = p.shape[0]
    m = bsz * seq
    dswa_bias, bias_t = _bias_tables(rel_bias)
    h = x.reshape(m, d)
    zeros_w2 = jnp.zeros((1, W2), F32)
    for i in range(depth):
        w1, w2, gain1, wa, ba = _layer_params(w_in[i], qn_a[i], kn_a[i], qn_c[i], kn_c[i], w_alpha2[i], b_alpha[i])
        gmix = norm_mix[i].reshape(1, d)
        p1 = _proj(h, gmix, w1, gain1, "qk")
        p2 = _proj(h, gmix, w2, zeros_w2, "plain")
        gates = _proj(h, gmix, w_gate[i].astype(BF16), b_gate[i].reshape(1, -1), "gate")
        oas, sas = [], []
        for g, (_, r) in enumerate(DSWA_PATTERNS):
            o, st = _dswa(p1, p2, dswa_bias[g], g, r, bsz, seq)
            oas.append(o)
            sas.append(st)
        ob = _gla(p2, wa, ba, gla_norm[i].reshape(1, -1), bsz, seq)
        cv = p2[:, P2_CV:P2_CV + HEAD_DIM].reshape(bsz, seq // DSA_CH, DSA_CH, HEAD_DIM)
        vt = jnp.transpose(cv, (0, 1, 3, 2))
        yc = _dsa(p1, p2, vt, bias_t, w_out_c[i].T.astype(BF16), bsz, seq)
        h = _merge(h, oas, sas, ob, yc, gates, w_out_a[i].astype(BF16), w_out_b[i].astype(BF16),
                   w_o[i].astype(BF16))
        h = _ffn(h, norm_ffn[i].reshape(1, d), w_ffn_gate[i].astype(BF16), w_ffn_up[i].astype(BF16),
                 w_ffn_down[i].astype(BF16))
        h = _ple(h, p[i].reshape(m, PLE_DIM), norm_ple[i].reshape(1, d), w_ple_gate[i].astype(BF16),
                 w_ple_proj[i].astype(BF16))
    return h.reshape(bsz, seq, d)
```

```python
import functools
import math

import numpy as np
import jax
import jax.numpy as jnp
from jax import lax
from jax.experimental import pallas as pl
from jax.experimental.pallas import tpu as pltpu

F32 = jnp.float32
BF16 = jnp.bfloat16
I32 = jnp.int32
I16 = jnp.int16

D_MODEL = 1024
HEAD_DIM = 64
RMS_EPS = 1e-6
DSWA_PATTERNS = ((128, 1), (512, 4), (2048, 16))
DSWA_HPG = 4
DSWA_HEADS = 12
DSWA_BLOCK = 128
GLA_HEADS = 4
GLA_DK = 128
GLA_DV = 256
GLA_RANK = 16
GLA_TAU = 16.0
GLA_CHUNK = 64
DSA_HEADS = 12
IDX_HEADS = 8
IDX_DIM = 64
IDX_TOPK = 256
REL_BUCKETS = 32
REL_MAX_DIST = 2048
D_FF = 2816
PLE_DIM = 256
IN_WIDTHS = (768, 768, 768, 512, 512, 1024, 1024, 16, 768, 64, 64, 512, 64, 8)

LANES = 128
MXU_N = 256
VMEM_LIMIT = 56 * 1024 * 1024

W1 = 2560
P1_AQ, P1_AK, P1_CQ, P1_CK = 0, 768, 1536, 2304
W2 = 5120
P2_BV, P2_BR, P2_BQ, P2_BK, P2_IQ, P2_AV = 0, 1024, 2048, 2560, 3072, 3584
P2_BAL, P2_CV, P2_IK, P2_IW = 4352, 4480, 4608, 4736

PROJ_TM = 1024
PROJ_TN = 512

DSA_TQ = 128
DSA_CH = 256
DSA_BIAS_CONST_FROM = 1512
DSA_Z0 = 1792
DSA_Z = DSA_Z0 + DSA_CH
INT_MIN = -(2 ** 31)
NEG_BIG = -1e30


def _cparams(sem):
    return pltpu.CompilerParams(dimension_semantics=sem, vmem_limit_bytes=VMEM_LIMIT)


def _proj_kernel(h_ref, g_ref, w_ref, e_ref, *rest, mode):
    if mode == "qk":
        bd_ref, o_ref, u_ref = rest
    else:
        o_ref, u_ref = rest

    @pl.when(pl.program_id(1) == 0)
    def _():
        x = h_ref[...]
        ms = jnp.mean(x * x, axis=-1, keepdims=True)
        u_ref[...] = (x * lax.rsqrt(ms + RMS_EPS) * g_ref[...]).astype(BF16)

    acc = jnp.dot(u_ref[...], w_ref[...], preferred_element_type=F32)
    if mode == "plain":
        out = acc
    elif mode == "gate":
        out = jax.nn.sigmoid(acc + e_ref[...])
    else:
        sq = acc * acc
        hi = sq.astype(BF16)
        lo = (sq - hi.astype(F32)).astype(BF16)
        ss = (jnp.dot(hi, bd_ref[...], preferred_element_type=F32)
              + jnp.dot(lo, bd_ref[...], preferred_element_type=F32))
        out = acc * lax.rsqrt(ss * (1.0 / HEAD_DIM) + RMS_EPS) * e_ref[...]
    o_ref[...] = out.astype(o_ref.dtype)


def _proj(h, g, w, e, mode):
    m, d = h.shape
    n = w.shape[1]
    tm, tn = PROJ_TM, PROJ_TN
    in_specs = [
        pl.BlockSpec((tm, d), lambda i, j: (i, 0)),
        pl.BlockSpec((1, d), lambda i, j: (0, 0)),
        pl.BlockSpec((d, tn), lambda i, j: (0, j)),
        pl.BlockSpec((1, tn), lambda i, j: (0, j)),
    ]
    args = [h, g, w, e]
    if mode == "qk":
        r = np.arange(tn) // HEAD_DIM
        bd = jnp.asarray((r[:, None] == r[None, :]).astype(np.float32), dtype=BF16)
        in_specs.append(pl.BlockSpec((tn, tn), lambda i, j: (0, 0)))
        args.append(bd)
    return pl.pallas_call(
        functools.partial(_proj_kernel, mode=mode),
        grid=(m // tm, n // tn),
        in_specs=in_specs,
        out_specs=pl.BlockSpec((tm, tn), lambda i, j: (i, j)),
        out_shape=jax.ShapeDtypeStruct((m, n), BF16),
        scratch_shapes=[pltpu.VMEM((tm, d), BF16)],
        compiler_params=_cparams(("parallel", "arbitrary")),
        name="proj_" + mode,
    )(*args)


def _dswa_kernel(q_ref, kp_ref, kc_ref, vp_ref, vc_ref, bias_ref, o_ref, st_ref):
    i = pl.program_id(2)
    blk = DSWA_BLOCK
    row = lax.broadcasted_iota(I32, (blk, 2 * blk), 0)
    col = lax.broadcasted_iota(I32, (blk, 2 * blk), 1)
    prev_off = jnp.where(i > 0, 0, 4 * blk)
    valid = ((col < blk) & (col >= row + prev_off)) | ((col >= blk) & ((col - blk) <= row))
    q = q_ref[...]
    k = jnp.concatenate([kp_ref[...], kc_ref[...]], axis=0)
    v = jnp.concatenate([vp_ref[...], vc_ref[...]], axis=0)
    lane = lax.broadcasted_iota(I32, (blk, LANES), 1)
    stats = jnp.zeros((blk, LANES), F32)
    for h in range(DSWA_HPG):
        sl = slice(h * HEAD_DIM, (h + 1) * HEAD_DIM)
        s = lax.dot_general(q[:, sl], k[:, sl], (((1,), (1,)), ((), ())), preferred_element_type=F32)
        s = jnp.where(valid, s + bias_ref[h], -jnp.inf)
        m = jnp.max(s, axis=-1, keepdims=True)
        p = jnp.exp(s - m)
        l = jnp.sum(p, axis=-1, keepdims=True)
        o = jnp.dot(p.astype(BF16), v[:, sl], preferred_element_type=F32) / l
        o_ref[:, sl] = o
        stats = jnp.where(lane == h, m, stats)
        stats = jnp.where(lane == DSWA_HPG + h, l, stats)
    st_ref[...] = stats


def _dswa(p1, p2, bias, g, r, bsz, seq):
    blk = DSWA_BLOCK
    L = seq // r
    nblk = L // blk
    gw = DSWA_HPG * HEAD_DIM
    if r == 1:
        p1v = p1.reshape(bsz, L, W1)
        p2v = p2.reshape(bsz, L, W2)
        qv, kv, vv = p1v, p1v, p2v
        q_col = lambda c: (P1_AQ // gw) + g
        k_col = lambda c: (P1_AK // gw) + g
        v_col = lambda c: (P2_AV // gw) + g
    else:
        def regroup(p, off):
            return p[:, off + g * gw:off + (g + 1) * gw].reshape(bsz, L, r * gw)
        qv, kv, vv = regroup(p1, P1_AQ), regroup(p1, P1_AK), regroup(p2, P2_AV)
        q_col = k_col = v_col = lambda c: c
    in_specs = [
        pl.BlockSpec((None, blk, gw), lambda b, c, i: (b, i, q_col(c))),
        pl.BlockSpec((None, blk, gw), lambda b, c, i: (b, jnp.maximum(i - 1, 0), k_col(c))),
        pl.BlockSpec((None, blk, gw), lambda b, c, i: (b, i, k_col(c))),
        pl.BlockSpec((None, blk, gw), lambda b, c, i: (b, jnp.maximum(i - 1, 0), v_col(c))),
        pl.BlockSpec((None, blk, gw), lambda b, c, i: (b, i, v_col(c))),
        pl.BlockSpec((DSWA_HPG, blk, 2 * blk), lambda b, c, i: (0, 0, 0)),
    ]
    out_specs = [
        pl.BlockSpec((None, blk, gw), lambda b, c, i: (b, i, c)),
        pl.BlockSpec((None, blk, LANES), lambda b, c, i: (b, i, c)),
    ]
    o, st = pl.pallas_call(
        _dswa_kernel,
        grid=(bsz, r, nblk),
        in_specs=in_specs,
        out_specs=out_specs,
        out_shape=[jax.ShapeDtypeStruct((bsz, L, r * gw), F32),
                   jax.ShapeDtypeStruct((bsz, L, r * LANES), F32)],
        compiler_params=_cparams(("parallel", "parallel", "arbitrary")),
        name="dswa_g%d" % g,
    )(qv, kv, kv, vv, vv, bias)
    return o.reshape(bsz * seq, gw), st.reshape(bsz * seq, LANES)


GLA_TC = 256


def _split3(x):
    a1 = x.astype(BF16)
    r1 = x - a1.astype(F32)
    a2 = r1.astype(BF16)
    a3 = (r1 - a2.astype(F32)).astype(BF16)
    return a1, a2, a3


def _gla_kernel(v_ref, r_ref, q_ref, k_ref, al_ref, wa_ref, ba_ref, gn_ref, o_ref, st_ref):
    C = GLA_CHUNK

    @pl.when(pl.program_id(1) == 0)
    def _():
        st_ref[...] = jnp.zeros_like(st_ref)

    row = lax.broadcasted_iota(I32, (C, C), 0)
    col = lax.broadcasted_iota(I32, (C, C), 1)
    tri = row >= col
    tri_bf = jnp.where(tri, 1.0, 0.0).astype(BF16)
    nt = (((1,), (1,)), ((), ()))
    tn = (((0,), (0,)), ((), ()))
    for c in range(GLA_TC // C):
        rs = slice(c * C, (c + 1) * C)
        z = jnp.dot(al_ref[rs, :], wa_ref[...], preferred_element_type=F32) + ba_ref[...]
        la = (jnp.minimum(z, 0.0) - jnp.log(1.0 + jnp.exp(-jnp.abs(z)))) * (1.0 / GLA_TAU)
        a1, a2, a3 = _split3(la)
        bcum = (jnp.dot(tri_bf, a1, preferred_element_type=F32)
                + jnp.dot(tri_bf, a2, preferred_element_type=F32)
                + jnp.dot(tri_bf, a3, preferred_element_type=F32))
        blast = bcum[C - 1:C, :]
        qf = q_ref[rs, :].astype(F32) * (GLA_DK ** -0.5)
        kf = k_ref[rs, :].astype(F32)
        q_in = (qf * jnp.exp(bcum)).astype(BF16)
        k_in = (kf * jnp.exp(-bcum)).astype(BF16)
        k_end = (kf * jnp.exp(blast - bcum)).astype(BF16)
        dec = jnp.exp(blast)
        for h in range(GLA_HEADS):
            ks = slice(h * GLA_DK, (h + 1) * GLA_DK)
            vs = slice(h * GLA_DV, (h + 1) * GLA_DV)
            vh = v_ref[rs, vs]
            att = lax.dot_general(q_in[:, ks], k_in[:, ks], nt, preferred_element_type=F32)
            att = jnp.where(tri, att, 0.0)
            o = jnp.dot(att.astype(BF16), vh, preferred_element_type=F32)
            st = st_ref[h]
            o = o + lax.dot_general(q_in[:, ks], st.astype(BF16), nt, preferred_element_type=F32)
            st_ref[h] = st * dec[:, ks] + lax.dot_general(vh, k_end[:, ks], tn, preferred_element_type=F32)
            ms = jnp.mean(o * o, axis=-1, keepdims=True)
            y = o * lax.rsqrt(ms + RMS_EPS) * gn_ref[...]
            rg = r_ref[rs, vs].astype(F32)
            y = y * (rg * jax.nn.sigmoid(rg))
            o_ref[rs, vs] = y.astype(o_ref.dtype)


def _gla(p2, wa, ba, gn, bsz, seq):
    tc = GLA_TC
    p2v = p2.reshape(bsz, seq, W2)
    in_specs = [
        pl.BlockSpec((None, tc, 1024), lambda b, t: (b, t, P2_BV // 1024)),
        pl.BlockSpec((None, tc, 1024), lambda b, t: (b, t, P2_BR // 1024)),
        pl.BlockSpec((None, tc, 512), lambda b, t: (b, t, P2_BQ // 512)),
        pl.BlockSpec((None, tc, 512), lambda b, t: (b, t, P2_BK // 512)),
        pl.BlockSpec((None, tc, LANES), lambda b, t: (b, t, P2_BAL // LANES)),
        pl.BlockSpec((LANES, 512), lambda b, t: (0, 0)),
        pl.BlockSpec((1, 512), lambda b, t: (0, 0)),
        pl.BlockSpec((1, GLA_DV), lambda b, t: (0, 0)),
    ]
    out = pl.pallas_call(
        _gla_kernel,
        grid=(bsz, seq // tc),
        in_specs=in_specs,
        out_specs=pl.BlockSpec((None, tc, 1024), lambda b, t: (b, t, 0)),
        out_shape=jax.ShapeDtypeStruct((bsz, seq, 1024), BF16),
        scratch_shapes=[pltpu.VMEM((GLA_HEADS, GLA_DV, GLA_DK), F32)],
        compiler_params=_cparams(("parallel", "arbitrary")),
        name="gla",
    )(p2v, p2v, p2v, p2v, p2v, wa, ba, gn)
    return out.reshape(bsz * seq, 1024)


def _tree_sum(xs):
    xs = list(xs)
    while len(xs) > 1:
        xs = [xs[i] + xs[i + 1] for i in range(0, len(xs) - 1, 2)] + ([xs[-1]] if len(xs) % 2 else [])
    return xs[0]


def _dsa_kernel(q_ref, iq_ref, iw_ref, k_ref, ik_ref, vt_ref, bias_ref, wot_ref, o_ref,
                keys_ref, planes_ref, qaug_ref, iqall_ref, acc_ref, st0_ref, st1_ref, *, topk):
    TQ, CH = DSA_TQ, DSA_CH
    CC = 2 * CH
    qi = pl.program_id(1)
    t0 = qi * TQ
    nch = qi // (CH // TQ) + 1
    npair = (nch + 1) // 2
    nt = (((1,), (1,)), ((), ()))

    eye = (lax.broadcasted_iota(I32, (TQ, TQ), 0) == lax.broadcasted_iota(I32, (TQ, TQ), 1))
    eye = jnp.where(eye, 1.0, 0.0).astype(BF16)
    q_t = q_ref[...].astype(F32).T.astype(BF16)
    iq_t = iq_ref[...].astype(F32).T.astype(BF16)
    for h in range(DSA_HEADS):
        cs = slice(h * TQ, (h + 1) * TQ)
        qaug_ref[0:TQ, cs] = eye
        qaug_ref[TQ:TQ + HEAD_DIM, cs] = q_t[h * HEAD_DIM:(h + 1) * HEAD_DIM, :]
        qaug_ref[TQ + HEAD_DIM:, cs] = jnp.zeros((TQ - HEAD_DIM, TQ), BF16)
    for h in range(IDX_HEADS):
        cs = slice(h * TQ, (h + 1) * TQ)
        iqall_ref[0:IDX_DIM, cs] = iq_t[h * IDX_DIM:(h + 1) * IDX_DIM, :]
        iqall_ref[IDX_DIM:, cs] = jnp.zeros((LANES - IDX_DIM, TQ), BF16)
    idx_scale = (IDX_HEADS ** -0.5) * (IDX_DIM ** -0.5)
    wt = (iw_ref[...].astype(F32) * idx_scale).T

    qpos = t0 + lax.broadcasted_iota(I32, (1, TQ), 1)
    RB = 128
    krow1 = lax.broadcasted_iota(I32, (RB, TQ), 0)
    krow = lax.broadcasted_iota(I32, (CH, TQ), 0)
    krow2 = lax.broadcasted_iota(I32, (CC, TQ), 0)

    def score_body(cp, carry):
        for sub in range(CC // RB):
            s0 = pl.multiple_of(cp * CC + sub * RB, RB)
            ikc = ik_ref[pl.ds(s0, RB), :]
            s = None
            for hp in range(IDX_HEADS // 2):
                x = jnp.dot(ikc, iqall_ref[:, hp * 2 * TQ:(hp + 1) * 2 * TQ],
                            preferred_element_type=F32)
                t = (jnp.maximum(x[:, :TQ], 0.0) * wt[2 * hp:2 * hp + 1, :]
                     + jnp.maximum(x[:, TQ:], 0.0) * wt[2 * hp + 1:2 * hp + 2, :])
                s = t if s is None else s + t
            s = jnp.where(s == 0.0, 0.0, s)
            bits = pltpu.bitcast(s, I32)
            key = bits ^ ((bits >> 31) & 0x7FFFFFFF)
            key = jnp.where(s0 + krow1 <= qpos, key, INT_MIN)
            keys_ref[pl.ds(s0, RB), :] = key
        return carry

    lax.fori_loop(0, npair, score_body, 0)

    kvec = jnp.minimum(topk, qpos + 1)
    NPL = 32

    def planes_body(c, carry):
        s0 = pl.multiple_of(c * CH, CH)
        a = [keys_ref[pl.ds(s0 + 8 * j, 8), :] ^ INT_MIN for j in range(NPL)]
        j, m = 16, 0x0000FFFF
        while j:
            sh = jnp.full((8, TQ), j, I32)
            k = 0
            while k < NPL:
                t = (a[k] ^ lax.shift_right_logical(a[k + j], sh)) & m
                a[k] = a[k] ^ t
                a[k + j] = a[k + j] ^ lax.shift_left(t, sh)
                k = (k + j + 1) & ~j
            j >>= 1
            m = (m ^ (m << j)) & 0xFFFFFFFF
        for p in range(NPL):
            planes_ref[c, p] = a[p]
        return carry

    def empty_body(c, carry):
        for p in range(NPL):
            planes_ref[c, p] = jnp.zeros((8, TQ), I32)
        return carry

    nck = 2 * npair
    NCK = keys_ref.shape[0] // CH
    lax.fori_loop(0, nck, planes_body, 0)
    lax.fori_loop(nck, NCK, empty_body, 0)

    def plane_body(p, carry):
        thr_u, n_gt, alive = carry
        hit = [alive[c] & planes_ref[c, p] for c in range(NCK)]
        ones = jnp.sum(_tree_sum([lax.population_count(h) for h in hit]), axis=0, keepdims=True)
        take = (n_gt + ones) >= kvec
        thr_u = jnp.where(take, thr_u | (jnp.int32(1) << (31 - p)), thr_u)
        n_gt = jnp.where(take, n_gt, n_gt + ones)
        alive = tuple(jnp.where(take, hit[c], alive[c] ^ hit[c]) for c in range(NCK))
        return thr_u, n_gt, alive

    zero = jnp.zeros((1, TQ), I32)
    alive0 = tuple(jnp.full((8, TQ), -1, I32) for _ in range(NCK))
    thr_u, n_gt, alive = lax.fori_loop(0, NPL, plane_body, (zero, zero, alive0))
    n_eq = jnp.sum(_tree_sum([lax.population_count(a) for a in alive]), axis=0, keepdims=True)
    thr = thr_u ^ INT_MIN

    def count(pred_fn):
        def body(c, acc):
            s0 = pl.multiple_of(c * CC, CC)
            kk = keys_ref[pl.ds(s0, CC), :]
            hit = jnp.where(pred_fn(kk, s0), 1, 0).astype(I32)
            return acc + jnp.sum(hit.reshape(CC // 8, 8, TQ), axis=0)
        acc = lax.fori_loop(0, npair, body, jnp.zeros((8, TQ), I32))
        return jnp.sum(acc, axis=0, keepdims=True)

    excess = n_gt + n_eq - kvec
    has_excess = jnp.max(excess) > 0

    @pl.when(has_excess)
    def _():
        need = kvec - n_gt

        def tie_lt(cut):
            return count(lambda kk, s0: (kk == thr) & (s0 + krow2 < cut))

        def cut_body(b, cut):
            cand = cut | (jnp.int32(1) << (12 - b))
            return jnp.where(tie_lt(cand) <= need, cand, cut)

        cut = lax.fori_loop(0, 13, cut_body, jnp.zeros((1, TQ), I32))

        def drop_body(c, carry):
            s0 = pl.multiple_of(c * CH, CH)
            kk = keys_ref[pl.ds(s0, CH), :]
            keys_ref[pl.ds(s0, CH), :] = jnp.where((kk == thr) & (s0 + krow >= cut), INT_MIN, kk)
            return carry

        lax.fori_loop(0, nch, drop_body, 0)

    acc_ref[...] = jnp.zeros_like(acc_ref)

    NP = DSA_HEADS // 2
    PW = 2 * TQ

    NB = CH // RB
    last_chunk = keys_ref.shape[0] // CH - 1

    def logits(c, st_ref):
        s0 = pl.multiple_of(c * CH, CH)
        zoff = pl.multiple_of(jnp.clip(s0 - t0 + DSA_Z0, 0, DSA_Z0), 8)
        kaug = []
        for rb in range(NB):
            rs = pl.ds(s0 + rb * RB, RB)
            pen = jnp.where(keys_ref[rs, :] >= thr, 0.0, NEG_BIG).astype(BF16)
            kaug.append(jnp.concatenate([pen, k_ref[rs, :]], axis=1))
        cms = []
        for hp in range(NP):
            ps = slice(hp * PW, (hp + 1) * PW)
            cm = None
            for rb in range(NB):
                st = jnp.dot(kaug[rb], qaug_ref[:, ps], preferred_element_type=F32)
                st = st + bias_ref[pl.ds(zoff + rb * RB, RB), ps]
                st_ref[rb * RB:(rb + 1) * RB, ps] = st
                tm = jnp.max(st, axis=0, keepdims=True)
                cm = tm if cm is None else jnp.maximum(cm, tm)
            cms.append(cm)
        return tuple(cms)

    def accumulate(c, st_ref, cms, ms, ls):
        vtc = vt_ref[c]
        new_ms, new_ls = [], []
        for hp in range(NP):
            ps = slice(hp * PW, (hp + 1) * PW)
            m_new = jnp.maximum(ms[hp], cms[hp])
            alpha = jnp.exp2(ms[hp] - m_new)
            l_new = ls[hp] * alpha
            pv = None
            for rb in range(NB):
                p = jnp.exp2(st_ref[rb * RB:(rb + 1) * RB, ps] - m_new)
                l_new = l_new + jnp.sum(p, axis=0, keepdims=True)
                d = jnp.dot(vtc[:, rb * RB:(rb + 1) * RB], p.astype(BF16), preferred_element_type=F32)
                pv = d if pv is None else pv + d
            new_ls.append(l_new)
            new_ms.append(m_new)
            acc_ref[:, ps] = acc_ref[:, ps] * alpha + pv
        return tuple(new_ms), tuple(new_ls)

    def att_body(cp, carry):
        ms, ls, cms0 = carry
        c0 = 2 * cp
        cms1 = logits(c0 + 1, st1_ref)
        ms, ls = accumulate(c0, st0_ref, cms0, ms, ls)
        cms0 = logits(jnp.minimum(c0 + 2, last_chunk), st0_ref)
        ms, ls = accumulate(c0 + 1, st1_ref, cms1, ms, ls)
        return ms, ls, cms0

    m0 = tuple(jnp.full((1, PW), NEG_BIG, F32) for _ in range(NP))
    l0 = tuple(jnp.zeros((1, PW), F32) for _ in range(NP))
    _, ls, _ = lax.fori_loop(0, npair, att_body, (m0, l0, logits(0, st0_ref)))

    ot = jnp.concatenate(
        [(acc_ref[:, h * TQ:(h + 1) * TQ] / ls[h // 2][:, (h % 2) * TQ:(h % 2 + 1) * TQ]).astype(BF16)
         for h in range(DSA_HEADS)], axis=0)
    yt = jnp.dot(wot_ref[...], ot, preferred_element_type=F32)
    o_ref[...] = yt.T


def _dsa(p1, p2, vt, bias_t, wot, bsz, seq):
    tq = DSA_TQ
    p1v = p1.reshape(bsz, seq, W1)
    p2v = p2.reshape(bsz, seq, W2)
    nq = DSA_HEADS * tq
    in_specs = [
        pl.BlockSpec((None, tq, 768), lambda b, i: (b, i, P1_CQ // 768)),
        pl.BlockSpec((None, tq, 512), lambda b, i: (b, i, P2_IQ // 512)),
        pl.BlockSpec((None, tq, LANES), lambda b, i: (b, i, P2_IW // LANES)),
        pl.BlockSpec((None, seq, LANES), lambda b, i: (b, 0, P1_CK // LANES)),
        pl.BlockSpec((None, seq, LANES), lambda b, i: (b, 0, P2_IK // LANES)),
        pl.BlockSpec((None, seq // DSA_CH, HEAD_DIM, DSA_CH), lambda b, i: (b, 0, 0, 0)),
        pl.BlockSpec((DSA_Z, nq), lambda b, i: (0, 0)),
        pl.BlockSpec((D_MODEL, DSA_HEADS * HEAD_DIM), lambda b, i: (0, 0)),
    ]
    out = pl.pallas_call(
        functools.partial(_dsa_kernel, topk=min(IDX_TOPK, seq // 4)),
        grid=(bsz, seq // tq),
        in_specs=in_specs,
        out_specs=pl.BlockSpec((None, tq, D_MODEL), lambda b, i: (b, i, 0)),
        out_shape=jax.ShapeDtypeStruct((bsz, seq, D_MODEL), F32),
        scratch_shapes=[
            pltpu.VMEM((seq, tq), I32),
            pltpu.VMEM((seq // DSA_CH, 32, 8, tq), I32),
            pltpu.VMEM((2 * tq, nq), BF16),
            pltpu.VMEM((LANES, IDX_HEADS * tq), BF16),
            pltpu.VMEM((HEAD_DIM, nq), F32),
            pltpu.VMEM((DSA_CH, nq), F32),
            pltpu.VMEM((DSA_CH, nq), F32),
        ],
        compiler_params=_cparams(("parallel", "arbitrary")),
        name="dsa",
    )(p1v, p2v, p2v, p1v, p2v, vt, bias_t, wot)
    return out.reshape(bsz * seq, D_MODEL)


MERGE_TM = 512


def _merge_kernel(h_ref, oa0_ref, oa1_ref, oa2_ref, sa0_ref, sa1_ref, sa2_ref, ob_ref, yc_ref, g_ref,
                  woa_ref, wob_ref, wo_ref, o_ref):
    H = DSWA_HPG
    sts = [sa0_ref[...], sa1_ref[...], sa2_ref[...]]
    oas = [oa0_ref, oa1_ref, oa2_ref]
    ms = [s[:, 0:H] for s in sts]
    ls = [s[:, H:2 * H] for s in sts]
    mmax = jnp.maximum(jnp.maximum(ms[0], ms[1]), ms[2])
    wts = [l * jnp.exp(m - mmax) for m, l in zip(ms, ls)]
    tot = wts[0] + wts[1] + wts[2]
    wts = [w / tot for w in wts]
    parts = []
    for h in range(H):
        sl = slice(h * HEAD_DIM, (h + 1) * HEAD_DIM)
        acc = wts[0][:, h:h + 1] * oas[0][:, sl]
        acc = acc + wts[1][:, h:h + 1] * oas[1][:, sl]
        acc = acc + wts[2][:, h:h + 1] * oas[2][:, sl]
        parts.append(acc)
    oa = jnp.concatenate(parts, axis=1).astype(BF16)
    y_a = jnp.dot(oa, woa_ref[...], preferred_element_type=F32)
    y_b = jnp.dot(ob_ref[...], wob_ref[...], preferred_element_type=F32)
    y_c = yc_ref[...]
    D = D_MODEL
    mix = (g_ref[:, 0:D].astype(F32) * y_a + g_ref[:, D:2 * D].astype(F32) * y_b
           + g_ref[:, 2 * D:3 * D].astype(F32) * y_c)
    o_ref[...] = h_ref[...] + jnp.dot(mix.astype(BF16), wo_ref[...], preferred_element_type=F32)


def _merge(h, oas, sas, ob, yc, gates, woa, wob, wo):
    m = h.shape[0]
    tm = MERGE_TM
    row = lambda w: pl.BlockSpec((tm, w), lambda i: (i, 0))
    full = lambda a: pl.BlockSpec(a.shape, lambda i: (0, 0))
    in_specs = ([row(D_MODEL)] + [row(256)] * 3 + [row(LANES)] * 3 + [row(1024), row(D_MODEL), row(3 * D_MODEL)]
                + [full(woa), full(wob), full(wo)])
    return pl.pallas_call(
        _merge_kernel,
        grid=(m // tm,),
        in_specs=in_specs,
        out_specs=row(D_MODEL),
        out_shape=jax.ShapeDtypeStruct((m, D_MODEL), F32),
        compiler_params=_cparams(("parallel",)),
        name="merge",
    )(h, *oas, *sas, ob, yc, gates, woa, wob, wo)


FFN_TM = 1024
FFN_TF = 256


def _ffn_kernel(h_ref, g_ref, wg_ref, wu_ref, wd_ref, o_ref, u_ref, acc_ref):
    j = pl.program_id(1)

    @pl.when(j == 0)
    def _():
        x = h_ref[...]
        ms = jnp.mean(x * x, axis=-1, keepdims=True)
        u_ref[...] = (x * lax.rsqrt(ms + RMS_EPS) * g_ref[...]).astype(BF16)
        acc_ref[...] = jnp.zeros_like(acc_ref)

    u = u_ref[...]
    a = jnp.dot(u, wg_ref[...], preferred_element_type=F32)
    b = jnp.dot(u, wu_ref[...], preferred_element_type=F32)
    t = (a * jax.nn.sigmoid(a) * b).astype(BF16)
    acc_ref[...] += jnp.dot(t, wd_ref[...], preferred_element_type=F32)

    @pl.when(j == pl.num_programs(1) - 1)
    def _():
        o_ref[...] = h_ref[...] + acc_ref[...]


def _ffn(h, g, wg, wu, wd):
    m, d = h.shape
    f = wg.shape[1]
    tm, tf = FFN_TM, FFN_TF
    return pl.pallas_call(
        _ffn_kernel,
        grid=(m // tm, f // tf),
        in_specs=[
            pl.BlockSpec((tm, d), lambda i, j: (i, 0)),
            pl.BlockSpec((1, d), lambda i, j: (0, 0)),
            pl.BlockSpec((d, tf), lambda i, j: (0, j)),
            pl.BlockSpec((d, tf), lambda i, j: (0, j)),
            pl.BlockSpec((tf, d), lambda i, j: (j, 0)),
        ],
        out_specs=pl.BlockSpec((tm, d), lambda i, j: (i, 0)),
        out_shape=jax.ShapeDtypeStruct((m, d), F32),
        scratch_shapes=[pltpu.VMEM((tm, d), BF16), pltpu.VMEM((tm, d), F32)],
        compiler_params=_cparams(("parallel", "arbitrary")),
        name="ffn",
    )(h, g, wg, wu, wd)


PLE_TM = 512


def _ple_kernel(h_ref, p_ref, g_ref, wg_ref, wp_ref, o_ref):
    x = h_ref[...]
    ms = jnp.mean(x * x, axis=-1, keepdims=True)
    e = (x * lax.rsqrt(ms + RMS_EPS) * g_ref[...]).astype(BF16)
    gate = jax.nn.sigmoid(jnp.dot(e, wg_ref[...], preferred_element_type=F32))
    proj = jnp.dot(p_ref[...].astype(BF16), wp_ref[...], preferred_element_type=F32)
    o_ref[...] = x + gate * proj


def _ple(h, p, g, wg, wp):
    m, d = h.shape
    tm = PLE_TM
    return pl.pallas_call(
        _ple_kernel,
        grid=(m // tm,),
        in_specs=[
            pl.BlockSpec((tm, d), lambda i: (i, 0)),
            pl.BlockSpec((tm, PLE_DIM), lambda i: (i, 0)),
            pl.BlockSpec((1, d), lambda i: (0, 0)),
            pl.BlockSpec((d, d), lambda i: (0, 0)),
            pl.BlockSpec((PLE_DIM, d), lambda i: (0, 0)),
        ],
        out_specs=pl.BlockSpec((tm, d), lambda i: (i, 0)),
        out_shape=jax.ShapeDtypeStruct((m, d), F32),
        compiler_params=_cparams(("parallel",)),
        name="ple",
    )(h, p, g, wg, wp)


def _rel_bucket(dist):
    max_exact = REL_BUCKETS // 2
    d = jnp.maximum(dist, 0)
    df = jnp.maximum(d, 1).astype(F32)
    large = max_exact + (jnp.log(df / max_exact) / math.log(REL_MAX_DIST / max_exact)
                         * (REL_BUCKETS - max_exact)).astype(I32)
    large = jnp.minimum(large, REL_BUCKETS - 1)
    return jnp.where(d < max_exact, d, large)


def _toeplitz(rev, n_rows, n_cols):
    nh = rev.shape[0]
    lw = rev.shape[1] + 1
    w = jnp.pad(rev, ((0, 0), (0, 1)))
    s = jnp.broadcast_to(w[:, None, :], (nh, n_rows, lw)).reshape(nh, n_rows * lw)
    s = s[:, :n_rows * (lw - 1)].reshape(nh, n_rows, lw - 1)
    return s[:, :, n_rows - 1:n_rows - 1 + n_cols]


def _bias_tables(rel_bias):
    blk = DSWA_BLOCK
    dswa = []
    for g, (_, r) in enumerate(DSWA_PATTERNS):
        delta = np.arange(3 * blk - 1)[::-1] - (blk - 1)
        rev = rel_bias[_rel_bucket(jnp.asarray(delta * r, I32))][:, g * DSWA_HPG:(g + 1) * DSWA_HPG]
        dswa.append(_toeplitz(rev.T, blk, 2 * blk))
    tq = DSA_TQ
    dist = np.arange(DSA_Z + tq - 1)[::-1] - (DSA_Z - 1) + DSA_Z0
    rev = rel_bias[_rel_bucket(jnp.asarray(dist, I32))][:, DSWA_HEADS:] * math.log2(math.e)
    bias_t = jnp.transpose(_toeplitz(rev.T, tq, DSA_Z), (2, 0, 1)).reshape(DSA_Z, DSA_HEADS * tq)
    return dswa, bias_t


def _pad_cols(w, width):
    return jnp.pad(w, ((0, 0), (0, width - w.shape[1])))


def _layer_params(w_in, qn_a, kn_a, qn_c, kn_c, w_alpha2, b_alpha):
    offs = np.cumsum((0,) + IN_WIDTHS)
    parts = [w_in[:, offs[i]:offs[i + 1]] for i in range(len(IN_WIDTHS))]
    a_q, a_k, a_v, b_q, b_k, b_v, b_r, b_al, c_q, c_k, c_v, i_q, i_k, i_w = parts
    w1 = _pad_cols(jnp.concatenate([a_q, a_k, c_q, c_k], axis=1), W1).astype(BF16)
    w2 = _pad_cols(jnp.concatenate(
        [b_v, b_r, b_q, b_k, i_q, a_v, _pad_cols(b_al, LANES), _pad_cols(c_v, LANES),
         _pad_cols(i_k, LANES), _pad_cols(i_w, LANES)], axis=1), W2).astype(BF16)
    scale = HEAD_DIM ** -0.5
    gain1 = jnp.concatenate([jnp.tile(qn_a, DSWA_HEADS) * scale, jnp.tile(kn_a, DSWA_HEADS),
                             jnp.tile(qn_c, DSA_HEADS) * (scale * math.log2(math.e)), kn_c])
    gain1 = jnp.pad(gain1, (0, W1 - gain1.shape[0])).reshape(1, W1)
    wa = jnp.pad(w_alpha2, ((0, LANES - GLA_RANK), (0, 0))).astype(BF16)
    return w1, w2, gain1, wa, b_alpha.reshape(1, -1)


def kernel(x, p, rel_bias, norm_mix, w_in, qn_a, kn_a, qn_c, kn_c, w_alpha2, b_alpha, gla_norm, w_out_a, w_out_b, w_out_c, w_gate, b_gate, w_o, norm_ffn, w_ffn_gate, w_ffn_up, w_ffn_down, norm_ple, w_ple_gate, w_ple_proj):
    bsz, seq, d = x.shape
    depth = p.shape[0]
    m = bsz * seq
    dswa_bias, bias_t = _bias_tables(rel_bias)
    h = x.reshape(m, d)
    zeros_w2 = jnp.zeros((1, W2), F32)
    for i in range(depth):
        w1, w2, gain1, wa, ba = _layer_params(w_in[i], qn_a[i], kn_a[i], qn_c[i], kn_c[i], w_alpha2[i], b_alpha[i])
        gmix = norm_mix[i].reshape(1, d)
        p1 = _proj(h, gmix, w1, gain1, "qk")
        p2 = _proj(h, gmix, w2, zeros_w2, "plain")
        gates = _proj(h, gmix, w_gate[i].astype(BF16), b_gate[i].reshape(1, -1), "gate")
        oas, sas = [], []
        for g, (_, r) in enumerate(DSWA_PATTERNS):
            o, st = _dswa(p1, p2, dswa_bias[g], g, r, bsz, seq)
            oas.append(o)
            sas.append(st)
        ob = _gla(p2, wa, ba, gla_norm[i].reshape(1, -1), bsz, seq)
        cv = p2[:, P2_CV:P2_CV + HEAD_DIM].reshape(bsz, seq // DSA_CH, DSA_CH, HEAD_DIM)
        vt = jnp.transpose(cv, (0, 1, 3, 2))
        yc = _dsa(p1, p2, vt, bias_t, w_out_c[i].T.astype(BF16), bsz, seq)
        h = _merge(h, oas, sas, ob, yc, gates, w_out_a[i].astype(BF16), w_out_b[i].astype(BF16),
                   w_o[i].astype(BF16))
        h = _ffn(h, norm_ffn[i].reshape(1, d), w_ffn_gate[i].astype(BF16), w_ffn_up[i].astype(BF16),
                 w_ffn_down[i].astype(BF16))
        h = _ple(h, p[i].reshape(m, PLE_DIM), norm_ple[i].reshape(1, d), w_ple_gate[i].astype(BF16),
                 w_ple_proj[i].astype(BF16))
    return h.reshape(bsz, seq, d)
```

```python
import functools
import math

import numpy as np
import jax
import jax.numpy as jnp
from jax import lax
from jax.experimental import pallas as pl
from jax.experimental.pallas import tpu as pltpu

F32 = jnp.float32
BF16 = jnp.bfloat16
I32 = jnp.int32
I16 = jnp.int16

D_MODEL = 1024
HEAD_DIM = 64
RMS_EPS = 1e-6
DSWA_PATTERNS = ((128, 1), (512, 4), (2048, 16))
DSWA_HPG = 4
DSWA_HEADS = 12
DSWA_BLOCK = 128
GLA_HEADS = 4
GLA_DK = 128
GLA_DV = 256
GLA_RANK = 16
GLA_TAU = 16.0
GLA_CHUNK = 64
DSA_HEADS = 12
IDX_HEADS = 8
IDX_DIM = 64
IDX_TOPK = 256
REL_BUCKETS = 32
REL_MAX_DIST = 2048
D_FF = 2816
PLE_DIM = 256
IN_WIDTHS = (768, 768, 768, 512, 512, 1024, 1024, 16, 768, 64, 64, 512, 64, 8)

LANES = 128
MXU_N = 256
VMEM_LIMIT = 56 * 1024 * 1024

W1 = 2560
P1_AQ, P1_AK, P1_CQ, P1_CK = 0, 768, 1536, 2304
W2 = 5120
P2_BV, P2_BR, P2_BQ, P2_BK, P2_IQ, P2_AV = 0, 1024, 2048, 2560, 3072, 3584
P2_BAL, P2_CV, P2_IK, P2_IW = 4352, 4480, 4608, 4736

PROJ_TM = 1024
PROJ_TN = 512

DSWA_QB = 4
DSA_TQ = 128
DSA_CH = 256
DSA_VR = 80
DSA_BIAS_CONST_FROM = 1512
DSA_Z0 = 1792
DSA_Z = DSA_Z0 + DSA_CH
INT_MIN = -(2 ** 31)
NEG_BIG = -1e30


def _cparams(sem, flags=None):
    return pltpu.CompilerParams(dimension_semantics=sem, vmem_limit_bytes=VMEM_LIMIT, flags=flags)


def _proj_kernel(h_ref, g_ref, w_ref, e_ref, *rest, mode):
    if mode == "qk":
        bd_ref, o_ref, u_ref = rest
    else:
        o_ref, u_ref = rest

    @pl.when(pl.program_id(1) == 0)
    def _():
        x = h_ref[...]
        ms = jnp.mean(x * x, axis=-1, keepdims=True)
        u_ref[...] = (x * lax.rsqrt(ms + RMS_EPS) * g_ref[...]).astype(BF16)

    acc = jnp.dot(u_ref[...], w_ref[...], preferred_element_type=F32)
    if mode == "plain":
        out = acc
    elif mode == "gate":
        out = jax.nn.sigmoid(acc + e_ref[...])
    else:
        sq = acc * acc
        hi = sq.astype(BF16)
        lo = (sq - hi.astype(F32)).astype(BF16)
        ss = (jnp.dot(hi, bd_ref[...], preferred_element_type=F32)
              + jnp.dot(lo, bd_ref[...], preferred_element_type=F32))
        out = acc * lax.rsqrt(ss * (1.0 / HEAD_DIM) + RMS_EPS) * e_ref[...]
    o_ref[...] = out.astype(o_ref.dtype)


def _proj(h, g, w, e, mode):
    m, d = h.shape
    n = w.shape[1]
    tm, tn = PROJ_TM, PROJ_TN
    in_specs = [
        pl.BlockSpec((tm, d), lambda i, j: (i, 0)),
        pl.BlockSpec((1, d), lambda i, j: (0, 0)),
        pl.BlockSpec((d, tn), lambda i, j: (0, j)),
        pl.BlockSpec((1, tn), lambda i, j: (0, j)),
    ]
    args = [h, g, w, e]
    if mode == "qk":
        r = np.arange(tn) // HEAD_DIM
        bd = jnp.asarray((r[:, None] == r[None, :]).astype(np.float32), dtype=BF16)
        in_specs.append(pl.BlockSpec((tn, tn), lambda i, j: (0, 0)))
        args.append(bd)
    return pl.pallas_call(
        functools.partial(_proj_kernel, mode=mode),
        grid=(m // tm, n // tn),
        in_specs=in_specs,
        out_specs=pl.BlockSpec((tm, tn), lambda i, j: (i, j)),
        out_shape=jax.ShapeDtypeStruct((m, n), BF16),
        scratch_shapes=[pltpu.VMEM((tm, d), BF16)],
        compiler_params=_cparams(("parallel", "arbitrary")),
        name="proj_" + mode,
    )(*args)


def _dswa_kernel(q_ref, kp_ref, kc_ref, vp_ref, vc_ref, bias_ref, o_ref, st_ref, *, qb):
    i = pl.program_id(2)
    blk = DSWA_BLOCK
    row = lax.broadcasted_iota(I32, (blk, 2 * blk), 0)
    col = lax.broadcasted_iota(I32, (blk, 2 * blk), 1)
    cur_ok = (col >= blk) & ((col - blk) <= row)
    valid_inner = ((col < blk) & (col >= row)) | cur_ok
    prev_off = jnp.where(i > 0, 0, 4 * blk)
    valid_first = ((col < blk) & (col >= row + prev_off)) | cur_ok
    lane = lax.broadcasted_iota(I32, (blk, LANES), 1)
    for j in range(qb):
        rs = slice(j * blk, (j + 1) * blk)
        ps = slice((j - 1) * blk, j * blk)
        valid = valid_first if j == 0 else valid_inner
        q = q_ref[rs, :]
        k = jnp.concatenate([kp_ref[...] if j == 0 else kc_ref[ps, :], kc_ref[rs, :]], axis=0)
        v = jnp.concatenate([vp_ref[...] if j == 0 else vc_ref[ps, :], vc_ref[rs, :]], axis=0)
        stats = jnp.zeros((blk, LANES), F32)
        for h in range(DSWA_HPG):
            sl = slice(h * HEAD_DIM, (h + 1) * HEAD_DIM)
            s = lax.dot_general(q[:, sl], k[:, sl], (((1,), (1,)), ((), ())), preferred_element_type=F32)
            s = jnp.where(valid, s + bias_ref[h], -jnp.inf)
            m = jnp.max(s, axis=-1, keepdims=True)
            p = jnp.exp(s - m)
            l = jnp.sum(p, axis=-1, keepdims=True)
            o = jnp.dot(p.astype(BF16), v[:, sl], preferred_element_type=F32) / l
            o_ref[rs, sl] = o
            stats = jnp.where(lane == h, m, stats)
            stats = jnp.where(lane == DSWA_HPG + h, l, stats)
        st_ref[rs, :] = stats


def _dswa(p1, p2, bias, g, r, bsz, seq):
    blk = DSWA_BLOCK
    L = seq // r
    nblk = L // blk
    gw = DSWA_HPG * HEAD_DIM
    if r == 1:
        p1v = p1.reshape(bsz, L, W1)
        p2v = p2.reshape(bsz, L, W2)
        qv, kv, vv = p1v, p1v, p2v
        q_col = lambda c: (P1_AQ // gw) + g
        k_col = lambda c: (P1_AK // gw) + g
        v_col = lambda c: (P2_AV // gw) + g
    else:
        def regroup(p, off):
            return p[:, off + g * gw:off + (g + 1) * gw].reshape(bsz, L, r * gw)
        qv, kv, vv = regroup(p1, P1_AQ), regroup(p1, P1_AK), regroup(p2, P2_AV)
        q_col = k_col = v_col = lambda c: c
    qb = min(DSWA_QB, nblk)
    prev = lambda i: jnp.maximum(i * qb - 1, 0)
    in_specs = [
        pl.BlockSpec((None, qb * blk, gw), lambda b, c, i: (b, i, q_col(c))),
        pl.BlockSpec((None, blk, gw), lambda b, c, i: (b, prev(i), k_col(c))),
        pl.BlockSpec((None, qb * blk, gw), lambda b, c, i: (b, i, k_col(c))),
        pl.BlockSpec((None, blk, gw), lambda b, c, i: (b, prev(i), v_col(c))),
        pl.BlockSpec((None, qb * blk, gw), lambda b, c, i: (b, i, v_col(c))),
        pl.BlockSpec((DSWA_HPG, blk, 2 * blk), lambda b, c, i: (0, 0, 0)),
    ]
    out_specs = [
        pl.BlockSpec((None, qb * blk, gw), lambda b, c, i: (b, i, c)),
        pl.BlockSpec((None, qb * blk, LANES), lambda b, c, i: (b, i, c)),
    ]
    o, st = pl.pallas_call(
        functools.partial(_dswa_kernel, qb=qb),
        grid=(bsz, r, nblk // qb),
        in_specs=in_specs,
        out_specs=out_specs,
        out_shape=[jax.ShapeDtypeStruct((bsz, L, r * gw), F32),
                   jax.ShapeDtypeStruct((bsz, L, r * LANES), F32)],
        compiler_params=_cparams(("parallel", "parallel", "arbitrary")),
        name="dswa_g%d" % g,
    )(qv, kv, kv, vv, vv, bias)
    return o.reshape(bsz * seq, gw), st.reshape(bsz * seq, LANES)


GLA_TC = 256


def _split3(x):
    a1 = x.astype(BF16)
    r1 = x - a1.astype(F32)
    a2 = r1.astype(BF16)
    a3 = (r1 - a2.astype(F32)).astype(BF16)
    return a1, a2, a3


def _gla_kernel(v_ref, r_ref, q_ref, k_ref, al_ref, wa_ref, ba_ref, gn_ref, o_ref, st_ref):
    C = GLA_CHUNK

    @pl.when(pl.program_id(1) == 0)
    def _():
        st_ref[...] = jnp.zeros_like(st_ref)

    row = lax.broadcasted_iota(I32, (C, C), 0)
    col = lax.broadcasted_iota(I32, (C, C), 1)
    tri = row >= col
    tri_bf = jnp.where(tri, 1.0, 0.0).astype(BF16)
    nt = (((1,), (1,)), ((), ()))
    tn = (((0,), (0,)), ((), ()))
    for c in range(GLA_TC // C):
        rs = slice(c * C, (c + 1) * C)
        z = jnp.dot(al_ref[rs, :], wa_ref[...], preferred_element_type=F32) + ba_ref[...]
        la = (jnp.minimum(z, 0.0) - jnp.log(1.0 + jnp.exp(-jnp.abs(z)))) * (1.0 / GLA_TAU)
        a1, a2, a3 = _split3(la)
        bcum = (jnp.dot(tri_bf, a1, preferred_element_type=F32)
                + jnp.dot(tri_bf, a2, preferred_element_type=F32)
                + jnp.dot(tri_bf, a3, preferred_element_type=F32))
        blast = bcum[C - 1:C, :]
        qf = q_ref[rs, :].astype(F32) * (GLA_DK ** -0.5)
        kf = k_ref[rs, :].astype(F32)
        q_in = (qf * jnp.exp(bcum)).astype(BF16)
        k_in = (kf * jnp.exp(-bcum)).astype(BF16)
        k_end = (kf * jnp.exp(blast - bcum)).astype(BF16)
        dec = jnp.exp(blast)
        for h in range(GLA_HEADS):
            ks = slice(h * GLA_DK, (h + 1) * GLA_DK)
            vs = slice(h * GLA_DV, (h + 1) * GLA_DV)
            vh = v_ref[rs, vs]
            att = lax.dot_general(q_in[:, ks], k_in[:, ks], nt, preferred_element_type=F32)
            att = jnp.where(tri, att, 0.0)
            o = jnp.dot(att.astype(BF16), vh, preferred_element_type=F32)
            st = st_ref[h]
            o = o + lax.dot_general(q_in[:, ks], st.astype(BF16), nt, preferred_element_type=F32)
            st_ref[h] = st * dec[:, ks] + lax.dot_general(vh, k_end[:, ks], tn, preferred_element_type=F32)
            ms = jnp.mean(o * o, axis=-1, keepdims=True)
            y = o * lax.rsqrt(ms + RMS_EPS) * gn_ref[...]
            rg = r_ref[rs, vs].astype(F32)
            y = y * (rg * jax.nn.sigmoid(rg))
            o_ref[rs, vs] = y.astype(o_ref.dtype)


def _gla(p2, wa, ba, gn, bsz, seq):
    tc = GLA_TC
    p2v = p2.reshape(bsz, seq, W2)
    in_specs = [
        pl.BlockSpec((None, tc, 1024), lambda b, t: (b, t, P2_BV // 1024)),
        pl.BlockSpec((None, tc, 1024), lambda b, t: (b, t, P2_BR // 1024)),
        pl.BlockSpec((None, tc, 512), lambda b, t: (b, t, P2_BQ // 512)),
        pl.BlockSpec((None, tc, 512), lambda b, t: (b, t, P2_BK // 512)),
        pl.BlockSpec((None, tc, LANES), lambda b, t: (b, t, P2_BAL // LANES)),
        pl.BlockSpec((LANES, 512), lambda b, t: (0, 0)),
        pl.BlockSpec((1, 512), lambda b, t: (0, 0)),
        pl.BlockSpec((1, GLA_DV), lambda b, t: (0, 0)),
    ]
    out = pl.pallas_call(
        _gla_kernel,
        grid=(bsz, seq // tc),
        in_specs=in_specs,
        out_specs=pl.BlockSpec((None, tc, 1024), lambda b, t: (b, t, 0)),
        out_shape=jax.ShapeDtypeStruct((bsz, seq, 1024), BF16),
        scratch_shapes=[pltpu.VMEM((GLA_HEADS, GLA_DV, GLA_DK), F32)],
        compiler_params=_cparams(("parallel", "arbitrary")),
        name="gla",
    )(p2v, p2v, p2v, p2v, p2v, wa, ba, gn)
    return out.reshape(bsz * seq, 1024)


def _tree_sum(xs):
    xs = list(xs)
    while len(xs) > 1:
        xs = [xs[i] + xs[i + 1] for i in range(0, len(xs) - 1, 2)] + ([xs[-1]] if len(xs) % 2 else [])
    return xs[0]


def _dsa_kernel(q_ref, iq_ref, iw_ref, k_ref, ik_ref, vt_ref, bias_ref, wot_ref, o_ref,
                keys_ref, planes_ref, qaug_ref, iqall_ref, acc_ref, st0_ref, st1_ref, m_ref, cm0_ref, cm1_ref,
                *, topk):
    TQ, CH = DSA_TQ, DSA_CH
    CC = 2 * CH
    qi = pl.program_id(1)
    t0 = qi * TQ
    nch = qi // (CH // TQ) + 1
    npair = (nch + 1) // 2
    nt = (((1,), (1,)), ((), ()))

    eye = (lax.broadcasted_iota(I32, (TQ, TQ), 0) == lax.broadcasted_iota(I32, (TQ, TQ), 1))
    eye = jnp.where(eye, 1.0, 0.0).astype(BF16)
    q_t = q_ref[...].astype(F32).T.astype(BF16)
    iq_t = iq_ref[...].astype(F32).T.astype(BF16)
    for h in range(DSA_HEADS):
        cs = slice(h * TQ, (h + 1) * TQ)
        qaug_ref[0:TQ, cs] = eye
        qaug_ref[TQ:TQ + HEAD_DIM, cs] = q_t[h * HEAD_DIM:(h + 1) * HEAD_DIM, :]
        qaug_ref[TQ + HEAD_DIM:, cs] = jnp.zeros((TQ - HEAD_DIM, TQ), BF16)
    for h in range(IDX_HEADS):
        cs = slice(h * TQ, (h + 1) * TQ)
        iqall_ref[0:IDX_DIM, cs] = iq_t[h * IDX_DIM:(h + 1) * IDX_DIM, :]
        iqall_ref[IDX_DIM:, cs] = jnp.zeros((LANES - IDX_DIM, TQ), BF16)
    idx_scale = (IDX_HEADS ** -0.5) * (IDX_DIM ** -0.5)
    wt = (iw_ref[...].astype(F32) * idx_scale).T

    qpos = t0 + lax.broadcasted_iota(I32, (1, TQ), 1)
    RB = 128
    krow1 = lax.broadcasted_iota(I32, (RB, TQ), 0)
    krow = lax.broadcasted_iota(I32, (CH, TQ), 0)
    krow2 = lax.broadcasted_iota(I32, (CC, TQ), 0)

    def score_body(cp, carry):
        for sub in range(CC // RB):
            s0 = pl.multiple_of(cp * CC + sub * RB, RB)
            ikc = ik_ref[pl.ds(s0, RB), :]
            s = None
            for hp in range(IDX_HEADS // 2):
                x = jnp.dot(ikc, iqall_ref[:, hp * 2 * TQ:(hp + 1) * 2 * TQ],
                            preferred_element_type=F32)
                t = (jnp.maximum(x[:, :TQ], 0.0) * wt[2 * hp:2 * hp + 1, :]
                     + jnp.maximum(x[:, TQ:], 0.0) * wt[2 * hp + 1:2 * hp + 2, :])
                s = t if s is None else s + t
            s = jnp.where(s == 0.0, 0.0, s)
            bits = pltpu.bitcast(s, I32)
            key = bits ^ ((bits >> 31) & 0x7FFFFFFF)
            key = jnp.where(s0 + krow1 <= qpos, key, INT_MIN)
            keys_ref[pl.ds(s0, RB), :] = key
        return carry

    lax.fori_loop(0, npair, score_body, 0)

    kvec = jnp.minimum(topk, qpos + 1)
    NPL = 32

    def planes_body(c, carry):
        s0 = pl.multiple_of(c * CH, CH)
        a = [keys_ref[pl.ds(s0 + 8 * j, 8), :] ^ INT_MIN for j in range(NPL)]
        j, m = 16, 0x0000FFFF
        while j:
            sh = jnp.full((8, TQ), j, I32)
            k = 0
            while k < NPL:
                t = (a[k] ^ lax.shift_right_logical(a[k + j], sh)) & m
                a[k] = a[k] ^ t
                a[k + j] = a[k + j] ^ lax.shift_left(t, sh)
                k = (k + j + 1) & ~j
            j >>= 1
            m = (m ^ (m << j)) & 0xFFFFFFFF
        for p in range(NPL):
            planes_ref[c, p] = a[p]
        return carry

    def empty_body(c, carry):
        for p in range(NPL):
            planes_ref[c, p] = jnp.zeros((8, TQ), I32)
        return carry

    nck = 2 * npair
    NCK = keys_ref.shape[0] // CH
    lax.fori_loop(0, nck, planes_body, 0)
    lax.fori_loop(nck, NCK, empty_body, 0)

    def plane_body(p, carry):
        thr_u, n_gt, alive = carry
        hit = [alive[c] & planes_ref[c, p] for c in range(NCK)]
        ones = jnp.sum(_tree_sum([lax.population_count(h) for h in hit]), axis=0, keepdims=True)
        take = (n_gt + ones) >= kvec
        thr_u = jnp.where(take, thr_u | (jnp.int32(1) << (31 - p)), thr_u)
        n_gt = jnp.where(take, n_gt, n_gt + ones)
        alive = tuple(jnp.where(take, hit[c], alive[c] ^ hit[c]) for c in range(NCK))
        return thr_u, n_gt, alive

    zero = jnp.zeros((1, TQ), I32)
    alive0 = tuple(jnp.full((8, TQ), -1, I32) for _ in range(NCK))
    thr_u, n_gt, alive = lax.fori_loop(0, NPL, plane_body, (zero, zero, alive0))
    n_eq = jnp.sum(_tree_sum([lax.population_count(a) for a in alive]), axis=0, keepdims=True)
    thr = thr_u ^ INT_MIN

    def count(pred_fn):
        def body(c, acc):
            s0 = pl.multiple_of(c * CC, CC)
            kk = keys_ref[pl.ds(s0, CC), :]
            hit = jnp.where(pred_fn(kk, s0), 1, 0).astype(I32)
            return acc + jnp.sum(hit.reshape(CC // 8, 8, TQ), axis=0)
        acc = lax.fori_loop(0, npair, body, jnp.zeros((8, TQ), I32))
        return jnp.sum(acc, axis=0, keepdims=True)

    excess = n_gt + n_eq - kvec
    has_excess = jnp.max(excess) > 0

    @pl.when(has_excess)
    def _():
        need = kvec - n_gt

        def tie_lt(cut):
            return count(lambda kk, s0: (kk == thr) & (s0 + krow2 < cut))

        def cut_body(b, cut):
            cand = cut | (jnp.int32(1) << (12 - b))
            return jnp.where(tie_lt(cand) <= need, cand, cut)

        cut = lax.fori_loop(0, 13, cut_body, jnp.zeros((1, TQ), I32))

        def drop_body(c, carry):
            s0 = pl.multiple_of(c * CH, CH)
            kk = keys_ref[pl.ds(s0, CH), :]
            keys_ref[pl.ds(s0, CH), :] = jnp.where((kk == thr) & (s0 + krow >= cut), INT_MIN, kk)
            return carry

        lax.fori_loop(0, nch, drop_body, 0)

    acc_ref[...] = jnp.zeros_like(acc_ref)

    NP = DSA_HEADS // 2
    PW = 2 * TQ

    NB = CH // RB
    last_chunk = keys_ref.shape[0] // CH - 1

    def logits(c, st_ref, cm_ref):
        s0 = pl.multiple_of(c * CH, CH)
        zoff = pl.multiple_of(jnp.clip(s0 - t0 + DSA_Z0, 0, DSA_Z0), 8)
        kaug = []
        for rb in range(NB):
            rs = pl.ds(s0 + rb * RB, RB)
            pen = jnp.where(keys_ref[rs, :] >= thr, 0.0, NEG_BIG).astype(BF16)
            kaug.append(jnp.concatenate([pen, k_ref[rs, :]], axis=1))
        for hp in range(NP):
            ps = slice(hp * PW, (hp + 1) * PW)
            cm = None
            for rb in range(NB):
                st = jnp.dot(kaug[rb], qaug_ref[:, ps], preferred_element_type=F32)
                st = st + bias_ref[pl.ds(zoff + rb * RB, RB), ps]
                st_ref[rb * RB:(rb + 1) * RB, ps] = st
                tm = jnp.max(st, axis=0, keepdims=True)
                cm = tm if cm is None else jnp.maximum(cm, tm)
            cm_ref[0:1, ps] = cm

    def accumulate(c, st_ref, cm_ref):
        vtc = vt_ref[c]
        for hp in range(NP):
            ps = slice(hp * PW, (hp + 1) * PW)
            m_old = m_ref[0:1, ps]
            m_new = jnp.maximum(m_old, cm_ref[0:1, ps])
            m_ref[0:1, ps] = m_new
            alpha = jnp.exp2(m_old - m_new)
            p = jnp.concatenate(
                [jnp.exp2(st_ref[rb * RB:(rb + 1) * RB, ps] - m_new).astype(BF16) for rb in range(NB)], axis=0)
            acc_ref[:, ps] = acc_ref[:, ps] * alpha + jnp.dot(vtc, p, preferred_element_type=F32)

    def att_body(cp, carry):
        c0 = 2 * cp
        logits(c0 + 1, st1_ref, cm1_ref)
        accumulate(c0, st0_ref, cm0_ref)
        logits(jnp.minimum(c0 + 2, last_chunk), st0_ref, cm0_ref)
        accumulate(c0 + 1, st1_ref, cm1_ref)
        return carry

    m_ref[...] = jnp.full(m_ref.shape, NEG_BIG, F32)
    logits(0, st0_ref, cm0_ref)
    lax.fori_loop(0, npair, att_body, 0)

    ot = jnp.concatenate(
        [(acc_ref[0:HEAD_DIM, h * TQ:(h + 1) * TQ] / acc_ref[HEAD_DIM:HEAD_DIM + 1, h * TQ:(h + 1) * TQ]).astype(BF16)
         for h in range(DSA_HEADS)], axis=0)
    yt = jnp.dot(wot_ref[...], ot, preferred_element_type=F32)
    o_ref[...] = yt.T


def _dsa(p1, p2, vt, bias_t, wot, bsz, seq):
    tq = DSA_TQ
    p1v = p1.reshape(bsz, seq, W1)
    p2v = p2.reshape(bsz, seq, W2)
    nq = DSA_HEADS * tq
    in_specs = [
        pl.BlockSpec((None, tq, 768), lambda b, i: (b, i, P1_CQ // 768)),
        pl.BlockSpec((None, tq, 512), lambda b, i: (b, i, P2_IQ // 512)),
        pl.BlockSpec((None, tq, LANES), lambda b, i: (b, i, P2_IW // LANES)),
        pl.BlockSpec((None, seq, LANES), lambda b, i: (b, 0, P1_CK // LANES)),
        pl.BlockSpec((None, seq, LANES), lambda b, i: (b, 0, P2_IK // LANES)),
        pl.BlockSpec((None, seq // DSA_CH, DSA_VR, DSA_CH), lambda b, i: (b, 0, 0, 0)),
        pl.BlockSpec((DSA_Z, nq), lambda b, i: (0, 0)),
        pl.BlockSpec((D_MODEL, DSA_HEADS * HEAD_DIM), lambda b, i: (0, 0)),
    ]
    out = pl.pallas_call(
        functools.partial(_dsa_kernel, topk=min(IDX_TOPK, seq // 4)),
        grid=(bsz, seq // tq),
        in_specs=in_specs,
        out_specs=pl.BlockSpec((None, tq, D_MODEL), lambda b, i: (b, i, 0)),
        out_shape=jax.ShapeDtypeStruct((bsz, seq, D_MODEL), F32),
        scratch_shapes=[
            pltpu.VMEM((seq, tq), I32),
            pltpu.VMEM((seq // DSA_CH, 32, 8, tq), I32),
            pltpu.VMEM((2 * tq, nq), BF16),
            pltpu.VMEM((LANES, IDX_HEADS * tq), BF16),
            pltpu.VMEM((DSA_VR, nq), F32),
            pltpu.VMEM((DSA_CH, nq), F32),
            pltpu.VMEM((DSA_CH, nq), F32),
            pltpu.VMEM((8, nq), F32),
            pltpu.VMEM((8, nq), F32),
            pltpu.VMEM((8, nq), F32),
        ],
        compiler_params=_cparams(("parallel", "arbitrary")),
        name="dsa",
    )(p1v, p2v, p2v, p1v, p2v, vt, bias_t, wot)
    return out.reshape(bsz * seq, D_MODEL)


MERGE_TM = 512


def _merge_kernel(h_ref, oa0_ref, oa1_ref, oa2_ref, sa0_ref, sa1_ref, sa2_ref, ob_ref, yc_ref, g_ref,
                  woa_ref, wob_ref, wo_ref, o_ref):
    H = DSWA_HPG
    sts = [sa0_ref[...], sa1_ref[...], sa2_ref[...]]
    oas = [oa0_ref, oa1_ref, oa2_ref]
    ms = [s[:, 0:H] for s in sts]
    ls = [s[:, H:2 * H] for s in sts]
    mmax = jnp.maximum(jnp.maximum(ms[0], ms[1]), ms[2])
    wts = [l * jnp.exp(m - mmax) for m, l in zip(ms, ls)]
    tot = wts[0] + wts[1] + wts[2]
    wts = [w / tot for w in wts]
    parts = []
    for h in range(H):
        sl = slice(h * HEAD_DIM, (h + 1) * HEAD_DIM)
        acc = wts[0][:, h:h + 1] * oas[0][:, sl]
        acc = acc + wts[1][:, h:h + 1] * oas[1][:, sl]
        acc = acc + wts[2][:, h:h + 1] * oas[2][:, sl]
        parts.append(acc)
    oa = jnp.concatenate(parts, axis=1).astype(BF16)
    y_a = jnp.dot(oa, woa_ref[...], preferred_element_type=F32)
    y_b = jnp.dot(ob_ref[...], wob_ref[...], preferred_element_type=F32)
    y_c = yc_ref[...]
    D = D_MODEL
    mix = (g_ref[:, 0:D].astype(F32) * y_a + g_ref[:, D:2 * D].astype(F32) * y_b
           + g_ref[:, 2 * D:3 * D].astype(F32) * y_c)
    o_ref[...] = h_ref[...] + jnp.dot(mix.astype(BF16), wo_ref[...], preferred_element_type=F32)


def _merge(h, oas, sas, ob, yc, gates, woa, wob, wo):
    m = h.shape[0]
    tm = MERGE_TM
    row = lambda w: pl.BlockSpec((tm, w), lambda i: (i, 0))
    full = lambda a: pl.BlockSpec(a.shape, lambda i: (0, 0))
    in_specs = ([row(D_MODEL)] + [row(256)] * 3 + [row(LANES)] * 3 + [row(1024), row(D_MODEL), row(3 * D_MODEL)]
                + [full(woa), full(wob), full(wo)])
    return pl.pallas_call(
        _merge_kernel,
        grid=(m // tm,),
        in_specs=in_specs,
        out_specs=row(D_MODEL),
        out_shape=jax.ShapeDtypeStruct((m, D_MODEL), F32),
        compiler_params=_cparams(("parallel",)),
        name="merge",
    )(h, *oas, *sas, ob, yc, gates, woa, wob, wo)


FFN_TM = 1024
FFN_TF = 256


def _ffn_kernel(h_ref, g_ref, wg_ref, wu_ref, wd_ref, o_ref, u_ref, acc_ref):
    j = pl.program_id(1)

    @pl.when(j == 0)
    def _():
        x = h_ref[...]
        ms = jnp.mean(x * x, axis=-1, keepdims=True)
        u_ref[...] = (x * lax.rsqrt(ms + RMS_EPS) * g_ref[...]).astype(BF16)
        acc_ref[...] = jnp.zeros_like(acc_ref)

    u = u_ref[...]
    a = jnp.dot(u, wg_ref[...], preferred_element_type=F32)
    b = jnp.dot(u, wu_ref[...], preferred_element_type=F32)
    t = (a * jax.nn.sigmoid(a) * b).astype(BF16)
    acc_ref[...] += jnp.dot(t, wd_ref[...], preferred_element_type=F32)

    @pl.when(j == pl.num_programs(1) - 1)
    def _():
        o_ref[...] = h_ref[...] + acc_ref[...]


def _ffn(h, g, wg, wu, wd):
    m, d = h.shape
    f = wg.shape[1]
    tm, tf = FFN_TM, FFN_TF
    return pl.pallas_call(
        _ffn_kernel,
        grid=(m // tm, f // tf),
        in_specs=[
            pl.BlockSpec((tm, d), lambda i, j: (i, 0)),
            pl.BlockSpec((1, d), lambda i, j: (0, 0)),
            pl.BlockSpec((d, tf), lambda i, j: (0, j)),
            pl.BlockSpec((d, tf), lambda i, j: (0, j)),
            pl.BlockSpec((tf, d), lambda i, j: (j, 0)),
        ],
        out_specs=pl.BlockSpec((tm, d), lambda i, j: (i, 0)),
        out_shape=jax.ShapeDtypeStruct((m, d), F32),
        scratch_shapes=[pltpu.VMEM((tm, d), BF16), pltpu.VMEM((tm, d), F32)],
        compiler_params=_cparams(("parallel", "arbitrary")),
        name="ffn",
    )(h, g, wg, wu, wd)


PLE_TM = 512


def _ple_kernel(h_ref, p_ref, g_ref, wg_ref, wp_ref, o_ref):
    x = h_ref[...]
    ms = jnp.mean(x * x, axis=-1, keepdims=True)
    e = (x * lax.rsqrt(ms + RMS_EPS) * g_ref[...]).astype(BF16)
    gate = jax.nn.sigmoid(jnp.dot(e, wg_ref[...], preferred_element_type=F32))
    proj = jnp.dot(p_ref[...].astype(BF16), wp_ref[...], preferred_element_type=F32)
    o_ref[...] = x + gate * proj


def _ple(h, p, g, wg, wp):
    m, d = h.shape
    tm = PLE_TM
    return pl.pallas_call(
        _ple_kernel,
        grid=(m // tm,),
        in_specs=[
            pl.BlockSpec((tm, d), lambda i: (i, 0)),
            pl.BlockSpec((tm, PLE_DIM), lambda i: (i, 0)),
            pl.BlockSpec((1, d), lambda i: (0, 0)),
            pl.BlockSpec((d, d), lambda i: (0, 0)),
            pl.BlockSpec((PLE_DIM, d), lambda i: (0, 0)),
        ],
        out_specs=pl.BlockSpec((tm, d), lambda i: (i, 0)),
        out_shape=jax.ShapeDtypeStruct((m, d), F32),
        compiler_params=_cparams(("parallel",)),
        name="ple",
    )(h, p, g, wg, wp)


def _rel_bucket(dist):
    max_exact = REL_BUCKETS // 2
    d = jnp.maximum(dist, 0)
    df = jnp.maximum(d, 1).astype(F32)
    large = max_exact + (jnp.log(df / max_exact) / math.log(REL_MAX_DIST / max_exact)
                         * (REL_BUCKETS - max_exact)).astype(I32)
    large = jnp.minimum(large, REL_BUCKETS - 1)
    return jnp.where(d < max_exact, d, large)


def _toeplitz(rev, n_rows, n_cols):
    nh = rev.shape[0]
    lw = rev.shape[1] + 1
    w = jnp.pad(rev, ((0, 0), (0, 1)))
    s = jnp.broadcast_to(w[:, None, :], (nh, n_rows, lw)).reshape(nh, n_rows * lw)
    s = s[:, :n_rows * (lw - 1)].reshape(nh, n_rows, lw - 1)
    return s[:, :, n_rows - 1:n_rows - 1 + n_cols]


def _bias_tables(rel_bias):
    blk = DSWA_BLOCK
    dswa = []
    for g, (_, r) in enumerate(DSWA_PATTERNS):
        delta = np.arange(3 * blk - 1)[::-1] - (blk - 1)
        rev = rel_bias[_rel_bucket(jnp.asarray(delta * r, I32))][:, g * DSWA_HPG:(g + 1) * DSWA_HPG]
        dswa.append(_toeplitz(rev.T, blk, 2 * blk))
    tq = DSA_TQ
    dist = np.arange(DSA_Z + tq - 1)[::-1] - (DSA_Z - 1) + DSA_Z0
    rev = rel_bias[_rel_bucket(jnp.asarray(dist, I32))][:, DSWA_HEADS:] * math.log2(math.e)
    bias_t = jnp.transpose(_toeplitz(rev.T, tq, DSA_Z), (2, 0, 1)).reshape(DSA_Z, DSA_HEADS * tq)
    return dswa, bias_t


def _pad_cols(w, width):
    return jnp.pad(w, ((0, 0), (0, width - w.shape[1])))


def _layer_params(w_in, qn_a, kn_a, qn_c, kn_c, w_alpha2, b_alpha):
    offs = np.cumsum((0,) + IN_WIDTHS)
    parts = [w_in[:, offs[i]:offs[i + 1]] for i in range(len(IN_WIDTHS))]
    a_q, a_k, a_v, b_q, b_k, b_v, b_r, b_al, c_q, c_k, c_v, i_q, i_k, i_w = parts
    w1 = _pad_cols(jnp.concatenate([a_q, a_k, c_q, c_k], axis=1), W1).astype(BF16)
    w2 = _pad_cols(jnp.concatenate(
        [b_v, b_r, b_q, b_k, i_q, a_v, _pad_cols(b_al, LANES), _pad_cols(c_v, LANES),
         _pad_cols(i_k, LANES), _pad_cols(i_w, LANES)], axis=1), W2).astype(BF16)
    scale = HEAD_DIM ** -0.5
    gain1 = jnp.concatenate([jnp.tile(qn_a, DSWA_HEADS) * scale, jnp.tile(kn_a, DSWA_HEADS),
                             jnp.tile(qn_c, DSA_HEADS) * (scale * math.log2(math.e)), kn_c])
    gain1 = jnp.pad(gain1, (0, W1 - gain1.shape[0])).reshape(1, W1)
    wa = jnp.pad(w_alpha2, ((0, LANES - GLA_RANK), (0, 0))).astype(BF16)
    return w1, w2, gain1, wa, b_alpha.reshape(1, -1)


def kernel(x, p, rel_bias, norm_mix, w_in, qn_a, kn_a, qn_c, kn_c, w_alpha2, b_alpha, gla_norm, w_out_a, w_out_b, w_out_c, w_gate, b_gate, w_o, norm_ffn, w_ffn_gate, w_ffn_up, w_ffn_down, norm_ple, w_ple_gate, w_ple_proj):
    bsz, seq, d = x.shape
    depth = p.shape[0]
    m = bsz * seq
    dswa_bias, bias_t = _bias_tables(rel_bias)
    h = x.reshape(m, d)
    zeros_w2 = jnp.zeros((1, W2), F32)
    for i in range(depth):
        w1, w2, gain1, wa, ba = _layer_params(w_in[i], qn_a[i], kn_a[i], qn_c[i], kn_c[i], w_alpha2[i], b_alpha[i])
        gmix = norm_mix[i].reshape(1, d)
        p1 = _proj(h, gmix, w1, gain1, "qk")
        p2 = _proj(h, gmix, w2, zeros_w2, "plain")
        gates = _proj(h, gmix, w_gate[i].astype(BF16), b_gate[i].reshape(1, -1), "gate")
        oas, sas = [], []
        for g, (_, r) in enumerate(DSWA_PATTERNS):
            o, st = _dswa(p1, p2, dswa_bias[g], g, r, bsz, seq)
            oas.append(o)
            sas.append(st)
        ob = _gla(p2, wa, ba, gla_norm[i].reshape(1, -1), bsz, seq)
        cv = p2[:, P2_CV:P2_CV + HEAD_DIM].reshape(bsz, seq // DSA_CH, DSA_CH, HEAD_DIM)
        ones_pad = jnp.zeros((bsz, seq // DSA_CH, DSA_VR - HEAD_DIM, DSA_CH), BF16).at[:, :, 0, :].set(1.0)
        vt = jnp.concatenate([jnp.transpose(cv, (0, 1, 3, 2)), ones_pad], axis=2)
        yc = _dsa(p1, p2, vt, bias_t, w_out_c[i].T.astype(BF16), bsz, seq)
        h = _merge(h, oas, sas, ob, yc, gates, w_out_a[i].astype(BF16), w_out_b[i].astype(BF16),
                   w_o[i].astype(BF16))
        h = _ffn(h, norm_ffn[i].reshape(1, d), w_ffn_gate[i].astype(BF16), w_ffn_up[i].astype(BF16),
                 w_ffn_down[i].astype(BF16))
        h = _ple(h, p[i].reshape(m, PLE_DIM), norm_ple[i].reshape(1, d), w_ple_gate[i].astype(BF16),
                 w_ple_proj[i].astype(BF16))
    return h.reshape(bsz, seq, d)
```

```python
import functools
import math

import numpy as np
import jax
import jax.numpy as jnp
from jax import lax
from jax.experimental import pallas as pl
from jax.experimental.pallas import tpu as pltpu

F32 = jnp.float32
BF16 = jnp.bfloat16
I32 = jnp.int32
I16 = jnp.int16

D_MODEL = 1024
HEAD_DIM = 64
RMS_EPS = 1e-6
DSWA_PATTERNS = ((128, 1), (512, 4), (2048, 16))
DSWA_HPG = 4
DSWA_HEADS = 12
DSWA_BLOCK = 128
GLA_HEADS = 4
GLA_DK = 128
GLA_DV = 256
GLA_RANK = 16
GLA_TAU = 16.0
GLA_CHUNK = 64
DSA_HEADS = 12
IDX_HEADS = 8
IDX_DIM = 64
IDX_TOPK = 256
REL_BUCKETS = 32
REL_MAX_DIST = 2048
D_FF = 2816
PLE_DIM = 256
IN_WIDTHS = (768, 768, 768, 512, 512, 1024, 1024, 16, 768, 64, 64, 512, 64, 8)

LANES = 128
MXU_N = 256
VMEM_LIMIT = 56 * 1024 * 1024

W1 = 2560
P1_AQ, P1_AK, P1_CQ, P1_CK = 0, 768, 1536, 2304
W2 = 5120
P2_BV, P2_BR, P2_BQ, P2_BK, P2_IQ, P2_AV = 0, 1024, 2048, 2560, 3072, 3584
P2_BAL, P2_CV, P2_IK, P2_IW = 4352, 4480, 4608, 4736

PROJ_TM = 512
PROJ_TN = 512

DSWA_QB = 4
DSA_TQ = 128
DSA_CH = 256
DSA_AB = 256
DSA_VR = 80
DSA_BIAS_CONST_FROM = 1512
DSA_Z0 = 1792
DSA_Z = DSA_Z0 + DSA_CH
INT_MIN = -(2 ** 31)
NEG_BIG = -1e30


def _cparams(sem, flags=None):
    return pltpu.CompilerParams(dimension_semantics=sem, vmem_limit_bytes=VMEM_LIMIT, flags=flags)


def _proj_kernel(h_ref, g_ref, w_ref, e_ref, *rest, mode, tn):
    if mode == "qk":
        bd_ref, o_ref = rest
    else:
        (o_ref,) = rest
    x = h_ref[...]
    ms = jnp.mean(x * x, axis=-1, keepdims=True)
    u = (x * lax.rsqrt(ms + RMS_EPS) * g_ref[...]).astype(BF16)
    for j in range(w_ref.shape[1] // tn):
        cs = slice(j * tn, (j + 1) * tn)
        acc = jnp.dot(u, w_ref[:, cs], preferred_element_type=F32)
        if mode == "plain":
            out = acc
        elif mode == "gate":
            out = jax.nn.sigmoid(acc + e_ref[:, cs])
        else:
            sq = acc * acc
            hi = sq.astype(BF16)
            lo = (sq - hi.astype(F32)).astype(BF16)
            ss = (jnp.dot(hi, bd_ref[...], preferred_element_type=F32)
                  + jnp.dot(lo, bd_ref[...], preferred_element_type=F32))
            out = acc * lax.rsqrt(ss * (1.0 / HEAD_DIM) + RMS_EPS) * e_ref[:, cs]
        o_ref[:, cs] = out.astype(o_ref.dtype)


def _proj(h, g, w, e, mode):
    m, d = h.shape
    n = w.shape[1]
    tm, tn = PROJ_TM, PROJ_TN
    const = lambda shape: pl.BlockSpec(shape, lambda i: (0, 0), pipeline_mode=pl.Buffered(1))
    in_specs = [pl.BlockSpec((tm, d), lambda i: (i, 0)), const((1, d)), const((d, n)), const((1, n))]
    args = [h, g, w, e]
    if mode == "qk":
        r = np.arange(tn) // HEAD_DIM
        bd = jnp.asarray((r[:, None] == r[None, :]).astype(np.float32), dtype=BF16)
        in_specs.append(const((tn, tn)))
        args.append(bd)
    return pl.pallas_call(
        functools.partial(_proj_kernel, mode=mode, tn=tn),
        grid=(m // tm,),
        in_specs=in_specs,
        out_specs=pl.BlockSpec((tm, n), lambda i: (i, 0)),
        out_shape=jax.ShapeDtypeStruct((m, n), BF16),
        compiler_params=_cparams(("parallel",)),
        name="proj_" + mode,
    )(*args)


def _dswa_kernel(q_ref, kp_ref, kc_ref, vp_ref, vc_ref, bias_ref, o_ref, st_ref, *, qb):
    i = pl.program_id(2)
    blk = DSWA_BLOCK
    row = lax.broadcasted_iota(I32, (blk, 2 * blk), 0)
    col = lax.broadcasted_iota(I32, (blk, 2 * blk), 1)
    cur_ok = (col >= blk) & ((col - blk) <= row)
    valid_inner = ((col < blk) & (col >= row)) | cur_ok
    prev_off = jnp.where(i > 0, 0, 4 * blk)
    valid_first = ((col < blk) & (col >= row + prev_off)) | cur_ok
    lane = lax.broadcasted_iota(I32, (blk, LANES), 1)
    for j in range(qb):
        rs = slice(j * blk, (j + 1) * blk)
        ps = slice((j - 1) * blk, j * blk)
        valid = valid_first if j == 0 else valid_inner
        q = q_ref[rs, :]
        k = jnp.concatenate([kp_ref[...] if j == 0 else kc_ref[ps, :], kc_ref[rs, :]], axis=0)
        v = jnp.concatenate([vp_ref[...] if j == 0 else vc_ref[ps, :], vc_ref[rs, :]], axis=0)
        stats = jnp.zeros((blk, LANES), F32)
        for h in range(DSWA_HPG):
            sl = slice(h * HEAD_DIM, (h + 1) * HEAD_DIM)
            s = lax.dot_general(q[:, sl], k[:, sl], (((1,), (1,)), ((), ())), preferred_element_type=F32)
            s = jnp.where(valid, s + bias_ref[h], -jnp.inf)
            m = jnp.max(s, axis=-1, keepdims=True)
            p = jnp.exp(s - m)
            l = jnp.sum(p, axis=-1, keepdims=True)
            o = jnp.dot(p.astype(BF16), v[:, sl], preferred_element_type=F32) / l
            o_ref[rs, sl] = o
            stats = jnp.where(lane == h, m, stats)
            stats = jnp.where(lane == DSWA_HPG + h, l, stats)
        st_ref[rs, :] = stats


def _dswa(p1, p2, bias, g, r, bsz, seq):
    blk = DSWA_BLOCK
    L = seq // r
    nblk = L // blk
    gw = DSWA_HPG * HEAD_DIM
    if r == 1:
        p1v = p1.reshape(bsz, L, W1)
        p2v = p2.reshape(bsz, L, W2)
        qv, kv, vv = p1v, p1v, p2v
        q_col = lambda c: (P1_AQ // gw) + g
        k_col = lambda c: (P1_AK // gw) + g
        v_col = lambda c: (P2_AV // gw) + g
    else:
        def regroup(p, off):
            return p[:, off + g * gw:off + (g + 1) * gw].reshape(bsz, L, r * gw)
        qv, kv, vv = regroup(p1, P1_AQ), regroup(p1, P1_AK), regroup(p2, P2_AV)
        q_col = k_col = v_col = lambda c: c
    qb = min(DSWA_QB, nblk)
    prev = lambda i: jnp.maximum(i * qb - 1, 0)
    in_specs = [
        pl.BlockSpec((None, qb * blk, gw), lambda b, c, i: (b, i, q_col(c))),
        pl.BlockSpec((None, blk, gw), lambda b, c, i: (b, prev(i), k_col(c))),
        pl.BlockSpec((None, qb * blk, gw), lambda b, c, i: (b, i, k_col(c))),
        pl.BlockSpec((None, blk, gw), lambda b, c, i: (b, prev(i), v_col(c))),
        pl.BlockSpec((None, qb * blk, gw), lambda b, c, i: (b, i, v_col(c))),
        pl.BlockSpec((DSWA_HPG, blk, 2 * blk), lambda b, c, i: (0, 0, 0)),
    ]
    out_specs = [
        pl.BlockSpec((None, qb * blk, gw), lambda b, c, i: (b, i, c)),
        pl.BlockSpec((None, qb * blk, LANES), lambda b, c, i: (b, i, c)),
    ]
    o, st = pl.pallas_call(
        functools.partial(_dswa_kernel, qb=qb),
        grid=(bsz, r, nblk // qb),
        in_specs=in_specs,
        out_specs=out_specs,
        out_shape=[jax.ShapeDtypeStruct((bsz, L, r * gw), F32),
                   jax.ShapeDtypeStruct((bsz, L, r * LANES), F32)],
        compiler_params=_cparams(("parallel", "parallel", "arbitrary")),
        name="dswa_g%d" % g,
    )(qv, kv, kv, vv, vv, bias)
    return o.reshape(bsz * seq, gw), st.reshape(bsz * seq, LANES)


GLA_TC = 256
GLA_GB = 2


def _split3(x):
    a1 = x.astype(BF16)
    r1 = x - a1.astype(F32)
    a2 = r1.astype(BF16)
    a3 = (r1 - a2.astype(F32)).astype(BF16)
    return a1, a2, a3


def _gla_kernel(v_ref, r_ref, q_ref, k_ref, al_ref, wa_ref, ba_ref, gn_ref, o_ref, st_ref, *, gb):
    C = GLA_CHUNK

    @pl.when(pl.program_id(1) == 0)
    def _():
        st_ref[...] = jnp.zeros_like(st_ref)

    row = lax.broadcasted_iota(I32, (C, C), 0)
    col = lax.broadcasted_iota(I32, (C, C), 1)
    tri = row >= col
    tri_bf = jnp.where(tri, 1.0, 0.0).astype(BF16)
    nt = (((1,), (1,)), ((), ()))
    tn = (((0,), (0,)), ((), ()))
    for c, bb in [(c, bb) for c in range(GLA_TC // C) for bb in range(gb)]:
        rs = slice(c * C, (c + 1) * C)
        z = jnp.dot(al_ref[bb, rs, :], wa_ref[...], preferred_element_type=F32) + ba_ref[...]
        la = (jnp.minimum(z, 0.0) - jnp.log(1.0 + jnp.exp(-jnp.abs(z)))) * (1.0 / GLA_TAU)
        a1, a2, a3 = _split3(la)
        bcum = (jnp.dot(tri_bf, a1, preferred_element_type=F32)
                + jnp.dot(tri_bf, a2, preferred_element_type=F32)
                + jnp.dot(tri_bf, a3, preferred_element_type=F32))
        blast = bcum[C - 1:C, :]
        qf = q_ref[bb, rs, :].astype(F32) * (GLA_DK ** -0.5)
        kf = k_ref[bb, rs, :].astype(F32)
        q_in = (qf * jnp.exp(bcum)).astype(BF16)
        k_in = (kf * jnp.exp(-bcum)).astype(BF16)
        k_end = (kf * jnp.exp(blast - bcum)).astype(BF16)
        dec = jnp.exp(blast)
        for h in range(GLA_HEADS):
            ks = slice(h * GLA_DK, (h + 1) * GLA_DK)
            vs = slice(h * GLA_DV, (h + 1) * GLA_DV)
            vh = v_ref[bb, rs, vs]
            att = lax.dot_general(q_in[:, ks], k_in[:, ks], nt, preferred_element_type=F32)
            att = jnp.where(tri, att, 0.0)
            o = jnp.dot(att.astype(BF16), vh, preferred_element_type=F32)
            st = st_ref[bb, h]
            o = o + lax.dot_general(q_in[:, ks], st.astype(BF16), nt, preferred_element_type=F32)
            st_ref[bb, h] = st * dec[:, ks] + lax.dot_general(vh, k_end[:, ks], tn, preferred_element_type=F32)
            ms = jnp.mean(o * o, axis=-1, keepdims=True)
            y = o * lax.rsqrt(ms + RMS_EPS) * gn_ref[...]
            rg = r_ref[bb, rs, vs].astype(F32)
            y = y * (rg * jax.nn.sigmoid(rg))
            o_ref[bb, rs, vs] = y.astype(o_ref.dtype)


def _gla(p2, wa, ba, gn, bsz, seq):
    tc = GLA_TC
    gb = GLA_GB if bsz % GLA_GB == 0 else 1
    p2v = p2.reshape(bsz, seq, W2)
    in_specs = [
        pl.BlockSpec((gb, tc, 1024), lambda b, t: (b, t, P2_BV // 1024)),
        pl.BlockSpec((gb, tc, 1024), lambda b, t: (b, t, P2_BR // 1024)),
        pl.BlockSpec((gb, tc, 512), lambda b, t: (b, t, P2_BQ // 512)),
        pl.BlockSpec((gb, tc, 512), lambda b, t: (b, t, P2_BK // 512)),
        pl.BlockSpec((gb, tc, LANES), lambda b, t: (b, t, P2_BAL // LANES)),
        pl.BlockSpec((LANES, 512), lambda b, t: (0, 0)),
        pl.BlockSpec((1, 512), lambda b, t: (0, 0)),
        pl.BlockSpec((1, GLA_DV), lambda b, t: (0, 0)),
    ]
    out = pl.pallas_call(
        functools.partial(_gla_kernel, gb=gb),
        grid=(bsz // gb, seq // tc),
        in_specs=in_specs,
        out_specs=pl.BlockSpec((gb, tc, 1024), lambda b, t: (b, t, 0)),
        out_shape=jax.ShapeDtypeStruct((bsz, seq, 1024), BF16),
        scratch_shapes=[pltpu.VMEM((gb, GLA_HEADS, GLA_DV, GLA_DK), F32)],
        compiler_params=_cparams(("parallel", "arbitrary")),
        name="gla",
    )(p2v, p2v, p2v, p2v, p2v, wa, ba, gn)
    return out.reshape(bsz * seq, 1024)


def _tree_sum(xs):
    xs = list(xs)
    while len(xs) > 1:
        xs = [xs[i] + xs[i + 1] for i in range(0, len(xs) - 1, 2)] + ([xs[-1]] if len(xs) % 2 else [])
    return xs[0]


def _dsa_kernel(q_ref, iq_ref, iw_ref, k_ref, ik_ref, vt_ref, bias_ref, wot_ref, o_ref,
                keys_ref, planes_ref, qaug_ref, iqall_ref, acc_ref, st0_ref, st1_ref, m_ref, cm0_ref, cm1_ref,
                *, topk):
    TQ, CH = DSA_TQ, DSA_CH
    CC = 2 * CH
    qi = pl.program_id(1)
    t0 = qi * TQ
    nch = qi // (CH // TQ) + 1
    npair = (nch + 1) // 2
    nt = (((1,), (1,)), ((), ()))

    eye = (lax.broadcasted_iota(I32, (TQ, TQ), 0) == lax.broadcasted_iota(I32, (TQ, TQ), 1))
    eye = jnp.where(eye, 1.0, 0.0).astype(BF16)
    q_t = q_ref[...].astype(F32).T.astype(BF16)
    iq_t = iq_ref[...].astype(F32).T.astype(BF16)
    for h in range(DSA_HEADS):
        cs = slice(h * TQ, (h + 1) * TQ)
        qaug_ref[0:TQ, cs] = eye
        qaug_ref[TQ:TQ + HEAD_DIM, cs] = q_t[h * HEAD_DIM:(h + 1) * HEAD_DIM, :]
        qaug_ref[TQ + HEAD_DIM:, cs] = jnp.zeros((TQ - HEAD_DIM, TQ), BF16)
    for h in range(IDX_HEADS):
        cs = slice(h * TQ, (h + 1) * TQ)
        iqall_ref[0:IDX_DIM, cs] = iq_t[h * IDX_DIM:(h + 1) * IDX_DIM, :]
        iqall_ref[IDX_DIM:, cs] = jnp.zeros((LANES - IDX_DIM, TQ), BF16)
    idx_scale = (IDX_HEADS ** -0.5) * (IDX_DIM ** -0.5)
    wt = (iw_ref[...].astype(F32) * idx_scale).T

    qpos = t0 + lax.broadcasted_iota(I32, (1, TQ), 1)
    RB = 128
    krow1 = lax.broadcasted_iota(I32, (RB, TQ), 0)
    krow = lax.broadcasted_iota(I32, (CH, TQ), 0)
    krow2 = lax.broadcasted_iota(I32, (CC, TQ), 0)

    def score_body(cp, carry):
        for sub in range(CC // RB):
            s0 = pl.multiple_of(cp * CC + sub * RB, RB)
            ikc = ik_ref[pl.ds(s0, RB), :]
            s = None
            for hp in range(IDX_HEADS // 2):
                x = jnp.dot(ikc, iqall_ref[:, hp * 2 * TQ:(hp + 1) * 2 * TQ],
                            preferred_element_type=F32)
                t = (jnp.maximum(x[:, :TQ], 0.0) * wt[2 * hp:2 * hp + 1, :]
                     + jnp.maximum(x[:, TQ:], 0.0) * wt[2 * hp + 1:2 * hp + 2, :])
                s = t if s is None else s + t
            s = jnp.where(s == 0.0, 0.0, s)
            bits = pltpu.bitcast(s, I32)
            key = bits ^ ((bits >> 31) & 0x7FFFFFFF)
            key = jnp.where(s0 + krow1 <= qpos, key, INT_MIN)
            keys_ref[pl.ds(s0, RB), :] = key
        return carry

    lax.fori_loop(0, npair, score_body, 0)

    kvec = jnp.minimum(topk, qpos + 1)
    NPL = 32

    def planes_body(c, carry):
        s0 = pl.multiple_of(c * CH, CH)
        a = [keys_ref[pl.ds(s0 + 8 * j, 8), :] ^ INT_MIN for j in range(NPL)]
        j, m = 16, 0x0000FFFF
        while j:
            sh = jnp.full((8, TQ), j, I32)
            k = 0
            while k < NPL:
                t = (a[k] ^ lax.shift_right_logical(a[k + j], sh)) & m
                a[k] = a[k] ^ t
                a[k + j] = a[k + j] ^ lax.shift_left(t, sh)
                k = (k + j + 1) & ~j
            j >>= 1
            m = (m ^ (m << j)) & 0xFFFFFFFF
        for p in range(NPL):
            planes_ref[c, p] = a[p]
        return carry

    def empty_body(c, carry):
        for p in range(NPL):
            planes_ref[c, p] = jnp.zeros((8, TQ), I32)
        return carry

    nck = 2 * npair
    NCK = keys_ref.shape[0] // CH
    lax.fori_loop(0, nck, planes_body, 0)
    lax.fori_loop(nck, NCK, empty_body, 0)

    def plane_body(p, carry):
        thr_u, n_gt, alive = carry
        hit = [alive[c] & planes_ref[c, p] for c in range(NCK)]
        ones = jnp.sum(_tree_sum([lax.population_count(h) for h in hit]), axis=0, keepdims=True)
        take = (n_gt + ones) >= kvec
        thr_u = jnp.where(take, thr_u | (jnp.int32(1) << (31 - p)), thr_u)
        n_gt = jnp.where(take, n_gt, n_gt + ones)
        alive = tuple(jnp.where(take, hit[c], alive[c] ^ hit[c]) for c in range(NCK))
        return thr_u, n_gt, alive

    zero = jnp.zeros((1, TQ), I32)
    alive0 = tuple(jnp.full((8, TQ), -1, I32) for _ in range(NCK))
    thr_u, n_gt, alive = lax.fori_loop(0, NPL, plane_body, (zero, zero, alive0))
    n_eq = jnp.sum(_tree_sum([lax.population_count(a) for a in alive]), axis=0, keepdims=True)
    thr = thr_u ^ INT_MIN

    def count(pred_fn):
        def body(c, acc):
            s0 = pl.multiple_of(c * CC, CC)
            kk = keys_ref[pl.ds(s0, CC), :]
            hit = jnp.where(pred_fn(kk, s0), 1, 0).astype(I32)
            return acc + jnp.sum(hit.reshape(CC // 8, 8, TQ), axis=0)
        acc = lax.fori_loop(0, npair, body, jnp.zeros((8, TQ), I32))
        return jnp.sum(acc, axis=0, keepdims=True)

    excess = n_gt + n_eq - kvec
    has_excess = jnp.max(excess) > 0

    @pl.when(has_excess)
    def _():
        need = kvec - n_gt

        def tie_lt(cut):
            return count(lambda kk, s0: (kk == thr) & (s0 + krow2 < cut))

        def cut_body(b, cut):
            cand = cut | (jnp.int32(1) << (12 - b))
            return jnp.where(tie_lt(cand) <= need, cand, cut)

        cut = lax.fori_loop(0, 13, cut_body, jnp.zeros((1, TQ), I32))

        def drop_body(c, carry):
            s0 = pl.multiple_of(c * CH, CH)
            kk = keys_ref[pl.ds(s0, CH), :]
            keys_ref[pl.ds(s0, CH), :] = jnp.where((kk == thr) & (s0 + krow >= cut), INT_MIN, kk)
            return carry

        lax.fori_loop(0, nch, drop_body, 0)

    acc_ref[...] = jnp.zeros_like(acc_ref)

    NP = DSA_HEADS // 2
    PW = 2 * TQ

    AB = DSA_AB
    NB = CH // AB
    last_chunk = keys_ref.shape[0] // CH - 1

    def logits(c, st_ref, cm_ref):
        s0 = pl.multiple_of(c * CH, CH)
        zoff = pl.multiple_of(jnp.clip(s0 - t0 + DSA_Z0, 0, DSA_Z0), 8)
        kaug = []
        for rb in range(NB):
            rs = pl.ds(s0 + rb * AB, AB)
            pen = jnp.where(keys_ref[rs, :] >= thr, 0.0, NEG_BIG).astype(BF16)
            kaug.append(jnp.concatenate([pen, k_ref[rs, :]], axis=1))
        for hp in range(NP):
            ps = slice(hp * PW, (hp + 1) * PW)
            cm = None
            for rb in range(NB):
                st = jnp.dot(kaug[rb], qaug_ref[:, ps], preferred_element_type=F32)
                st = st + bias_ref[pl.ds(zoff + rb * AB, AB), ps]
                st_ref[rb * AB:(rb + 1) * AB, ps] = st
                tm = jnp.max(st.reshape(AB // 8, 8, PW), axis=0)
                cm = tm if cm is None else jnp.maximum(cm, tm)
            cm_ref[0:1, ps] = jnp.max(cm, axis=0, keepdims=True)

    def accumulate(c, st_ref, cm_ref):
        vtc = vt_ref[c]
        for hp in range(NP):
            ps = slice(hp * PW, (hp + 1) * PW)
            m_old = m_ref[0:1, ps]
            m_new = jnp.maximum(m_old, cm_ref[0:1, ps])
            m_ref[0:1, ps] = m_new
            alpha = jnp.exp2(m_old - m_new)
            p = jnp.concatenate(
                [jnp.exp2(st_ref[rb * AB:(rb + 1) * AB, ps] - m_new).astype(BF16) for rb in range(NB)], axis=0)
            acc_ref[:, ps] = acc_ref[:, ps] * alpha + jnp.dot(vtc, p, preferred_element_type=F32)

    def att_body(cp, carry):
        c0 = 2 * cp
        logits(c0 + 1, st1_ref, cm1_ref)
        accumulate(c0, st0_ref, cm0_ref)
        logits(jnp.minimum(c0 + 2, last_chunk), st0_ref, cm0_ref)
        accumulate(c0 + 1, st1_ref, cm1_ref)
        return carry

    m_ref[...] = jnp.full(m_ref.shape, NEG_BIG, F32)
    logits(0, st0_ref, cm0_ref)
    lax.fori_loop(0, npair, att_body, 0)

    ot = jnp.concatenate(
        [(acc_ref[0:HEAD_DIM, h * TQ:(h + 1) * TQ] / acc_ref[HEAD_DIM:HEAD_DIM + 1, h * TQ:(h + 1) * TQ]).astype(BF16)
         for h in range(DSA_HEADS)], axis=0)
    yt = jnp.dot(wot_ref[...], ot, preferred_element_type=F32)
    o_ref[...] = yt.T


def _dsa(p1, p2, vt, bias_t, wot, bsz, seq):
    tq = DSA_TQ
    p1v = p1.reshape(bsz, seq, W1)
    p2v = p2.reshape(bsz, seq, W2)
    nq = DSA_HEADS * tq
    in_specs = [
        pl.BlockSpec((None, tq, 768), lambda b, i: (b, i, P1_CQ // 768)),
        pl.BlockSpec((None, tq, 512), lambda b, i: (b, i, P2_IQ // 512)),
        pl.BlockSpec((None, tq, LANES), lambda b, i: (b, i, P2_IW // LANES)),
        pl.BlockSpec((None, seq, LANES), lambda b, i: (b, 0, P1_CK // LANES)),
        pl.BlockSpec((None, seq, LANES), lambda b, i: (b, 0, P2_IK // LANES)),
        pl.BlockSpec((None, seq // DSA_CH, DSA_VR, DSA_CH), lambda b, i: (b, 0, 0, 0)),
        pl.BlockSpec((DSA_Z, nq), lambda b, i: (0, 0)),
        pl.BlockSpec((D_MODEL, DSA_HEADS * HEAD_DIM), lambda b, i: (0, 0)),
    ]
    out = pl.pallas_call(
        functools.partial(_dsa_kernel, topk=min(IDX_TOPK, seq // 4)),
        grid=(bsz, seq // tq),
        in_specs=in_specs,
        out_specs=pl.BlockSpec((None, tq, D_MODEL), lambda b, i: (b, i, 0)),
        out_shape=jax.ShapeDtypeStruct((bsz, seq, D_MODEL), F32),
        scratch_shapes=[
            pltpu.VMEM((seq, tq), I32),
            pltpu.VMEM((seq // DSA_CH, 32, 8, tq), I32),
            pltpu.VMEM((2 * tq, nq), BF16),
            pltpu.VMEM((LANES, IDX_HEADS * tq), BF16),
            pltpu.VMEM((DSA_VR, nq), F32),
            pltpu.VMEM((DSA_CH, nq), F32),
            pltpu.VMEM((DSA_CH, nq), F32),
            pltpu.VMEM((8, nq), F32),
            pltpu.VMEM((8, nq), F32),
            pltpu.VMEM((8, nq), F32),
        ],
        compiler_params=_cparams(("parallel", "arbitrary")),
        name="dsa",
    )(p1v, p2v, p2v, p1v, p2v, vt, bias_t, wot)
    return out.reshape(bsz * seq, D_MODEL)


MERGE_TM = 512


def _merge_kernel(h_ref, oa0_ref, oa1_ref, oa2_ref, sa0_ref, sa1_ref, sa2_ref, ob_ref, yc_ref, g_ref,
                  woa_ref, wob_ref, wo_ref, o_ref):
    H = DSWA_HPG
    sts = [sa0_ref[...], sa1_ref[...], sa2_ref[...]]
    oas = [oa0_ref, oa1_ref, oa2_ref]
    ms = [s[:, 0:H] for s in sts]
    ls = [s[:, H:2 * H] for s in sts]
    mmax = jnp.maximum(jnp.maximum(ms[0], ms[1]), ms[2])
    wts = [l * jnp.exp(m - mmax) for m, l in zip(ms, ls)]
    tot = wts[0] + wts[1] + wts[2]
    wts = [w / tot for w in wts]
    parts = []
    for h in range(H):
        sl = slice(h * HEAD_DIM, (h + 1) * HEAD_DIM)
        acc = wts[0][:, h:h + 1] * oas[0][:, sl]
        acc = acc + wts[1][:, h:h + 1] * oas[1][:, sl]
        acc = acc + wts[2][:, h:h + 1] * oas[2][:, sl]
        parts.append(acc)
    oa = jnp.concatenate(parts, axis=1).astype(BF16)
    y_a = jnp.dot(oa, woa_ref[...], preferred_element_type=F32)
    y_b = jnp.dot(ob_ref[...], wob_ref[...], preferred_element_type=F32)
    y_c = yc_ref[...]
    D = D_MODEL
    mix = (g_ref[:, 0:D].astype(F32) * y_a + g_ref[:, D:2 * D].astype(F32) * y_b
           + g_ref[:, 2 * D:3 * D].astype(F32) * y_c)
    o_ref[...] = h_ref[...] + jnp.dot(mix.astype(BF16), wo_ref[...], preferred_element_type=F32)


def _merge(h, oas, sas, ob, yc, gates, woa, wob, wo):
    m = h.shape[0]
    tm = MERGE_TM
    row = lambda w: pl.BlockSpec((tm, w), lambda i: (i, 0))
    full = lambda a: pl.BlockSpec(a.shape, lambda i: (0, 0))
    in_specs = ([row(D_MODEL)] + [row(256)] * 3 + [row(LANES)] * 3 + [row(1024), row(D_MODEL), row(3 * D_MODEL)]
                + [full(woa), full(wob), full(wo)])
    return pl.pallas_call(
        _merge_kernel,
        grid=(m // tm,),
        in_specs=in_specs,
        out_specs=row(D_MODEL),
        out_shape=jax.ShapeDtypeStruct((m, D_MODEL), F32),
        compiler_params=_cparams(("parallel",)),
        name="merge",
    )(h, *oas, *sas, ob, yc, gates, woa, wob, wo)


FFN_TM = 1024
FFN_TF = 256


def _ffn_kernel(h_ref, g_ref, wg_ref, wu_ref, wd_ref, o_ref, u_ref, acc_ref):
    j = pl.program_id(1)

    @pl.when(j == 0)
    def _():
        x = h_ref[...]
        ms = jnp.mean(x * x, axis=-1, keepdims=True)
        u_ref[...] = (x * lax.rsqrt(ms + RMS_EPS) * g_ref[...]).astype(BF16)
        acc_ref[...] = jnp.zeros_like(acc_ref)

    u = u_ref[...]
    a = jnp.dot(u, wg_ref[...], preferred_element_type=F32)
    b = jnp.dot(u, wu_ref[...], preferred_element_type=F32)
    t = (a * jax.nn.sigmoid(a) * b).astype(BF16)
    acc_ref[...] += jnp.dot(t, wd_ref[...], preferred_element_type=F32)

    @pl.when(j == pl.num_programs(1) - 1)
    def _():
        o_ref[...] = h_ref[...] + acc_ref[...]


def _ffn(h, g, wg, wu, wd):
    m, d = h.shape
    f = wg.shape[1]
    tm, tf = FFN_TM, FFN_TF
    return pl.pallas_call(
        _ffn_kernel,
        grid=(m // tm, f // tf),
        in_specs=[
            pl.BlockSpec((tm, d), lambda i, j: (i, 0)),
            pl.BlockSpec((1, d), lambda i, j: (0, 0)),
            pl.BlockSpec((d, tf), lambda i, j: (0, j)),
            pl.BlockSpec((d, tf), lambda i, j: (0, j)),
            pl.BlockSpec((tf, d), lambda i, j: (j, 0)),
        ],
        out_specs=pl.BlockSpec((tm, d), lambda i, j: (i, 0)),
        out_shape=jax.ShapeDtypeStruct((m, d), F32),
        scratch_shapes=[pltpu.VMEM((tm, d), BF16), pltpu.VMEM((tm, d), F32)],
        compiler_params=_cparams(("parallel", "arbitrary")),
        name="ffn",
    )(h, g, wg, wu, wd)


PLE_TM = 512


def _ple_kernel(h_ref, p_ref, g_ref, wg_ref, wp_ref, o_ref):
    x = h_ref[...]
    ms = jnp.mean(x * x, axis=-1, keepdims=True)
    e = (x * lax.rsqrt(ms + RMS_EPS) * g_ref[...]).astype(BF16)
    gate = jax.nn.sigmoid(jnp.dot(e, wg_ref[...], preferred_element_type=F32))
    proj = jnp.dot(p_ref[...].astype(BF16), wp_ref[...], preferred_element_type=F32)
    o_ref[...] = x + gate * proj


def _ple(h, p, g, wg, wp):
    m, d = h.shape
    tm = PLE_TM
    return pl.pallas_call(
        _ple_kernel,
        grid=(m // tm,),
        in_specs=[
            pl.BlockSpec((tm, d), lambda i: (i, 0)),
            pl.BlockSpec((tm, PLE_DIM), lambda i: (i, 0)),
            pl.BlockSpec((1, d), lambda i: (0, 0)),
            pl.BlockSpec((d, d), lambda i: (0, 0)),
            pl.BlockSpec((PLE_DIM, d), lambda i: (0, 0)),
        ],
        out_specs=pl.BlockSpec((tm, d), lambda i: (i, 0)),
        out_shape=jax.ShapeDtypeStruct((m, d), F32),
        compiler_params=_cparams(("parallel",)),
        name="ple",
    )(h, p, g, wg, wp)


def _rel_bucket(dist):
    max_exact = REL_BUCKETS // 2
    d = jnp.maximum(dist, 0)
    df = jnp.maximum(d, 1).astype(F32)
    large = max_exact + (jnp.log(df / max_exact) / math.log(REL_MAX_DIST / max_exact)
                         * (REL_BUCKETS - max_exact)).astype(I32)
    large = jnp.minimum(large, REL_BUCKETS - 1)
    return jnp.where(d < max_exact, d, large)


def _toeplitz(rev, n_rows, n_cols):
    nh = rev.shape[0]
    lw = rev.shape[1] + 1
    w = jnp.pad(rev, ((0, 0), (0, 1)))
    s = jnp.broadcast_to(w[:, None, :], (nh, n_rows, lw)).reshape(nh, n_rows * lw)
    s = s[:, :n_rows * (lw - 1)].reshape(nh, n_rows, lw - 1)
    return s[:, :, n_rows - 1:n_rows - 1 + n_cols]


def _bias_tables(rel_bias):
    blk = DSWA_BLOCK
    dswa = []
    for g, (_, r) in enumerate(DSWA_PATTERNS):
        delta = np.arange(3 * blk - 1)[::-1] - (blk - 1)
        rev = rel_bias[_rel_bucket(jnp.asarray(delta * r, I32))][:, g * DSWA_HPG:(g + 1) * DSWA_HPG]
        dswa.append(_toeplitz(rev.T, blk, 2 * blk))
    tq = DSA_TQ
    dist = np.arange(DSA_Z + tq - 1)[::-1] - (DSA_Z - 1) + DSA_Z0
    rev = rel_bias[_rel_bucket(jnp.asarray(dist, I32))][:, DSWA_HEADS:] * math.log2(math.e)
    bias_t = jnp.transpose(_toeplitz(rev.T, tq, DSA_Z), (2, 0, 1)).reshape(DSA_Z, DSA_HEADS * tq)
    return dswa, bias_t


def _pad_cols(w, width):
    return jnp.pad(w, ((0, 0), (0, width - w.shape[1])))


def _layer_params(w_in, qn_a, kn_a, qn_c, kn_c, w_alpha2, b_alpha):
    offs = np.cumsum((0,) + IN_WIDTHS)
    parts = [w_in[:, offs[i]:offs[i + 1]] for i in range(len(IN_WIDTHS))]
    a_q, a_k, a_v, b_q, b_k, b_v, b_r, b_al, c_q, c_k, c_v, i_q, i_k, i_w = parts
    w1 = _pad_cols(jnp.concatenate([a_q, a_k, c_q, c_k], axis=1), W1).astype(BF16)
    w2 = _pad_cols(jnp.concatenate(
        [b_v, b_r, b_q, b_k, i_q, a_v, _pad_cols(b_al, LANES), _pad_cols(c_v, LANES),
         _pad_cols(i_k, LANES), _pad_cols(i_w, LANES)], axis=1), W2).astype(BF16)
    scale = HEAD_DIM ** -0.5
    gain1 = jnp.concatenate([jnp.tile(qn_a, DSWA_HEADS) * scale, jnp.tile(kn_a, DSWA_HEADS),
                             jnp.tile(qn_c, DSA_HEADS) * (scale * math.log2(math.e)), kn_c])
    gain1 = jnp.pad(gain1, (0, W1 - gain1.shape[0])).reshape(1, W1)
    wa = jnp.pad(w_alpha2, ((0, LANES - GLA_RANK), (0, 0))).astype(BF16)
    return w1, w2, gain1, wa, b_alpha.reshape(1, -1)


def kernel(x, p, rel_bias, norm_mix, w_in, qn_a, kn_a, qn_c, kn_c, w_alpha2, b_alpha, gla_norm, w_out_a, w_out_b, w_out_c, w_gate, b_gate, w_o, norm_ffn, w_ffn_gate, w_ffn_up, w_ffn_down, norm_ple, w_ple_gate, w_ple_proj):
    bsz, seq, d = x.shape
    depth = p.shape[0]
    m = bsz * seq
    dswa_bias, bias_t = _bias_tables(rel_bias)
    h = x.reshape(m, d)
    zeros_w2 = jnp.zeros((1, W2), F32)
    for i in range(depth):
        w1, w2, gain1, wa, ba = _layer_params(w_in[i], qn_a[i], kn_a[i], qn_c[i], kn_c[i], w_alpha2[i], b_alpha[i])
        gmix = norm_mix[i].reshape(1, d)
        p1 = _proj(h, gmix, w1, gain1, "qk")
        p2 = _proj(h, gmix, w2, zeros_w2, "plain")
        gates = _proj(h, gmix, w_gate[i].astype(BF16), b_gate[i].reshape(1, -1), "gate")
        oas, sas = [], []
        for g, (_, r) in enumerate(DSWA_PATTERNS):
            o, st = _dswa(p1, p2, dswa_bias[g], g, r, bsz, seq)
            oas.append(o)
            sas.append(st)
        ob = _gla(p2, wa, ba, gla_norm[i].reshape(1, -1), bsz, seq)
        cv = p2[:, P2_CV:P2_CV + HEAD_DIM].reshape(bsz, seq // DSA_CH, DSA_CH, HEAD_DIM)
        ones_pad = jnp.zeros((bsz, seq // DSA_CH, DSA_VR - HEAD_DIM, DSA_CH), BF16).at[:, :, 0, :].set(1.0)
        vt = jnp.concatenate([jnp.transpose(cv, (0, 1, 3, 2)), ones_pad], axis=2)
        yc = _dsa(p1, p2, vt, bias_t, w_out_c[i].T.astype(BF16), bsz, seq)
        h = _merge(h, oas, sas, ob, yc, gates, w_out_a[i].astype(BF16), w_out_b[i].astype(BF16),
                   w_o[i].astype(BF16))
        h = _ffn(h, norm_ffn[i].reshape(1, d), w_ffn_gate[i].astype(BF16), w_ffn_up[i].astype(BF16),
                 w_ffn_down[i].astype(BF16))
        h = _ple(h, p[i].reshape(m, PLE_DIM), norm_ple[i].reshape(1, d), w_ple_gate[i].astype(BF16),
                 w_ple_proj[i].astype(BF16))
    return h.reshape(bsz, seq, d)
```

```python
import functools
import math

import numpy as np
import jax
import jax.numpy as jnp
from jax import lax
from jax.experimental import pallas as pl
from jax.experimental.pallas import tpu as pltpu

F32 = jnp.float32
BF16 = jnp.bfloat16
I32 = jnp.int32
I16 = jnp.int16

D_MODEL = 1024
HEAD_DIM = 64
RMS_EPS = 1e-6
DSWA_PATTERNS = ((128, 1), (512, 4), (2048, 16))
DSWA_HPG = 4
DSWA_HEADS = 12
DSWA_BLOCK = 128
GLA_HEADS = 4
GLA_DK = 128
GLA_DV = 256
GLA_RANK = 16
GLA_TAU = 16.0
GLA_CHUNK = 64
DSA_HEADS = 12
IDX_HEADS = 8
IDX_DIM = 64
IDX_TOPK = 256
REL_BUCKETS = 32
REL_MAX_DIST = 2048
D_FF = 2816
PLE_DIM = 256
IN_WIDTHS = (768, 768, 768, 512, 512, 1024, 1024, 16, 768, 64, 64, 512, 64, 8)

LANES = 128
MXU_N = 256
VMEM_LIMIT = 56 * 1024 * 1024

W1 = 2560
P1_AQ, P1_AK, P1_CQ, P1_CK = 0, 768, 1536, 2304
W2 = 5120
P2_BV, P2_BR, P2_BQ, P2_BK, P2_IQ, P2_AV = 0, 1024, 2048, 2560, 3072, 3584
P2_BAL, P2_CV, P2_IK, P2_IW = 4352, 4480, 4608, 4736

PROJ_TM = 512
PROJ_TN = 512

DSWA_QB = 4
DSA_TQ = 128
DSA_CH = 256
DSA_AB = 256
DSA_VR = 80
DSA_BIAS_CONST_FROM = 1512
DSA_Z0 = 1792
DSA_Z = DSA_Z0 + DSA_CH
INT_MIN = -(2 ** 31)
NEG_BIG = -1e30


def _cparams(sem, flags=None):
    return pltpu.CompilerParams(dimension_semantics=sem, vmem_limit_bytes=VMEM_LIMIT, flags=flags)


def _proj_kernel(h_ref, g_ref, w_ref, e_ref, *rest, mode, tn):
    if mode == "qk":
        bd_ref, o_ref = rest
    else:
        (o_ref,) = rest
    x = h_ref[...]
    ms = jnp.mean(x * x, axis=-1, keepdims=True)
    u = (x * lax.rsqrt(ms + RMS_EPS) * g_ref[...]).astype(BF16)
    for j in range(w_ref.shape[1] // tn):
        cs = slice(j * tn, (j + 1) * tn)
        acc = jnp.dot(u, w_ref[:, cs], preferred_element_type=F32)
        if mode == "plain":
            out = acc
        elif mode == "gate":
            out = jax.nn.sigmoid(acc + e_ref[:, cs])
        else:
            sq = acc * acc
            hi = sq.astype(BF16)
            lo = (sq - hi.astype(F32)).astype(BF16)
            ss = (jnp.dot(hi, bd_ref[...], preferred_element_type=F32)
                  + jnp.dot(lo, bd_ref[...], preferred_element_type=F32))
            out = acc * lax.rsqrt(ss * (1.0 / HEAD_DIM) + RMS_EPS) * e_ref[:, cs]
        o_ref[:, cs] = out.astype(o_ref.dtype)


def _proj(h, g, w, e, mode):
    m, d = h.shape
    n = w.shape[1]
    tm, tn = PROJ_TM, PROJ_TN
    const = lambda shape: pl.BlockSpec(shape, lambda i: (0, 0), pipeline_mode=pl.Buffered(1))
    in_specs = [pl.BlockSpec((tm, d), lambda i: (i, 0)), const((1, d)), const((d, n)), const((1, n))]
    args = [h, g, w, e]
    if mode == "qk":
        r = np.arange(tn) // HEAD_DIM
        bd = jnp.asarray((r[:, None] == r[None, :]).astype(np.float32), dtype=BF16)
        in_specs.append(const((tn, tn)))
        args.append(bd)
    return pl.pallas_call(
        functools.partial(_proj_kernel, mode=mode, tn=tn),
        grid=(m // tm,),
        in_specs=in_specs,
        out_specs=pl.BlockSpec((tm, n), lambda i: (i, 0)),
        out_shape=jax.ShapeDtypeStruct((m, n), BF16),
        compiler_params=_cparams(("parallel",)),
        name="proj_" + mode,
    )(*args)


def _dswa_kernel(q_ref, kp_ref, kc_ref, vp_ref, vc_ref, bias_ref, o_ref, st_ref, *, qb):
    i = pl.program_id(2)
    blk = DSWA_BLOCK
    row = lax.broadcasted_iota(I32, (blk, 2 * blk), 0)
    col = lax.broadcasted_iota(I32, (blk, 2 * blk), 1)
    cur_ok = (col >= blk) & ((col - blk) <= row)
    valid_inner = ((col < blk) & (col >= row)) | cur_ok
    prev_off = jnp.where(i > 0, 0, 4 * blk)
    valid_first = ((col < blk) & (col >= row + prev_off)) | cur_ok
    lane = lax.broadcasted_iota(I32, (blk, LANES), 1)
    for j in range(qb):
        rs = slice(j * blk, (j + 1) * blk)
        ps = slice((j - 1) * blk, j * blk)
        valid = valid_first if j == 0 else valid_inner
        q = q_ref[rs, :]
        k = jnp.concatenate([kp_ref[...] if j == 0 else kc_ref[ps, :], kc_ref[rs, :]], axis=0)
        v = jnp.concatenate([vp_ref[...] if j == 0 else vc_ref[ps, :], vc_ref[rs, :]], axis=0)
        stats = jnp.zeros((blk, LANES), F32)
        for h in range(DSWA_HPG):
            sl = slice(h * HEAD_DIM, (h + 1) * HEAD_DIM)
            s = lax.dot_general(q[:, sl], k[:, sl], (((1,), (1,)), ((), ())), preferred_element_type=F32)
            s = jnp.where(valid, s + bias_ref[h], -jnp.inf)
            m = jnp.max(s, axis=-1, keepdims=True)
            p = jnp.exp(s - m)
            l = jnp.sum(p, axis=-1, keepdims=True)
            o = jnp.dot(p.astype(BF16), v[:, sl], preferred_element_type=F32) / l
            o_ref[rs, sl] = o
            stats = jnp.where(lane == h, m, stats)
            stats = jnp.where(lane == DSWA_HPG + h, l, stats)
        st_ref[rs, :] = stats


def _dswa(p1, p2, bias, g, r, bsz, seq):
    blk = DSWA_BLOCK
    L = seq // r
    nblk = L // blk
    gw = DSWA_HPG * HEAD_DIM
    if r == 1:
        p1v = p1.reshape(bsz, L, W1)
        p2v = p2.reshape(bsz, L, W2)
        qv, kv, vv = p1v, p1v, p2v
        q_col = lambda c: (P1_AQ // gw) + g
        k_col = lambda c: (P1_AK // gw) + g
        v_col = lambda c: (P2_AV // gw) + g
    else:
        def regroup(p, off):
            return p[:, off + g * gw:off + (g + 1) * gw].reshape(bsz, L, r * gw)
        qv, kv, vv = regroup(p1, P1_AQ), regroup(p1, P1_AK), regroup(p2, P2_AV)
        q_col = k_col = v_col = lambda c: c
    qb = min(DSWA_QB, nblk)
    prev = lambda i: jnp.maximum(i * qb - 1, 0)
    in_specs = [
        pl.BlockSpec((None, qb * blk, gw), lambda b, c, i: (b, i, q_col(c))),
        pl.BlockSpec((None, blk, gw), lambda b, c, i: (b, prev(i), k_col(c))),
        pl.BlockSpec((None, qb * blk, gw), lambda b, c, i: (b, i, k_col(c))),
        pl.BlockSpec((None, blk, gw), lambda b, c, i: (b, prev(i), v_col(c))),
        pl.BlockSpec((None, qb * blk, gw), lambda b, c, i: (b, i, v_col(c))),
        pl.BlockSpec((DSWA_HPG, blk, 2 * blk), lambda b, c, i: (0, 0, 0)),
    ]
    out_specs = [
        pl.BlockSpec((None, qb * blk, gw), lambda b, c, i: (b, i, c)),
        pl.BlockSpec((None, qb * blk, LANES), lambda b, c, i: (b, i, c)),
    ]
    o, st = pl.pallas_call(
        functools.partial(_dswa_kernel, qb=qb),
        grid=(bsz, r, nblk // qb),
        in_specs=in_specs,
        out_specs=out_specs,
        out_shape=[jax.ShapeDtypeStruct((bsz, L, r * gw), F32),
                   jax.ShapeDtypeStruct((bsz, L, r * LANES), F32)],
        compiler_params=_cparams(("parallel", "parallel", "arbitrary")),
        name="dswa_g%d" % g,
    )(qv, kv, kv, vv, vv, bias)
    return o.reshape(bsz * seq, gw), st.reshape(bsz * seq, LANES)


GLA_TC = 256
GLA_GB = 2


def _split3(x):
    a1 = x.astype(BF16)
    r1 = x - a1.astype(F32)
    a2 = r1.astype(BF16)
    a3 = (r1 - a2.astype(F32)).astype(BF16)
    return a1, a2, a3


def _gla_kernel(v_ref, r_ref, q_ref, k_ref, al_ref, wa_ref, ba_ref, gn_ref, o_ref, st_ref, *, gb):
    C = GLA_CHUNK

    @pl.when(pl.program_id(1) == 0)
    def _():
        st_ref[...] = jnp.zeros_like(st_ref)

    row = lax.broadcasted_iota(I32, (C, C), 0)
    col = lax.broadcasted_iota(I32, (C, C), 1)
    tri = row >= col
    tri_bf = jnp.where(tri, 1.0, 0.0).astype(BF16)
    nt = (((1,), (1,)), ((), ()))
    tn = (((0,), (0,)), ((), ()))
    for c, bb in [(c, bb) for c in range(GLA_TC // C) for bb in range(gb)]:
        rs = slice(c * C, (c + 1) * C)
        z = jnp.dot(al_ref[bb, rs, :], wa_ref[...], preferred_element_type=F32) + ba_ref[...]
        la = (jnp.minimum(z, 0.0) - jnp.log(1.0 + jnp.exp(-jnp.abs(z)))) * (1.0 / GLA_TAU)
        a1, a2, a3 = _split3(la)
        bcum = (jnp.dot(tri_bf, a1, preferred_element_type=F32)
                + jnp.dot(tri_bf, a2, preferred_element_type=F32)
                + jnp.dot(tri_bf, a3, preferred_element_type=F32))
        blast = bcum[C - 1:C, :]
        qf = q_ref[bb, rs, :].astype(F32) * (GLA_DK ** -0.5)
        kf = k_ref[bb, rs, :].astype(F32)
        q_in = (qf * jnp.exp(bcum)).astype(BF16)
        k_in = (kf * jnp.exp(-bcum)).astype(BF16)
        k_end = (kf * jnp.exp(blast - bcum)).astype(BF16)
        dec = jnp.exp(blast)
        for h in range(GLA_HEADS):
            ks = slice(h * GLA_DK, (h + 1) * GLA_DK)
            vs = slice(h * GLA_DV, (h + 1) * GLA_DV)
            vh = v_ref[bb, rs, vs]
            att = lax.dot_general(q_in[:, ks], k_in[:, ks], nt, preferred_element_type=F32)
            att = jnp.where(tri, att, 0.0)
            o = jnp.dot(att.astype(BF16), vh, preferred_element_type=F32)
            st = st_ref[bb, h]
            o = o + lax.dot_general(q_in[:, ks], st.astype(BF16), nt, preferred_element_type=F32)
            st_ref[bb, h] = st * dec[:, ks] + lax.dot_general(vh, k_end[:, ks], tn, preferred_element_type=F32)
            ms = jnp.mean(o * o, axis=-1, keepdims=True)
            y = o * lax.rsqrt(ms + RMS_EPS) * gn_ref[...]
            rg = r_ref[bb, rs, vs].astype(F32)
            y = y * (rg * jax.nn.sigmoid(rg))
            o_ref[bb, rs, vs] = y.astype(o_ref.dtype)


def _gla(p2, wa, ba, gn, bsz, seq):
    tc = GLA_TC
    gb = GLA_GB if bsz % GLA_GB == 0 else 1
    p2v = p2.reshape(bsz, seq, W2)
    in_specs = [
        pl.BlockSpec((gb, tc, 1024), lambda b, t: (b, t, P2_BV // 1024)),
        pl.BlockSpec((gb, tc, 1024), lambda b, t: (b, t, P2_BR // 1024)),
        pl.BlockSpec((gb, tc, 512), lambda b, t: (b, t, P2_BQ // 512)),
        pl.BlockSpec((gb, tc, 512), lambda b, t: (b, t, P2_BK // 512)),
        pl.BlockSpec((gb, tc, LANES), lambda b, t: (b, t, P2_BAL // LANES)),
        pl.BlockSpec((LANES, 512), lambda b, t: (0, 0)),
        pl.BlockSpec((1, 512), lambda b, t: (0, 0)),
        pl.BlockSpec((1, GLA_DV), lambda b, t: (0, 0)),
    ]
    out = pl.pallas_call(
        functools.partial(_gla_kernel, gb=gb),
        grid=(bsz // gb, seq // tc),
        in_specs=in_specs,
        out_specs=pl.BlockSpec((gb, tc, 1024), lambda b, t: (b, t, 0)),
        out_shape=jax.ShapeDtypeStruct((bsz, seq, 1024), BF16),
        scratch_shapes=[pltpu.VMEM((gb, GLA_HEADS, GLA_DV, GLA_DK), F32)],
        compiler_params=_cparams(("parallel", "arbitrary")),
        name="gla",
    )(p2v, p2v, p2v, p2v, p2v, wa, ba, gn)
    return out.reshape(bsz * seq, 1024)


def _tree_sum(xs):
    xs = list(xs)
    while len(xs) > 1:
        xs = [xs[i] + xs[i + 1] for i in range(0, len(xs) - 1, 2)] + ([xs[-1]] if len(xs) % 2 else [])
    return xs[0]


def _dsa_kernel(q_ref, iq_ref, iw_ref, k_ref, ik_ref, vt_ref, bias_ref, wot_ref, o_ref,
                keys_ref, planes_ref, qaug_ref, iqall_ref, acc_ref, st0_ref, st1_ref, m_ref, cm0_ref, cm1_ref,
                *, topk):
    TQ, CH = DSA_TQ, DSA_CH
    CC = 2 * CH
    qi = pl.program_id(1)
    t0 = qi * TQ
    nch = qi // (CH // TQ) + 1
    npair = (nch + 1) // 2
    nt = (((1,), (1,)), ((), ()))

    eye = (lax.broadcasted_iota(I32, (TQ, TQ), 0) == lax.broadcasted_iota(I32, (TQ, TQ), 1))
    eye = jnp.where(eye, 1.0, 0.0).astype(BF16)
    q_t = q_ref[...].astype(F32).T.astype(BF16)
    iq_t = iq_ref[...].astype(F32).T.astype(BF16)
    for h in range(DSA_HEADS):
        cs = slice(h * TQ, (h + 1) * TQ)
        qaug_ref[0:TQ, cs] = eye
        qaug_ref[TQ:TQ + HEAD_DIM, cs] = q_t[h * HEAD_DIM:(h + 1) * HEAD_DIM, :]
        qaug_ref[TQ + HEAD_DIM:, cs] = jnp.zeros((TQ - HEAD_DIM, TQ), BF16)
    for h in range(IDX_HEADS):
        cs = slice(h * TQ, (h + 1) * TQ)
        iqall_ref[0:IDX_DIM, cs] = iq_t[h * IDX_DIM:(h + 1) * IDX_DIM, :]
        iqall_ref[IDX_DIM:, cs] = jnp.zeros((LANES - IDX_DIM, TQ), BF16)
    idx_scale = (IDX_HEADS ** -0.5) * (IDX_DIM ** -0.5)
    wt = (iw_ref[...].astype(F32) * idx_scale).T

    qpos = t0 + lax.broadcasted_iota(I32, (1, TQ), 1)
    RB = 128
    krow1 = lax.broadcasted_iota(I32, (RB, TQ), 0)
    krow = lax.broadcasted_iota(I32, (CH, TQ), 0)
    krow2 = lax.broadcasted_iota(I32, (CC, TQ), 0)

    def score_body(cp, carry):
        for sub in range(CC // RB):
            s0 = pl.multiple_of(cp * CC + sub * RB, RB)
            ikc = ik_ref[pl.ds(s0, RB), :]
            s = None
            for hp in range(IDX_HEADS // 2):
                x = jnp.dot(ikc, iqall_ref[:, hp * 2 * TQ:(hp + 1) * 2 * TQ],
                            preferred_element_type=F32)
                t = (jnp.maximum(x[:, :TQ], 0.0) * wt[2 * hp:2 * hp + 1, :]
                     + jnp.maximum(x[:, TQ:], 0.0) * wt[2 * hp + 1:2 * hp + 2, :])
                s = t if s is None else s + t
            s = jnp.where(s == 0.0, 0.0, s)
            bits = pltpu.bitcast(s, I32)
            key = bits ^ ((bits >> 31) & 0x7FFFFFFF)
            key = jnp.where(s0 + krow1 <= qpos, key, INT_MIN)
            keys_ref[pl.ds(s0, RB), :] = key
        return carry

    lax.fori_loop(0, npair, score_body, 0)

    kvec = jnp.minimum(topk, qpos + 1)
    NPL = 32

    def planes_body(c, carry):
        s0 = pl.multiple_of(c * CH, CH)
        a = [keys_ref[pl.ds(s0 + 8 * j, 8), :] ^ INT_MIN for j in range(NPL)]
        j, m = 16, 0x0000FFFF
        while j:
            sh = jnp.full((8, TQ), j, I32)
            k = 0
            while k < NPL:
                t = (a[k] ^ lax.shift_right_logical(a[k + j], sh)) & m
                a[k] = a[k] ^ t
                a[k + j] = a[k + j] ^ lax.shift_left(t, sh)
                k = (k + j + 1) & ~j
            j >>= 1
            m = (m ^ (m << j)) & 0xFFFFFFFF
        for p in range(NPL):
            planes_ref[c, p] = a[p]
        return carry

    def empty_body(c, carry):
        for p in range(NPL):
            planes_ref[c, p] = jnp.zeros((8, TQ), I32)
        return carry

    nck = 2 * npair
    NCK = keys_ref.shape[0] // CH
    lax.fori_loop(0, nck, planes_body, 0)
    lax.fori_loop(nck, NCK, empty_body, 0)

    def plane_body(p, carry):
        thr_u, n_gt, alive = carry
        hit = [alive[c] & planes_ref[c, p] for c in range(NCK)]
        ones = jnp.sum(_tree_sum([lax.population_count(h) for h in hit]), axis=0, keepdims=True)
        take = (n_gt + ones) >= kvec
        thr_u = jnp.where(take, thr_u | (jnp.int32(1) << (31 - p)), thr_u)
        n_gt = jnp.where(take, n_gt, n_gt + ones)
        alive = tuple(jnp.where(take, hit[c], alive[c] ^ hit[c]) for c in range(NCK))
        return thr_u, n_gt, alive

    zero = jnp.zeros((1, TQ), I32)
    alive0 = tuple(jnp.full((8, TQ), -1, I32) for _ in range(NCK))
    thr_u, n_gt, alive = lax.fori_loop(0, NPL, plane_body, (zero, zero, alive0))
    n_eq = jnp.sum(_tree_sum([lax.population_count(a) for a in alive]), axis=0, keepdims=True)
    thr = thr_u ^ INT_MIN

    def count(pred_fn):
        def body(c, acc):
            s0 = pl.multiple_of(c * CC, CC)
            kk = keys_ref[pl.ds(s0, CC), :]
            hit = jnp.where(pred_fn(kk, s0), 1, 0).astype(I32)
            return acc + jnp.sum(hit.reshape(CC // 8, 8, TQ), axis=0)
        acc = lax.fori_loop(0, npair, body, jnp.zeros((8, TQ), I32))
        return jnp.sum(acc, axis=0, keepdims=True)

    excess = n_gt + n_eq - kvec
    has_excess = jnp.max(excess) > 0

    @pl.when(has_excess)
    def _():
        need = kvec - n_gt

        def tie_lt(cut):
            return count(lambda kk, s0: (kk == thr) & (s0 + krow2 < cut))

        def cut_body(b, cut):
            cand = cut | (jnp.int32(1) << (12 - b))
            return jnp.where(tie_lt(cand) <= need, cand, cut)

        cut = lax.fori_loop(0, 13, cut_body, jnp.zeros((1, TQ), I32))

        def drop_body(c, carry):
            s0 = pl.multiple_of(c * CH, CH)
            kk = keys_ref[pl.ds(s0, CH), :]
            keys_ref[pl.ds(s0, CH), :] = jnp.where((kk == thr) & (s0 + krow >= cut), INT_MIN, kk)
            return carry

        lax.fori_loop(0, nch, drop_body, 0)

    acc_ref[...] = jnp.zeros_like(acc_ref)

    NP = DSA_HEADS // 2
    PW = 2 * TQ

    AB = DSA_AB
    NB = CH // AB
    last_chunk = keys_ref.shape[0] // CH - 1

    def logits(c, st_ref, cm_ref):
        s0 = pl.multiple_of(c * CH, CH)
        zoff = pl.multiple_of(jnp.clip(s0 - t0 + DSA_Z0, 0, DSA_Z0), 8)
        kaug = []
        for rb in range(NB):
            rs = pl.ds(s0 + rb * AB, AB)
            pen = jnp.where(keys_ref[rs, :] >= thr, 0.0, NEG_BIG).astype(BF16)
            kaug.append(jnp.concatenate([pen, k_ref[rs, :]], axis=1))
        for hp in range(NP):
            ps = slice(hp * PW, (hp + 1) * PW)
            cm = None
            for rb in range(NB):
                st = jnp.dot(kaug[rb], qaug_ref[:, ps], preferred_element_type=F32)
                st = st + bias_ref[pl.ds(zoff + rb * AB, AB), ps]
                st_ref[rb * AB:(rb + 1) * AB, ps] = st
                tm = jnp.max(st.reshape(AB // 8, 8, PW), axis=0)
                cm = tm if cm is None else jnp.maximum(cm, tm)
            cm_ref[0:1, ps] = jnp.max(cm, axis=0, keepdims=True)

    def accumulate(c, st_ref, cm_ref):
        vtc = vt_ref[c]
        for hp in range(NP):
            ps = slice(hp * PW, (hp + 1) * PW)
            m_old = m_ref[0:1, ps]
            m_new = jnp.maximum(m_old, cm_ref[0:1, ps])
            m_ref[0:1, ps] = m_new
            alpha = jnp.exp2(m_old - m_new)
            p = jnp.concatenate(
                [jnp.exp2(st_ref[rb * AB:(rb + 1) * AB, ps] - m_new).astype(BF16) for rb in range(NB)], axis=0)
            acc_ref[:, ps] = acc_ref[:, ps] * alpha + jnp.dot(vtc, p, preferred_element_type=F32)

    def att_body(cp, carry):
        c0 = 2 * cp
        logits(c0 + 1, st1_ref, cm1_ref)
        accumulate(c0, st0_ref, cm0_ref)
        logits(jnp.minimum(c0 + 2, last_chunk), st0_ref, cm0_ref)
        accumulate(c0 + 1, st1_ref, cm1_ref)
        return carry

    m_ref[...] = jnp.full(m_ref.shape, NEG_BIG, F32)
    logits(0, st0_ref, cm0_ref)
    lax.fori_loop(0, npair, att_body, 0)

    ot = jnp.concatenate(
        [(acc_ref[0:HEAD_DIM, h * TQ:(h + 1) * TQ] / acc_ref[HEAD_DIM:HEAD_DIM + 1, h * TQ:(h + 1) * TQ]).astype(BF16)
         for h in range(DSA_HEADS)], axis=0)
    yt = jnp.dot(wot_ref[...], ot, preferred_element_type=F32)
    o_ref[...] = yt.T


def _dsa(p1, p2, vt, bias_t, wot, bsz, seq):
    tq = DSA_TQ
    p1v = p1.reshape(bsz, seq, W1)
    p2v = p2.reshape(bsz, seq, W2)
    nq = DSA_HEADS * tq
    in_specs = [
        pl.BlockSpec((None, tq, 768), lambda b, i: (b, i, P1_CQ // 768)),
        pl.BlockSpec((None, tq, 512), lambda b, i: (b, i, P2_IQ // 512)),
        pl.BlockSpec((None, tq, LANES), lambda b, i: (b, i, P2_IW // LANES)),
        pl.BlockSpec((None, seq, LANES), lambda b, i: (b, 0, P1_CK // LANES)),
        pl.BlockSpec((None, seq, LANES), lambda b, i: (b, 0, P2_IK // LANES)),
        pl.BlockSpec((None, seq // DSA_CH, DSA_VR, DSA_CH), lambda b, i: (b, 0, 0, 0)),
        pl.BlockSpec((DSA_Z, nq), lambda b, i: (0, 0)),
        pl.BlockSpec((D_MODEL, DSA_HEADS * HEAD_DIM), lambda b, i: (0, 0)),
    ]
    out = pl.pallas_call(
        functools.partial(_dsa_kernel, topk=min(IDX_TOPK, seq // 4)),
        grid=(bsz, seq // tq),
        in_specs=in_specs,
        out_specs=pl.BlockSpec((None, tq, D_MODEL), lambda b, i: (b, i, 0)),
        out_shape=jax.ShapeDtypeStruct((bsz, seq, D_MODEL), F32),
        scratch_shapes=[
            pltpu.VMEM((seq, tq), I32),
            pltpu.VMEM((seq // DSA_CH, 32, 8, tq), I32),
            pltpu.VMEM((2 * tq, nq), BF16),
            pltpu.VMEM((LANES, IDX_HEADS * tq), BF16),
            pltpu.VMEM((DSA_VR, nq), F32),
            pltpu.VMEM((DSA_CH, nq), F32),
            pltpu.VMEM((DSA_CH, nq), F32),
            pltpu.VMEM((8, nq), F32),
            pltpu.VMEM((8, nq), F32),
            pltpu.VMEM((8, nq), F32),
        ],
        compiler_params=_cparams(("parallel", "arbitrary")),
        name="dsa",
    )(p1v, p2v, p2v, p1v, p2v, vt, bias_t, wot)
    return out.reshape(bsz * seq, D_MODEL)


MERGE_TM = 512


def _merge_kernel(h_ref, oa0_ref, oa1_ref, oa2_ref, sa0_ref, sa1_ref, sa2_ref, ob_ref, yc_ref, g_ref,
                  woa_ref, wob_ref, wo_ref, o_ref):
    H = DSWA_HPG
    sts = [sa0_ref[...], sa1_ref[...], sa2_ref[...]]
    lane = lax.broadcasted_iota(I32, sts[0].shape, 1)
    mmax = jnp.maximum(jnp.maximum(sts[0], sts[1]), sts[2])
    wts = [pltpu.roll(s, LANES - H, 1) * jnp.exp(s - mmax) for s in sts]
    tot = wts[0] + wts[1] + wts[2]
    hrow = lax.broadcasted_iota(I32, (LANES, H * HEAD_DIM), 0)
    hcol = lax.broadcasted_iota(I32, (LANES, H * HEAD_DIM), 1) // HEAD_DIM
    expand = jnp.where(hrow == hcol, 1.0, 0.0).astype(BF16)
    oa = None
    for g in range(3):
        w = jnp.where(lane < H, wts[g] / tot, 0.0)
        hi = w.astype(BF16)
        lo = (w - hi.astype(F32)).astype(BF16)
        wfull = (jnp.dot(hi, expand, preferred_element_type=F32)
                 + jnp.dot(lo, expand, preferred_element_type=F32))
        term = wfull * (oa0_ref, oa1_ref, oa2_ref)[g][...]
        oa = term if oa is None else oa + term
    oa = oa.astype(BF16)
    y_a = jnp.dot(oa, woa_ref[...], preferred_element_type=F32)
    y_b = jnp.dot(ob_ref[...], wob_ref[...], preferred_element_type=F32)
    y_c = yc_ref[...]
    D = D_MODEL
    mix = (g_ref[:, 0:D].astype(F32) * y_a + g_ref[:, D:2 * D].astype(F32) * y_b
           + g_ref[:, 2 * D:3 * D].astype(F32) * y_c)
    o_ref[...] = h_ref[...] + jnp.dot(mix.astype(BF16), wo_ref[...], preferred_element_type=F32)


def _merge(h, oas, sas, ob, yc, gates, woa, wob, wo):
    m = h.shape[0]
    tm = MERGE_TM
    row = lambda w: pl.BlockSpec((tm, w), lambda i: (i, 0))
    full = lambda a: pl.BlockSpec(a.shape, lambda i: (0, 0))
    in_specs = ([row(D_MODEL)] + [row(256)] * 3 + [row(LANES)] * 3 + [row(1024), row(D_MODEL), row(3 * D_MODEL)]
                + [full(woa), full(wob), full(wo)])
    return pl.pallas_call(
        _merge_kernel,
        grid=(m // tm,),
        in_specs=in_specs,
        out_specs=row(D_MODEL),
        out_shape=jax.ShapeDtypeStruct((m, D_MODEL), F32),
        compiler_params=_cparams(("parallel",)),
        name="merge",
    )(h, *oas, *sas, ob, yc, gates, woa, wob, wo)


FFN_TM = 512
FFN_TF = 256


def _rms(x, g):
    ms = jnp.mean(x * x, axis=-1, keepdims=True)
    return (x * lax.rsqrt(ms + RMS_EPS) * g).astype(BF16)


def _ffn_ple_kernel(h_ref, p_ref, gf_ref, wg_ref, wu_ref, wd_ref, gp_ref, wpg_ref, wpp_ref, o_ref, *, tf):
    x = h_ref[...]
    u = _rms(x, gf_ref[...])
    acc = None
    for j in range(wg_ref.shape[1] // tf):
        cs = slice(j * tf, (j + 1) * tf)
        a = jnp.dot(u, wg_ref[:, cs], preferred_element_type=F32)
        b = jnp.dot(u, wu_ref[:, cs], preferred_element_type=F32)
        t = (a * jax.nn.sigmoid(a) * b).astype(BF16)
        d = jnp.dot(t, wd_ref[cs, :], preferred_element_type=F32)
        acc = d if acc is None else acc + d
    h2 = x + acc
    e = _rms(h2, gp_ref[...])
    gate = jax.nn.sigmoid(jnp.dot(e, wpg_ref[...], preferred_element_type=F32))
    proj = jnp.dot(p_ref[...].astype(BF16), wpp_ref[...], preferred_element_type=F32)
    o_ref[...] = h2 + gate * proj


def _ffn_ple(h, p, gf, wg, wu, wd, gp, wpg, wpp):
    m, d = h.shape
    tm, tf = FFN_TM, FFN_TF
    const = lambda a: pl.BlockSpec(a.shape, lambda i: (0, 0), pipeline_mode=pl.Buffered(1))
    return pl.pallas_call(
        functools.partial(_ffn_ple_kernel, tf=tf),
        grid=(m // tm,),
        in_specs=[pl.BlockSpec((tm, d), lambda i: (i, 0)), pl.BlockSpec((tm, PLE_DIM), lambda i: (i, 0)),
                  const(gf), const(wg), const(wu), const(wd), const(gp), const(wpg), const(wpp)],
        out_specs=pl.BlockSpec((tm, d), lambda i: (i, 0)),
        out_shape=jax.ShapeDtypeStruct((m, d), F32),
        compiler_params=_cparams(("parallel",)),
        name="ffn_ple",
    )(h, p, gf, wg, wu, wd, gp, wpg, wpp)


def _rel_bucket(dist):
    max_exact = REL_BUCKETS // 2
    d = jnp.maximum(dist, 0)
    df = jnp.maximum(d, 1).astype(F32)
    large = max_exact + (jnp.log(df / max_exact) / math.log(REL_MAX_DIST / max_exact)
                         * (REL_BUCKETS - max_exact)).astype(I32)
    large = jnp.minimum(large, REL_BUCKETS - 1)
    return jnp.where(d < max_exact, d, large)


def _toeplitz(rev, n_rows, n_cols):
    nh = rev.shape[0]
    lw = rev.shape[1] + 1
    w = jnp.pad(rev, ((0, 0), (0, 1)))
    s = jnp.broadcast_to(w[:, None, :], (nh, n_rows, lw)).reshape(nh, n_rows * lw)
    s = s[:, :n_rows * (lw - 1)].reshape(nh, n_rows, lw - 1)
    return s[:, :, n_rows - 1:n_rows - 1 + n_cols]


def _bias_tables(rel_bias):
    blk = DSWA_BLOCK
    dswa = []
    for g, (_, r) in enumerate(DSWA_PATTERNS):
        delta = np.arange(3 * blk - 1)[::-1] - (blk - 1)
        rev = rel_bias[_rel_bucket(jnp.asarray(delta * r, I32))][:, g * DSWA_HPG:(g + 1) * DSWA_HPG]
        dswa.append(_toeplitz(rev.T, blk, 2 * blk))
    tq = DSA_TQ
    dist = np.arange(DSA_Z + tq - 1)[::-1] - (DSA_Z - 1) + DSA_Z0
    rev = rel_bias[_rel_bucket(jnp.asarray(dist, I32))][:, DSWA_HEADS:] * math.log2(math.e)
    bias_t = jnp.transpose(_toeplitz(rev.T, tq, DSA_Z), (2, 0, 1)).reshape(DSA_Z, DSA_HEADS * tq)
    return dswa, bias_t


def _pad_cols(w, width):
    return jnp.pad(w, ((0, 0), (0, width - w.shape[1])))


def _layer_params(w_in, qn_a, kn_a, qn_c, kn_c, w_alpha2, b_alpha):
    offs = np.cumsum((0,) + IN_WIDTHS)
    parts = [w_in[:, offs[i]:offs[i + 1]] for i in range(len(IN_WIDTHS))]
    a_q, a_k, a_v, b_q, b_k, b_v, b_r, b_al, c_q, c_k, c_v, i_q, i_k, i_w = parts
    w1 = _pad_cols(jnp.concatenate([a_q, a_k, c_q, c_k], axis=1), W1).astype(BF16)
    w2 = _pad_cols(jnp.concatenate(
        [b_v, b_r, b_q, b_k, i_q, a_v, _pad_cols(b_al, LANES), _pad_cols(c_v, LANES),
         _pad_cols(i_k, LANES), _pad_cols(i_w, LANES)], axis=1), W2).astype(BF16)
    scale = HEAD_DIM ** -0.5
    gain1 = jnp.concatenate([jnp.tile(qn_a, DSWA_HEADS) * scale, jnp.tile(kn_a, DSWA_HEADS),
                             jnp.tile(qn_c, DSA_HEADS) * (scale * math.log2(math.e)), kn_c])
    gain1 = jnp.pad(gain1, (0, W1 - gain1.shape[0])).reshape(1, W1)
    wa = jnp.pad(w_alpha2, ((0, LANES - GLA_RANK), (0, 0))).astype(BF16)
    return w1, w2, gain1, wa, b_alpha.reshape(1, -1)


def kernel(x, p, rel_bias, norm_mix, w_in, qn_a, kn_a, qn_c, kn_c, w_alpha2, b_alpha, gla_norm, w_out_a, w_out_b, w_out_c, w_gate, b_gate, w_o, norm_ffn, w_ffn_gate, w_ffn_up, w_ffn_down, norm_ple, w_ple_gate, w_ple_proj):
    bsz, seq, d = x.shape
    depth = p.shape[0]
    m = bsz * seq
    dswa_bias, bias_t = _bias_tables(rel_bias)
    h = x.reshape(m, d)
    zeros_w2 = jnp.zeros((1, W2), F32)
    for i in range(depth):
        w1, w2, gain1, wa, ba = _layer_params(w_in[i], qn_a[i], kn_a[i], qn_c[i], kn_c[i], w_alpha2[i], b_alpha[i])
        gmix = norm_mix[i].reshape(1, d)
        p1 = _proj(h, gmix, w1, gain1, "qk")
        p2 = _proj(h, gmix, w2, zeros_w2, "plain")
        gates = _proj(h, gmix, w_gate[i].astype(BF16), b_gate[i].reshape(1, -1), "gate")
        oas, sas = [], []
        for g, (_, r) in enumerate(DSWA_PATTERNS):
            o, st = _dswa(p1, p2, dswa_bias[g], g, r, bsz, seq)
            oas.append(o)
            sas.append(st)
        ob = _gla(p2, wa, ba, gla_norm[i].reshape(1, -1), bsz, seq)
        cv = p2[:, P2_CV:P2_CV + HEAD_DIM].reshape(bsz, seq // DSA_CH, DSA_CH, HEAD_DIM)
        ones_pad = jnp.zeros((bsz, seq // DSA_CH, DSA_VR - HEAD_DIM, DSA_CH), BF16).at[:, :, 0, :].set(1.0)
        vt = jnp.concatenate([jnp.transpose(cv, (0, 1, 3, 2)), ones_pad], axis=2)
        yc = _dsa(p1, p2, vt, bias_t, w_out_c[i].T.astype(BF16), bsz, seq)
        h = _merge(h, oas, sas, ob, yc, gates, w_out_a[i].astype(BF16), w_out_b[i].astype(BF16),
                   w_o[i].astype(BF16))
        h = _ffn_ple(h, p[i].reshape(m, PLE_DIM), norm_ffn[i].reshape(1, d), w_ffn_gate[i].astype(BF16),
                     w_ffn_up[i].astype(BF16), w_ffn_down[i].astype(BF16), norm_ple[i].reshape(1, d),
                     w_ple_gate[i].astype(BF16), w_ple_proj[i].astype(BF16))
    return h.reshape(bsz, seq, d)
```

```python
import functools
import math

import numpy as np
import jax
import jax.numpy as jnp
from jax import lax
from jax.experimental import pallas as pl
from jax.experimental.pallas import tpu as pltpu

F32 = jnp.float32
BF16 = jnp.bfloat16
I32 = jnp.int32
I16 = jnp.int16

D_MODEL = 1024
HEAD_DIM = 64
RMS_EPS = 1e-6
DSWA_PATTERNS = ((128, 1), (512, 4), (2048, 16))
DSWA_HPG = 4
DSWA_HEADS = 12
DSWA_BLOCK = 128
GLA_HEADS = 4
GLA_DK = 128
GLA_DV = 256
GLA_RANK = 16
GLA_TAU = 16.0
GLA_CHUNK = 64
DSA_HEADS = 12
IDX_HEADS = 8
IDX_DIM = 64
IDX_TOPK = 256
REL_BUCKETS = 32
REL_MAX_DIST = 2048
D_FF = 2816
PLE_DIM = 256
IN_WIDTHS = (768, 768, 768, 512, 512, 1024, 1024, 16, 768, 64, 64, 512, 64, 8)

LANES = 128
MXU_N = 256
VMEM_LIMIT = 56 * 1024 * 1024

W1 = 2560
P1_AQ, P1_AK, P1_CQ, P1_CK = 0, 768, 1536, 2304
W2 = 5120
P2_BV, P2_BR, P2_BQ, P2_BK, P2_IQ, P2_AV = 0, 1024, 2048, 2560, 3072, 3584
P2_BAL, P2_CV, P2_IK, P2_IW = 4352, 4480, 4608, 4736

PROJ_TM = 512
PROJ_TN = 512

DSWA_QB = 4
DSA_TQ = 128
DSA_CH = 256
DSA_AB = 256
DSA_VR = 80
DSA_BIAS_CONST_FROM = 1512
DSA_Z0 = 1792
DSA_Z = DSA_Z0 + DSA_CH
INT_MIN = -(2 ** 31)
NEG_BIG = -1e30


def _cparams(sem, flags=None):
    return pltpu.CompilerParams(dimension_semantics=sem, vmem_limit_bytes=VMEM_LIMIT, flags=flags)


def _proj_kernel(h_ref, g_ref, w_ref, e_ref, *rest, mode, tn, regroup):
    if mode == "qk":
        bd_ref, rest = rest[0], rest[1:]
    o_ref, rg_refs = rest[0], rest[1:1 + len(regroup)]
    scr_ref = rest[-1] if regroup else None
    tm = h_ref.shape[0]
    gw = DSWA_HPG * HEAD_DIM
    x = h_ref[...]
    ms = jnp.mean(x * x, axis=-1, keepdims=True)
    u = (x * lax.rsqrt(ms + RMS_EPS) * g_ref[...]).astype(BF16)
    for j in range(w_ref.shape[1] // tn):
        cs = slice(j * tn, (j + 1) * tn)
        acc = jnp.dot(u, w_ref[:, cs], preferred_element_type=F32)
        if mode == "plain":
            out = acc
        elif mode == "gate":
            out = jax.nn.sigmoid(acc + e_ref[:, cs])
        else:
            sq = acc * acc
            hi = sq.astype(BF16)
            lo = (sq - hi.astype(F32)).astype(BF16)
            ss = (jnp.dot(hi, bd_ref[...], preferred_element_type=F32)
                  + jnp.dot(lo, bd_ref[...], preferred_element_type=F32))
            out = acc * lax.rsqrt(ss * (1.0 / HEAD_DIM) + RMS_EPS) * e_ref[:, cs]
        o_ref[:, cs] = out.astype(o_ref.dtype)
        for (col, r), rg_ref in zip(regroup, rg_refs):
            if col // tn == j:
                for k in range(gw // LANES):
                    scr_ref[k] = out[:, col % tn + k * LANES:col % tn + (k + 1) * LANES]
                for c in range(r):
                    for k in range(gw // LANES):
                        rg_ref[:, c * gw + k * LANES:c * gw + (k + 1) * LANES] = (
                            scr_ref[k, pl.ds(c, tm // r, stride=r), :].astype(rg_ref.dtype))


def _proj(h, g, w, e, mode, regroup=()):
    m, d = h.shape
    n = w.shape[1]
    tm, tn = PROJ_TM, PROJ_TN
    gw = DSWA_HPG * HEAD_DIM
    const = lambda shape: pl.BlockSpec(shape, lambda i: (0, 0), pipeline_mode=pl.Buffered(1))
    in_specs = [pl.BlockSpec((tm, d), lambda i: (i, 0)), const((1, d)), const((d, n)), const((1, n))]
    args = [h, g, w, e]
    if mode == "qk":
        r = np.arange(tn) // HEAD_DIM
        bd = jnp.asarray((r[:, None] == r[None, :]).astype(np.float32), dtype=BF16)
        in_specs.append(const((tn, tn)))
        args.append(bd)
    out_specs = [pl.BlockSpec((tm, n), lambda i: (i, 0))]
    out_shape = [jax.ShapeDtypeStruct((m, n), BF16)]
    for _, r in regroup:
        out_specs.append(pl.BlockSpec((tm // r, r * gw), lambda i: (i, 0)))
        out_shape.append(jax.ShapeDtypeStruct((m // r, r * gw), BF16))
    outs = pl.pallas_call(
        functools.partial(_proj_kernel, mode=mode, tn=tn, regroup=tuple(regroup)),
        grid=(m // tm,),
        in_specs=in_specs,
        out_specs=out_specs,
        out_shape=out_shape,
        scratch_shapes=[pltpu.VMEM((gw // LANES, tm, LANES), F32)] if regroup else [],
        compiler_params=_cparams(("parallel",)),
        name="proj_" + mode,
    )(*args)
    return outs if regroup else outs[0]


def _dswa_kernel(q_ref, kp_ref, kc_ref, vp_ref, vc_ref, bias_ref, o_ref, st_ref, *, qb):
    i = pl.program_id(2)
    blk = DSWA_BLOCK
    row = lax.broadcasted_iota(I32, (blk, 2 * blk), 0)
    col = lax.broadcasted_iota(I32, (blk, 2 * blk), 1)
    cur_ok = (col >= blk) & ((col - blk) <= row)
    valid_inner = ((col < blk) & (col >= row)) | cur_ok
    prev_off = jnp.where(i > 0, 0, 4 * blk)
    valid_first = ((col < blk) & (col >= row + prev_off)) | cur_ok
    lane = lax.broadcasted_iota(I32, (blk, LANES), 1)
    for j in range(qb):
        rs = slice(j * blk, (j + 1) * blk)
        ps = slice((j - 1) * blk, j * blk)
        valid = valid_first if j == 0 else valid_inner
        q = q_ref[rs, :]
        k = jnp.concatenate([kp_ref[...] if j == 0 else kc_ref[ps, :], kc_ref[rs, :]], axis=0)
        v = jnp.concatenate([vp_ref[...] if j == 0 else vc_ref[ps, :], vc_ref[rs, :]], axis=0)
        stats = jnp.zeros((blk, LANES), F32)
        for h in range(DSWA_HPG):
            sl = slice(h * HEAD_DIM, (h + 1) * HEAD_DIM)
            s = lax.dot_general(q[:, sl], k[:, sl], (((1,), (1,)), ((), ())), preferred_element_type=F32)
            s = jnp.where(valid, s + bias_ref[h], -jnp.inf)
            m = jnp.max(s, axis=-1, keepdims=True)
            p = jnp.exp(s - m)
            l = jnp.sum(p, axis=-1, keepdims=True)
            o = jnp.dot(p.astype(BF16), v[:, sl], preferred_element_type=F32) / l
            o_ref[rs, sl] = o
            stats = jnp.where(lane == h, m, stats)
            stats = jnp.where(lane == DSWA_HPG + h, l, stats)
        st_ref[rs, :] = stats


def _dswa(p1, p2, rm, bias, g, r, bsz, seq):
    blk = DSWA_BLOCK
    L = seq // r
    nblk = L // blk
    gw = DSWA_HPG * HEAD_DIM
    if r == 1:
        p1v = p1.reshape(bsz, L, W1)
        p2v = p2.reshape(bsz, L, W2)
        qv, kv, vv = p1v, p1v, p2v
        q_col = lambda c: (P1_AQ // gw) + g
        k_col = lambda c: (P1_AK // gw) + g
        v_col = lambda c: (P2_AV // gw) + g
    else:
        qv, kv, vv = (a.reshape(bsz, L, r * gw) for a in rm)
        q_col = k_col = v_col = lambda c: c
    qb = min(DSWA_QB, nblk)
    prev = lambda i: jnp.maximum(i * qb - 1, 0)
    in_specs = [
        pl.BlockSpec((None, qb * blk, gw), lambda b, c, i: (b, i, q_col(c))),
        pl.BlockSpec((None, blk, gw), lambda b, c, i: (b, prev(i), k_col(c))),
        pl.BlockSpec((None, qb * blk, gw), lambda b, c, i: (b, i, k_col(c))),
        pl.BlockSpec((None, blk, gw), lambda b, c, i: (b, prev(i), v_col(c))),
        pl.BlockSpec((None, qb * blk, gw), lambda b, c, i: (b, i, v_col(c))),
        pl.BlockSpec((DSWA_HPG, blk, 2 * blk), lambda b, c, i: (0, 0, 0)),
    ]
    out_specs = [
        pl.BlockSpec((None, qb * blk, gw), lambda b, c, i: (b, i, c)),
        pl.BlockSpec((None, qb * blk, LANES), lambda b, c, i: (b, i, c)),
    ]
    o, st = pl.pallas_call(
        functools.partial(_dswa_kernel, qb=qb),
        grid=(bsz, r, nblk // qb),
        in_specs=in_specs,
        out_specs=out_specs,
        out_shape=[jax.ShapeDtypeStruct((bsz, L, r * gw), F32),
                   jax.ShapeDtypeStruct((bsz, L, r * LANES), F32)],
        compiler_params=_cparams(("parallel", "parallel", "arbitrary")),
        name="dswa_g%d" % g,
    )(qv, kv, kv, vv, vv, bias)
    return o.reshape(bsz * L, r * gw), st.reshape(bsz * L, r * LANES)


GLA_TC = 256
GLA_GB = 2


def _split3(x):
    a1 = x.astype(BF16)
    r1 = x - a1.astype(F32)
    a2 = r1.astype(BF16)
    a3 = (r1 - a2.astype(F32)).astype(BF16)
    return a1, a2, a3


def _gla_kernel(v_ref, r_ref, q_ref, k_ref, al_ref, wa_ref, ba_ref, gn_ref, o_ref, st_ref, *, gb):
    C = GLA_CHUNK

    @pl.when(pl.program_id(1) == 0)
    def _():
        st_ref[...] = jnp.zeros_like(st_ref)

    row = lax.broadcasted_iota(I32, (C, C), 0)
    col = lax.broadcasted_iota(I32, (C, C), 1)
    tri = row >= col
    tri_bf = jnp.where(tri, 1.0, 0.0).astype(BF16)
    nt = (((1,), (1,)), ((), ()))
    tn = (((0,), (0,)), ((), ()))
    for c, bb in [(c, bb) for c in range(GLA_TC // C) for bb in range(gb)]:
        rs = slice(c * C, (c + 1) * C)
        z = jnp.dot(al_ref[bb, rs, :], wa_ref[...], preferred_element_type=F32) + ba_ref[...]
        la = (jnp.minimum(z, 0.0) - jnp.log(1.0 + jnp.exp(-jnp.abs(z)))) * (1.0 / GLA_TAU)
        a1, a2, a3 = _split3(la)
        bcum = (jnp.dot(tri_bf, a1, preferred_element_type=F32)
                + jnp.dot(tri_bf, a2, preferred_element_type=F32)
                + jnp.dot(tri_bf, a3, preferred_element_type=F32))
        blast = bcum[C - 1:C, :]
        qf = q_ref[bb, rs, :].astype(F32) * (GLA_DK ** -0.5)
        kf = k_ref[bb, rs, :].astype(F32)
        q_in = (qf * jnp.exp(bcum)).astype(BF16)
        k_in = (kf * jnp.exp(-bcum)).astype(BF16)
        k_end = (kf * jnp.exp(blast - bcum)).astype(BF16)
        dec = jnp.exp(blast)
        for h in range(GLA_HEADS):
            ks = slice(h * GLA_DK, (h + 1) * GLA_DK)
            vs = slice(h * GLA_DV, (h + 1) * GLA_DV)
            vh = v_ref[bb, rs, vs]
            att = lax.dot_general(q_in[:, ks], k_in[:, ks], nt, preferred_element_type=F32)
            att = jnp.where(tri, att, 0.0)
            o = jnp.dot(att.astype(BF16), vh, preferred_element_type=F32)
            st = st_ref[bb, h]
            o = o + lax.dot_general(q_in[:, ks], st.astype(BF16), nt, preferred_element_type=F32)
            st_ref[bb, h] = st * dec[:, ks] + lax.dot_general(vh, k_end[:, ks], tn, preferred_element_type=F32)
            ms = jnp.mean(o * o, axis=-1, keepdims=True)
            y = o * lax.rsqrt(ms + RMS_EPS) * gn_ref[...]
            rg = r_ref[bb, rs, vs].astype(F32)
            y = y * (rg * jax.nn.sigmoid(rg))
            o_ref[bb, rs, vs] = y.astype(o_ref.dtype)


def _gla(p2, wa, ba, gn, bsz, seq):
    tc = GLA_TC
    gb = GLA_GB if bsz % GLA_GB == 0 else 1
    p2v = p2.reshape(bsz, seq, W2)
    in_specs = [
        pl.BlockSpec((gb, tc, 1024), lambda b, t: (b, t, P2_BV // 1024)),
        pl.BlockSpec((gb, tc, 1024), lambda b, t: (b, t, P2_BR // 1024)),
        pl.BlockSpec((gb, tc, 512), lambda b, t: (b, t, P2_BQ // 512)),
        pl.BlockSpec((gb, tc, 512), lambda b, t: (b, t, P2_BK // 512)),
        pl.BlockSpec((gb, tc, LANES), lambda b, t: (b, t, P2_BAL // LANES)),
        pl.BlockSpec((LANES, 512), lambda b, t: (0, 0)),
        pl.BlockSpec((1, 512), lambda b, t: (0, 0)),
        pl.BlockSpec((1, GLA_DV), lambda b, t: (0, 0)),
    ]
    out = pl.pallas_call(
        functools.partial(_gla_kernel, gb=gb),
        grid=(bsz // gb, seq // tc),
        in_specs=in_specs,
        out_specs=pl.BlockSpec((gb, tc, 1024), lambda b, t: (b, t, 0)),
        out_shape=jax.ShapeDtypeStruct((bsz, seq, 1024), BF16),
        scratch_shapes=[pltpu.VMEM((gb, GLA_HEADS, GLA_DV, GLA_DK), F32)],
        compiler_params=_cparams(("parallel", "arbitrary")),
        name="gla",
    )(p2v, p2v, p2v, p2v, p2v, wa, ba, gn)
    return out.reshape(bsz * seq, 1024)


def _tree_sum(xs):
    xs = list(xs)
    while len(xs) > 1:
        xs = [xs[i] + xs[i + 1] for i in range(0, len(xs) - 1, 2)] + ([xs[-1]] if len(xs) % 2 else [])
    return xs[0]


def _dsa_kernel(q_ref, iq_ref, iw_ref, k_ref, ik_ref, vt_ref, bias_ref, wot_ref, o_ref,
                keys_ref, planes_ref, qaug_ref, iqall_ref, acc_ref, st0_ref, st1_ref, m_ref, cm0_ref, cm1_ref,
                *, topk):
    TQ, CH = DSA_TQ, DSA_CH
    CC = 2 * CH
    qi = pl.program_id(1)
    t0 = qi * TQ
    nch = qi // (CH // TQ) + 1
    npair = (nch + 1) // 2
    nt = (((1,), (1,)), ((), ()))

    eye = (lax.broadcasted_iota(I32, (TQ, TQ), 0) == lax.broadcasted_iota(I32, (TQ, TQ), 1))
    eye = jnp.where(eye, 1.0, 0.0).astype(BF16)
    q_t = q_ref[...].astype(F32).T.astype(BF16)
    iq_t = iq_ref[...].astype(F32).T.astype(BF16)
    for h in range(DSA_HEADS):
        cs = slice(h * TQ, (h + 1) * TQ)
        qaug_ref[0:TQ, cs] = eye
        qaug_ref[TQ:TQ + HEAD_DIM, cs] = q_t[h * HEAD_DIM:(h + 1) * HEAD_DIM, :]
        qaug_ref[TQ + HEAD_DIM:, cs] = jnp.zeros((TQ - HEAD_DIM, TQ), BF16)
    for h in range(IDX_HEADS):
        cs = slice(h * TQ, (h + 1) * TQ)
        iqall_ref[0:IDX_DIM, cs] = iq_t[h * IDX_DIM:(h + 1) * IDX_DIM, :]
        iqall_ref[IDX_DIM:, cs] = jnp.zeros((LANES - IDX_DIM, TQ), BF16)
    idx_scale = (IDX_HEADS ** -0.5) * (IDX_DIM ** -0.5)
    wt = (iw_ref[...].astype(F32) * idx_scale).T

    qpos = t0 + lax.broadcasted_iota(I32, (1, TQ), 1)
    RB = 128
    krow1 = lax.broadcasted_iota(I32, (RB, TQ), 0)
    krow = lax.broadcasted_iota(I32, (CH, TQ), 0)
    krow2 = lax.broadcasted_iota(I32, (CC, TQ), 0)

    def score_body(cp, carry):
        for sub in range(CC // RB):
            s0 = pl.multiple_of(cp * CC + sub * RB, RB)
            ikc = ik_ref[pl.ds(s0, RB), :]
            s = None
            for hp in range(IDX_HEADS // 2):
                x = jnp.dot(ikc, iqall_ref[:, hp * 2 * TQ:(hp + 1) * 2 * TQ],
                            preferred_element_type=F32)
                t = (jnp.maximum(x[:, :TQ], 0.0) * wt[2 * hp:2 * hp + 1, :]
                     + jnp.maximum(x[:, TQ:], 0.0) * wt[2 * hp + 1:2 * hp + 2, :])
                s = t if s is None else s + t
            s = jnp.where(s == 0.0, 0.0, s)
            bits = pltpu.bitcast(s, I32)
            key = bits ^ ((bits >> 31) & 0x7FFFFFFF)
            key = jnp.where(s0 + krow1 <= qpos, key, INT_MIN)
            keys_ref[pl.ds(s0, RB), :] = key
        return carry

    lax.fori_loop(0, npair, score_body, 0)

    kvec = jnp.minimum(topk, qpos + 1)
    NPL = 32

    def planes_body(c, carry):
        s0 = pl.multiple_of(c * CH, CH)
        a = [keys_ref[pl.ds(s0 + 8 * j, 8), :] ^ INT_MIN for j in range(NPL)]
        j, m = 16, 0x0000FFFF
        while j:
            sh = jnp.full((8, TQ), j, I32)
            k = 0
            while k < NPL:
                t = (a[k] ^ lax.shift_right_logical(a[k + j], sh)) & m
                a[k] = a[k] ^ t
                a[k + j] = a[k + j] ^ lax.shift_left(t, sh)
                k = (k + j + 1) & ~j
            j >>= 1
            m = (m ^ (m << j)) & 0xFFFFFFFF
        for p in range(NPL):
            planes_ref[c, p] = a[p]
        return carry

    def empty_body(c, carry):
        for p in range(NPL):
            planes_ref[c, p] = jnp.zeros((8, TQ), I32)
        return carry

    nck = 2 * npair
    NCK = keys_ref.shape[0] // CH
    lax.fori_loop(0, nck, planes_body, 0)
    lax.fori_loop(nck, NCK, empty_body, 0)

    def plane_body(p, carry):
        thr_u, n_gt, alive = carry
        hit = [alive[c] & planes_ref[c, p] for c in range(NCK)]
        ones = jnp.sum(_tree_sum([lax.population_count(h) for h in hit]), axis=0, keepdims=True)
        take = (n_gt + ones) >= kvec
        thr_u = jnp.where(take, thr_u | (jnp.int32(1) << (31 - p)), thr_u)
        n_gt = jnp.where(take, n_gt, n_gt + ones)
        alive = tuple(jnp.where(take, hit[c], alive[c] ^ hit[c]) for c in range(NCK))
        return thr_u, n_gt, alive

    zero = jnp.zeros((1, TQ), I32)
    alive0 = tuple(jnp.full((8, TQ), -1, I32) for _ in range(NCK))
    thr_u, n_gt, alive = lax.fori_loop(0, NPL, plane_body, (zero, zero, alive0))
    n_eq = jnp.sum(_tree_sum([lax.population_count(a) for a in alive]), axis=0, keepdims=True)
    thr = thr_u ^ INT_MIN

    def count(pred_fn):
        def body(c, acc):
            s0 = pl.multiple_of(c * CC, CC)
            kk = keys_ref[pl.ds(s0, CC), :]
            hit = jnp.where(pred_fn(kk, s0), 1, 0).astype(I32)
            return acc + jnp.sum(hit.reshape(CC // 8, 8, TQ), axis=0)
        acc = lax.fori_loop(0, npair, body, jnp.zeros((8, TQ), I32))
        return jnp.sum(acc, axis=0, keepdims=True)

    excess = n_gt + n_eq - kvec
    has_excess = jnp.max(excess) > 0

    @pl.when(has_excess)
    def _():
        need = kvec - n_gt

        def tie_lt(cut):
            return count(lambda kk, s0: (kk == thr) & (s0 + krow2 < cut))

        def cut_body(b, cut):
            cand = cut | (jnp.int32(1) << (12 - b))
            return jnp.where(tie_lt(cand) <= need, cand, cut)

        cut = lax.fori_loop(0, 13, cut_body, jnp.zeros((1, TQ), I32))

        def drop_body(c, carry):
            s0 = pl.multiple_of(c * CH, CH)
            kk = keys_ref[pl.ds(s0, CH), :]
            keys_ref[pl.ds(s0, CH), :] = jnp.where((kk == thr) & (s0 + krow >= cut), INT_MIN, kk)
            return carry

        lax.fori_loop(0, nch, drop_body, 0)

    acc_ref[...] = jnp.zeros_like(acc_ref)

    NP = DSA_HEADS // 2
    PW = 2 * TQ

    AB = DSA_AB
    NB = CH // AB
    last_chunk = keys_ref.shape[0] // CH - 1

    def logits(c, st_ref, cm_ref):
        s0 = pl.multiple_of(c * CH, CH)
        zoff = pl.multiple_of(jnp.clip(s0 - t0 + DSA_Z0, 0, DSA_Z0), 8)
        kaug = []
        for rb in range(NB):
            rs = pl.ds(s0 + rb * AB, AB)
            pen = jnp.where(keys_ref[rs, :] >= thr, 0.0, NEG_BIG).astype(BF16)
            kaug.append(jnp.concatenate([pen, k_ref[rs, :]], axis=1))
        for hp in range(NP):
            ps = slice(hp * PW, (hp + 1) * PW)
            cm = None
            for rb in range(NB):
                st = jnp.dot(kaug[rb], qaug_ref[:, ps], preferred_element_type=F32)
                st = st + bias_ref[pl.ds(zoff + rb * AB, AB), ps]
                st_ref[rb * AB:(rb + 1) * AB, ps] = st
                tm = jnp.max(st.reshape(AB // 8, 8, PW), axis=0)
                cm = tm if cm is None else jnp.maximum(cm, tm)
            cm_ref[0:1, ps] = jnp.max(cm, axis=0, keepdims=True)

    def accumulate(c, st_ref, cm_ref):
        vtc = vt_ref[c]
        for hp in range(NP):
            ps = slice(hp * PW, (hp + 1) * PW)
            m_old = m_ref[0:1, ps]
            m_new = jnp.maximum(m_old, cm_ref[0:1, ps])
            m_ref[0:1, ps] = m_new
            alpha = jnp.exp2(m_old - m_new)
            p = jnp.concatenate(
                [jnp.exp2(st_ref[rb * AB:(rb + 1) * AB, ps] - m_new).astype(BF16) for rb in range(NB)], axis=0)
            acc_ref[:, ps] = acc_ref[:, ps] * alpha + jnp.dot(vtc, p, preferred_element_type=F32)

    def att_body(cp, carry):
        c0 = 2 * cp
        logits(c0 + 1, st1_ref, cm1_ref)
        accumulate(c0, st0_ref, cm0_ref)
        logits(jnp.minimum(c0 + 2, last_chunk), st0_ref, cm0_ref)
        accumulate(c0 + 1, st1_ref, cm1_ref)
        return carry

    m_ref[...] = jnp.full(m_ref.shape, NEG_BIG, F32)
    logits(0, st0_ref, cm0_ref)
    lax.fori_loop(0, npair, att_body, 0)

    ot = jnp.concatenate(
        [(acc_ref[0:HEAD_DIM, h * TQ:(h + 1) * TQ] / acc_ref[HEAD_DIM:HEAD_DIM + 1, h * TQ:(h + 1) * TQ]).astype(BF16)
         for h in range(DSA_HEADS)], axis=0)
    yt = jnp.dot(wot_ref[...], ot, preferred_element_type=F32)
    o_ref[...] = yt.T


def _dsa(p1, p2, vt, bias_t, wot, bsz, seq):
    tq = DSA_TQ
    p1v = p1.reshape(bsz, seq, W1)
    p2v = p2.reshape(bsz, seq, W2)
    nq = DSA_HEADS * tq
    in_specs = [
        pl.BlockSpec((None, tq, 768), lambda b, i: (b, i, P1_CQ // 768)),
        pl.BlockSpec((None, tq, 512), lambda b, i: (b, i, P2_IQ // 512)),
        pl.BlockSpec((None, tq, LANES), lambda b, i: (b, i, P2_IW // LANES)),
        pl.BlockSpec((None, seq, LANES), lambda b, i: (b, 0, P1_CK // LANES)),
        pl.BlockSpec((None, seq, LANES), lambda b, i: (b, 0, P2_IK // LANES)),
        pl.BlockSpec((None, seq // DSA_CH, DSA_VR, DSA_CH), lambda b, i: (b, 0, 0, 0)),
        pl.BlockSpec((DSA_Z, nq), lambda b, i: (0, 0)),
        pl.BlockSpec((D_MODEL, DSA_HEADS * HEAD_DIM), lambda b, i: (0, 0)),
    ]
    out = pl.pallas_call(
        functools.partial(_dsa_kernel, topk=min(IDX_TOPK, seq // 4)),
        grid=(bsz, seq // tq),
        in_specs=in_specs,
        out_specs=pl.BlockSpec((None, tq, D_MODEL), lambda b, i: (b, i, 0)),
        out_shape=jax.ShapeDtypeStruct((bsz, seq, D_MODEL), F32),
        scratch_shapes=[
            pltpu.VMEM((seq, tq), I32),
            pltpu.VMEM((seq // DSA_CH, 32, 8, tq), I32),
            pltpu.VMEM((2 * tq, nq), BF16),
            pltpu.VMEM((LANES, IDX_HEADS * tq), BF16),
            pltpu.VMEM((DSA_VR, nq), F32),
            pltpu.VMEM((DSA_CH, nq), F32),
            pltpu.VMEM((DSA_CH, nq), F32),
            pltpu.VMEM((8, nq), F32),
            pltpu.VMEM((8, nq), F32),
            pltpu.VMEM((8, nq), F32),
        ],
        compiler_params=_cparams(("parallel", "arbitrary")),
        name="dsa",
    )(p1v, p2v, p2v, p1v, p2v, vt, bias_t, wot)
    return out.reshape(bsz * seq, D_MODEL)


MERGE_TM = 512


def _merge_kernel(h_ref, oa0_ref, oa1_ref, oa2_ref, sa0_ref, sa1_ref, sa2_ref, ob_ref, yc_ref, g_ref,
                  woa_ref, wob_ref, wo_ref, o_ref, on1_ref, on2_ref, sn1_ref, sn2_ref):
    H = DSWA_HPG
    tm = h_ref.shape[0]
    gw = H * HEAD_DIM
    for (_, r), src_o, src_s, dst_o, dst_s in zip(DSWA_PATTERNS[1:], (oa1_ref, oa2_ref), (sa1_ref, sa2_ref),
                                                  (on1_ref, on2_ref), (sn1_ref, sn2_ref)):
        for c in range(r):
            for k in range(gw // LANES):
                dst_o[k, pl.ds(c, tm // r, stride=r), :] = src_o[:, c * gw + k * LANES:c * gw + (k + 1) * LANES]
            dst_s[pl.ds(c, tm // r, stride=r), :] = src_s[:, c * LANES:(c + 1) * LANES]
    oa_vals = (oa0_ref[...],) + tuple(
        jnp.concatenate([ref[k] for k in range(gw // LANES)], axis=1) for ref in (on1_ref, on2_ref))
    sts = [sa0_ref[...], sn1_ref[...], sn2_ref[...]]
    lane = lax.broadcasted_iota(I32, sts[0].shape, 1)
    mmax = jnp.maximum(jnp.maximum(sts[0], sts[1]), sts[2])
    wts = [pltpu.roll(s, LANES - H, 1) * jnp.exp(s - mmax) for s in sts]
    tot = wts[0] + wts[1] + wts[2]
    hrow = lax.broadcasted_iota(I32, (LANES, H * HEAD_DIM), 0)
    hcol = lax.broadcasted_iota(I32, (LANES, H * HEAD_DIM), 1) // HEAD_DIM
    expand = jnp.where(hrow == hcol, 1.0, 0.0).astype(BF16)
    oa = None
    for g in range(3):
        w = jnp.where(lane < H, wts[g] / tot, 0.0)
        hi = w.astype(BF16)
        lo = (w - hi.astype(F32)).astype(BF16)
        wfull = (jnp.dot(hi, expand, preferred_element_type=F32)
                 + jnp.dot(lo, expand, preferred_element_type=F32))
        term = wfull * oa_vals[g]
        oa = term if oa is None else oa + term
    oa = oa.astype(BF16)
    y_a = jnp.dot(oa, woa_ref[...], preferred_element_type=F32)
    y_b = jnp.dot(ob_ref[...], wob_ref[...], preferred_element_type=F32)
    y_c = yc_ref[...]
    D = D_MODEL
    mix = (g_ref[:, 0:D].astype(F32) * y_a + g_ref[:, D:2 * D].astype(F32) * y_b
           + g_ref[:, 2 * D:3 * D].astype(F32) * y_c)
    o_ref[...] = h_ref[...] + jnp.dot(mix.astype(BF16), wo_ref[...], preferred_element_type=F32)


def _merge(h, oas, sas, ob, yc, gates, woa, wob, wo):
    m = h.shape[0]
    tm = MERGE_TM
    gw = DSWA_HPG * HEAD_DIM
    row = lambda w: pl.BlockSpec((tm, w), lambda i: (i, 0))
    rmrow = lambda w, r: pl.BlockSpec((tm // r, r * w), lambda i: (i, 0))
    full = lambda a: pl.BlockSpec(a.shape, lambda i: (0, 0), pipeline_mode=pl.Buffered(1))
    rs = [r for _, r in DSWA_PATTERNS]
    in_specs = ([row(D_MODEL)] + [rmrow(gw, r) for r in rs] + [rmrow(LANES, r) for r in rs]
                + [row(1024), row(D_MODEL), row(3 * D_MODEL)] + [full(woa), full(wob), full(wo)])
    return pl.pallas_call(
        _merge_kernel,
        grid=(m // tm,),
        in_specs=in_specs,
        out_specs=row(D_MODEL),
        out_shape=jax.ShapeDtypeStruct((m, D_MODEL), F32),
        scratch_shapes=[pltpu.VMEM((gw // LANES, tm, LANES), F32), pltpu.VMEM((gw // LANES, tm, LANES), F32),
                        pltpu.VMEM((tm, LANES), F32), pltpu.VMEM((tm, LANES), F32)],
        compiler_params=_cparams(("parallel",)),
        name="merge",
    )(h, *oas, *sas, ob, yc, gates, woa, wob, wo)


FFN_TM = 512
FFN_TF = 256


def _rms(x, g):
    ms = jnp.mean(x * x, axis=-1, keepdims=True)
    return (x * lax.rsqrt(ms + RMS_EPS) * g).astype(BF16)


def _ffn_ple_kernel(h_ref, p_ref, gf_ref, wg_ref, wu_ref, wd_ref, gp_ref, wpg_ref, wpp_ref, o_ref, *, tf):
    x = h_ref[...]
    u = _rms(x, gf_ref[...])
    acc = None
    for j in range(wg_ref.shape[1] // tf):
        cs = slice(j * tf, (j + 1) * tf)
        a = jnp.dot(u, wg_ref[:, cs], preferred_element_type=F32)
        b = jnp.dot(u, wu_ref[:, cs], preferred_element_type=F32)
        t = (a * jax.nn.sigmoid(a) * b).astype(BF16)
        d = jnp.dot(t, wd_ref[cs, :], preferred_element_type=F32)
        acc = d if acc is None else acc + d
    h2 = x + acc
    e = _rms(h2, gp_ref[...])
    gate = jax.nn.sigmoid(jnp.dot(e, wpg_ref[...], preferred_element_type=F32))
    proj = jnp.dot(p_ref[...].astype(BF16), wpp_ref[...], preferred_element_type=F32)
    o_ref[...] = h2 + gate * proj


def _ffn_ple(h, p, gf, wg, wu, wd, gp, wpg, wpp):
    m, d = h.shape
    tm, tf = FFN_TM, FFN_TF
    const = lambda a: pl.BlockSpec(a.shape, lambda i: (0, 0), pipeline_mode=pl.Buffered(1))
    return pl.pallas_call(
        functools.partial(_ffn_ple_kernel, tf=tf),
        grid=(m // tm,),
        in_specs=[pl.BlockSpec((tm, d), lambda i: (i, 0)), pl.BlockSpec((tm, PLE_DIM), lambda i: (i, 0)),
                  const(gf), const(wg), const(wu), const(wd), const(gp), const(wpg), const(wpp)],
        out_specs=pl.BlockSpec((tm, d), lambda i: (i, 0)),
        out_shape=jax.ShapeDtypeStruct((m, d), F32),
        compiler_params=_cparams(("parallel",)),
        name="ffn_ple",
    )(h, p, gf, wg, wu, wd, gp, wpg, wpp)


def _rel_bucket(dist):
    max_exact = REL_BUCKETS // 2
    d = jnp.maximum(dist, 0)
    df = jnp.maximum(d, 1).astype(F32)
    large = max_exact + (jnp.log(df / max_exact) / math.log(REL_MAX_DIST / max_exact)
                         * (REL_BUCKETS - max_exact)).astype(I32)
    large = jnp.minimum(large, REL_BUCKETS - 1)
    return jnp.where(d < max_exact, d, large)


def _toeplitz(rev, n_rows, n_cols):
    nh = rev.shape[0]
    lw = rev.shape[1] + 1
    w = jnp.pad(rev, ((0, 0), (0, 1)))
    s = jnp.broadcast_to(w[:, None, :], (nh, n_rows, lw)).reshape(nh, n_rows * lw)
    s = s[:, :n_rows * (lw - 1)].reshape(nh, n_rows, lw - 1)
    return s[:, :, n_rows - 1:n_rows - 1 + n_cols]


def _bias_tables(rel_bias):
    blk = DSWA_BLOCK
    dswa = []
    for g, (_, r) in enumerate(DSWA_PATTERNS):
        delta = np.arange(3 * blk - 1)[::-1] - (blk - 1)
        rev = rel_bias[_rel_bucket(jnp.asarray(delta * r, I32))][:, g * DSWA_HPG:(g + 1) * DSWA_HPG]
        dswa.append(_toeplitz(rev.T, blk, 2 * blk))
    tq = DSA_TQ
    dist = np.arange(DSA_Z + tq - 1)[::-1] - (DSA_Z - 1) + DSA_Z0
    rev = rel_bias[_rel_bucket(jnp.asarray(dist, I32))][:, DSWA_HEADS:] * math.log2(math.e)
    bias_t = jnp.transpose(_toeplitz(rev.T, tq, DSA_Z), (2, 0, 1)).reshape(DSA_Z, DSA_HEADS * tq)
    return dswa, bias_t


def _pad_cols(w, width):
    return jnp.pad(w, ((0, 0), (0, width - w.shape[1])))


def _layer_params(w_in, qn_a, kn_a, qn_c, kn_c, w_alpha2, b_alpha):
    offs = np.cumsum((0,) + IN_WIDTHS)
    parts = [w_in[:, offs[i]:offs[i + 1]] for i in range(len(IN_WIDTHS))]
    a_q, a_k, a_v, b_q, b_k, b_v, b_r, b_al, c_q, c_k, c_v, i_q, i_k, i_w = parts
    w1 = _pad_cols(jnp.concatenate([a_q, a_k, c_q, c_k], axis=1), W1).astype(BF16)
    w2 = _pad_cols(jnp.concatenate(
        [b_v, b_r, b_q, b_k, i_q, a_v, _pad_cols(b_al, LANES), _pad_cols(c_v, LANES),
         _pad_cols(i_k, LANES), _pad_cols(i_w, LANES)], axis=1), W2).astype(BF16)
    scale = HEAD_DIM ** -0.5
    gain1 = jnp.concatenate([jnp.tile(qn_a, DSWA_HEADS) * scale, jnp.tile(kn_a, DSWA_HEADS),
                             jnp.tile(qn_c, DSA_HEADS) * (scale * math.log2(math.e)), kn_c])
    gain1 = jnp.pad(gain1, (0, W1 - gain1.shape[0])).reshape(1, W1)
    wa = jnp.pad(w_alpha2, ((0, LANES - GLA_RANK), (0, 0))).astype(BF16)
    return w1, w2, gain1, wa, b_alpha.reshape(1, -1)


def kernel(x, p, rel_bias, norm_mix, w_in, qn_a, kn_a, qn_c, kn_c, w_alpha2, b_alpha, gla_norm, w_out_a, w_out_b, w_out_c, w_gate, b_gate, w_o, norm_ffn, w_ffn_gate, w_ffn_up, w_ffn_down, norm_ple, w_ple_gate, w_ple_proj):
    bsz, seq, d = x.shape
    depth = p.shape[0]
    m = bsz * seq
    dswa_bias, bias_t = _bias_tables(rel_bias)
    h = x.reshape(m, d)
    zeros_w2 = jnp.zeros((1, W2), F32)
    for i in range(depth):
        w1, w2, gain1, wa, ba = _layer_params(w_in[i], qn_a[i], kn_a[i], qn_c[i], kn_c[i], w_alpha2[i], b_alpha[i])
        gmix = norm_mix[i].reshape(1, d)
        gw = DSWA_HPG * HEAD_DIM
        dil = [(g, r) for g, (_, r) in enumerate(DSWA_PATTERNS) if r > 1]
        p1, *qk_rm = _proj(h, gmix, w1, gain1, "qk",
                           regroup=[(off + g * gw, r) for off in (P1_AQ, P1_AK) for g, r in dil])
        p2, *v_rm = _proj(h, gmix, w2, zeros_w2, "plain", regroup=[(P2_AV + g * gw, r) for g, r in dil])
        gates = _proj(h, gmix, w_gate[i].astype(BF16), b_gate[i].reshape(1, -1), "gate")
        rm = {g: (qk_rm[n], qk_rm[len(dil) + n], v_rm[n]) for n, (g, r) in enumerate(dil)}
        oas, sas = [], []
        for g, (_, r) in enumerate(DSWA_PATTERNS):
            o, st = _dswa(p1, p2, rm.get(g), dswa_bias[g], g, r, bsz, seq)
            oas.append(o)
            sas.append(st)
        ob = _gla(p2, wa, ba, gla_norm[i].reshape(1, -1), bsz, seq)
        cv = p2[:, P2_CV:P2_CV + HEAD_DIM].reshape(bsz, seq // DSA_CH, DSA_CH, HEAD_DIM)
        ones_pad = jnp.zeros((bsz, seq // DSA_CH, DSA_VR - HEAD_DIM, DSA_CH), BF16).at[:, :, 0, :].set(1.0)
        vt = jnp.concatenate([jnp.transpose(cv, (0, 1, 3, 2)), ones_pad], axis=2)
        yc = _dsa(p1, p2, vt, bias_t, w_out_c[i].T.astype(BF16), bsz, seq)
        h = _merge(h, oas, sas, ob, yc, gates, w_out_a[i].astype(BF16), w_out_b[i].astype(BF16),
                   w_o[i].astype(BF16))
        h = _ffn_ple(h, p[i].reshape(m, PLE_DIM), norm_ffn[i].reshape(1, d), w_ffn_gate[i].astype(BF16),
                     w_ffn_up[i].astype(BF16), w_ffn_down[i].astype(BF16), norm_ple[i].reshape(1, d),
                     w_ple_gate[i].astype(BF16), w_ple_proj[i].astype(BF16))
    return h.reshape(bsz, seq, d)
```

```python
import functools
import math

import numpy as np
import jax
import jax.numpy as jnp
from jax import lax
from jax.experimental import pallas as pl
from jax.experimental.pallas import tpu as pltpu

F32 = jnp.float32
BF16 = jnp.bfloat16
I32 = jnp.int32
I16 = jnp.int16

D_MODEL = 1024
HEAD_DIM = 64
RMS_EPS = 1e-6
DSWA_PATTERNS = ((128, 1), (512, 4), (2048, 16))
DSWA_HPG = 4
DSWA_HEADS = 12
DSWA_BLOCK = 128
GLA_HEADS = 4
GLA_DK = 128
GLA_DV = 256
GLA_RANK = 16
GLA_TAU = 16.0
GLA_CHUNK = 64
DSA_HEADS = 12
IDX_HEADS = 8
IDX_DIM = 64
IDX_TOPK = 256
REL_BUCKETS = 32
REL_MAX_DIST = 2048
D_FF = 2816
PLE_DIM = 256
IN_WIDTHS = (768, 768, 768, 512, 512, 1024, 1024, 16, 768, 64, 64, 512, 64, 8)

LANES = 128
MXU_N = 256
VMEM_LIMIT = 56 * 1024 * 1024

W1 = 2560
P1_AQ, P1_AK, P1_CQ, P1_CK = 0, 768, 1536, 2304
W2 = 5120
P2_BV, P2_BR, P2_BQ, P2_BK, P2_IQ, P2_AV = 0, 1024, 2048, 2560, 3072, 3584
P2_BAL, P2_CV, P2_IK, P2_IW = 4352, 4480, 4608, 4736

PROJ_TM = 512
PROJ_TN = 512

DSWA_QB = 4
DSA_TQ = 128
DSA_CH = 256
DSA_AB = 256
DSA_VR = 80
DSA_BIAS_CONST_FROM = 1512
DSA_Z0 = 1792
DSA_Z = DSA_Z0 + DSA_CH
INT_MIN = -(2 ** 31)
NEG_BIG = -1e30


def _cparams(sem, flags=None):
    return pltpu.CompilerParams(dimension_semantics=sem, vmem_limit_bytes=VMEM_LIMIT, flags=flags)


def _proj_kernel(h_ref, g_ref, w_ref, e_ref, *rest, mode, tn, regroup):
    if mode == "qk":
        bd_ref, rest = rest[0], rest[1:]
    o_ref, rg_refs = rest[0], rest[1:1 + len(regroup)]
    scr_ref = rest[-1] if regroup else None
    tm = h_ref.shape[0]
    gw = DSWA_HPG * HEAD_DIM
    x = h_ref[...]
    ms = jnp.mean(x * x, axis=-1, keepdims=True)
    u = (x * lax.rsqrt(ms + RMS_EPS) * g_ref[...]).astype(BF16)
    for j in range(w_ref.shape[1] // tn):
        cs = slice(j * tn, (j + 1) * tn)
        acc = jnp.dot(u, w_ref[:, cs], preferred_element_type=F32)
        if mode == "plain":
            out = acc
        elif mode == "gate":
            out = jax.nn.sigmoid(acc + e_ref[:, cs])
        else:
            sq = acc * acc
            hi = sq.astype(BF16)
            lo = (sq - hi.astype(F32)).astype(BF16)
            bw = bd_ref.shape[0]
            ss = jnp.concatenate(
                [jnp.dot(hi[:, k:k + bw], bd_ref[...], preferred_element_type=F32)
                 + jnp.dot(lo[:, k:k + bw], bd_ref[...], preferred_element_type=F32) for k in range(0, tn, bw)],
                axis=1)
            out = acc * lax.rsqrt(ss * (1.0 / HEAD_DIM) + RMS_EPS) * e_ref[:, cs]
        o_ref[:, cs] = out.astype(o_ref.dtype)
        for (col, r), rg_ref in zip(regroup, rg_refs):
            if col // tn == j:
                for k in range(gw // LANES):
                    scr_ref[k] = out[:, col % tn + k * LANES:col % tn + (k + 1) * LANES]
                for c in range(r):
                    for k in range(gw // LANES):
                        rg_ref[:, c * gw + k * LANES:c * gw + (k + 1) * LANES] = (
                            scr_ref[k, pl.ds(c, tm // r, stride=r), :].astype(rg_ref.dtype))


def _proj(h, g, w, e, mode, regroup=()):
    m, d = h.shape
    n = w.shape[1]
    tm, tn = PROJ_TM, PROJ_TN
    gw = DSWA_HPG * HEAD_DIM
    const = lambda shape: pl.BlockSpec(shape, lambda i: (0, 0), pipeline_mode=pl.Buffered(1))
    in_specs = [pl.BlockSpec((tm, d), lambda i: (i, 0)), const((1, d)), const((d, n)), const((1, n))]
    args = [h, g, w, e]
    if mode == "qk":
        r = np.arange(MXU_N) // HEAD_DIM
        bd = jnp.asarray((r[:, None] == r[None, :]).astype(np.float32), dtype=BF16)
        in_specs.append(const((MXU_N, MXU_N)))
        args.append(bd)
    out_specs = [pl.BlockSpec((tm, n), lambda i: (i, 0))]
    out_shape = [jax.ShapeDtypeStruct((m, n), BF16)]
    for _, r in regroup:
        out_specs.append(pl.BlockSpec((tm // r, r * gw), lambda i: (i, 0)))
        out_shape.append(jax.ShapeDtypeStruct((m // r, r * gw), BF16))
    outs = pl.pallas_call(
        functools.partial(_proj_kernel, mode=mode, tn=tn, regroup=tuple(regroup)),
        grid=(m // tm,),
        in_specs=in_specs,
        out_specs=out_specs,
        out_shape=out_shape,
        scratch_shapes=[pltpu.VMEM((gw // LANES, tm, LANES), F32)] if regroup else [],
        compiler_params=_cparams(("parallel",)),
        name="proj_" + mode,
    )(*args)
    return outs if regroup else outs[0]


def _dswa_kernel(q_ref, kp_ref, kc_ref, vp_ref, vc_ref, bias_ref, o_ref, st_ref, *, qb):
    i = pl.program_id(2)
    blk = DSWA_BLOCK
    row = lax.broadcasted_iota(I32, (blk, 2 * blk), 0)
    col = lax.broadcasted_iota(I32, (blk, 2 * blk), 1)
    cur_ok = (col >= blk) & ((col - blk) <= row)
    valid_inner = ((col < blk) & (col >= row)) | cur_ok
    prev_off = jnp.where(i > 0, 0, 4 * blk)
    valid_first = ((col < blk) & (col >= row + prev_off)) | cur_ok
    lane = lax.broadcasted_iota(I32, (blk, LANES), 1)
    for j in range(qb):
        rs = slice(j * blk, (j + 1) * blk)
        ps = slice((j - 1) * blk, j * blk)
        valid = valid_first if j == 0 else valid_inner
        q = q_ref[rs, :]
        k = jnp.concatenate([kp_ref[...] if j == 0 else kc_ref[ps, :], kc_ref[rs, :]], axis=0)
        v = jnp.concatenate([vp_ref[...] if j == 0 else vc_ref[ps, :], vc_ref[rs, :]], axis=0)
        stats = jnp.zeros((blk, LANES), F32)
        for h in range(DSWA_HPG):
            sl = slice(h * HEAD_DIM, (h + 1) * HEAD_DIM)
            s = lax.dot_general(q[:, sl], k[:, sl], (((1,), (1,)), ((), ())), preferred_element_type=F32)
            s = jnp.where(valid, s + bias_ref[h], -jnp.inf)
            m = jnp.max(s, axis=-1, keepdims=True)
            p = jnp.exp(s - m)
            l = jnp.sum(p, axis=-1, keepdims=True)
            o = jnp.dot(p.astype(BF16), v[:, sl], preferred_element_type=F32) / l
            o_ref[rs, sl] = o
            stats = jnp.where(lane == h, m, stats)
            stats = jnp.where(lane == DSWA_HPG + h, l, stats)
        st_ref[rs, :] = stats


def _dswa(p1, p2, rm, bias, g, r, bsz, seq):
    blk = DSWA_BLOCK
    L = seq // r
    nblk = L // blk
    gw = DSWA_HPG * HEAD_DIM
    if r == 1:
        p1v = p1.reshape(bsz, L, W1)
        p2v = p2.reshape(bsz, L, W2)
        qv, kv, vv = p1v, p1v, p2v
        q_col = lambda c: (P1_AQ // gw) + g
        k_col = lambda c: (P1_AK // gw) + g
        v_col = lambda c: (P2_AV // gw) + g
    else:
        qv, kv, vv = (a.reshape(bsz, L, r * gw) for a in rm)
        q_col = k_col = v_col = lambda c: c
    qb = min(DSWA_QB, nblk)
    prev = lambda i: jnp.maximum(i * qb - 1, 0)
    in_specs = [
        pl.BlockSpec((None, qb * blk, gw), lambda b, c, i: (b, i, q_col(c))),
        pl.BlockSpec((None, blk, gw), lambda b, c, i: (b, prev(i), k_col(c))),
        pl.BlockSpec((None, qb * blk, gw), lambda b, c, i: (b, i, k_col(c))),
        pl.BlockSpec((None, blk, gw), lambda b, c, i: (b, prev(i), v_col(c))),
        pl.BlockSpec((None, qb * blk, gw), lambda b, c, i: (b, i, v_col(c))),
        pl.BlockSpec((DSWA_HPG, blk, 2 * blk), lambda b, c, i: (0, 0, 0)),
    ]
    out_specs = [
        pl.BlockSpec((None, qb * blk, gw), lambda b, c, i: (b, i, c)),
        pl.BlockSpec((None, qb * blk, LANES), lambda b, c, i: (b, i, c)),
    ]
    o, st = pl.pallas_call(
        functools.partial(_dswa_kernel, qb=qb),
        grid=(bsz, r, nblk // qb),
        in_specs=in_specs,
        out_specs=out_specs,
        out_shape=[jax.ShapeDtypeStruct((bsz, L, r * gw), F32),
                   jax.ShapeDtypeStruct((bsz, L, r * LANES), F32)],
        compiler_params=_cparams(("parallel", "parallel", "arbitrary")),
        name="dswa_g%d" % g,
    )(qv, kv, kv, vv, vv, bias)
    return o.reshape(bsz * L, r * gw), st.reshape(bsz * L, r * LANES)


GLA_TC = 256
GLA_GB = 2


def _split3(x):
    a1 = x.astype(BF16)
    r1 = x - a1.astype(F32)
    a2 = r1.astype(BF16)
    a3 = (r1 - a2.astype(F32)).astype(BF16)
    return a1, a2, a3


def _gla_kernel(v_ref, r_ref, q_ref, k_ref, al_ref, wa_ref, ba_ref, gn_ref, o_ref, st_ref, *, gb):
    C = GLA_CHUNK

    @pl.when(pl.program_id(1) == 0)
    def _():
        st_ref[...] = jnp.zeros_like(st_ref)

    TC = GLA_TC
    NC = TC // C
    row = lax.broadcasted_iota(I32, (TC, TC), 0)
    col = lax.broadcasted_iota(I32, (TC, TC), 1)
    tri = ((row // C) == (col // C)) & (row >= col)
    tri_bf = jnp.where(tri, 1.0, 0.0).astype(BF16)
    nt = (((1,), (1,)), ((), ()))
    tn = (((0,), (0,)), ((), ()))
    for bb in range(gb):
        z = jnp.dot(al_ref[bb], wa_ref[...], preferred_element_type=F32) + ba_ref[...]
        la = (jnp.minimum(z, 0.0) - jnp.log(1.0 + jnp.exp(-jnp.abs(z)))) * (1.0 / GLA_TAU)
        a1, a2, a3 = _split3(la)
        bcum = (jnp.dot(tri_bf, a1, preferred_element_type=F32)
                + jnp.dot(tri_bf, a2, preferred_element_type=F32)
                + jnp.dot(tri_bf, a3, preferred_element_type=F32))
        blast = [bcum[(c + 1) * C - 1:(c + 1) * C, :] for c in range(NC)]
        blast_rows = jnp.concatenate([jnp.broadcast_to(b, (C, b.shape[1])) for b in blast], axis=0)
        qf = q_ref[bb].astype(F32) * (GLA_DK ** -0.5)
        kf = k_ref[bb].astype(F32)
        q_in = (qf * jnp.exp(bcum)).astype(BF16)
        k_in = (kf * jnp.exp(-bcum)).astype(BF16)
        k_end = (kf * jnp.exp(blast_rows - bcum)).astype(BF16)
        dec = [jnp.exp(b) for b in blast]
        for h in range(GLA_HEADS):
            ks = slice(h * GLA_DK, (h + 1) * GLA_DK)
            vs = slice(h * GLA_DV, (h + 1) * GLA_DV)
            vh = v_ref[bb, :, vs]
            att = lax.dot_general(q_in[:, ks], k_in[:, ks], nt, preferred_element_type=F32)
            att = jnp.where(tri, att, 0.0)
            o = jnp.dot(att.astype(BF16), vh, preferred_element_type=F32)
            st = st_ref[bb, h]
            inter = []
            for c in range(NC):
                rs = slice(c * C, (c + 1) * C)
                inter.append(lax.dot_general(q_in[rs, ks], st.astype(BF16), nt, preferred_element_type=F32))
                st = st * dec[c][:, ks] + lax.dot_general(vh[rs, :], k_end[rs, ks], tn, preferred_element_type=F32)
            st_ref[bb, h] = st
            o = o + jnp.concatenate(inter, axis=0)
            ms = jnp.mean(o * o, axis=-1, keepdims=True)
            y = o * lax.rsqrt(ms + RMS_EPS) * gn_ref[...]
            rg = r_ref[bb, :, vs].astype(F32)
            y = y * (rg * jax.nn.sigmoid(rg))
            o_ref[bb, :, vs] = y.astype(o_ref.dtype)


def _gla(p2, wa, ba, gn, bsz, seq):
    tc = GLA_TC
    gb = GLA_GB if bsz % GLA_GB == 0 else 1
    p2v = p2.reshape(bsz, seq, W2)
    in_specs = [
        pl.BlockSpec((gb, tc, 1024), lambda b, t: (b, t, P2_BV // 1024)),
        pl.BlockSpec((gb, tc, 1024), lambda b, t: (b, t, P2_BR // 1024)),
        pl.BlockSpec((gb, tc, 512), lambda b, t: (b, t, P2_BQ // 512)),
        pl.BlockSpec((gb, tc, 512), lambda b, t: (b, t, P2_BK // 512)),
        pl.BlockSpec((gb, tc, LANES), lambda b, t: (b, t, P2_BAL // LANES)),
        pl.BlockSpec((LANES, 512), lambda b, t: (0, 0)),
        pl.BlockSpec((1, 512), lambda b, t: (0, 0)),
        pl.BlockSpec((1, GLA_DV), lambda b, t: (0, 0)),
    ]
    out = pl.pallas_call(
        functools.partial(_gla_kernel, gb=gb),
        grid=(bsz // gb, seq // tc),
        in_specs=in_specs,
        out_specs=pl.BlockSpec((gb, tc, 1024), lambda b, t: (b, t, 0)),
        out_shape=jax.ShapeDtypeStruct((bsz, seq, 1024), BF16),
        scratch_shapes=[pltpu.VMEM((gb, GLA_HEADS, GLA_DV, GLA_DK), F32)],
        compiler_params=_cparams(("parallel", "arbitrary")),
        name="gla",
    )(p2v, p2v, p2v, p2v, p2v, wa, ba, gn)
    return out.reshape(bsz * seq, 1024)


def _tree_sum(xs):
    xs = list(xs)
    while len(xs) > 1:
        xs = [xs[i] + xs[i + 1] for i in range(0, len(xs) - 1, 2)] + ([xs[-1]] if len(xs) % 2 else [])
    return xs[0]


def _dsa_kernel(q_ref, iq_ref, iw_ref, k_ref, ik_ref, vt_ref, bias_ref, wot_ref, o_ref,
                keys_ref, planes_ref, qaug_ref, iqall_ref, acc_ref, st0_ref, st1_ref, m_ref, cm0_ref, cm1_ref,
                *, topk):
    TQ, CH = DSA_TQ, DSA_CH
    CC = 2 * CH
    qi = pl.program_id(1)
    t0 = qi * TQ
    nch = qi // (CH // TQ) + 1
    npair = (nch + 1) // 2
    nt = (((1,), (1,)), ((), ()))

    eye = (lax.broadcasted_iota(I32, (TQ, TQ), 0) == lax.broadcasted_iota(I32, (TQ, TQ), 1))
    eye = jnp.where(eye, 1.0, 0.0).astype(BF16)
    q_t = q_ref[...].astype(F32).T.astype(BF16)
    iq_t = iq_ref[...].astype(F32).T.astype(BF16)
    for h in range(DSA_HEADS):
        cs = slice(h * TQ, (h + 1) * TQ)
        qaug_ref[0:TQ, cs] = eye
        qaug_ref[TQ:TQ + HEAD_DIM, cs] = q_t[h * HEAD_DIM:(h + 1) * HEAD_DIM, :]
        qaug_ref[TQ + HEAD_DIM:, cs] = jnp.zeros((TQ - HEAD_DIM, TQ), BF16)
    for h in range(IDX_HEADS):
        cs = slice(h * TQ, (h + 1) * TQ)
        iqall_ref[0:IDX_DIM, cs] = iq_t[h * IDX_DIM:(h + 1) * IDX_DIM, :]
        iqall_ref[IDX_DIM:, cs] = jnp.zeros((LANES - IDX_DIM, TQ), BF16)
    idx_scale = (IDX_HEADS ** -0.5) * (IDX_DIM ** -0.5)
    wt = (iw_ref[...].astype(F32) * idx_scale).T

    qpos = t0 + lax.broadcasted_iota(I32, (1, TQ), 1)
    RB = 128
    krow1 = lax.broadcasted_iota(I32, (RB, TQ), 0)
    krow = lax.broadcasted_iota(I32, (CH, TQ), 0)
    krow2 = lax.broadcasted_iota(I32, (CC, TQ), 0)

    def score_body(cp, carry):
        for sub in range(CC // RB):
            s0 = pl.multiple_of(cp * CC + sub * RB, RB)
            ikc = ik_ref[pl.ds(s0, RB), :]
            s = None
            for hp in range(IDX_HEADS // 2):
                x = jnp.dot(ikc, iqall_ref[:, hp * 2 * TQ:(hp + 1) * 2 * TQ],
                            preferred_element_type=F32)
                t = (jnp.maximum(x[:, :TQ], 0.0) * wt[2 * hp:2 * hp + 1, :]
                     + jnp.maximum(x[:, TQ:], 0.0) * wt[2 * hp + 1:2 * hp + 2, :])
                s = t if s is None else s + t
            s = jnp.where(s == 0.0, 0.0, s)
            bits = pltpu.bitcast(s, I32)
            key = bits ^ ((bits >> 31) & 0x7FFFFFFF)
            key = jnp.where(s0 + krow1 <= qpos, key, INT_MIN)
            keys_ref[pl.ds(s0, RB), :] = key
        return carry

    lax.fori_loop(0, npair, score_body, 0)

    kvec = jnp.minimum(topk, qpos + 1)
    NPL = 32

    def planes_body(c, carry):
        s0 = pl.multiple_of(c * CH, CH)
        a = [keys_ref[pl.ds(s0 + 8 * j, 8), :] ^ INT_MIN for j in range(NPL)]
        j, m = 16, 0x0000FFFF
        while j:
            sh = jnp.full((8, TQ), j, I32)
            k = 0
            while k < NPL:
                t = (a[k] ^ lax.shift_right_logical(a[k + j], sh)) & m
                a[k] = a[k] ^ t
                a[k + j] = a[k + j] ^ lax.shift_left(t, sh)
                k = (k + j + 1) & ~j
            j >>= 1
            m = (m ^ (m << j)) & 0xFFFFFFFF
        for p in range(NPL):
            planes_ref[c, p] = a[p]
        return carry

    def empty_body(c, carry):
        for p in range(NPL):
            planes_ref[c, p] = jnp.zeros((8, TQ), I32)
        return carry

    nck = 2 * npair
    NCK = keys_ref.shape[0] // CH
    lax.fori_loop(0, nck, planes_body, 0)
    lax.fori_loop(nck, NCK, empty_body, 0)

    def plane_body(p, carry):
        thr_u, n_gt, alive = carry
        hit = [alive[c] & planes_ref[c, p] for c in range(NCK)]
        ones = jnp.sum(_tree_sum([lax.population_count(h) for h in hit]), axis=0, keepdims=True)
        take = (n_gt + ones) >= kvec
        thr_u = jnp.where(take, thr_u | (jnp.int32(1) << (31 - p)), thr_u)
        n_gt = jnp.where(take, n_gt, n_gt + ones)
        alive = tuple(jnp.where(take, hit[c], alive[c] ^ hit[c]) for c in range(NCK))
        return thr_u, n_gt, alive

    zero = jnp.zeros((1, TQ), I32)
    alive0 = tuple(jnp.full((8, TQ), -1, I32) for _ in range(NCK))
    thr_u, n_gt, alive = lax.fori_loop(0, NPL, plane_body, (zero, zero, alive0))
    n_eq = jnp.sum(_tree_sum([lax.population_count(a) for a in alive]), axis=0, keepdims=True)
    thr = thr_u ^ INT_MIN

    def count(pred_fn):
        def body(c, acc):
            s0 = pl.multiple_of(c * CC, CC)
            kk = keys_ref[pl.ds(s0, CC), :]
            hit = jnp.where(pred_fn(kk, s0), 1, 0).astype(I32)
            return acc + jnp.sum(hit.reshape(CC // 8, 8, TQ), axis=0)
        acc = lax.fori_loop(0, npair, body, jnp.zeros((8, TQ), I32))
        return jnp.sum(acc, axis=0, keepdims=True)

    excess = n_gt + n_eq - kvec
    has_excess = jnp.max(excess) > 0

    @pl.when(has_excess)
    def _():
        need = kvec - n_gt

        def tie_lt(cut):
            return count(lambda kk, s0: (kk == thr) & (s0 + krow2 < cut))

        def cut_body(b, cut):
            cand = cut | (jnp.int32(1) << (12 - b))
            return jnp.where(tie_lt(cand) <= need, cand, cut)

        cut = lax.fori_loop(0, 13, cut_body, jnp.zeros((1, TQ), I32))

        def drop_body(c, carry):
            s0 = pl.multiple_of(c * CH, CH)
            kk = keys_ref[pl.ds(s0, CH), :]
            keys_ref[pl.ds(s0, CH), :] = jnp.where((kk == thr) & (s0 + krow >= cut), INT_MIN, kk)
            return carry

        lax.fori_loop(0, nch, drop_body, 0)

    acc_ref[...] = jnp.zeros_like(acc_ref)

    NP = DSA_HEADS // 2
    PW = 2 * TQ

    AB = DSA_AB
    NB = CH // AB
    last_chunk = keys_ref.shape[0] // CH - 1

    def logits(c, st_ref, cm_ref):
        s0 = pl.multiple_of(c * CH, CH)
        zoff = pl.multiple_of(jnp.clip(s0 - t0 + DSA_Z0, 0, DSA_Z0), 8)
        kaug = []
        for rb in range(NB):
            rs = pl.ds(s0 + rb * AB, AB)
            pen = jnp.where(keys_ref[rs, :] >= thr, 0.0, NEG_BIG).astype(BF16)
            kaug.append(jnp.concatenate([pen, k_ref[rs, :]], axis=1))
        for hp in range(NP):
            ps = slice(hp * PW, (hp + 1) * PW)
            cm = None
            for rb in range(NB):
                st = jnp.dot(kaug[rb], qaug_ref[:, ps], preferred_element_type=F32)
                st = st + bias_ref[pl.ds(zoff + rb * AB, AB), ps]
                st_ref[rb * AB:(rb + 1) * AB, ps] = st
                tm = jnp.max(st.reshape(AB // 8, 8, PW), axis=0)
                cm = tm if cm is None else jnp.maximum(cm, tm)
            cm_ref[0:1, ps] = jnp.max(cm, axis=0, keepdims=True)

    def accumulate(c, st_ref, cm_ref):
        vtc = vt_ref[c]
        for hp in range(NP):
            ps = slice(hp * PW, (hp + 1) * PW)
            m_old = m_ref[0:1, ps]
            m_new = jnp.maximum(m_old, cm_ref[0:1, ps])
            m_ref[0:1, ps] = m_new
            alpha = jnp.exp2(m_old - m_new)
            p = jnp.concatenate(
                [jnp.exp2(st_ref[rb * AB:(rb + 1) * AB, ps] - m_new).astype(BF16) for rb in range(NB)], axis=0)
            acc_ref[:, ps] = acc_ref[:, ps] * alpha + jnp.dot(vtc, p, preferred_element_type=F32)

    def att_body(cp, carry):
        c0 = 2 * cp
        logits(c0 + 1, st1_ref, cm1_ref)
        accumulate(c0, st0_ref, cm0_ref)
        logits(jnp.minimum(c0 + 2, last_chunk), st0_ref, cm0_ref)
        accumulate(c0 + 1, st1_ref, cm1_ref)
        return carry

    m_ref[...] = jnp.full(m_ref.shape, NEG_BIG, F32)
    logits(0, st0_ref, cm0_ref)
    lax.fori_loop(0, npair, att_body, 0)

    ot = jnp.concatenate(
        [(acc_ref[0:HEAD_DIM, h * TQ:(h + 1) * TQ] / acc_ref[HEAD_DIM:HEAD_DIM + 1, h * TQ:(h + 1) * TQ]).astype(BF16)
         for h in range(DSA_HEADS)], axis=0)
    yt = jnp.dot(wot_ref[...], ot, preferred_element_type=F32)
    o_ref[...] = yt.T


def _dsa(p1, p2, vt, bias_t, wot, bsz, seq):
    tq = DSA_TQ
    p1v = p1.reshape(bsz, seq, W1)
    p2v = p2.reshape(bsz, seq, W2)
    nq = DSA_HEADS * tq
    in_specs = [
        pl.BlockSpec((None, tq, 768), lambda b, i: (b, i, P1_CQ // 768)),
        pl.BlockSpec((None, tq, 512), lambda b, i: (b, i, P2_IQ // 512)),
        pl.BlockSpec((None, tq, LANES), lambda b, i: (b, i, P2_IW // LANES)),
        pl.BlockSpec((None, seq, LANES), lambda b, i: (b, 0, P1_CK // LANES)),
        pl.BlockSpec((None, seq, LANES), lambda b, i: (b, 0, P2_IK // LANES)),
        pl.BlockSpec((None, seq // DSA_CH, DSA_VR, DSA_CH), lambda b, i: (b, 0, 0, 0)),
        pl.BlockSpec((DSA_Z, nq), lambda b, i: (0, 0)),
        pl.BlockSpec((D_MODEL, DSA_HEADS * HEAD_DIM), lambda b, i: (0, 0)),
    ]
    out = pl.pallas_call(
        functools.partial(_dsa_kernel, topk=min(IDX_TOPK, seq // 4)),
        grid=(bsz, seq // tq),
        in_specs=in_specs,
        out_specs=pl.BlockSpec((None, tq, D_MODEL), lambda b, i: (b, i, 0)),
        out_shape=jax.ShapeDtypeStruct((bsz, seq, D_MODEL), F32),
        scratch_shapes=[
            pltpu.VMEM((seq, tq), I32),
            pltpu.VMEM((seq // DSA_CH, 32, 8, tq), I32),
            pltpu.VMEM((2 * tq, nq), BF16),
            pltpu.VMEM((LANES, IDX_HEADS * tq), BF16),
            pltpu.VMEM((DSA_VR, nq), F32),
            pltpu.VMEM((DSA_CH, nq), F32),
            pltpu.VMEM((DSA_CH, nq), F32),
            pltpu.VMEM((8, nq), F32),
            pltpu.VMEM((8, nq), F32),
            pltpu.VMEM((8, nq), F32),
        ],
        compiler_params=_cparams(("parallel", "arbitrary")),
        name="dsa",
    )(p1v, p2v, p2v, p1v, p2v, vt, bias_t, wot)
    return out.reshape(bsz * seq, D_MODEL)


MERGE_TM = 512


def _merge_kernel(h_ref, oa0_ref, oa1_ref, oa2_ref, sa0_ref, sa1_ref, sa2_ref, ob_ref, yc_ref, g_ref,
                  woa_ref, wob_ref, wo_ref, o_ref, on1_ref, on2_ref, sn1_ref, sn2_ref):
    H = DSWA_HPG
    tm = h_ref.shape[0]
    gw = H * HEAD_DIM
    for (_, r), src_o, src_s, dst_o, dst_s in zip(DSWA_PATTERNS[1:], (oa1_ref, oa2_ref), (sa1_ref, sa2_ref),
                                                  (on1_ref, on2_ref), (sn1_ref, sn2_ref)):
        for c in range(r):
            for k in range(gw // LANES):
                dst_o[k, pl.ds(c, tm // r, stride=r), :] = src_o[:, c * gw + k * LANES:c * gw + (k + 1) * LANES]
            dst_s[pl.ds(c, tm // r, stride=r), :] = src_s[:, c * LANES:(c + 1) * LANES]
    oa_vals = (oa0_ref[...],) + tuple(
        jnp.concatenate([ref[k] for k in range(gw // LANES)], axis=1) for ref in (on1_ref, on2_ref))
    sts = [sa0_ref[...], sn1_ref[...], sn2_ref[...]]
    lane = lax.broadcasted_iota(I32, sts[0].shape, 1)
    mmax = jnp.maximum(jnp.maximum(sts[0], sts[1]), sts[2])
    wts = [pltpu.roll(s, LANES - H, 1) * jnp.exp(s - mmax) for s in sts]
    tot = wts[0] + wts[1] + wts[2]
    hrow = lax.broadcasted_iota(I32, (LANES, H * HEAD_DIM), 0)
    hcol = lax.broadcasted_iota(I32, (LANES, H * HEAD_DIM), 1) // HEAD_DIM
    expand = jnp.where(hrow == hcol, 1.0, 0.0).astype(BF16)
    oa = None
    for g in range(3):
        w = jnp.where(lane < H, wts[g] / tot, 0.0)
        hi = w.astype(BF16)
        lo = (w - hi.astype(F32)).astype(BF16)
        wfull = (jnp.dot(hi, expand, preferred_element_type=F32)
                 + jnp.dot(lo, expand, preferred_element_type=F32))
        term = wfull * oa_vals[g]
        oa = term if oa is None else oa + term
    oa = oa.astype(BF16)
    y_a = jnp.dot(oa, woa_ref[...], preferred_element_type=F32)
    y_b = jnp.dot(ob_ref[...], wob_ref[...], preferred_element_type=F32)
    y_c = yc_ref[...]
    D = D_MODEL
    mix = (g_ref[:, 0:D].astype(F32) * y_a + g_ref[:, D:2 * D].astype(F32) * y_b
           + g_ref[:, 2 * D:3 * D].astype(F32) * y_c)
    o_ref[...] = h_ref[...] + jnp.dot(mix.astype(BF16), wo_ref[...], preferred_element_type=F32)


def _merge(h, oas, sas, ob, yc, gates, woa, wob, wo):
    m = h.shape[0]
    tm = MERGE_TM
    gw = DSWA_HPG * HEAD_DIM
    row = lambda w: pl.BlockSpec((tm, w), lambda i: (i, 0))
    rmrow = lambda w, r: pl.BlockSpec((tm // r, r * w), lambda i: (i, 0))
    full = lambda a: pl.BlockSpec(a.shape, lambda i: (0, 0), pipeline_mode=pl.Buffered(1))
    rs = [r for _, r in DSWA_PATTERNS]
    in_specs = ([row(D_MODEL)] + [rmrow(gw, r) for r in rs] + [rmrow(LANES, r) for r in rs]
                + [row(1024), row(D_MODEL), row(3 * D_MODEL)] + [full(woa), full(wob), full(wo)])
    return pl.pallas_call(
        _merge_kernel,
        grid=(m // tm,),
        in_specs=in_specs,
        out_specs=row(D_MODEL),
        out_shape=jax.ShapeDtypeStruct((m, D_MODEL), F32),
        scratch_shapes=[pltpu.VMEM((gw // LANES, tm, LANES), F32), pltpu.VMEM((gw // LANES, tm, LANES), F32),
                        pltpu.VMEM((tm, LANES), F32), pltpu.VMEM((tm, LANES), F32)],
        compiler_params=_cparams(("parallel",)),
        name="merge",
    )(h, *oas, *sas, ob, yc, gates, woa, wob, wo)


FFN_TM = 512
FFN_TF = 256


def _rms(x, g):
    ms = jnp.mean(x * x, axis=-1, keepdims=True)
    return (x * lax.rsqrt(ms + RMS_EPS) * g).astype(BF16)


def _ffn_ple_kernel(h_ref, p_ref, gf_ref, wg_ref, wu_ref, wd_ref, gp_ref, wpg_ref, wpp_ref, o_ref, *, tf):
    x = h_ref[...]
    u = _rms(x, gf_ref[...])
    acc = None
    for j in range(wg_ref.shape[1] // tf):
        cs = slice(j * tf, (j + 1) * tf)
        a = jnp.dot(u, wg_ref[:, cs], preferred_element_type=F32)
        b = jnp.dot(u, wu_ref[:, cs], preferred_element_type=F32)
        t = (a * jax.nn.sigmoid(a) * b).astype(BF16)
        d = jnp.dot(t, wd_ref[cs, :], preferred_element_type=F32)
        acc = d if acc is None else acc + d
    h2 = x + acc
    e = _rms(h2, gp_ref[...])
    gate = jax.nn.sigmoid(jnp.dot(e, wpg_ref[...], preferred_element_type=F32))
    proj = jnp.dot(p_ref[...].astype(BF16), wpp_ref[...], preferred_element_type=F32)
    o_ref[...] = h2 + gate * proj


def _ffn_ple(h, p, gf, wg, wu, wd, gp, wpg, wpp):
    m, d = h.shape
    tm, tf = FFN_TM, FFN_TF
    const = lambda a: pl.BlockSpec(a.shape, lambda i: (0, 0), pipeline_mode=pl.Buffered(1))
    return pl.pallas_call(
        functools.partial(_ffn_ple_kernel, tf=tf),
        grid=(m // tm,),
        in_specs=[pl.BlockSpec((tm, d), lambda i: (i, 0)), pl.BlockSpec((tm, PLE_DIM), lambda i: (i, 0)),
                  const(gf), const(wg), const(wu), const(wd), const(gp), const(wpg), const(wpp)],
        out_specs=pl.BlockSpec((tm, d), lambda i: (i, 0)),
        out_shape=jax.ShapeDtypeStruct((m, d), F32),
        compiler_params=_cparams(("parallel",)),
        name="ffn_ple",
    )(h, p, gf, wg, wu, wd, gp, wpg, wpp)


def _rel_bucket(dist):
    max_exact = REL_BUCKETS // 2
    d = jnp.maximum(dist, 0)
    df = jnp.maximum(d, 1).astype(F32)
    large = max_exact + (jnp.log(df / max_exact) / math.log(REL_MAX_DIST / max_exact)
                         * (REL_BUCKETS - max_exact)).astype(I32)
    large = jnp.minimum(large, REL_BUCKETS - 1)
    return jnp.where(d < max_exact, d, large)


def _toeplitz(rev, n_rows, n_cols):
    nh = rev.shape[0]
    lw = rev.shape[1] + 1
    w = jnp.pad(rev, ((0, 0), (0, 1)))
    s = jnp.broadcast_to(w[:, None, :], (nh, n_rows, lw)).reshape(nh, n_rows * lw)
    s = s[:, :n_rows * (lw - 1)].reshape(nh, n_rows, lw - 1)
    return s[:, :, n_rows - 1:n_rows - 1 + n_cols]


def _bias_tables(rel_bias):
    blk = DSWA_BLOCK
    dswa = []
    for g, (_, r) in enumerate(DSWA_PATTERNS):
        delta = np.arange(3 * blk - 1)[::-1] - (blk - 1)
        rev = rel_bias[_rel_bucket(jnp.asarray(delta * r, I32))][:, g * DSWA_HPG:(g + 1) * DSWA_HPG]
        dswa.append(_toeplitz(rev.T, blk, 2 * blk))
    tq = DSA_TQ
    dist = np.arange(DSA_Z + tq - 1)[::-1] - (DSA_Z - 1) + DSA_Z0
    rev = rel_bias[_rel_bucket(jnp.asarray(dist, I32))][:, DSWA_HEADS:] * math.log2(math.e)
    bias_t = jnp.transpose(_toeplitz(rev.T, tq, DSA_Z), (2, 0, 1)).reshape(DSA_Z, DSA_HEADS * tq)
    return dswa, bias_t


def _pad_cols(w, width):
    return jnp.pad(w, ((0, 0), (0, width - w.shape[1])))


def _layer_params(w_in, qn_a, kn_a, qn_c, kn_c, w_alpha2, b_alpha):
    offs = np.cumsum((0,) + IN_WIDTHS)
    parts = [w_in[:, offs[i]:offs[i + 1]] for i in range(len(IN_WIDTHS))]
    a_q, a_k, a_v, b_q, b_k, b_v, b_r, b_al, c_q, c_k, c_v, i_q, i_k, i_w = parts
    w1 = _pad_cols(jnp.concatenate([a_q, a_k, c_q, c_k], axis=1), W1).astype(BF16)
    w2 = _pad_cols(jnp.concatenate(
        [b_v, b_r, b_q, b_k, i_q, a_v, _pad_cols(b_al, LANES), _pad_cols(c_v, LANES),
         _pad_cols(i_k, LANES), _pad_cols(i_w, LANES)], axis=1), W2).astype(BF16)
    scale = HEAD_DIM ** -0.5
    gain1 = jnp.concatenate([jnp.tile(qn_a, DSWA_HEADS) * scale, jnp.tile(kn_a, DSWA_HEADS),
                             jnp.tile(qn_c, DSA_HEADS) * (scale * math.log2(math.e)), kn_c])
    gain1 = jnp.pad(gain1, (0, W1 - gain1.shape[0])).reshape(1, W1)
    wa = jnp.pad(w_alpha2, ((0, LANES - GLA_RANK), (0, 0))).astype(BF16)
    return w1, w2, gain1, wa, b_alpha.reshape(1, -1)


def kernel(x, p, rel_bias, norm_mix, w_in, qn_a, kn_a, qn_c, kn_c, w_alpha2, b_alpha, gla_norm, w_out_a, w_out_b, w_out_c, w_gate, b_gate, w_o, norm_ffn, w_ffn_gate, w_ffn_up, w_ffn_down, norm_ple, w_ple_gate, w_ple_proj):
    bsz, seq, d = x.shape
    depth = p.shape[0]
    m = bsz * seq
    dswa_bias, bias_t = _bias_tables(rel_bias)
    h = x.reshape(m, d)
    zeros_w2 = jnp.zeros((1, W2), F32)
    for i in range(depth):
        w1, w2, gain1, wa, ba = _layer_params(w_in[i], qn_a[i], kn_a[i], qn_c[i], kn_c[i], w_alpha2[i], b_alpha[i])
        gmix = norm_mix[i].reshape(1, d)
        gw = DSWA_HPG * HEAD_DIM
        dil = [(g, r) for g, (_, r) in enumerate(DSWA_PATTERNS) if r > 1]
        p1, *qk_rm = _proj(h, gmix, w1, gain1, "qk",
                           regroup=[(off + g * gw, r) for off in (P1_AQ, P1_AK) for g, r in dil])
        p2, *v_rm = _proj(h, gmix, w2, zeros_w2, "plain", regroup=[(P2_AV + g * gw, r) for g, r in dil])
        gates = _proj(h, gmix, w_gate[i].astype(BF16), b_gate[i].reshape(1, -1), "gate")
        rm = {g: (qk_rm[n], qk_rm[len(dil) + n], v_rm[n]) for n, (g, r) in enumerate(dil)}
        oas, sas = [], []
        for g, (_, r) in enumerate(DSWA_PATTERNS):
            o, st = _dswa(p1, p2, rm.get(g), dswa_bias[g], g, r, bsz, seq)
            oas.append(o)
            sas.append(st)
        ob = _gla(p2, wa, ba, gla_norm[i].reshape(1, -1), bsz, seq)
        cv = p2[:, P2_CV:P2_CV + HEAD_DIM].reshape(bsz, seq // DSA_CH, DSA_CH, HEAD_DIM)
        ones_pad = jnp.zeros((bsz, seq // DSA_CH, DSA_VR - HEAD_DIM, DSA_CH), BF16).at[:, :, 0, :].set(1.0)
        vt = jnp.concatenate([jnp.transpose(cv, (0, 1, 3, 2)), ones_pad], axis=2)
        yc = _dsa(p1, p2, vt, bias_t, w_out_c[i].T.astype(BF16), bsz, seq)
        h = _merge(h, oas, sas, ob, yc, gates, w_out_a[i].astype(BF16), w_out_b[i].astype(BF16),
                   w_o[i].astype(BF16))
        h = _ffn_ple(h, p[i].reshape(m, PLE_DIM), norm_ffn[i].reshape(1, d), w_ffn_gate[i].astype(BF16),
                     w_ffn_up[i].astype(BF16), w_ffn_down[i].astype(BF16), norm_ple[i].reshape(1, d),
                     w_ple_gate[i].astype(BF16), w_ple_proj[i].astype(BF16))
    return h.reshape(bsz, seq, d)
```

```python
import functools
import math

import numpy as np
import jax
import jax.numpy as jnp
from jax import lax
from jax.experimental import pallas as pl
from jax.experimental.pallas import tpu as pltpu

F32 = jnp.float32
BF16 = jnp.bfloat16
I32 = jnp.int32
I16 = jnp.int16

D_MODEL = 1024
HEAD_DIM = 64
RMS_EPS = 1e-6
DSWA_PATTERNS = ((128, 1), (512, 4), (2048, 16))
DSWA_HPG = 4
DSWA_HEADS = 12
DSWA_BLOCK = 128
GLA_HEADS = 4
GLA_DK = 128
GLA_DV = 256
GLA_RANK = 16
GLA_TAU = 16.0
GLA_CHUNK = 64
DSA_HEADS = 12
IDX_HEADS = 8
IDX_DIM = 64
IDX_TOPK = 256
REL_BUCKETS = 32
REL_MAX_DIST = 2048
D_FF = 2816
PLE_DIM = 256
IN_WIDTHS = (768, 768, 768, 512, 512, 1024, 1024, 16, 768, 64, 64, 512, 64, 8)

LANES = 128
MXU_N = 256
VMEM_LIMIT = 56 * 1024 * 1024

W1 = 2560
P1_AQ, P1_AK, P1_CQ, P1_CK = 0, 768, 1536, 2304
W2 = 5120
P2_BV, P2_BR, P2_BQ, P2_BK, P2_IQ, P2_AV = 0, 1024, 2048, 2560, 3072, 3584
P2_BAL, P2_CV, P2_IK, P2_IW = 4352, 4480, 4608, 4736

PROJ_TM = 512
PROJ_TN = 512

DSWA_QB = 4
DSA_TQ = 128
DSA_CH = 256
DSA_AB = 256
DSA_VR = 80
DSA_BIAS_CONST_FROM = 1512
DSA_Z0 = 1792
DSA_Z = DSA_Z0 + DSA_CH
INT_MIN = -(2 ** 31)
NEG_BIG = -1e30


def _cparams(sem, flags=None):
    return pltpu.CompilerParams(dimension_semantics=sem, vmem_limit_bytes=VMEM_LIMIT, flags=flags)


def _proj_kernel(h_ref, g_ref, w_ref, e_ref, *rest, mode, tn, regroup):
    if mode == "qk":
        bd_ref, rest = rest[0], rest[1:]
    o_ref, rg_refs = rest[0], rest[1:1 + len(regroup)]
    scr_ref = rest[-1] if regroup else None
    tm = h_ref.shape[0]
    gw = DSWA_HPG * HEAD_DIM
    x = h_ref[...]
    ms = jnp.mean(x * x, axis=-1, keepdims=True)
    u = (x * lax.rsqrt(ms + RMS_EPS) * g_ref[...]).astype(BF16)
    for j in range(w_ref.shape[1] // tn):
        cs = slice(j * tn, (j + 1) * tn)
        acc = jnp.dot(u, w_ref[:, cs], preferred_element_type=F32)
        if mode == "plain":
            out = acc
        elif mode == "gate":
            out = jax.nn.sigmoid(acc + e_ref[:, cs])
        else:
            sq = acc * acc
            hi = sq.astype(BF16)
            lo = (sq - hi.astype(F32)).astype(BF16)
            bw = bd_ref.shape[0]
            ss = jnp.concatenate(
                [jnp.dot(hi[:, k:k + bw], bd_ref[...], preferred_element_type=F32)
                 + jnp.dot(lo[:, k:k + bw], bd_ref[...], preferred_element_type=F32) for k in range(0, tn, bw)],
                axis=1)
            out = acc * lax.rsqrt(ss * (1.0 / HEAD_DIM) + RMS_EPS) * e_ref[:, cs]
        o_ref[:, cs] = out.astype(o_ref.dtype)
        for (col, r), rg_ref in zip(regroup, rg_refs):
            if col // tn == j:
                for k in range(gw // LANES):
                    scr_ref[k] = out[:, col % tn + k * LANES:col % tn + (k + 1) * LANES]
                for c in range(r):
                    for k in range(gw // LANES):
                        rg_ref[:, c * gw + k * LANES:c * gw + (k + 1) * LANES] = (
                            scr_ref[k, pl.ds(c, tm // r, stride=r), :].astype(rg_ref.dtype))


def _proj(h, g, w, e, mode, regroup=()):
    m, d = h.shape
    n = w.shape[1]
    tm, tn = PROJ_TM, PROJ_TN
    gw = DSWA_HPG * HEAD_DIM
    const = lambda shape: pl.BlockSpec(shape, lambda i: (0, 0), pipeline_mode=pl.Buffered(1))
    in_specs = [pl.BlockSpec((tm, d), lambda i: (i, 0)), const((1, d)), const((d, n)), const((1, n))]
    args = [h, g, w, e]
    if mode == "qk":
        r = np.arange(MXU_N) // HEAD_DIM
        bd = jnp.asarray((r[:, None] == r[None, :]).astype(np.float32), dtype=BF16)
        in_specs.append(const((MXU_N, MXU_N)))
        args.append(bd)
    out_specs = [pl.BlockSpec((tm, n), lambda i: (i, 0))]
    out_shape = [jax.ShapeDtypeStruct((m, n), BF16)]
    for _, r in regroup:
        out_specs.append(pl.BlockSpec((tm // r, r * gw), lambda i: (i, 0)))
        out_shape.append(jax.ShapeDtypeStruct((m // r, r * gw), BF16))
    outs = pl.pallas_call(
        functools.partial(_proj_kernel, mode=mode, tn=tn, regroup=tuple(regroup)),
        grid=(m // tm,),
        in_specs=in_specs,
        out_specs=out_specs,
        out_shape=out_shape,
        scratch_shapes=[pltpu.VMEM((gw // LANES, tm, LANES), F32)] if regroup else [],
        compiler_params=_cparams(("parallel",)),
        name="proj_" + mode,
    )(*args)
    return outs if regroup else outs[0]


def _dswa_kernel(q_ref, kp_ref, kc_ref, vp_ref, vc_ref, bias_ref, o_ref, st_ref, *, qb):
    i = pl.program_id(2)
    blk = DSWA_BLOCK
    row = lax.broadcasted_iota(I32, (blk, 2 * blk), 0)
    col = lax.broadcasted_iota(I32, (blk, 2 * blk), 1)
    cur_ok = (col >= blk) & ((col - blk) <= row)
    valid_inner = ((col < blk) & (col >= row)) | cur_ok
    prev_off = jnp.where(i > 0, 0, 4 * blk)
    valid_first = ((col < blk) & (col >= row + prev_off)) | cur_ok
    lane = lax.broadcasted_iota(I32, (blk, LANES), 1)
    for j in range(qb):
        rs = slice(j * blk, (j + 1) * blk)
        ps = slice((j - 1) * blk, j * blk)
        valid = valid_first if j == 0 else valid_inner
        q = q_ref[rs, :]
        k = jnp.concatenate([kp_ref[...] if j == 0 else kc_ref[ps, :], kc_ref[rs, :]], axis=0)
        v = jnp.concatenate([vp_ref[...] if j == 0 else vc_ref[ps, :], vc_ref[rs, :]], axis=0)
        stats = jnp.zeros((blk, LANES), F32)
        for h in range(DSWA_HPG):
            sl = slice(h * HEAD_DIM, (h + 1) * HEAD_DIM)
            s = lax.dot_general(q[:, sl], k[:, sl], (((1,), (1,)), ((), ())), preferred_element_type=F32)
            s = jnp.where(valid, s + bias_ref[h], -jnp.inf)
            m = jnp.max(s, axis=-1, keepdims=True)
            p = jnp.exp(s - m)
            l = jnp.sum(p, axis=-1, keepdims=True)
            o = jnp.dot(p.astype(BF16), v[:, sl], preferred_element_type=F32) / l
            o_ref[rs, sl] = o
            stats = jnp.where(lane == h, m, stats)
            stats = jnp.where(lane == DSWA_HPG + h, l, stats)
        st_ref[rs, :] = stats


def _dswa(p1, p2, rm, bias, g, r, bsz, seq):
    blk = DSWA_BLOCK
    L = seq // r
    nblk = L // blk
    gw = DSWA_HPG * HEAD_DIM
    if r == 1:
        p1v = p1.reshape(bsz, L, W1)
        p2v = p2.reshape(bsz, L, W2)
        qv, kv, vv = p1v, p1v, p2v
        q_col = lambda c: (P1_AQ // gw) + g
        k_col = lambda c: (P1_AK // gw) + g
        v_col = lambda c: (P2_AV // gw) + g
    else:
        qv, kv, vv = (a.reshape(bsz, L, r * gw) for a in rm)
        q_col = k_col = v_col = lambda c: c
    qb = min(DSWA_QB, nblk)
    prev = lambda i: jnp.maximum(i * qb - 1, 0)
    in_specs = [
        pl.BlockSpec((None, qb * blk, gw), lambda b, c, i: (b, i, q_col(c))),
        pl.BlockSpec((None, blk, gw), lambda b, c, i: (b, prev(i), k_col(c))),
        pl.BlockSpec((None, qb * blk, gw), lambda b, c, i: (b, i, k_col(c))),
        pl.BlockSpec((None, blk, gw), lambda b, c, i: (b, prev(i), v_col(c))),
        pl.BlockSpec((None, qb * blk, gw), lambda b, c, i: (b, i, v_col(c))),
        pl.BlockSpec((DSWA_HPG, blk, 2 * blk), lambda b, c, i: (0, 0, 0)),
    ]
    out_specs = [
        pl.BlockSpec((None, qb * blk, gw), lambda b, c, i: (b, i, c)),
        pl.BlockSpec((None, qb * blk, LANES), lambda b, c, i: (b, i, c)),
    ]
    o, st = pl.pallas_call(
        functools.partial(_dswa_kernel, qb=qb),
        grid=(bsz, r, nblk // qb),
        in_specs=in_specs,
        out_specs=out_specs,
        out_shape=[jax.ShapeDtypeStruct((bsz, L, r * gw), F32),
                   jax.ShapeDtypeStruct((bsz, L, r * LANES), F32)],
        compiler_params=_cparams(("parallel", "parallel", "arbitrary")),
        name="dswa_g%d" % g,
    )(qv, kv, kv, vv, vv, bias)
    return o.reshape(bsz * L, r * gw), st.reshape(bsz * L, r * LANES)


GLA_TC = 256
GLA_GB = 2


def _split3(x):
    a1 = x.astype(BF16)
    r1 = x - a1.astype(F32)
    a2 = r1.astype(BF16)
    a3 = (r1 - a2.astype(F32)).astype(BF16)
    return a1, a2, a3


def _gla_kernel(v_ref, r_ref, q_ref, k_ref, al_ref, wa_ref, ba_ref, gn_ref, o_ref, st_ref, *, gb):
    C = GLA_CHUNK

    @pl.when(pl.program_id(1) == 0)
    def _():
        st_ref[...] = jnp.zeros_like(st_ref)

    TC = GLA_TC
    NC = TC // C
    row = lax.broadcasted_iota(I32, (TC, TC), 0)
    col = lax.broadcasted_iota(I32, (TC, TC), 1)
    tri = ((row // C) == (col // C)) & (row >= col)
    tri_bf = jnp.where(tri, 1.0, 0.0).astype(BF16)
    nt = (((1,), (1,)), ((), ()))
    tn = (((0,), (0,)), ((), ()))
    for bb in range(gb):
        z = jnp.dot(al_ref[bb], wa_ref[...], preferred_element_type=F32) + ba_ref[...]
        la = (jnp.minimum(z, 0.0) - jnp.log(1.0 + jnp.exp(-jnp.abs(z)))) * (1.0 / GLA_TAU)
        a1, a2, a3 = _split3(la)
        bcum = (jnp.dot(tri_bf, a1, preferred_element_type=F32)
                + jnp.dot(tri_bf, a2, preferred_element_type=F32)
                + jnp.dot(tri_bf, a3, preferred_element_type=F32))
        blast = [bcum[(c + 1) * C - 1:(c + 1) * C, :] for c in range(NC)]
        blast_rows = jnp.concatenate([jnp.broadcast_to(b, (C, b.shape[1])) for b in blast], axis=0)
        qf = q_ref[bb].astype(F32) * (GLA_DK ** -0.5)
        kf = k_ref[bb].astype(F32)
        q_in = (qf * jnp.exp(bcum)).astype(BF16)
        k_in = (kf * jnp.exp(-bcum)).astype(BF16)
        k_end = (kf * jnp.exp(blast_rows - bcum)).astype(BF16)
        dec = [jnp.exp(b) for b in blast]
        for h in range(GLA_HEADS):
            ks = slice(h * GLA_DK, (h + 1) * GLA_DK)
            vs = slice(h * GLA_DV, (h + 1) * GLA_DV)
            vh = v_ref[bb, :, vs]
            att = lax.dot_general(q_in[:, ks], k_in[:, ks], nt, preferred_element_type=F32)
            att = jnp.where(tri, att, 0.0)
            o = jnp.dot(att.astype(BF16), vh, preferred_element_type=F32)
            st = st_ref[bb, h]
            inter = []
            for c in range(NC):
                rs = slice(c * C, (c + 1) * C)
                inter.append(lax.dot_general(q_in[rs, ks], st.astype(BF16), nt, preferred_element_type=F32))
                st = st * dec[c][:, ks] + lax.dot_general(vh[rs, :], k_end[rs, ks], tn, preferred_element_type=F32)
            st_ref[bb, h] = st
            o = o + jnp.concatenate(inter, axis=0)
            ms = jnp.mean(o * o, axis=-1, keepdims=True)
            y = o * lax.rsqrt(ms + RMS_EPS) * gn_ref[...]
            rg = r_ref[bb, :, vs].astype(F32)
            y = y * (rg * jax.nn.sigmoid(rg))
            o_ref[bb, :, vs] = y.astype(o_ref.dtype)


def _gla(p2, wa, ba, gn, bsz, seq):
    tc = GLA_TC
    gb = GLA_GB if bsz % GLA_GB == 0 else 1
    p2v = p2.reshape(bsz, seq, W2)
    in_specs = [
        pl.BlockSpec((gb, tc, 1024), lambda b, t: (b, t, P2_BV // 1024)),
        pl.BlockSpec((gb, tc, 1024), lambda b, t: (b, t, P2_BR // 1024)),
        pl.BlockSpec((gb, tc, 512), lambda b, t: (b, t, P2_BQ // 512)),
        pl.BlockSpec((gb, tc, 512), lambda b, t: (b, t, P2_BK // 512)),
        pl.BlockSpec((gb, tc, LANES), lambda b, t: (b, t, P2_BAL // LANES)),
        pl.BlockSpec((LANES, 512), lambda b, t: (0, 0)),
        pl.BlockSpec((1, 512), lambda b, t: (0, 0)),
        pl.BlockSpec((1, GLA_DV), lambda b, t: (0, 0)),
    ]
    out = pl.pallas_call(
        functools.partial(_gla_kernel, gb=gb),
        grid=(bsz // gb, seq // tc),
        in_specs=in_specs,
        out_specs=pl.BlockSpec((gb, tc, 1024), lambda b, t: (b, t, 0)),
        out_shape=jax.ShapeDtypeStruct((bsz, seq, 1024), BF16),
        scratch_shapes=[pltpu.VMEM((gb, GLA_HEADS, GLA_DV, GLA_DK), F32)],
        compiler_params=_cparams(("parallel", "arbitrary")),
        name="gla",
    )(p2v, p2v, p2v, p2v, p2v, wa, ba, gn)
    return out.reshape(bsz * seq, 1024)


def _tree_sum(xs):
    xs = list(xs)
    while len(xs) > 1:
        xs = [xs[i] + xs[i + 1] for i in range(0, len(xs) - 1, 2)] + ([xs[-1]] if len(xs) % 2 else [])
    return xs[0]


def _dsa_kernel(q_ref, iq_ref, iw_ref, k_ref, ik_ref, vt_ref, bias_ref, wot_ref, o_ref,
                keys_ref, planes_ref, qaug_ref, iqall_ref, acc_ref, st0_ref, st1_ref, m_ref, cm0_ref, cm1_ref,
                *, topk):
    TQ, CH = DSA_TQ, DSA_CH
    CC = 2 * CH
    qi = pl.program_id(1)
    t0 = qi * TQ
    nch = qi // (CH // TQ) + 1
    npair = (nch + 1) // 2
    nt = (((1,), (1,)), ((), ()))

    eye = (lax.broadcasted_iota(I32, (TQ, TQ), 0) == lax.broadcasted_iota(I32, (TQ, TQ), 1))
    eye = jnp.where(eye, 1.0, 0.0).astype(BF16)
    q_t = q_ref[...].astype(F32).T.astype(BF16)
    iq_t = iq_ref[...].astype(F32).T.astype(BF16)
    for h in range(DSA_HEADS):
        cs = slice(h * TQ, (h + 1) * TQ)
        qaug_ref[0:TQ, cs] = eye
        qaug_ref[TQ:TQ + HEAD_DIM, cs] = q_t[h * HEAD_DIM:(h + 1) * HEAD_DIM, :]
        qaug_ref[TQ + HEAD_DIM:, cs] = jnp.zeros((TQ - HEAD_DIM, TQ), BF16)
    for h in range(IDX_HEADS):
        cs = slice(h * TQ, (h + 1) * TQ)
        iqall_ref[0:IDX_DIM, cs] = iq_t[h * IDX_DIM:(h + 1) * IDX_DIM, :]
        iqall_ref[IDX_DIM:, cs] = jnp.zeros((LANES - IDX_DIM, TQ), BF16)
    idx_scale = (IDX_HEADS ** -0.5) * (IDX_DIM ** -0.5)
    wt = (iw_ref[...].astype(F32) * idx_scale).T

    qpos = t0 + lax.broadcasted_iota(I32, (1, TQ), 1)
    RB = 128
    krow1 = lax.broadcasted_iota(I32, (RB, TQ), 0)
    krow = lax.broadcasted_iota(I32, (CH, TQ), 0)
    krow2 = lax.broadcasted_iota(I32, (CC, TQ), 0)

    def score_body(cp, carry):
        for sub in range(CC // RB):
            s0 = pl.multiple_of(cp * CC + sub * RB, RB)
            ikc = ik_ref[pl.ds(s0, RB), :]
            s = None
            for hp in range(IDX_HEADS // 2):
                x = jnp.dot(ikc, iqall_ref[:, hp * 2 * TQ:(hp + 1) * 2 * TQ],
                            preferred_element_type=F32)
                t = (jnp.maximum(x[:, :TQ], 0.0) * wt[2 * hp:2 * hp + 1, :]
                     + jnp.maximum(x[:, TQ:], 0.0) * wt[2 * hp + 1:2 * hp + 2, :])
                s = t if s is None else s + t
            s = jnp.where(s == 0.0, 0.0, s)
            bits = pltpu.bitcast(s, I32)
            key = bits ^ ((bits >> 31) & 0x7FFFFFFF)
            key = jnp.where(s0 + krow1 <= qpos, key, INT_MIN)
            keys_ref[pl.ds(s0, RB), :] = key
        return carry

    lax.fori_loop(0, npair, score_body, 0)

    kvec = jnp.minimum(topk, qpos + 1)
    NPL = 32

    def planes_body(c, carry):
        s0 = pl.multiple_of(c * CH, CH)
        a = [keys_ref[pl.ds(s0 + 8 * j, 8), :] ^ INT_MIN for j in range(NPL)]
        j, m = 16, 0x0000FFFF
        while j:
            sh = jnp.full((8, TQ), j, I32)
            k = 0
            while k < NPL:
                t = (a[k] ^ lax.shift_right_logical(a[k + j], sh)) & m
                a[k] = a[k] ^ t
                a[k + j] = a[k + j] ^ lax.shift_left(t, sh)
                k = (k + j + 1) & ~j
            j >>= 1
            m = (m ^ (m << j)) & 0xFFFFFFFF
        for p in range(NPL):
            planes_ref[c, p] = a[p]
        return carry

    def empty_body(c, carry):
        for p in range(NPL):
            planes_ref[c, p] = jnp.zeros((8, TQ), I32)
        return carry

    nck = 2 * npair
    NCK = keys_ref.shape[0] // CH
    lax.fori_loop(0, nck, planes_body, 0)
    lax.fori_loop(nck, NCK, empty_body, 0)

    def plane_body(p, carry):
        thr_u, n_gt, alive = carry
        hit = [alive[c] & planes_ref[c, p] for c in range(NCK)]
        ones = jnp.sum(_tree_sum([lax.population_count(h) for h in hit]), axis=0, keepdims=True)
        take = (n_gt + ones) >= kvec
        thr_u = jnp.where(take, thr_u | (jnp.int32(1) << (31 - p)), thr_u)
        n_gt = jnp.where(take, n_gt, n_gt + ones)
        alive = tuple(jnp.where(take, hit[c], alive[c] ^ hit[c]) for c in range(NCK))
        return thr_u, n_gt, alive

    zero = jnp.zeros((1, TQ), I32)
    alive0 = tuple(jnp.full((8, TQ), -1, I32) for _ in range(NCK))
    thr_u, n_gt, alive = lax.fori_loop(0, NPL, plane_body, (zero, zero, alive0))
    n_eq = jnp.sum(_tree_sum([lax.population_count(a) for a in alive]), axis=0, keepdims=True)
    thr = thr_u ^ INT_MIN

    def count(pred_fn):
        def body(c, acc):
            s0 = pl.multiple_of(c * CC, CC)
            kk = keys_ref[pl.ds(s0, CC), :]
            hit = jnp.where(pred_fn(kk, s0), 1, 0).astype(I32)
            return acc + jnp.sum(hit.reshape(CC // 8, 8, TQ), axis=0)
        acc = lax.fori_loop(0, npair, body, jnp.zeros((8, TQ), I32))
        return jnp.sum(acc, axis=0, keepdims=True)

    excess = n_gt + n_eq - kvec
    has_excess = jnp.max(excess) > 0

    @pl.when(has_excess)
    def _():
        need = kvec - n_gt

        def tie_lt(cut):
            return count(lambda kk, s0: (kk == thr) & (s0 + krow2 < cut))

        def cut_body(b, cut):
            cand = cut | (jnp.int32(1) << (12 - b))
            return jnp.where(tie_lt(cand) <= need, cand, cut)

        cut = lax.fori_loop(0, 13, cut_body, jnp.zeros((1, TQ), I32))

        def drop_body(c, carry):
            s0 = pl.multiple_of(c * CH, CH)
            kk = keys_ref[pl.ds(s0, CH), :]
            keys_ref[pl.ds(s0, CH), :] = jnp.where((kk == thr) & (s0 + krow >= cut), INT_MIN, kk)
            return carry

        lax.fori_loop(0, nch, drop_body, 0)

    acc_ref[...] = jnp.zeros_like(acc_ref)

    NP = DSA_HEADS // 2
    PW = 2 * TQ

    AB = DSA_AB
    NB = CH // AB
    last_chunk = keys_ref.shape[0] // CH - 1

    def logits(c, st_ref, cm_ref):
        s0 = pl.multiple_of(c * CH, CH)
        zoff = pl.multiple_of(jnp.clip(s0 - t0 + DSA_Z0, 0, DSA_Z0), 8)
        kaug = []
        for rb in range(NB):
            rs = pl.ds(s0 + rb * AB, AB)
            pen = jnp.where(keys_ref[rs, :] >= thr, 0.0, NEG_BIG).astype(BF16)
            kaug.append(jnp.concatenate([pen, k_ref[rs, :]], axis=1))
        for hp in range(NP):
            ps = slice(hp * PW, (hp + 1) * PW)
            cm = None
            for rb in range(NB):
                st = jnp.dot(kaug[rb], qaug_ref[:, ps], preferred_element_type=F32)
                st = st + bias_ref[pl.ds(zoff + rb * AB, AB), ps]
                st_ref[rb * AB:(rb + 1) * AB, ps] = st
                tm = jnp.max(st.reshape(AB // 8, 8, PW), axis=0)
                cm = tm if cm is None else jnp.maximum(cm, tm)
            cm_ref[0:1, ps] = jnp.max(cm, axis=0, keepdims=True)

    def accumulate(c, st_ref, cm_ref):
        vtc = vt_ref[c]
        for hp in range(NP):
            ps = slice(hp * PW, (hp + 1) * PW)
            m_old = m_ref[0:1, ps]
            m_new = jnp.maximum(m_old, cm_ref[0:1, ps])
            m_ref[0:1, ps] = m_new
            alpha = jnp.exp2(m_old - m_new)
            p = jnp.concatenate(
                [jnp.exp2(st_ref[rb * AB:(rb + 1) * AB, ps] - m_new).astype(BF16) for rb in range(NB)], axis=0)
            acc_ref[:, ps] = acc_ref[:, ps] * alpha + jnp.dot(vtc, p, preferred_element_type=F32)

    def att_body(cp, carry):
        c0 = 2 * cp
        logits(c0 + 1, st1_ref, cm1_ref)
        accumulate(c0, st0_ref, cm0_ref)
        logits(c0 + 2, st0_ref, cm0_ref)
        accumulate(c0 + 1, st1_ref, cm1_ref)
        return carry

    m_ref[...] = jnp.full(m_ref.shape, NEG_BIG, F32)
    logits(0, st0_ref, cm0_ref)
    nfull = (nch - 1) // 2
    lax.fori_loop(0, nfull, att_body, 0)
    c_last = 2 * nfull

    @pl.when(nch % 2 == 1)
    def _():
        accumulate(c_last, st0_ref, cm0_ref)

    @pl.when(nch % 2 == 0)
    def _():
        logits(c_last + 1, st1_ref, cm1_ref)
        accumulate(c_last, st0_ref, cm0_ref)
        accumulate(c_last + 1, st1_ref, cm1_ref)

    ot = jnp.concatenate(
        [acc_ref[0:HEAD_DIM, h * TQ:(h + 1) * TQ] / acc_ref[HEAD_DIM:HEAD_DIM + 1, h * TQ:(h + 1) * TQ]
         for h in range(DSA_HEADS)], axis=0)
    o_ref[...] = jnp.dot(ot.T.astype(BF16), wot_ref[...], preferred_element_type=F32)


def _dsa(p1, p2, vt, bias_t, wot, bsz, seq):
    tq = DSA_TQ
    p1v = p1.reshape(bsz, seq, W1)
    p2v = p2.reshape(bsz, seq, W2)
    nq = DSA_HEADS * tq
    in_specs = [
        pl.BlockSpec((None, tq, 768), lambda b, i: (b, i, P1_CQ // 768)),
        pl.BlockSpec((None, tq, 512), lambda b, i: (b, i, P2_IQ // 512)),
        pl.BlockSpec((None, tq, LANES), lambda b, i: (b, i, P2_IW // LANES)),
        pl.BlockSpec((None, seq, LANES), lambda b, i: (b, 0, P1_CK // LANES)),
        pl.BlockSpec((None, seq, LANES), lambda b, i: (b, 0, P2_IK // LANES)),
        pl.BlockSpec((None, seq // DSA_CH, DSA_VR, DSA_CH), lambda b, i: (b, 0, 0, 0)),
        pl.BlockSpec((DSA_Z, nq), lambda b, i: (0, 0)),
        pl.BlockSpec((DSA_HEADS * HEAD_DIM, D_MODEL), lambda b, i: (0, 0)),
    ]
    out = pl.pallas_call(
        functools.partial(_dsa_kernel, topk=min(IDX_TOPK, seq // 4)),
        grid=(bsz, seq // tq),
        in_specs=in_specs,
        out_specs=pl.BlockSpec((None, tq, D_MODEL), lambda b, i: (b, i, 0)),
        out_shape=jax.ShapeDtypeStruct((bsz, seq, D_MODEL), F32),
        scratch_shapes=[
            pltpu.VMEM((seq, tq), I32),
            pltpu.VMEM((seq // DSA_CH, 32, 8, tq), I32),
            pltpu.VMEM((2 * tq, nq), BF16),
            pltpu.VMEM((LANES, IDX_HEADS * tq), BF16),
            pltpu.VMEM((DSA_VR, nq), F32),
            pltpu.VMEM((DSA_CH, nq), F32),
            pltpu.VMEM((DSA_CH, nq), F32),
            pltpu.VMEM((8, nq), F32),
            pltpu.VMEM((8, nq), F32),
            pltpu.VMEM((8, nq), F32),
        ],
        compiler_params=_cparams(("parallel", "arbitrary")),
        name="dsa",
    )(p1v, p2v, p2v, p1v, p2v, vt, bias_t, wot)
    return out.reshape(bsz * seq, D_MODEL)


MERGE_TM = 512


def _merge_kernel(h_ref, oa0_ref, oa1_ref, oa2_ref, sa0_ref, sa1_ref, sa2_ref, ob_ref, yc_ref, g_ref,
                  woa_ref, wob_ref, wo_ref, o_ref, on1_ref, on2_ref, sn1_ref, sn2_ref):
    H = DSWA_HPG
    tm = h_ref.shape[0]
    gw = H * HEAD_DIM
    for (_, r), src_o, src_s, dst_o, dst_s in zip(DSWA_PATTERNS[1:], (oa1_ref, oa2_ref), (sa1_ref, sa2_ref),
                                                  (on1_ref, on2_ref), (sn1_ref, sn2_ref)):
        for c in range(r):
            for k in range(gw // LANES):
                dst_o[k, pl.ds(c, tm // r, stride=r), :] = src_o[:, c * gw + k * LANES:c * gw + (k + 1) * LANES]
            dst_s[pl.ds(c, tm // r, stride=r), :] = src_s[:, c * LANES:(c + 1) * LANES]
    oa_vals = (oa0_ref[...],) + tuple(
        jnp.concatenate([ref[k] for k in range(gw // LANES)], axis=1) for ref in (on1_ref, on2_ref))
    sts = [sa0_ref[...], sn1_ref[...], sn2_ref[...]]
    lane = lax.broadcasted_iota(I32, sts[0].shape, 1)
    mmax = jnp.maximum(jnp.maximum(sts[0], sts[1]), sts[2])
    wts = [pltpu.roll(s, LANES - H, 1) * jnp.exp(s - mmax) for s in sts]
    tot = wts[0] + wts[1] + wts[2]
    hrow = lax.broadcasted_iota(I32, (LANES, H * HEAD_DIM), 0)
    hcol = lax.broadcasted_iota(I32, (LANES, H * HEAD_DIM), 1) // HEAD_DIM
    expand = jnp.where(hrow == hcol, 1.0, 0.0).astype(BF16)
    oa = None
    for g in range(3):
        w = jnp.where(lane < H, wts[g] / tot, 0.0)
        hi = w.astype(BF16)
        lo = (w - hi.astype(F32)).astype(BF16)
        wfull = (jnp.dot(hi, expand, preferred_element_type=F32)
                 + jnp.dot(lo, expand, preferred_element_type=F32))
        term = wfull * oa_vals[g]
        oa = term if oa is None else oa + term
    oa = oa.astype(BF16)
    y_a = jnp.dot(oa, woa_ref[...], preferred_element_type=F32)
    y_b = jnp.dot(ob_ref[...], wob_ref[...], preferred_element_type=F32)
    y_c = yc_ref[...]
    D = D_MODEL
    mix = (g_ref[:, 0:D].astype(F32) * y_a + g_ref[:, D:2 * D].astype(F32) * y_b
           + g_ref[:, 2 * D:3 * D].astype(F32) * y_c)
    o_ref[...] = h_ref[...] + jnp.dot(mix.astype(BF16), wo_ref[...], preferred_element_type=F32)


def _merge(h, oas, sas, ob, yc, gates, woa, wob, wo):
    m = h.shape[0]
    tm = MERGE_TM
    gw = DSWA_HPG * HEAD_DIM
    row = lambda w: pl.BlockSpec((tm, w), lambda i: (i, 0))
    rmrow = lambda w, r: pl.BlockSpec((tm // r, r * w), lambda i: (i, 0))
    full = lambda a: pl.BlockSpec(a.shape, lambda i: (0, 0), pipeline_mode=pl.Buffered(1))
    rs = [r for _, r in DSWA_PATTERNS]
    in_specs = ([row(D_MODEL)] + [rmrow(gw, r) for r in rs] + [rmrow(LANES, r) for r in rs]
                + [row(1024), row(D_MODEL), row(3 * D_MODEL)] + [full(woa), full(wob), full(wo)])
    return pl.pallas_call(
        _merge_kernel,
        grid=(m // tm,),
        in_specs=in_specs,
        out_specs=row(D_MODEL),
        out_shape=jax.ShapeDtypeStruct((m, D_MODEL), F32),
        scratch_shapes=[pltpu.VMEM((gw // LANES, tm, LANES), F32), pltpu.VMEM((gw // LANES, tm, LANES), F32),
                        pltpu.VMEM((tm, LANES), F32), pltpu.VMEM((tm, LANES), F32)],
        compiler_params=_cparams(("parallel",)),
        name="merge",
    )(h, *oas, *sas, ob, yc, gates, woa, wob, wo)


FFN_TM = 512
FFN_TF = 256


def _rms(x, g):
    ms = jnp.mean(x * x, axis=-1, keepdims=True)
    return (x * lax.rsqrt(ms + RMS_EPS) * g).astype(BF16)


def _ffn_ple_kernel(h_ref, p_ref, gf_ref, wg_ref, wu_ref, wd_ref, gp_ref, wpg_ref, wpp_ref, o_ref, *, tf):
    x = h_ref[...]
    u = _rms(x, gf_ref[...])
    acc = None
    for j in range(wg_ref.shape[1] // tf):
        cs = slice(j * tf, (j + 1) * tf)
        a = jnp.dot(u, wg_ref[:, cs], preferred_element_type=F32)
        b = jnp.dot(u, wu_ref[:, cs], preferred_element_type=F32)
        t = (a * jax.nn.sigmoid(a) * b).astype(BF16)
        d = jnp.dot(t, wd_ref[cs, :], preferred_element_type=F32)
        acc = d if acc is None else acc + d
    h2 = x + acc
    e = _rms(h2, gp_ref[...])
    gate = jax.nn.sigmoid(jnp.dot(e, wpg_ref[...], preferred_element_type=F32))
    proj = jnp.dot(p_ref[...].astype(BF16), wpp_ref[...], preferred_element_type=F32)
    o_ref[...] = h2 + gate * proj


def _ffn_ple(h, p, gf, wg, wu, wd, gp, wpg, wpp):
    m, d = h.shape
    tm, tf = FFN_TM, FFN_TF
    const = lambda a: pl.BlockSpec(a.shape, lambda i: (0, 0), pipeline_mode=pl.Buffered(1))
    return pl.pallas_call(
        functools.partial(_ffn_ple_kernel, tf=tf),
        grid=(m // tm,),
        in_specs=[pl.BlockSpec((tm, d), lambda i: (i, 0)), pl.BlockSpec((tm, PLE_DIM), lambda i: (i, 0)),
                  const(gf), const(wg), const(wu), const(wd), const(gp), const(wpg), const(wpp)],
        out_specs=pl.BlockSpec((tm, d), lambda i: (i, 0)),
        out_shape=jax.ShapeDtypeStruct((m, d), F32),
        compiler_params=_cparams(("parallel",)),
        name="ffn_ple",
    )(h, p, gf, wg, wu, wd, gp, wpg, wpp)


def _rel_bucket(dist):
    max_exact = REL_BUCKETS // 2
    d = jnp.maximum(dist, 0)
    df = jnp.maximum(d, 1).astype(F32)
    large = max_exact + (jnp.log(df / max_exact) / math.log(REL_MAX_DIST / max_exact)
                         * (REL_BUCKETS - max_exact)).astype(I32)
    large = jnp.minimum(large, REL_BUCKETS - 1)
    return jnp.where(d < max_exact, d, large)


def _toeplitz(rev, n_rows, n_cols):
    nh = rev.shape[0]
    lw = rev.shape[1] + 1
    w = jnp.pad(rev, ((0, 0), (0, 1)))
    s = jnp.broadcast_to(w[:, None, :], (nh, n_rows, lw)).reshape(nh, n_rows * lw)
    s = s[:, :n_rows * (lw - 1)].reshape(nh, n_rows, lw - 1)
    return s[:, :, n_rows - 1:n_rows - 1 + n_cols]


def _bias_tables(rel_bias):
    blk = DSWA_BLOCK
    dswa = []
    for g, (_, r) in enumerate(DSWA_PATTERNS):
        delta = np.arange(3 * blk - 1)[::-1] - (blk - 1)
        rev = rel_bias[_rel_bucket(jnp.asarray(delta * r, I32))][:, g * DSWA_HPG:(g + 1) * DSWA_HPG]
        dswa.append(_toeplitz(rev.T, blk, 2 * blk))
    tq = DSA_TQ
    dist = np.arange(DSA_Z + tq - 1)[::-1] - (DSA_Z - 1) + DSA_Z0
    rev = rel_bias[_rel_bucket(jnp.asarray(dist, I32))][:, DSWA_HEADS:] * math.log2(math.e)
    bias_t = jnp.transpose(_toeplitz(rev.T, tq, DSA_Z), (2, 0, 1)).reshape(DSA_Z, DSA_HEADS * tq)
    return dswa, bias_t


def _pad_cols(w, width):
    return jnp.pad(w, ((0, 0), (0, width - w.shape[1])))


def _layer_params(w_in, qn_a, kn_a, qn_c, kn_c, w_alpha2, b_alpha):
    offs = np.cumsum((0,) + IN_WIDTHS)
    parts = [w_in[:, offs[i]:offs[i + 1]] for i in range(len(IN_WIDTHS))]
    a_q, a_k, a_v, b_q, b_k, b_v, b_r, b_al, c_q, c_k, c_v, i_q, i_k, i_w = parts
    w1 = _pad_cols(jnp.concatenate([a_q, a_k, c_q, c_k], axis=1), W1).astype(BF16)
    w2 = _pad_cols(jnp.concatenate(
        [b_v, b_r, b_q, b_k, i_q, a_v, _pad_cols(b_al, LANES), _pad_cols(c_v, LANES),
         _pad_cols(i_k, LANES), _pad_cols(i_w, LANES)], axis=1), W2).astype(BF16)
    scale = HEAD_DIM ** -0.5
    gain1 = jnp.concatenate([jnp.tile(qn_a, DSWA_HEADS) * scale, jnp.tile(kn_a, DSWA_HEADS),
                             jnp.tile(qn_c, DSA_HEADS) * (scale * math.log2(math.e)), kn_c])
    gain1 = jnp.pad(gain1, (0, W1 - gain1.shape[0])).reshape(1, W1)
    wa = jnp.pad(w_alpha2, ((0, LANES - GLA_RANK), (0, 0))).astype(BF16)
    return w1, w2, gain1, wa, b_alpha.reshape(1, -1)


def kernel(x, p, rel_bias, norm_mix, w_in, qn_a, kn_a, qn_c, kn_c, w_alpha2, b_alpha, gla_norm, w_out_a, w_out_b, w_out_c, w_gate, b_gate, w_o, norm_ffn, w_ffn_gate, w_ffn_up, w_ffn_down, norm_ple, w_ple_gate, w_ple_proj):
    bsz, seq, d = x.shape
    depth = p.shape[0]
    m = bsz * seq
    dswa_bias, bias_t = _bias_tables(rel_bias)
    h = x.reshape(m, d)
    zeros_w2 = jnp.zeros((1, W2), F32)
    for i in range(depth):
        w1, w2, gain1, wa, ba = _layer_params(w_in[i], qn_a[i], kn_a[i], qn_c[i], kn_c[i], w_alpha2[i], b_alpha[i])
        gmix = norm_mix[i].reshape(1, d)
        gw = DSWA_HPG * HEAD_DIM
        dil = [(g, r) for g, (_, r) in enumerate(DSWA_PATTERNS) if r > 1]
        p1, *qk_rm = _proj(h, gmix, w1, gain1, "qk",
                           regroup=[(off + g * gw, r) for off in (P1_AQ, P1_AK) for g, r in dil])
        p2, *v_rm = _proj(h, gmix, w2, zeros_w2, "plain", regroup=[(P2_AV + g * gw, r) for g, r in dil])
        gates = _proj(h, gmix, w_gate[i].astype(BF16), b_gate[i].reshape(1, -1), "gate")
        rm = {g: (qk_rm[n], qk_rm[len(dil) + n], v_rm[n]) for n, (g, r) in enumerate(dil)}
        oas, sas = [], []
        for g, (_, r) in enumerate(DSWA_PATTERNS):
            o, st = _dswa(p1, p2, rm.get(g), dswa_bias[g], g, r, bsz, seq)
            oas.append(o)
            sas.append(st)
        ob = _gla(p2, wa, ba, gla_norm[i].reshape(1, -1), bsz, seq)
        cv = p2[:, P2_CV:P2_CV + HEAD_DIM].reshape(bsz, seq // DSA_CH, DSA_CH, HEAD_DIM)
        ones_pad = jnp.zeros((bsz, seq // DSA_CH, DSA_VR - HEAD_DIM, DSA_CH), BF16).at[:, :, 0, :].set(1.0)
        vt = jnp.concatenate([jnp.transpose(cv, (0, 1, 3, 2)), ones_pad], axis=2)
        yc = _dsa(p1, p2, vt, bias_t, w_out_c[i].astype(BF16), bsz, seq)
        h = _merge(h, oas, sas, ob, yc, gates, w_out_a[i].astype(BF16), w_out_b[i].astype(BF16),
                   w_o[i].astype(BF16))
        h = _ffn_ple(h, p[i].reshape(m, PLE_DIM), norm_ffn[i].reshape(1, d), w_ffn_gate[i].astype(BF16),
                     w_ffn_up[i].astype(BF16), w_ffn_down[i].astype(BF16), norm_ple[i].reshape(1, d),
                     w_ple_gate[i].astype(BF16), w_ple_proj[i].astype(BF16))
    return h.reshape(bsz, seq, d)
```

```python
import functools
import math

import numpy as np
import jax
import jax.numpy as jnp
from jax import lax
from jax.experimental import pallas as pl
from jax.experimental.pallas import tpu as pltpu

F32 = jnp.float32
BF16 = jnp.bfloat16
I32 = jnp.int32
I16 = jnp.int16

D_MODEL = 1024
HEAD_DIM = 64
RMS_EPS = 1e-6
DSWA_PATTERNS = ((128, 1), (512, 4), (2048, 16))
DSWA_HPG = 4
DSWA_HEADS = 12
DSWA_BLOCK = 128
GLA_HEADS = 4
GLA_DK = 128
GLA_DV = 256
GLA_RANK = 16
GLA_TAU = 16.0
GLA_CHUNK = 64
DSA_HEADS = 12
IDX_HEADS = 8
IDX_DIM = 64
IDX_TOPK = 256
REL_BUCKETS = 32
REL_MAX_DIST = 2048
D_FF = 2816
PLE_DIM = 256
IN_WIDTHS = (768, 768, 768, 512, 512, 1024, 1024, 16, 768, 64, 64, 512, 64, 8)

LANES = 128
MXU_N = 256
VMEM_LIMIT = 56 * 1024 * 1024

W1 = 2560
P1_AQ, P1_AK, P1_CQ, P1_CK = 0, 768, 1536, 2304
W2 = 5120
P2_BV, P2_BR, P2_BQ, P2_BK, P2_IQ, P2_AV = 0, 1024, 2048, 2560, 3072, 3584
P2_BAL, P2_CV, P2_IK, P2_IW = 4352, 4480, 4608, 4736

PROJ_TM = 512
PROJ_TN = 512

DSWA_QB = 4
DSA_TQ = 128
DSA_CH = 256
DSA_AB = 256
DSA_VR = 80
DSA_BIAS_CONST_FROM = 1512
DSA_Z0 = 1792
DSA_Z = DSA_Z0 + DSA_CH
INT_MIN = -(2 ** 31)
NEG_BIG = -1e30


def _cparams(sem, flags=None):
    return pltpu.CompilerParams(dimension_semantics=sem, vmem_limit_bytes=VMEM_LIMIT, flags=flags)


def _proj_kernel(h_ref, g_ref, w_ref, e_ref, *rest, mode, tn, regroup):
    if mode == "qk":
        bd_ref, rest = rest[0], rest[1:]
    o_ref, rg_refs = rest[0], rest[1:1 + len(regroup)]
    scr_ref = rest[-1] if regroup else None
    tm = h_ref.shape[0]
    gw = DSWA_HPG * HEAD_DIM
    x = h_ref[...]
    ms = jnp.mean(x * x, axis=-1, keepdims=True)
    u = (x * lax.rsqrt(ms + RMS_EPS) * g_ref[...]).astype(BF16)
    for j in range(w_ref.shape[1] // tn):
        cs = slice(j * tn, (j + 1) * tn)
        acc = jnp.dot(u, w_ref[:, cs], preferred_element_type=F32)
        if mode == "plain":
            out = acc
        elif mode == "gate":
            out = jax.nn.sigmoid(acc + e_ref[:, cs])
        else:
            sq = acc * acc
            hi = sq.astype(BF16)
            lo = (sq - hi.astype(F32)).astype(BF16)
            bw = bd_ref.shape[0]
            ss = jnp.concatenate(
                [jnp.dot(hi[:, k:k + bw], bd_ref[...], preferred_element_type=F32)
                 + jnp.dot(lo[:, k:k + bw], bd_ref[...], preferred_element_type=F32) for k in range(0, tn, bw)],
                axis=1)
            out = acc * lax.rsqrt(ss * (1.0 / HEAD_DIM) + RMS_EPS) * e_ref[:, cs]
        o_ref[:, cs] = out.astype(o_ref.dtype)
        for (col, r), rg_ref in zip(regroup, rg_refs):
            if col // tn == j:
                for k in range(gw // LANES):
                    scr_ref[k] = out[:, col % tn + k * LANES:col % tn + (k + 1) * LANES]
                for c in range(r):
                    for k in range(gw // LANES):
                        rg_ref[:, c * gw + k * LANES:c * gw + (k + 1) * LANES] = (
                            scr_ref[k, pl.ds(c, tm // r, stride=r), :].astype(rg_ref.dtype))


def _proj(h, g, w, e, mode, regroup=()):
    m, d = h.shape
    n = w.shape[1]
    tm, tn = PROJ_TM, PROJ_TN
    gw = DSWA_HPG * HEAD_DIM
    const = lambda shape: pl.BlockSpec(shape, lambda i: (0, 0), pipeline_mode=pl.Buffered(1))
    in_specs = [pl.BlockSpec((tm, d), lambda i: (i, 0)), const((1, d)), const((d, n)), const((1, n))]
    args = [h, g, w, e]
    if mode == "qk":
        r = np.arange(MXU_N) // HEAD_DIM
        bd = jnp.asarray((r[:, None] == r[None, :]).astype(np.float32), dtype=BF16)
        in_specs.append(const((MXU_N, MXU_N)))
        args.append(bd)
    out_specs = [pl.BlockSpec((tm, n), lambda i: (i, 0))]
    out_shape = [jax.ShapeDtypeStruct((m, n), BF16)]
    for _, r in regroup:
        out_specs.append(pl.BlockSpec((tm // r, r * gw), lambda i: (i, 0)))
        out_shape.append(jax.ShapeDtypeStruct((m // r, r * gw), BF16))
    outs = pl.pallas_call(
        functools.partial(_proj_kernel, mode=mode, tn=tn, regroup=tuple(regroup)),
        grid=(m // tm,),
        in_specs=in_specs,
        out_specs=out_specs,
        out_shape=out_shape,
        scratch_shapes=[pltpu.VMEM((gw // LANES, tm, LANES), F32)] if regroup else [],
        compiler_params=_cparams(("parallel",)),
        name="proj_" + mode,
    )(*args)
    return outs if regroup else outs[0]


def _dswa_kernel(q_ref, kp_ref, kc_ref, vp_ref, vc_ref, bias_ref, o_ref, st_ref, *, qb):
    i = pl.program_id(2)
    blk = DSWA_BLOCK
    row = lax.broadcasted_iota(I32, (blk, 2 * blk), 0)
    col = lax.broadcasted_iota(I32, (blk, 2 * blk), 1)
    cur_ok = (col >= blk) & ((col - blk) <= row)
    valid_inner = ((col < blk) & (col >= row)) | cur_ok
    prev_off = jnp.where(i > 0, 0, 4 * blk)
    valid_first = ((col < blk) & (col >= row + prev_off)) | cur_ok
    lane = lax.broadcasted_iota(I32, (blk, LANES), 1)
    for j in range(qb):
        rs = slice(j * blk, (j + 1) * blk)
        ps = slice((j - 1) * blk, j * blk)
        valid = valid_first if j == 0 else valid_inner
        q = q_ref[rs, :]
        k = jnp.concatenate([kp_ref[...] if j == 0 else kc_ref[ps, :], kc_ref[rs, :]], axis=0)
        v = jnp.concatenate([vp_ref[...] if j == 0 else vc_ref[ps, :], vc_ref[rs, :]], axis=0)
        stats = jnp.zeros((blk, LANES), F32)
        for hp in range(DSWA_HPG // 2):
            ts = slice(hp * LANES, (hp + 1) * LANES)
            q2, k2, v2 = q[:, ts], k[:, ts], v[:, ts]
            halves = []
            for hh in range(2):
                h = 2 * hp + hh
                qm = jnp.where((lane // HEAD_DIM) == hh, q2, jnp.zeros_like(q2))
                s = lax.dot_general(qm, k2, (((1,), (1,)), ((), ())), preferred_element_type=F32)
                s = jnp.where(valid, s + bias_ref[h], -jnp.inf)
                m = jnp.max(s, axis=-1, keepdims=True)
                p = jnp.exp(s - m)
                l = jnp.sum(p, axis=-1, keepdims=True)
                halves.append(jnp.dot(p.astype(BF16), v2, preferred_element_type=F32) / l)
                stats = jnp.where(lane == h, m, stats)
                stats = jnp.where(lane == DSWA_HPG + h, l, stats)
            o_ref[rs, ts] = jnp.where(lane < HEAD_DIM, halves[0], halves[1])
        st_ref[rs, :] = stats


def _dswa(p1, p2, rm, bias, g, r, bsz, seq):
    blk = DSWA_BLOCK
    L = seq // r
    nblk = L // blk
    gw = DSWA_HPG * HEAD_DIM
    if r == 1:
        p1v = p1.reshape(bsz, L, W1)
        p2v = p2.reshape(bsz, L, W2)
        qv, kv, vv = p1v, p1v, p2v
        q_col = lambda c: (P1_AQ // gw) + g
        k_col = lambda c: (P1_AK // gw) + g
        v_col = lambda c: (P2_AV // gw) + g
    else:
        qv, kv, vv = (a.reshape(bsz, L, r * gw) for a in rm)
        q_col = k_col = v_col = lambda c: c
    qb = min(DSWA_QB, nblk)
    prev = lambda i: jnp.maximum(i * qb - 1, 0)
    in_specs = [
        pl.BlockSpec((None, qb * blk, gw), lambda b, c, i: (b, i, q_col(c))),
        pl.BlockSpec((None, blk, gw), lambda b, c, i: (b, prev(i), k_col(c))),
        pl.BlockSpec((None, qb * blk, gw), lambda b, c, i: (b, i, k_col(c))),
        pl.BlockSpec((None, blk, gw), lambda b, c, i: (b, prev(i), v_col(c))),
        pl.BlockSpec((None, qb * blk, gw), lambda b, c, i: (b, i, v_col(c))),
        pl.BlockSpec((DSWA_HPG, blk, 2 * blk), lambda b, c, i: (0, 0, 0)),
    ]
    out_specs = [
        pl.BlockSpec((None, qb * blk, gw), lambda b, c, i: (b, i, c)),
        pl.BlockSpec((None, qb * blk, LANES), lambda b, c, i: (b, i, c)),
    ]
    o, st = pl.pallas_call(
        functools.partial(_dswa_kernel, qb=qb),
        grid=(bsz, r, nblk // qb),
        in_specs=in_specs,
        out_specs=out_specs,
        out_shape=[jax.ShapeDtypeStruct((bsz, L, r * gw), F32),
                   jax.ShapeDtypeStruct((bsz, L, r * LANES), F32)],
        compiler_params=_cparams(("parallel", "parallel", "arbitrary")),
        name="dswa_g%d" % g,
    )(qv, kv, kv, vv, vv, bias)
    return o.reshape(bsz * L, r * gw), st.reshape(bsz * L, r * LANES)


GLA_TC = 256
GLA_GB = 2


def _split3(x):
    a1 = x.astype(BF16)
    r1 = x - a1.astype(F32)
    a2 = r1.astype(BF16)
    a3 = (r1 - a2.astype(F32)).astype(BF16)
    return a1, a2, a3


def _gla_kernel(v_ref, r_ref, q_ref, k_ref, al_ref, wa_ref, ba_ref, gn_ref, o_ref, st_ref, *, gb):
    C = GLA_CHUNK

    @pl.when(pl.program_id(1) == 0)
    def _():
        st_ref[...] = jnp.zeros_like(st_ref)

    TC = GLA_TC
    NC = TC // C
    row = lax.broadcasted_iota(I32, (TC, TC), 0)
    col = lax.broadcasted_iota(I32, (TC, TC), 1)
    tri = ((row // C) == (col // C)) & (row >= col)
    tri_bf = jnp.where(tri, 1.0, 0.0).astype(BF16)
    nt = (((1,), (1,)), ((), ()))
    tn = (((0,), (0,)), ((), ()))
    for bb in range(gb):
        z = jnp.dot(al_ref[bb], wa_ref[...], preferred_element_type=F32) + ba_ref[...]
        la = (jnp.minimum(z, 0.0) - jnp.log(1.0 + jnp.exp(-jnp.abs(z)))) * (1.0 / GLA_TAU)
        a1, a2, a3 = _split3(la)
        bcum = (jnp.dot(tri_bf, a1, preferred_element_type=F32)
                + jnp.dot(tri_bf, a2, preferred_element_type=F32)
                + jnp.dot(tri_bf, a3, preferred_element_type=F32))
        blast = [bcum[(c + 1) * C - 1:(c + 1) * C, :] for c in range(NC)]
        blast_rows = jnp.concatenate([jnp.broadcast_to(b, (C, b.shape[1])) for b in blast], axis=0)
        qf = q_ref[bb].astype(F32) * (GLA_DK ** -0.5)
        kf = k_ref[bb].astype(F32)
        q_in = (qf * jnp.exp(bcum)).astype(BF16)
        k_in = (kf * jnp.exp(-bcum)).astype(BF16)
        k_end = (kf * jnp.exp(blast_rows - bcum)).astype(BF16)
        dec = [jnp.exp(b) for b in blast]
        for h in range(GLA_HEADS):
            ks = slice(h * GLA_DK, (h + 1) * GLA_DK)
            vs = slice(h * GLA_DV, (h + 1) * GLA_DV)
            vh = v_ref[bb, :, vs]
            att = lax.dot_general(q_in[:, ks], k_in[:, ks], nt, preferred_element_type=F32)
            att = jnp.where(tri, att, 0.0)
            o = jnp.dot(att.astype(BF16), vh, preferred_element_type=F32)
            st = st_ref[bb, h]
            inter = []
            for c in range(NC):
                rs = slice(c * C, (c + 1) * C)
                inter.append(lax.dot_general(q_in[rs, ks], st.astype(BF16), nt, preferred_element_type=F32))
                st = st * dec[c][:, ks] + lax.dot_general(vh[rs, :], k_end[rs, ks], tn, preferred_element_type=F32)
            st_ref[bb, h] = st
            o = o + jnp.concatenate(inter, axis=0)
            ms = jnp.mean(o * o, axis=-1, keepdims=True)
            y = o * lax.rsqrt(ms + RMS_EPS) * gn_ref[...]
            rg = r_ref[bb, :, vs].astype(F32)
            y = y * (rg * jax.nn.sigmoid(rg))
            o_ref[bb, :, vs] = y.astype(o_ref.dtype)


def _gla(p2, wa, ba, gn, bsz, seq):
    tc = GLA_TC
    gb = GLA_GB if bsz % GLA_GB == 0 else 1
    p2v = p2.reshape(bsz, seq, W2)
    in_specs = [
        pl.BlockSpec((gb, tc, 1024), lambda b, t: (b, t, P2_BV // 1024)),
        pl.BlockSpec((gb, tc, 1024), lambda b, t: (b, t, P2_BR // 1024)),
        pl.BlockSpec((gb, tc, 512), lambda b, t: (b, t, P2_BQ // 512)),
        pl.BlockSpec((gb, tc, 512), lambda b, t: (b, t, P2_BK // 512)),
        pl.BlockSpec((gb, tc, LANES), lambda b, t: (b, t, P2_BAL // LANES)),
        pl.BlockSpec((LANES, 512), lambda b, t: (0, 0)),
        pl.BlockSpec((1, 512), lambda b, t: (0, 0)),
        pl.BlockSpec((1, GLA_DV), lambda b, t: (0, 0)),
    ]
    out = pl.pallas_call(
        functools.partial(_gla_kernel, gb=gb),
        grid=(bsz // gb, seq // tc),
        in_specs=in_specs,
        out_specs=pl.BlockSpec((gb, tc, 1024), lambda b, t: (b, t, 0)),
        out_shape=jax.ShapeDtypeStruct((bsz, seq, 1024), BF16),
        scratch_shapes=[pltpu.VMEM((gb, GLA_HEADS, GLA_DV, GLA_DK), F32)],
        compiler_params=_cparams(("parallel", "arbitrary")),
        name="gla",
    )(p2v, p2v, p2v, p2v, p2v, wa, ba, gn)
    return out.reshape(bsz * seq, 1024)


def _tree_sum(xs):
    xs = list(xs)
    while len(xs) > 1:
        xs = [xs[i] + xs[i + 1] for i in range(0, len(xs) - 1, 2)] + ([xs[-1]] if len(xs) % 2 else [])
    return xs[0]


def _dsa_kernel(q_ref, iq_ref, iw_ref, k_ref, ik_ref, vt_ref, bias_ref, wot_ref, o_ref,
                keys_ref, planes_ref, qaug_ref, iqall_ref, acc_ref, st0_ref, st1_ref, m_ref, cm0_ref, cm1_ref,
                *, topk):
    TQ, CH = DSA_TQ, DSA_CH
    CC = 2 * CH
    qi = pl.program_id(1)
    t0 = qi * TQ
    nch = qi // (CH // TQ) + 1
    npair = (nch + 1) // 2
    nt = (((1,), (1,)), ((), ()))

    eye = (lax.broadcasted_iota(I32, (TQ, TQ), 0) == lax.broadcasted_iota(I32, (TQ, TQ), 1))
    eye = jnp.where(eye, 1.0, 0.0).astype(BF16)
    q_t = q_ref[...].astype(F32).T.astype(BF16)
    iq_t = iq_ref[...].astype(F32).T.astype(BF16)
    for h in range(DSA_HEADS):
        cs = slice(h * TQ, (h + 1) * TQ)
        qaug_ref[0:TQ, cs] = eye
        qaug_ref[TQ:TQ + HEAD_DIM, cs] = q_t[h * HEAD_DIM:(h + 1) * HEAD_DIM, :]
        qaug_ref[TQ + HEAD_DIM:, cs] = jnp.zeros((TQ - HEAD_DIM, TQ), BF16)
    for h in range(IDX_HEADS):
        cs = slice(h * TQ, (h + 1) * TQ)
        iqall_ref[0:IDX_DIM, cs] = iq_t[h * IDX_DIM:(h + 1) * IDX_DIM, :]
        iqall_ref[IDX_DIM:, cs] = jnp.zeros((LANES - IDX_DIM, TQ), BF16)
    idx_scale = (IDX_HEADS ** -0.5) * (IDX_DIM ** -0.5)
    wt = (iw_ref[...].astype(F32) * idx_scale).T

    qpos = t0 + lax.broadcasted_iota(I32, (1, TQ), 1)
    RB = 128
    krow1 = lax.broadcasted_iota(I32, (RB, TQ), 0)
    krow = lax.broadcasted_iota(I32, (CH, TQ), 0)
    krow2 = lax.broadcasted_iota(I32, (CC, TQ), 0)

    def score_body(cp, carry):
        for sub in range(CC // RB):
            s0 = pl.multiple_of(cp * CC + sub * RB, RB)
            ikc = ik_ref[pl.ds(s0, RB), :]
            s = None
            for hp in range(IDX_HEADS // 2):
                x = jnp.dot(ikc, iqall_ref[:, hp * 2 * TQ:(hp + 1) * 2 * TQ],
                            preferred_element_type=F32)
                t = (jnp.maximum(x[:, :TQ], 0.0) * wt[2 * hp:2 * hp + 1, :]
                     + jnp.maximum(x[:, TQ:], 0.0) * wt[2 * hp + 1:2 * hp + 2, :])
                s = t if s is None else s + t
            s = jnp.where(s == 0.0, 0.0, s)
            bits = pltpu.bitcast(s, I32)
            key = bits ^ ((bits >> 31) & 0x7FFFFFFF)
            key = jnp.where(s0 + krow1 <= qpos, key, INT_MIN)
            keys_ref[pl.ds(s0, RB), :] = key
        return carry

    lax.fori_loop(0, npair, score_body, 0)

    kvec = jnp.minimum(topk, qpos + 1)
    NPL = 32

    def planes_body(c, carry):
        s0 = pl.multiple_of(c * CH, CH)
        a = [keys_ref[pl.ds(s0 + 8 * j, 8), :] ^ INT_MIN for j in range(NPL)]
        j, m = 16, 0x0000FFFF
        while j:
            sh = jnp.full((8, TQ), j, I32)
            k = 0
            while k < NPL:
                t = (a[k] ^ lax.shift_right_logical(a[k + j], sh)) & m
                a[k] = a[k] ^ t
                a[k + j] = a[k + j] ^ lax.shift_left(t, sh)
                k = (k + j + 1) & ~j
            j >>= 1
            m = (m ^ (m << j)) & 0xFFFFFFFF
        for p in range(NPL):
            planes_ref[c, p] = a[p]
        return carry

    def empty_body(c, carry):
        for p in range(NPL):
            planes_ref[c, p] = jnp.zeros((8, TQ), I32)
        return carry

    nck = 2 * npair
    NCK = keys_ref.shape[0] // CH
    lax.fori_loop(0, nck, planes_body, 0)
    lax.fori_loop(nck, NCK, empty_body, 0)

    def plane_body(p, carry):
        thr_u, n_gt, alive = carry
        hit = [alive[c] & planes_ref[c, p] for c in range(NCK)]
        ones = jnp.sum(_tree_sum([lax.population_count(h) for h in hit]), axis=0, keepdims=True)
        take = (n_gt + ones) >= kvec
        thr_u = jnp.where(take, thr_u | (jnp.int32(1) << (31 - p)), thr_u)
        n_gt = jnp.where(take, n_gt, n_gt + ones)
        alive = tuple(jnp.where(take, hit[c], alive[c] ^ hit[c]) for c in range(NCK))
        return thr_u, n_gt, alive

    zero = jnp.zeros((1, TQ), I32)
    alive0 = tuple(jnp.full((8, TQ), -1, I32) for _ in range(NCK))
    thr_u, n_gt, alive = lax.fori_loop(0, NPL, plane_body, (zero, zero, alive0))
    n_eq = jnp.sum(_tree_sum([lax.population_count(a) for a in alive]), axis=0, keepdims=True)
    thr = thr_u ^ INT_MIN

    def count(pred_fn):
        def body(c, acc):
            s0 = pl.multiple_of(c * CC, CC)
            kk = keys_ref[pl.ds(s0, CC), :]
            hit = jnp.where(pred_fn(kk, s0), 1, 0).astype(I32)
            return acc + jnp.sum(hit.reshape(CC // 8, 8, TQ), axis=0)
        acc = lax.fori_loop(0, npair, body, jnp.zeros((8, TQ), I32))
        return jnp.sum(acc, axis=0, keepdims=True)

    excess = n_gt + n_eq - kvec
    has_excess = jnp.max(excess) > 0

    @pl.when(has_excess)
    def _():
        need = kvec - n_gt

        def tie_lt(cut):
            return count(lambda kk, s0: (kk == thr) & (s0 + krow2 < cut))

        def cut_body(b, cut):
            cand = cut | (jnp.int32(1) << (12 - b))
            return jnp.where(tie_lt(cand) <= need, cand, cut)

        cut = lax.fori_loop(0, 13, cut_body, jnp.zeros((1, TQ), I32))

        def drop_body(c, carry):
            s0 = pl.multiple_of(c * CH, CH)
            kk = keys_ref[pl.ds(s0, CH), :]
            keys_ref[pl.ds(s0, CH), :] = jnp.where((kk == thr) & (s0 + krow >= cut), INT_MIN, kk)
            return carry

        lax.fori_loop(0, nch, drop_body, 0)

    acc_ref[...] = jnp.zeros_like(acc_ref)

    NP = DSA_HEADS // 2
    PW = 2 * TQ

    AB = DSA_AB
    NB = CH // AB
    last_chunk = keys_ref.shape[0] // CH - 1

    def logits(c, st_ref, cm_ref):
        s0 = pl.multiple_of(c * CH, CH)
        zoff = pl.multiple_of(jnp.clip(s0 - t0 + DSA_Z0, 0, DSA_Z0), 8)
        kaug = []
        for rb in range(NB):
            rs = pl.ds(s0 + rb * AB, AB)
            pen = jnp.where(keys_ref[rs, :] >= thr, 0.0, NEG_BIG).astype(BF16)
            kaug.append(jnp.concatenate([pen, k_ref[rs, :]], axis=1))
        for hp in range(NP):
            ps = slice(hp * PW, (hp + 1) * PW)
            cm = None
            for rb in range(NB):
                st = jnp.dot(kaug[rb], qaug_ref[:, ps], preferred_element_type=F32)
                st = st + bias_ref[pl.ds(zoff + rb * AB, AB), ps]
                st_ref[rb * AB:(rb + 1) * AB, ps] = st
                tm = jnp.max(st.reshape(AB // 8, 8, PW), axis=0)
                cm = tm if cm is None else jnp.maximum(cm, tm)
            cm_ref[0:1, ps] = jnp.max(cm, axis=0, keepdims=True)

    def accumulate(c, st_ref, cm_ref):
        vtc = vt_ref[c]
        for hp in range(NP):
            ps = slice(hp * PW, (hp + 1) * PW)
            m_old = m_ref[0:1, ps]
            m_new = jnp.maximum(m_old, cm_ref[0:1, ps])
            m_ref[0:1, ps] = m_new
            alpha = jnp.exp2(m_old - m_new)
            p = jnp.concatenate(
                [jnp.exp2(st_ref[rb * AB:(rb + 1) * AB, ps] - m_new).astype(BF16) for rb in range(NB)], axis=0)
            acc_ref[:, ps] = acc_ref[:, ps] * alpha + jnp.dot(vtc, p, preferred_element_type=F32)

    def att_body(cp, carry):
        c0 = 2 * cp
        logits(c0 + 1, st1_ref, cm1_ref)
        accumulate(c0, st0_ref, cm0_ref)
        logits(c0 + 2, st0_ref, cm0_ref)
        accumulate(c0 + 1, st1_ref, cm1_ref)
        return carry

    m_ref[...] = jnp.full(m_ref.shape, NEG_BIG, F32)
    logits(0, st0_ref, cm0_ref)
    nfull = (nch - 1) // 2
    lax.fori_loop(0, nfull, att_body, 0)
    c_last = 2 * nfull

    @pl.when(nch % 2 == 1)
    def _():
        accumulate(c_last, st0_ref, cm0_ref)

    @pl.when(nch % 2 == 0)
    def _():
        logits(c_last + 1, st1_ref, cm1_ref)
        accumulate(c_last, st0_ref, cm0_ref)
        accumulate(c_last + 1, st1_ref, cm1_ref)

    ot = jnp.concatenate(
        [acc_ref[0:HEAD_DIM, h * TQ:(h + 1) * TQ] / acc_ref[HEAD_DIM:HEAD_DIM + 1, h * TQ:(h + 1) * TQ]
         for h in range(DSA_HEADS)], axis=0)
    o_ref[...] = jnp.dot(ot.T.astype(BF16), wot_ref[...], preferred_element_type=F32)


def _dsa(p1, p2, vt, bias_t, wot, bsz, seq):
    tq = DSA_TQ
    p1v = p1.reshape(bsz, seq, W1)
    p2v = p2.reshape(bsz, seq, W2)
    nq = DSA_HEADS * tq
    in_specs = [
        pl.BlockSpec((None, tq, 768), lambda b, i: (b, i, P1_CQ // 768)),
        pl.BlockSpec((None, tq, 512), lambda b, i: (b, i, P2_IQ // 512)),
        pl.BlockSpec((None, tq, LANES), lambda b, i: (b, i, P2_IW // LANES)),
        pl.BlockSpec((None, seq, LANES), lambda b, i: (b, 0, P1_CK // LANES)),
        pl.BlockSpec((None, seq, LANES), lambda b, i: (b, 0, P2_IK // LANES)),
        pl.BlockSpec((None, seq // DSA_CH, DSA_VR, DSA_CH), lambda b, i: (b, 0, 0, 0)),
        pl.BlockSpec((DSA_Z, nq), lambda b, i: (0, 0)),
        pl.BlockSpec((DSA_HEADS * HEAD_DIM, D_MODEL), lambda b, i: (0, 0)),
    ]
    out = pl.pallas_call(
        functools.partial(_dsa_kernel, topk=min(IDX_TOPK, seq // 4)),
        grid=(bsz, seq // tq),
        in_specs=in_specs,
        out_specs=pl.BlockSpec((None, tq, D_MODEL), lambda b, i: (b, i, 0)),
        out_shape=jax.ShapeDtypeStruct((bsz, seq, D_MODEL), F32),
        scratch_shapes=[
            pltpu.VMEM((seq, tq), I32),
            pltpu.VMEM((seq // DSA_CH, 32, 8, tq), I32),
            pltpu.VMEM((2 * tq, nq), BF16),
            pltpu.VMEM((LANES, IDX_HEADS * tq), BF16),
            pltpu.VMEM((DSA_VR, nq), F32),
            pltpu.VMEM((DSA_CH, nq), F32),
            pltpu.VMEM((DSA_CH, nq), F32),
            pltpu.VMEM((8, nq), F32),
            pltpu.VMEM((8, nq), F32),
            pltpu.VMEM((8, nq), F32),
        ],
        compiler_params=_cparams(("parallel", "arbitrary")),
        name="dsa",
    )(p1v, p2v, p2v, p1v, p2v, vt, bias_t, wot)
    return out.reshape(bsz * seq, D_MODEL)


MERGE_TM = 512


def _merge_kernel(h_ref, oa0_ref, oa1_ref, oa2_ref, sa0_ref, sa1_ref, sa2_ref, ob_ref, yc_ref, g_ref,
                  woa_ref, wob_ref, wo_ref, o_ref, on1_ref, on2_ref, sn1_ref, sn2_ref):
    H = DSWA_HPG
    tm = h_ref.shape[0]
    gw = H * HEAD_DIM
    for (_, r), src_o, src_s, dst_o, dst_s in zip(DSWA_PATTERNS[1:], (oa1_ref, oa2_ref), (sa1_ref, sa2_ref),
                                                  (on1_ref, on2_ref), (sn1_ref, sn2_ref)):
        for c in range(r):
            for k in range(gw // LANES):
                dst_o[k, pl.ds(c, tm // r, stride=r), :] = src_o[:, c * gw + k * LANES:c * gw + (k + 1) * LANES]
            dst_s[pl.ds(c, tm // r, stride=r), :] = src_s[:, c * LANES:(c + 1) * LANES]
    oa_vals = (oa0_ref[...],) + tuple(
        jnp.concatenate([ref[k] for k in range(gw // LANES)], axis=1) for ref in (on1_ref, on2_ref))
    sts = [sa0_ref[...], sn1_ref[...], sn2_ref[...]]
    lane = lax.broadcasted_iota(I32, sts[0].shape, 1)
    mmax = jnp.maximum(jnp.maximum(sts[0], sts[1]), sts[2])
    wts = [pltpu.roll(s, LANES - H, 1) * jnp.exp(s - mmax) for s in sts]
    tot = wts[0] + wts[1] + wts[2]
    hrow = lax.broadcasted_iota(I32, (LANES, H * HEAD_DIM), 0)
    hcol = lax.broadcasted_iota(I32, (LANES, H * HEAD_DIM), 1) // HEAD_DIM
    expand = jnp.where(hrow == hcol, 1.0, 0.0).astype(BF16)
    oa = None
    for g in range(3):
        w = jnp.where(lane < H, wts[g] / tot, 0.0)
        hi = w.astype(BF16)
        lo = (w - hi.astype(F32)).astype(BF16)
        wfull = (jnp.dot(hi, expand, preferred_element_type=F32)
                 + jnp.dot(lo, expand, preferred_element_type=F32))
        term = wfull * oa_vals[g]
        oa = term if oa is None else oa + term
    oa = oa.astype(BF16)
    y_a = jnp.dot(oa, woa_ref[...], preferred_element_type=F32)
    y_b = jnp.dot(ob_ref[...], wob_ref[...], preferred_element_type=F32)
    y_c = yc_ref[...]
    D = D_MODEL
    mix = (g_ref[:, 0:D].astype(F32) * y_a + g_ref[:, D:2 * D].astype(F32) * y_b
           + g_ref[:, 2 * D:3 * D].astype(F32) * y_c)
    o_ref[...] = h_ref[...] + jnp.dot(mix.astype(BF16), wo_ref[...], preferred_element_type=F32)


def _merge(h, oas, sas, ob, yc, gates, woa, wob, wo):
    m = h.shape[0]
    tm = MERGE_TM
    gw = DSWA_HPG * HEAD_DIM
    row = lambda w: pl.BlockSpec((tm, w), lambda i: (i, 0))
    rmrow = lambda w, r: pl.BlockSpec((tm // r, r * w), lambda i: (i, 0))
    full = lambda a: pl.BlockSpec(a.shape, lambda i: (0, 0), pipeline_mode=pl.Buffered(1))
    rs = [r for _, r in DSWA_PATTERNS]
    in_specs = ([row(D_MODEL)] + [rmrow(gw, r) for r in rs] + [rmrow(LANES, r) for r in rs]
                + [row(1024), row(D_MODEL), row(3 * D_MODEL)] + [full(woa), full(wob), full(wo)])
    return pl.pallas_call(
        _merge_kernel,
        grid=(m // tm,),
        in_specs=in_specs,
        out_specs=row(D_MODEL),
        out_shape=jax.ShapeDtypeStruct((m, D_MODEL), F32),
        scratch_shapes=[pltpu.VMEM((gw // LANES, tm, LANES), F32), pltpu.VMEM((gw // LANES, tm, LANES), F32),
                        pltpu.VMEM((tm, LANES), F32), pltpu.VMEM((tm, LANES), F32)],
        compiler_params=_cparams(("parallel",)),
        name="merge",
    )(h, *oas, *sas, ob, yc, gates, woa, wob, wo)


FFN_TM = 512
FFN_TF = 256


def _rms(x, g):
    ms = jnp.mean(x * x, axis=-1, keepdims=True)
    return (x * lax.rsqrt(ms + RMS_EPS) * g).astype(BF16)


def _ffn_ple_kernel(h_ref, p_ref, gf_ref, wg_ref, wu_ref, wd_ref, gp_ref, wpg_ref, wpp_ref, o_ref, *, tf):
    x = h_ref[...]
    u = _rms(x, gf_ref[...])
    acc = None
    for j in range(wg_ref.shape[1] // tf):
        cs = slice(j * tf, (j + 1) * tf)
        a = jnp.dot(u, wg_ref[:, cs], preferred_element_type=F32)
        b = jnp.dot(u, wu_ref[:, cs], preferred_element_type=F32)
        t = (a * jax.nn.sigmoid(a) * b).astype(BF16)
        d = jnp.dot(t, wd_ref[cs, :], preferred_element_type=F32)
        acc = d if acc is None else acc + d
    h2 = x + acc
    e = _rms(h2, gp_ref[...])
    gate = jax.nn.sigmoid(jnp.dot(e, wpg_ref[...], preferred_element_type=F32))
    proj = jnp.dot(p_ref[...].astype(BF16), wpp_ref[...], preferred_element_type=F32)
    o_ref[...] = h2 + gate * proj


def _ffn_ple(h, p, gf, wg, wu, wd, gp, wpg, wpp):
    m, d = h.shape
    tm, tf = FFN_TM, FFN_TF
    const = lambda a: pl.BlockSpec(a.shape, lambda i: (0, 0), pipeline_mode=pl.Buffered(1))
    return pl.pallas_call(
        functools.partial(_ffn_ple_kernel, tf=tf),
        grid=(m // tm,),
        in_specs=[pl.BlockSpec((tm, d), lambda i: (i, 0)), pl.BlockSpec((tm, PLE_DIM), lambda i: (i, 0)),
                  const(gf), const(wg), const(wu), const(wd), const(gp), const(wpg), const(wpp)],
        out_specs=pl.BlockSpec((tm, d), lambda i: (i, 0)),
        out_shape=jax.ShapeDtypeStruct((m, d), F32),
        compiler_params=_cparams(("parallel",)),
        name="ffn_ple",
    )(h, p, gf, wg, wu, wd, gp, wpg, wpp)


def _rel_bucket(dist):
    max_exact = REL_BUCKETS // 2
    d = jnp.maximum(dist, 0)
    df = jnp.maximum(d, 1).astype(F32)
    large = max_exact + (jnp.log(df / max_exact) / math.log(REL_MAX_DIST / max_exact)
                         * (REL_BUCKETS - max_exact)).astype(I32)
    large = jnp.minimum(large, REL_BUCKETS - 1)
    return jnp.where(d < max_exact, d, large)


def _toeplitz(rev, n_rows, n_cols):
    nh = rev.shape[0]
    lw = rev.shape[1] + 1
    w = jnp.pad(rev, ((0, 0), (0, 1)))
    s = jnp.broadcast_to(w[:, None, :], (nh, n_rows, lw)).reshape(nh, n_rows * lw)
    s = s[:, :n_rows * (lw - 1)].reshape(nh, n_rows, lw - 1)
    return s[:, :, n_rows - 1:n_rows - 1 + n_cols]


def _bias_tables(rel_bias):
    blk = DSWA_BLOCK
    dswa = []
    for g, (_, r) in enumerate(DSWA_PATTERNS):
        delta = np.arange(3 * blk - 1)[::-1] - (blk - 1)
        rev = rel_bias[_rel_bucket(jnp.asarray(delta * r, I32))][:, g * DSWA_HPG:(g + 1) * DSWA_HPG]
        dswa.append(_toeplitz(rev.T, blk, 2 * blk))
    tq = DSA_TQ
    dist = np.arange(DSA_Z + tq - 1)[::-1] - (DSA_Z - 1) + DSA_Z0
    rev = rel_bias[_rel_bucket(jnp.asarray(dist, I32))][:, DSWA_HEADS:] * math.log2(math.e)
    bias_t = jnp.transpose(_toeplitz(rev.T, tq, DSA_Z), (2, 0, 1)).reshape(DSA_Z, DSA_HEADS * tq)
    return dswa, bias_t


def _pad_cols(w, width):
    return jnp.pad(w, ((0, 0), (0, width - w.shape[1])))


def _layer_params(w_in, qn_a, kn_a, qn_c, kn_c, w_alpha2, b_alpha):
    offs = np.cumsum((0,) + IN_WIDTHS)
    parts = [w_in[:, offs[i]:offs[i + 1]] for i in range(len(IN_WIDTHS))]
    a_q, a_k, a_v, b_q, b_k, b_v, b_r, b_al, c_q, c_k, c_v, i_q, i_k, i_w = parts
    w1 = _pad_cols(jnp.concatenate([a_q, a_k, c_q, c_k], axis=1), W1).astype(BF16)
    w2 = _pad_cols(jnp.concatenate(
        [b_v, b_r, b_q, b_k, i_q, a_v, _pad_cols(b_al, LANES), _pad_cols(c_v, LANES),
         _pad_cols(i_k, LANES), _pad_cols(i_w, LANES)], axis=1), W2).astype(BF16)
    scale = HEAD_DIM ** -0.5
    gain1 = jnp.concatenate([jnp.tile(qn_a, DSWA_HEADS) * scale, jnp.tile(kn_a, DSWA_HEADS),
                             jnp.tile(qn_c, DSA_HEADS) * (scale * math.log2(math.e)), kn_c])
    gain1 = jnp.pad(gain1, (0, W1 - gain1.shape[0])).reshape(1, W1)
    wa = jnp.pad(w_alpha2, ((0, LANES - GLA_RANK), (0, 0))).astype(BF16)
    return w1, w2, gain1, wa, b_alpha.reshape(1, -1)


def kernel(x, p, rel_bias, norm_mix, w_in, qn_a, kn_a, qn_c, kn_c, w_alpha2, b_alpha, gla_norm, w_out_a, w_out_b, w_out_c, w_gate, b_gate, w_o, norm_ffn, w_ffn_gate, w_ffn_up, w_ffn_down, norm_ple, w_ple_gate, w_ple_proj):
    bsz, seq, d = x.shape
    depth = p.shape[0]
    m = bsz * seq
    dswa_bias, bias_t = _bias_tables(rel_bias)
    h = x.reshape(m, d)
    zeros_w2 = jnp.zeros((1, W2), F32)
    for i in range(depth):
        w1, w2, gain1, wa, ba = _layer_params(w_in[i], qn_a[i], kn_a[i], qn_c[i], kn_c[i], w_alpha2[i], b_alpha[i])
        gmix = norm_mix[i].reshape(1, d)
        gw = DSWA_HPG * HEAD_DIM
        dil = [(g, r) for g, (_, r) in enumerate(DSWA_PATTERNS) if r > 1]
        p1, *qk_rm = _proj(h, gmix, w1, gain1, "qk",
                           regroup=[(off + g * gw, r) for off in (P1_AQ, P1_AK) for g, r in dil])
        p2, *v_rm = _proj(h, gmix, w2, zeros_w2, "plain", regroup=[(P2_AV + g * gw, r) for g, r in dil])
        gates = _proj(h, gmix, w_gate[i].astype(BF16), b_gate[i].reshape(1, -1), "gate")
        rm = {g: (qk_rm[n], qk_rm[len(dil) + n], v_rm[n]) for n, (g, r) in enumerate(dil)}
        oas, sas = [], []
        for g, (_, r) in enumerate(DSWA_PATTERNS):
            o, st = _dswa(p1, p2, rm.get(g), dswa_bias[g], g, r, bsz, seq)
            oas.append(o)
            sas.append(st)
        ob = _gla(p2, wa, ba, gla_norm[i].reshape(1, -1), bsz, seq)
        cv = p2[:, P2_CV:P2_CV + HEAD_DIM].reshape(bsz, seq // DSA_CH, DSA_CH, HEAD_DIM)
        ones_pad = jnp.zeros((bsz, seq // DSA_CH, DSA_VR - HEAD_DIM, DSA_CH), BF16).at[:, :, 0, :].set(1.0)
        vt = jnp.concatenate([jnp.transpose(cv, (0, 1, 3, 2)), ones_pad], axis=2)
        yc = _dsa(p1, p2, vt, bias_t, w_out_c[i].astype(BF16), bsz, seq)
        h = _merge(h, oas, sas, ob, yc, gates, w_out_a[i].astype(BF16), w_out_b[i].astype(BF16),
                   w_o[i].astype(BF16))
        h = _ffn_ple(h, p[i].reshape(m, PLE_DIM), norm_ffn[i].reshape(1, d), w_ffn_gate[i].astype(BF16),
                     w_ffn_up[i].astype(BF16), w_ffn_down[i].astype(BF16), norm_ple[i].reshape(1, d),
                     w_ple_gate[i].astype(BF16), w_ple_proj[i].astype(BF16))
    return h.reshape(bsz, seq, d)
```

```python
import functools
import math

import numpy as np
import jax
import jax.numpy as jnp
from jax import lax
from jax.experimental import pallas as pl
from jax.experimental.pallas import tpu as pltpu

F32 = jnp.float32
BF16 = jnp.bfloat16
I32 = jnp.int32
I16 = jnp.int16

D_MODEL = 1024
HEAD_DIM = 64
RMS_EPS = 1e-6
DSWA_PATTERNS = ((128, 1), (512, 4), (2048, 16))
DSWA_HPG = 4
DSWA_HEADS = 12
DSWA_BLOCK = 128
GLA_HEADS = 4
GLA_DK = 128
GLA_DV = 256
GLA_RANK = 16
GLA_TAU = 16.0
GLA_CHUNK = 64
DSA_HEADS = 12
IDX_HEADS = 8
IDX_DIM = 64
IDX_TOPK = 256
REL_BUCKETS = 32
REL_MAX_DIST = 2048
D_FF = 2816
PLE_DIM = 256
IN_WIDTHS = (768, 768, 768, 512, 512, 1024, 1024, 16, 768, 64, 64, 512, 64, 8)

LANES = 128
MXU_N = 256
VMEM_LIMIT = 56 * 1024 * 1024

W1 = 2560
P1_AQ, P1_AK, P1_CQ, P1_CK = 0, 768, 1536, 2304
W2 = 5120
P2_BV, P2_BR, P2_BQ, P2_BK, P2_IQ, P2_AV = 0, 1024, 2048, 2560, 3072, 3584
P2_BAL, P2_CV, P2_IK, P2_IW = 4352, 4480, 4608, 4736

PROJ_TM = 512
PROJ_TN = 512

DSWA_QB = 4
DSA_TQ = 128
DSA_CH = 256
DSA_AB = 256
DSA_VR = 80
DSA_BIAS_CONST_FROM = 1512
DSA_Z0 = 1792
DSA_Z = DSA_Z0 + DSA_CH
INT_MIN = -(2 ** 31)
NEG_BIG = -1e30


def _cparams(sem, flags=None):
    return pltpu.CompilerParams(dimension_semantics=sem, vmem_limit_bytes=VMEM_LIMIT, flags=flags)


def _proj_kernel(h_ref, g_ref, w_ref, e_ref, *rest, mode, tn, regroup):
    if mode == "qk":
        bd_ref, rest = rest[0], rest[1:]
    o_ref, rg_refs = rest[0], rest[1:1 + len(regroup)]
    scr_ref = rest[-1] if regroup else None
    tm = h_ref.shape[0]
    gw = DSWA_HPG * HEAD_DIM
    x = h_ref[...]
    ms = jnp.mean(x * x, axis=-1, keepdims=True)
    u = (x * lax.rsqrt(ms + RMS_EPS) * g_ref[...]).astype(BF16)
    for j in range(w_ref.shape[1] // tn):
        cs = slice(j * tn, (j + 1) * tn)
        acc = jnp.dot(u, w_ref[:, cs], preferred_element_type=F32)
        if mode == "plain":
            out = acc
        elif mode == "gate":
            out = jax.nn.sigmoid(acc + e_ref[:, cs])
        else:
            sq = acc * acc
            hi = sq.astype(BF16)
            lo = (sq - hi.astype(F32)).astype(BF16)
            bw = bd_ref.shape[0]
            ss = jnp.concatenate(
                [jnp.dot(hi[:, k:k + bw], bd_ref[...], preferred_element_type=F32)
                 + jnp.dot(lo[:, k:k + bw], bd_ref[...], preferred_element_type=F32) for k in range(0, tn, bw)],
                axis=1)
            out = acc * lax.rsqrt(ss * (1.0 / HEAD_DIM) + RMS_EPS) * e_ref[:, cs]
        o_ref[:, cs] = out.astype(o_ref.dtype)
        for (col, r), rg_ref in zip(regroup, rg_refs):
            if col // tn == j:
                for k in range(gw // LANES):
                    scr_ref[k] = out[:, col % tn + k * LANES:col % tn + (k + 1) * LANES]
                for c in range(r):
                    for k in range(gw // LANES):
                        rg_ref[:, c * gw + k * LANES:c * gw + (k + 1) * LANES] = (
                            scr_ref[k, pl.ds(c, tm // r, stride=r), :].astype(rg_ref.dtype))


def _proj(h, g, w, e, mode, regroup=()):
    m, d = h.shape
    n = w.shape[1]
    tm, tn = PROJ_TM, PROJ_TN
    gw = DSWA_HPG * HEAD_DIM
    const = lambda shape: pl.BlockSpec(shape, lambda i: (0, 0), pipeline_mode=pl.Buffered(1))
    in_specs = [pl.BlockSpec((tm, d), lambda i: (i, 0)), const((1, d)), const((d, n)), const((1, n))]
    args = [h, g, w, e]
    if mode == "qk":
        r = np.arange(MXU_N) // HEAD_DIM
        bd = jnp.asarray((r[:, None] == r[None, :]).astype(np.float32), dtype=BF16)
        in_specs.append(const((MXU_N, MXU_N)))
        args.append(bd)
    out_specs = [pl.BlockSpec((tm, n), lambda i: (i, 0))]
    out_shape = [jax.ShapeDtypeStruct((m, n), BF16)]
    for _, r in regroup:
        out_specs.append(pl.BlockSpec((tm // r, r * gw), lambda i: (i, 0)))
        out_shape.append(jax.ShapeDtypeStruct((m // r, r * gw), BF16))
    outs = pl.pallas_call(
        functools.partial(_proj_kernel, mode=mode, tn=tn, regroup=tuple(regroup)),
        grid=(m // tm,),
        in_specs=in_specs,
        out_specs=out_specs,
        out_shape=out_shape,
        scratch_shapes=[pltpu.VMEM((gw // LANES, tm, LANES), F32)] if regroup else [],
        compiler_params=_cparams(("parallel",)),
        name="proj_" + mode,
    )(*args)
    return outs if regroup else outs[0]


def _dswa_kernel(q_ref, kp_ref, kc_ref, vp_ref, vc_ref, bias_ref, o_ref, st_ref, *, qb):
    i = pl.program_id(2)
    blk = DSWA_BLOCK
    row = lax.broadcasted_iota(I32, (blk, 2 * blk), 0)
    col = lax.broadcasted_iota(I32, (blk, 2 * blk), 1)
    cur_ok = (col >= blk) & ((col - blk) <= row)
    valid_inner = ((col < blk) & (col >= row)) | cur_ok
    prev_off = jnp.where(i > 0, 0, 4 * blk)
    valid_first = ((col < blk) & (col >= row + prev_off)) | cur_ok
    lane = lax.broadcasted_iota(I32, (blk, LANES), 1)
    for j in range(qb):
        rs = slice(j * blk, (j + 1) * blk)
        ps = slice((j - 1) * blk, j * blk)
        valid = valid_first if j == 0 else valid_inner
        q = q_ref[rs, :]
        k = jnp.concatenate([kp_ref[...] if j == 0 else kc_ref[ps, :], kc_ref[rs, :]], axis=0)
        v = jnp.concatenate([vp_ref[...] if j == 0 else vc_ref[ps, :], vc_ref[rs, :]], axis=0)
        stats = jnp.zeros((blk, LANES), F32)
        for hp in range(DSWA_HPG // 2):
            ts = slice(hp * LANES, (hp + 1) * LANES)
            q2, k2, v2 = q[:, ts], k[:, ts], v[:, ts]
            halves = []
            for hh in range(2):
                h = 2 * hp + hh
                qm = jnp.where((lane // HEAD_DIM) == hh, q2, jnp.zeros_like(q2))
                s = lax.dot_general(qm, k2, (((1,), (1,)), ((), ())), preferred_element_type=F32)
                s = jnp.where(valid, s + bias_ref[h], -jnp.inf)
                m = jnp.max(s, axis=-1, keepdims=True)
                p = jnp.exp(s - m)
                l = jnp.sum(p, axis=-1, keepdims=True)
                halves.append(jnp.dot(p.astype(BF16), v2, preferred_element_type=F32) / l)
                stats = jnp.where(lane == h, m, stats)
                stats = jnp.where(lane == DSWA_HPG + h, l, stats)
            o_ref[rs, ts] = jnp.where(lane < HEAD_DIM, halves[0], halves[1])
        st_ref[rs, :] = stats


def _dswa(p1, p2, rm, bias, g, r, bsz, seq):
    blk = DSWA_BLOCK
    L = seq // r
    nblk = L // blk
    gw = DSWA_HPG * HEAD_DIM
    if r == 1:
        p1v = p1.reshape(bsz, L, W1)
        p2v = p2.reshape(bsz, L, W2)
        qv, kv, vv = p1v, p1v, p2v
        q_col = lambda c: (P1_AQ // gw) + g
        k_col = lambda c: (P1_AK // gw) + g
        v_col = lambda c: (P2_AV // gw) + g
    else:
        qv, kv, vv = (a.reshape(bsz, L, r * gw) for a in rm)
        q_col = k_col = v_col = lambda c: c
    qb = min(DSWA_QB, nblk)
    prev = lambda i: jnp.maximum(i * qb - 1, 0)
    in_specs = [
        pl.BlockSpec((None, qb * blk, gw), lambda b, c, i: (b, i, q_col(c))),
        pl.BlockSpec((None, blk, gw), lambda b, c, i: (b, prev(i), k_col(c))),
        pl.BlockSpec((None, qb * blk, gw), lambda b, c, i: (b, i, k_col(c))),
        pl.BlockSpec((None, blk, gw), lambda b, c, i: (b, prev(i), v_col(c))),
        pl.BlockSpec((None, qb * blk, gw), lambda b, c, i: (b, i, v_col(c))),
        pl.BlockSpec((DSWA_HPG, blk, 2 * blk), lambda b, c, i: (0, 0, 0)),
    ]
    out_specs = [
        pl.BlockSpec((None, qb * blk, gw), lambda b, c, i: (b, i, c)),
        pl.BlockSpec((None, qb * blk, LANES), lambda b, c, i: (b, i, c)),
    ]
    o, st = pl.pallas_call(
        functools.partial(_dswa_kernel, qb=qb),
        grid=(bsz, r, nblk // qb),
        in_specs=in_specs,
        out_specs=out_specs,
        out_shape=[jax.ShapeDtypeStruct((bsz, L, r * gw), F32),
                   jax.ShapeDtypeStruct((bsz, L, r * LANES), F32)],
        compiler_params=_cparams(("parallel", "parallel", "arbitrary")),
        name="dswa_g%d" % g,
    )(qv, kv, kv, vv, vv, bias)
    return o.reshape(bsz * L, r * gw), st.reshape(bsz * L, r * LANES)


GLA_TC = 256
GLA_GB = 2


def _split3(x):
    a1 = x.astype(BF16)
    r1 = x - a1.astype(F32)
    a2 = r1.astype(BF16)
    a3 = (r1 - a2.astype(F32)).astype(BF16)
    return a1, a2, a3


def _gla_kernel(v_ref, r_ref, q_ref, k_ref, al_ref, wa_ref, ba_ref, gn_ref, o_ref, st_ref, *, gb):
    C = GLA_CHUNK

    @pl.when(pl.program_id(1) == 0)
    def _():
        st_ref[...] = jnp.zeros_like(st_ref)

    TC = GLA_TC
    NC = TC // C
    row = lax.broadcasted_iota(I32, (TC, TC), 0)
    col = lax.broadcasted_iota(I32, (TC, TC), 1)
    tri = ((row // C) == (col // C)) & (row >= col)
    tri_bf = jnp.where(tri, 1.0, 0.0).astype(BF16)
    nt = (((1,), (1,)), ((), ()))
    tn = (((0,), (0,)), ((), ()))
    for bb in range(gb):
        z = jnp.dot(al_ref[bb], wa_ref[...], preferred_element_type=F32) + ba_ref[...]
        la = (jnp.minimum(z, 0.0) - jnp.log(1.0 + jnp.exp(-jnp.abs(z)))) * (1.0 / GLA_TAU)
        a1, a2, a3 = _split3(la)
        bcum = (jnp.dot(tri_bf, a1, preferred_element_type=F32)
                + jnp.dot(tri_bf, a2, preferred_element_type=F32)
                + jnp.dot(tri_bf, a3, preferred_element_type=F32))
        blast = [bcum[(c + 1) * C - 1:(c + 1) * C, :] for c in range(NC)]
        blast_rows = jnp.concatenate([jnp.broadcast_to(b, (C, b.shape[1])) for b in blast], axis=0)
        qf = q_ref[bb].astype(F32) * (GLA_DK ** -0.5)
        kf = k_ref[bb].astype(F32)
        q_in = (qf * jnp.exp(bcum)).astype(BF16)
        k_in = (kf * jnp.exp(-bcum)).astype(BF16)
        k_end = (kf * jnp.exp(blast_rows - bcum)).astype(BF16)
        dec = [jnp.exp(b) for b in blast]
        for h in range(GLA_HEADS):
            ks = slice(h * GLA_DK, (h + 1) * GLA_DK)
            vs = slice(h * GLA_DV, (h + 1) * GLA_DV)
            vh = v_ref[bb, :, vs]
            att = lax.dot_general(q_in[:, ks], k_in[:, ks], nt, preferred_element_type=F32)
            att = jnp.where(tri, att, 0.0)
            o = jnp.dot(att.astype(BF16), vh, preferred_element_type=F32)
            st = st_ref[bb, h]
            inter = []
            for c in range(NC):
                rs = slice(c * C, (c + 1) * C)
                inter.append(lax.dot_general(q_in[rs, ks], st.astype(BF16), nt, preferred_element_type=F32))
                st = st * dec[c][:, ks] + lax.dot_general(vh[rs, :], k_end[rs, ks], tn, preferred_element_type=F32)
            st_ref[bb, h] = st
            o = o + jnp.concatenate(inter, axis=0)
            ms = jnp.mean(o * o, axis=-1, keepdims=True)
            y = o * lax.rsqrt(ms + RMS_EPS) * gn_ref[...]
            rg = r_ref[bb, :, vs].astype(F32)
            y = y * (rg * jax.nn.sigmoid(rg))
            o_ref[bb, :, vs] = y.astype(o_ref.dtype)


def _gla(p2, wa, ba, gn, bsz, seq):
    tc = GLA_TC
    gb = GLA_GB if bsz % GLA_GB == 0 else 1
    p2v = p2.reshape(bsz, seq, W2)
    in_specs = [
        pl.BlockSpec((gb, tc, 1024), lambda b, t: (b, t, P2_BV // 1024)),
        pl.BlockSpec((gb, tc, 1024), lambda b, t: (b, t, P2_BR // 1024)),
        pl.BlockSpec((gb, tc, 512), lambda b, t: (b, t, P2_BQ // 512)),
        pl.BlockSpec((gb, tc, 512), lambda b, t: (b, t, P2_BK // 512)),
        pl.BlockSpec((gb, tc, LANES), lambda b, t: (b, t, P2_BAL // LANES)),
        pl.BlockSpec((LANES, 512), lambda b, t: (0, 0)),
        pl.BlockSpec((1, 512), lambda b, t: (0, 0)),
        pl.BlockSpec((1, GLA_DV), lambda b, t: (0, 0)),
    ]
    out = pl.pallas_call(
        functools.partial(_gla_kernel, gb=gb),
        grid=(bsz // gb, seq // tc),
        in_specs=in_specs,
        out_specs=pl.BlockSpec((gb, tc, 1024), lambda b, t: (b, t, 0)),
        out_shape=jax.ShapeDtypeStruct((bsz, seq, 1024), BF16),
        scratch_shapes=[pltpu.VMEM((gb, GLA_HEADS, GLA_DV, GLA_DK), F32)],
        compiler_params=_cparams(("parallel", "arbitrary")),
        name="gla",
    )(p2v, p2v, p2v, p2v, p2v, wa, ba, gn)
    return out.reshape(bsz * seq, 1024)


def _tree_sum(xs):
    xs = list(xs)
    while len(xs) > 1:
        xs = [xs[i] + xs[i + 1] for i in range(0, len(xs) - 1, 2)] + ([xs[-1]] if len(xs) % 2 else [])
    return xs[0]


def _dsa_kernel(q_ref, iq_ref, iw_ref, k_ref, ik_ref, vt_ref, bias_ref, wot_ref, o_ref,
                keys_ref, planes_ref, qaug_ref, iqall_ref, acc_ref, st0_ref, st1_ref, m_ref, cm0_ref, cm1_ref,
                *, topk):
    TQ, CH = DSA_TQ, DSA_CH
    CC = 2 * CH
    qi = pl.program_id(1)
    t0 = qi * TQ
    nch = qi // (CH // TQ) + 1
    npair = (nch + 1) // 2
    nt = (((1,), (1,)), ((), ()))

    eye = (lax.broadcasted_iota(I32, (TQ, TQ), 0) == lax.broadcasted_iota(I32, (TQ, TQ), 1))
    eye = jnp.where(eye, 1.0, 0.0).astype(BF16)
    q_t = q_ref[...].astype(F32).T.astype(BF16)
    iq_t = iq_ref[...].astype(F32).T.astype(BF16)
    for h in range(DSA_HEADS):
        cs = slice(h * TQ, (h + 1) * TQ)
        qaug_ref[0:TQ, cs] = eye
        qaug_ref[TQ:TQ + HEAD_DIM, cs] = q_t[h * HEAD_DIM:(h + 1) * HEAD_DIM, :]
        qaug_ref[TQ + HEAD_DIM:, cs] = jnp.zeros((TQ - HEAD_DIM, TQ), BF16)
    for h in range(IDX_HEADS):
        cs = slice(h * TQ, (h + 1) * TQ)
        iqall_ref[0:IDX_DIM, cs] = iq_t[h * IDX_DIM:(h + 1) * IDX_DIM, :]
        iqall_ref[IDX_DIM:, cs] = jnp.zeros((LANES - IDX_DIM, TQ), BF16)
    idx_scale = (IDX_HEADS ** -0.5) * (IDX_DIM ** -0.5)
    wt = (iw_ref[...].astype(F32) * idx_scale).T

    qpos = t0 + lax.broadcasted_iota(I32, (1, TQ), 1)
    RB = 128
    krow1 = lax.broadcasted_iota(I32, (RB, TQ), 0)
    krow = lax.broadcasted_iota(I32, (CH, TQ), 0)
    krow2 = lax.broadcasted_iota(I32, (CC, TQ), 0)

    def score_block(s0):
        s0 = pl.multiple_of(s0, RB)
        ikc = ik_ref[pl.ds(s0, RB), :]
        s = None
        for hp in range(IDX_HEADS // 2):
            x = jnp.dot(ikc, iqall_ref[:, hp * 2 * TQ:(hp + 1) * 2 * TQ],
                        preferred_element_type=F32)
            t = (jnp.maximum(x[:, :TQ], 0.0) * wt[2 * hp:2 * hp + 1, :]
                 + jnp.maximum(x[:, TQ:], 0.0) * wt[2 * hp + 1:2 * hp + 2, :])
            s = t if s is None else s + t
        s = jnp.where(s == 0.0, 0.0, s)
        bits = pltpu.bitcast(s, I32)
        key = bits ^ ((bits >> 31) & 0x7FFFFFFF)
        key = jnp.where(s0 + krow1 <= qpos, key, INT_MIN)
        keys_ref[pl.ds(s0, RB), :] = key

    def score_body(cp, carry):
        for sub in range(CC // RB):
            score_block(cp * CC + sub * RB)
        return carry

    lax.fori_loop(0, nch // 2, score_body, 0)

    @pl.when(nch % 2 == 1)
    def _():
        for sub in range(CH // RB):
            score_block((nch - 1) * CH + sub * RB)
        keys_ref[pl.ds(pl.multiple_of(nch * CH, CH), CH), :] = jnp.full((CH, TQ), INT_MIN, I32)

    kvec = jnp.minimum(topk, qpos + 1)
    NPL = 32

    def planes_body(c, carry):
        s0 = pl.multiple_of(c * CH, CH)
        a = [keys_ref[pl.ds(s0 + 8 * j, 8), :] ^ INT_MIN for j in range(NPL)]
        j, m = 16, 0x0000FFFF
        while j:
            sh = jnp.full((8, TQ), j, I32)
            k = 0
            while k < NPL:
                t = (a[k] ^ lax.shift_right_logical(a[k + j], sh)) & m
                a[k] = a[k] ^ t
                a[k + j] = a[k + j] ^ lax.shift_left(t, sh)
                k = (k + j + 1) & ~j
            j >>= 1
            m = (m ^ (m << j)) & 0xFFFFFFFF
        for p in range(NPL):
            planes_ref[c, p] = a[p]
        return carry

    def empty_body(c, carry):
        for p in range(NPL):
            planes_ref[c, p] = jnp.zeros((8, TQ), I32)
        return carry

    NCK = keys_ref.shape[0] // CH
    short = nch <= NCK // 2
    lax.fori_loop(0, nch, planes_body, 0)
    lax.fori_loop(nch, jnp.where(short, NCK // 2, NCK), empty_body, 0)

    def radix_select(nc):
        def run():
            def plane_body(p, carry):
                thr_u, n_gt, alive = carry
                hit = [alive[c] & planes_ref[c, p] for c in range(nc)]
                ones = jnp.sum(_tree_sum([lax.population_count(h) for h in hit]), axis=0, keepdims=True)
                take = (n_gt + ones) >= kvec
                thr_u = jnp.where(take, thr_u | (jnp.int32(1) << (31 - p)), thr_u)
                n_gt = jnp.where(take, n_gt, n_gt + ones)
                alive = tuple(jnp.where(take, hit[c], alive[c] ^ hit[c]) for c in range(nc))
                return thr_u, n_gt, alive

            zero = jnp.zeros((1, TQ), I32)
            alive0 = tuple(jnp.full((8, TQ), -1, I32) for _ in range(nc))
            thr_u, n_gt, alive = lax.fori_loop(0, NPL, plane_body, (zero, zero, alive0))
            n_eq = jnp.sum(_tree_sum([lax.population_count(a) for a in alive]), axis=0, keepdims=True)
            return thr_u, n_gt, n_eq
        return run

    thr_u, n_gt, n_eq = lax.cond(short, radix_select(NCK // 2), radix_select(NCK))
    thr = thr_u ^ INT_MIN

    def count(pred_fn):
        def body(c, acc):
            s0 = pl.multiple_of(c * CC, CC)
            kk = keys_ref[pl.ds(s0, CC), :]
            hit = jnp.where(pred_fn(kk, s0), 1, 0).astype(I32)
            return acc + jnp.sum(hit.reshape(CC // 8, 8, TQ), axis=0)
        acc = lax.fori_loop(0, npair, body, jnp.zeros((8, TQ), I32))
        return jnp.sum(acc, axis=0, keepdims=True)

    excess = n_gt + n_eq - kvec
    has_excess = jnp.max(excess) > 0

    @pl.when(has_excess)
    def _():
        need = kvec - n_gt

        def tie_lt(cut):
            return count(lambda kk, s0: (kk == thr) & (s0 + krow2 < cut))

        def cut_body(b, cut):
            cand = cut | (jnp.int32(1) << (12 - b))
            return jnp.where(tie_lt(cand) <= need, cand, cut)

        cut = lax.fori_loop(0, 13, cut_body, jnp.zeros((1, TQ), I32))

        def drop_body(c, carry):
            s0 = pl.multiple_of(c * CH, CH)
            kk = keys_ref[pl.ds(s0, CH), :]
            keys_ref[pl.ds(s0, CH), :] = jnp.where((kk == thr) & (s0 + krow >= cut), INT_MIN, kk)
            return carry

        lax.fori_loop(0, nch, drop_body, 0)

    acc_ref[...] = jnp.zeros_like(acc_ref)

    NP = DSA_HEADS // 2
    PW = 2 * TQ

    AB = DSA_AB
    NB = CH // AB
    last_chunk = keys_ref.shape[0] // CH - 1

    def logits(c, st_ref, cm_ref):
        s0 = pl.multiple_of(c * CH, CH)
        zoff = pl.multiple_of(jnp.clip(s0 - t0 + DSA_Z0, 0, DSA_Z0), 8)
        kaug = []
        for rb in range(NB):
            rs = pl.ds(s0 + rb * AB, AB)
            pen = jnp.where(keys_ref[rs, :] >= thr, 0.0, NEG_BIG).astype(BF16)
            kaug.append(jnp.concatenate([pen, k_ref[rs, :]], axis=1))
        for hp in range(NP):
            ps = slice(hp * PW, (hp + 1) * PW)
            cm = None
            for rb in range(NB):
                st = jnp.dot(kaug[rb], qaug_ref[:, ps], preferred_element_type=F32)
                st = st + bias_ref[pl.ds(zoff + rb * AB, AB), ps]
                st_ref[rb * AB:(rb + 1) * AB, ps] = st
                tm = jnp.max(st.reshape(AB // 8, 8, PW), axis=0)
                cm = tm if cm is None else jnp.maximum(cm, tm)
            cm_ref[0:1, ps] = jnp.max(cm, axis=0, keepdims=True)

    def accumulate(c, st_ref, cm_ref):
        vtc = vt_ref[c]
        for hp in range(NP):
            ps = slice(hp * PW, (hp + 1) * PW)
            m_old = m_ref[0:1, ps]
            m_new = jnp.maximum(m_old, cm_ref[0:1, ps])
            m_ref[0:1, ps] = m_new
            alpha = jnp.exp2(m_old - m_new)
            p = jnp.concatenate(
                [jnp.exp2(st_ref[rb * AB:(rb + 1) * AB, ps] - m_new).astype(BF16) for rb in range(NB)], axis=0)
            acc_ref[:, ps] = acc_ref[:, ps] * alpha + jnp.dot(vtc, p, preferred_element_type=F32)

    def att_body(cp, carry):
        c0 = 2 * cp
        logits(c0 + 1, st1_ref, cm1_ref)
        accumulate(c0, st0_ref, cm0_ref)
        logits(c0 + 2, st0_ref, cm0_ref)
        accumulate(c0 + 1, st1_ref, cm1_ref)
        return carry

    m_ref[...] = jnp.full(m_ref.shape, NEG_BIG, F32)
    logits(0, st0_ref, cm0_ref)
    nfull = (nch - 1) // 2
    lax.fori_loop(0, nfull, att_body, 0)
    c_last = 2 * nfull

    @pl.when(nch % 2 == 1)
    def _():
        accumulate(c_last, st0_ref, cm0_ref)

    @pl.when(nch % 2 == 0)
    def _():
        logits(c_last + 1, st1_ref, cm1_ref)
        accumulate(c_last, st0_ref, cm0_ref)
        accumulate(c_last + 1, st1_ref, cm1_ref)

    ot = jnp.concatenate(
        [acc_ref[0:HEAD_DIM, h * TQ:(h + 1) * TQ] / acc_ref[HEAD_DIM:HEAD_DIM + 1, h * TQ:(h + 1) * TQ]
         for h in range(DSA_HEADS)], axis=0)
    o_ref[...] = jnp.dot(ot.T.astype(BF16), wot_ref[...], preferred_element_type=F32)


def _dsa(p1, p2, vt, bias_t, wot, bsz, seq):
    tq = DSA_TQ
    p1v = p1.reshape(bsz, seq, W1)
    p2v = p2.reshape(bsz, seq, W2)
    nq = DSA_HEADS * tq
    in_specs = [
        pl.BlockSpec((None, tq, 768), lambda b, i: (b, i, P1_CQ // 768)),
        pl.BlockSpec((None, tq, 512), lambda b, i: (b, i, P2_IQ // 512)),
        pl.BlockSpec((None, tq, LANES), lambda b, i: (b, i, P2_IW // LANES)),
        pl.BlockSpec((None, seq, LANES), lambda b, i: (b, 0, P1_CK // LANES)),
        pl.BlockSpec((None, seq, LANES), lambda b, i: (b, 0, P2_IK // LANES)),
        pl.BlockSpec((None, seq // DSA_CH, DSA_VR, DSA_CH), lambda b, i: (b, 0, 0, 0)),
        pl.BlockSpec((DSA_Z, nq), lambda b, i: (0, 0)),
        pl.BlockSpec((DSA_HEADS * HEAD_DIM, D_MODEL), lambda b, i: (0, 0)),
    ]
    out = pl.pallas_call(
        functools.partial(_dsa_kernel, topk=min(IDX_TOPK, seq // 4)),
        grid=(bsz, seq // tq),
        in_specs=in_specs,
        out_specs=pl.BlockSpec((None, tq, D_MODEL), lambda b, i: (b, i, 0)),
        out_shape=jax.ShapeDtypeStruct((bsz, seq, D_MODEL), F32),
        scratch_shapes=[
            pltpu.VMEM((seq, tq), I32),
            pltpu.VMEM((seq // DSA_CH, 32, 8, tq), I32),
            pltpu.VMEM((2 * tq, nq), BF16),
            pltpu.VMEM((LANES, IDX_HEADS * tq), BF16),
            pltpu.VMEM((DSA_VR, nq), F32),
            pltpu.VMEM((DSA_CH, nq), F32),
            pltpu.VMEM((DSA_CH, nq), F32),
            pltpu.VMEM((8, nq), F32),
            pltpu.VMEM((8, nq), F32),
            pltpu.VMEM((8, nq), F32),
        ],
        compiler_params=_cparams(("parallel", "arbitrary")),
        name="dsa",
    )(p1v, p2v, p2v, p1v, p2v, vt, bias_t, wot)
    return out.reshape(bsz * seq, D_MODEL)


MERGE_TM = 512


def _merge_kernel(h_ref, oa0_ref, oa1_ref, oa2_ref, sa0_ref, sa1_ref, sa2_ref, ob_ref, yc_ref, g_ref,
                  woa_ref, wob_ref, wo_ref, o_ref, on1_ref, on2_ref, sn1_ref, sn2_ref):
    H = DSWA_HPG
    tm = h_ref.shape[0]
    gw = H * HEAD_DIM
    for (_, r), src_o, src_s, dst_o, dst_s in zip(DSWA_PATTERNS[1:], (oa1_ref, oa2_ref), (sa1_ref, sa2_ref),
                                                  (on1_ref, on2_ref), (sn1_ref, sn2_ref)):
        for c in range(r):
            for k in range(gw // LANES):
                dst_o[k, pl.ds(c, tm // r, stride=r), :] = src_o[:, c * gw + k * LANES:c * gw + (k + 1) * LANES]
            dst_s[pl.ds(c, tm // r, stride=r), :] = src_s[:, c * LANES:(c + 1) * LANES]
    oa_vals = (oa0_ref[...],) + tuple(
        jnp.concatenate([ref[k] for k in range(gw // LANES)], axis=1) for ref in (on1_ref, on2_ref))
    sts = [sa0_ref[...], sn1_ref[...], sn2_ref[...]]
    lane = lax.broadcasted_iota(I32, sts[0].shape, 1)
    mmax = jnp.maximum(jnp.maximum(sts[0], sts[1]), sts[2])
    wts = [pltpu.roll(s, LANES - H, 1) * jnp.exp(s - mmax) for s in sts]
    tot = wts[0] + wts[1] + wts[2]
    hrow = lax.broadcasted_iota(I32, (LANES, H * HEAD_DIM), 0)
    hcol = lax.broadcasted_iota(I32, (LANES, H * HEAD_DIM), 1) // HEAD_DIM
    expand = jnp.where(hrow == hcol, 1.0, 0.0).astype(BF16)
    oa = None
    for g in range(3):
        w = jnp.where(lane < H, wts[g] / tot, 0.0)
        hi = w.astype(BF16)
        lo = (w - hi.astype(F32)).astype(BF16)
        wfull = (jnp.dot(hi, expand, preferred_element_type=F32)
                 + jnp.dot(lo, expand, preferred_element_type=F32))
        term = wfull * oa_vals[g]
        oa = term if oa is None else oa + term
    oa = oa.astype(BF16)
    y_a = jnp.dot(oa, woa_ref[...], preferred_element_type=F32)
    y_b = jnp.dot(ob_ref[...], wob_ref[...], preferred_element_type=F32)
    y_c = yc_ref[...]
    D = D_MODEL
    mix = (g_ref[:, 0:D].astype(F32) * y_a + g_ref[:, D:2 * D].astype(F32) * y_b
           + g_ref[:, 2 * D:3 * D].astype(F32) * y_c)
    o_ref[...] = h_ref[...] + jnp.dot(mix.astype(BF16), wo_ref[...], preferred_element_type=F32)


def _merge(h, oas, sas, ob, yc, gates, woa, wob, wo):
    m = h.shape[0]
    tm = MERGE_TM
    gw = DSWA_HPG * HEAD_DIM
    row = lambda w: pl.BlockSpec((tm, w), lambda i: (i, 0))
    rmrow = lambda w, r: pl.BlockSpec((tm // r, r * w), lambda i: (i, 0))
    full = lambda a: pl.BlockSpec(a.shape, lambda i: (0, 0), pipeline_mode=pl.Buffered(1))
    rs = [r for _, r in DSWA_PATTERNS]
    in_specs = ([row(D_MODEL)] + [rmrow(gw, r) for r in rs] + [rmrow(LANES, r) for r in rs]
                + [row(1024), row(D_MODEL), row(3 * D_MODEL)] + [full(woa), full(wob), full(wo)])
    return pl.pallas_call(
        _merge_kernel,
        grid=(m // tm,),
        in_specs=in_specs,
        out_specs=row(D_MODEL),
        out_shape=jax.ShapeDtypeStruct((m, D_MODEL), F32),
        scratch_shapes=[pltpu.VMEM((gw // LANES, tm, LANES), F32), pltpu.VMEM((gw // LANES, tm, LANES), F32),
                        pltpu.VMEM((tm, LANES), F32), pltpu.VMEM((tm, LANES), F32)],
        compiler_params=_cparams(("parallel",)),
        name="merge",
    )(h, *oas, *sas, ob, yc, gates, woa, wob, wo)


FFN_TM = 512
FFN_TF = 256


def _rms(x, g):
    ms = jnp.mean(x * x, axis=-1, keepdims=True)
    return (x * lax.rsqrt(ms + RMS_EPS) * g).astype(BF16)


def _ffn_ple_kernel(h_ref, p_ref, gf_ref, wg_ref, wu_ref, wd_ref, gp_ref, wpg_ref, wpp_ref, o_ref, *, tf):
    x = h_ref[...]
    u = _rms(x, gf_ref[...])
    acc = None
    for j in range(wg_ref.shape[1] // tf):
        cs = slice(j * tf, (j + 1) * tf)
        a = jnp.dot(u, wg_ref[:, cs], preferred_element_type=F32)
        b = jnp.dot(u, wu_ref[:, cs], preferred_element_type=F32)
        t = (a * jax.nn.sigmoid(a) * b).astype(BF16)
        d = jnp.dot(t, wd_ref[cs, :], preferred_element_type=F32)
        acc = d if acc is None else acc + d
    h2 = x + acc
    e = _rms(h2, gp_ref[...])
    gate = jax.nn.sigmoid(jnp.dot(e, wpg_ref[...], preferred_element_type=F32))
    proj = jnp.dot(p_ref[...].astype(BF16), wpp_ref[...], preferred_element_type=F32)
    o_ref[...] = h2 + gate * proj


def _ffn_ple(h, p, gf, wg, wu, wd, gp, wpg, wpp):
    m, d = h.shape
    tm, tf = FFN_TM, FFN_TF
    const = lambda a: pl.BlockSpec(a.shape, lambda i: (0, 0), pipeline_mode=pl.Buffered(1))
    return pl.pallas_call(
        functools.partial(_ffn_ple_kernel, tf=tf),
        grid=(m // tm,),
        in_specs=[pl.BlockSpec((tm, d), lambda i: (i, 0)), pl.BlockSpec((tm, PLE_DIM), lambda i: (i, 0)),
                  const(gf), const(wg), const(wu), const(wd), const(gp), const(wpg), const(wpp)],
        out_specs=pl.BlockSpec((tm, d), lambda i: (i, 0)),
        out_shape=jax.ShapeDtypeStruct((m, d), F32),
        compiler_params=_cparams(("parallel",)),
        name="ffn_ple",
    )(h, p, gf, wg, wu, wd, gp, wpg, wpp)


def _rel_bucket(dist):
    max_exact = REL_BUCKETS // 2
    d = jnp.maximum(dist, 0)
    df = jnp.maximum(d, 1).astype(F32)
    large = max_exact + (jnp.log(df / max_exact) / math.log(REL_MAX_DIST / max_exact)
                         * (REL_BUCKETS - max_exact)).astype(I32)
    large = jnp.minimum(large, REL_BUCKETS - 1)
    return jnp.where(d < max_exact, d, large)


def _toeplitz(rev, n_rows, n_cols):
    nh = rev.shape[0]
    lw = rev.shape[1] + 1
    w = jnp.pad(rev, ((0, 0), (0, 1)))
    s = jnp.broadcast_to(w[:, None, :], (nh, n_rows, lw)).reshape(nh, n_rows * lw)
    s = s[:, :n_rows * (lw - 1)].reshape(nh, n_rows, lw - 1)
    return s[:, :, n_rows - 1:n_rows - 1 + n_cols]


def _bias_tables(rel_bias):
    blk = DSWA_BLOCK
    dswa = []
    for g, (_, r) in enumerate(DSWA_PATTERNS):
        delta = np.arange(3 * blk - 1)[::-1] - (blk - 1)
        rev = rel_bias[_rel_bucket(jnp.asarray(delta * r, I32))][:, g * DSWA_HPG:(g + 1) * DSWA_HPG]
        dswa.append(_toeplitz(rev.T, blk, 2 * blk))
    tq = DSA_TQ
    dist = np.arange(DSA_Z + tq - 1)[::-1] - (DSA_Z - 1) + DSA_Z0
    rev = rel_bias[_rel_bucket(jnp.asarray(dist, I32))][:, DSWA_HEADS:] * math.log2(math.e)
    bias_t = jnp.transpose(_toeplitz(rev.T, tq, DSA_Z), (2, 0, 1)).reshape(DSA_Z, DSA_HEADS * tq)
    return dswa, bias_t


def _pad_cols(w, width):
    return jnp.pad(w, ((0, 0), (0, width - w.shape[1])))


def _layer_params(w_in, qn_a, kn_a, qn_c, kn_c, w_alpha2, b_alpha):
    offs = np.cumsum((0,) + IN_WIDTHS)
    parts = [w_in[:, offs[i]:offs[i + 1]] for i in range(len(IN_WIDTHS))]
    a_q, a_k, a_v, b_q, b_k, b_v, b_r, b_al, c_q, c_k, c_v, i_q, i_k, i_w = parts
    w1 = _pad_cols(jnp.concatenate([a_q, a_k, c_q, c_k], axis=1), W1).astype(BF16)
    w2 = _pad_cols(jnp.concatenate(
        [b_v, b_r, b_q, b_k, i_q, a_v, _pad_cols(b_al, LANES), _pad_cols(c_v, LANES),
         _pad_cols(i_k, LANES), _pad_cols(i_w, LANES)], axis=1), W2).astype(BF16)
    scale = HEAD_DIM ** -0.5
    gain1 = jnp.concatenate([jnp.tile(qn_a, DSWA_HEADS) * scale, jnp.tile(kn_a, DSWA_HEADS),
                             jnp.tile(qn_c, DSA_HEADS) * (scale * math.log2(math.e)), kn_c])
    gain1 = jnp.pad(gain1, (0, W1 - gain1.shape[0])).reshape(1, W1)
    wa = jnp.pad(w_alpha2, ((0, LANES - GLA_RANK), (0, 0))).astype(BF16)
    return w1, w2, gain1, wa, b_alpha.reshape(1, -1)


def kernel(x, p, rel_bias, norm_mix, w_in, qn_a, kn_a, qn_c, kn_c, w_alpha2, b_alpha, gla_norm, w_out_a, w_out_b, w_out_c, w_gate, b_gate, w_o, norm_ffn, w_ffn_gate, w_ffn_up, w_ffn_down, norm_ple, w_ple_gate, w_ple_proj):
    bsz, seq, d = x.shape
    depth = p.shape[0]
    m = bsz * seq
    dswa_bias, bias_t = _bias_tables(rel_bias)
    h = x.reshape(m, d)
    zeros_w2 = jnp.zeros((1, W2), F32)
    for i in range(depth):
        w1, w2, gain1, wa, ba = _layer_params(w_in[i], qn_a[i], kn_a[i], qn_c[i], kn_c[i], w_alpha2[i], b_alpha[i])
        gmix = norm_mix[i].reshape(1, d)
        gw = DSWA_HPG * HEAD_DIM
        dil = [(g, r) for g, (_, r) in enumerate(DSWA_PATTERNS) if r > 1]
        p1, *qk_rm = _proj(h, gmix, w1, gain1, "qk",
                           regroup=[(off + g * gw, r) for off in (P1_AQ, P1_AK) for g, r in dil])
        p2, *v_rm = _proj(h, gmix, w2, zeros_w2, "plain", regroup=[(P2_AV + g * gw, r) for g, r in dil])
        gates = _proj(h, gmix, w_gate[i].astype(BF16), b_gate[i].reshape(1, -1), "gate")
        rm = {g: (qk_rm[n], qk_rm[len(dil) + n], v_rm[n]) for n, (g, r) in enumerate(dil)}
        oas, sas = [], []
        for g, (_, r) in enumerate(DSWA_PATTERNS):
            o, st = _dswa(p1, p2, rm.get(g), dswa_bias[g], g, r, bsz, seq)
            oas.append(o)
            sas.append(st)
        ob = _gla(p2, wa, ba, gla_norm[i].reshape(1, -1), bsz, seq)
        cv = p2[:, P2_CV:P2_CV + HEAD_DIM].reshape(bsz, seq // DSA_CH, DSA_CH, HEAD_DIM)
        ones_pad = jnp.zeros((bsz, seq // DSA_CH, DSA_VR - HEAD_DIM, DSA_CH), BF16).at[:, :, 0, :].set(1.0)
        vt = jnp.concatenate([jnp.transpose(cv, (0, 1, 3, 2)), ones_pad], axis=2)
        yc = _dsa(p1, p2, vt, bias_t, w_out_c[i].astype(BF16), bsz, seq)
        h = _merge(h, oas, sas, ob, yc, gates, w_out_a[i].astype(BF16), w_out_b[i].astype(BF16),
                   w_o[i].astype(BF16))
        h = _ffn_ple(h, p[i].reshape(m, PLE_DIM), norm_ffn[i].reshape(1, d), w_ffn_gate[i].astype(BF16),
                     w_ffn_up[i].astype(BF16), w_ffn_down[i].astype(BF16), norm_ple[i].reshape(1, d),
                     w_ple_gate[i].astype(BF16), w_ple_proj[i].astype(BF16))
    return h.reshape(bsz, seq, d)
```

```python
import functools
import math

import numpy as np
import jax
import jax.numpy as jnp
from jax import lax
from jax.experimental import pallas as pl
from jax.experimental.pallas import tpu as pltpu

F32 = jnp.float32
BF16 = jnp.bfloat16
I32 = jnp.int32
I16 = jnp.int16

D_MODEL = 1024
HEAD_DIM = 64
RMS_EPS = 1e-6
DSWA_PATTERNS = ((128, 1), (512, 4), (2048, 16))
DSWA_HPG = 4
DSWA_HEADS = 12
DSWA_BLOCK = 128
GLA_HEADS = 4
GLA_DK = 128
GLA_DV = 256
GLA_RANK = 16
GLA_TAU = 16.0
GLA_CHUNK = 64
DSA_HEADS = 12
IDX_HEADS = 8
IDX_DIM = 64
IDX_TOPK = 256
REL_BUCKETS = 32
REL_MAX_DIST = 2048
D_FF = 2816
PLE_DIM = 256
IN_WIDTHS = (768, 768, 768, 512, 512, 1024, 1024, 16, 768, 64, 64, 512, 64, 8)

LANES = 128
MXU_N = 256
VMEM_LIMIT = 56 * 1024 * 1024

W1 = 2560
P1_AQ, P1_AK, P1_CQ, P1_CK = 0, 768, 1536, 2304
W2 = 5120
P2_BV, P2_BR, P2_BQ, P2_BK, P2_IQ, P2_AV = 0, 1024, 2048, 2560, 3072, 3584
P2_BAL, P2_CV, P2_IK, P2_IW = 4352, 4480, 4608, 4736

PROJ_TM = 512
PROJ_TN = 512

DSWA_QB = 4
DSA_TQ = 128
DSA_CH = 256
DSA_AB = 256
DSA_VR = 80
DSA_BIAS_CONST_FROM = 1512
DSA_Z0 = 1792
DSA_Z = DSA_Z0 + DSA_CH
INT_MIN = -(2 ** 31)
NEG_BIG = -1e30


def _cparams(sem, flags=None):
    return pltpu.CompilerParams(dimension_semantics=sem, vmem_limit_bytes=VMEM_LIMIT, flags=flags)


def _proj_kernel(h_ref, g_ref, w_ref, e_ref, *rest, mode, tn, regroup):
    if mode == "qk":
        bd_ref, rest = rest[0], rest[1:]
    o_ref, rg_refs = rest[0], rest[1:1 + len(regroup)]
    scr_ref = rest[-1] if regroup else None
    tm = h_ref.shape[0]
    gw = DSWA_HPG * HEAD_DIM
    x = h_ref[...]
    ms = jnp.mean(x * x, axis=-1, keepdims=True)
    u = (x * lax.rsqrt(ms + RMS_EPS) * g_ref[...]).astype(BF16)
    for j in range(w_ref.shape[1] // tn):
        cs = slice(j * tn, (j + 1) * tn)
        acc = jnp.dot(u, w_ref[:, cs], preferred_element_type=F32)
        if mode == "plain":
            out = acc
        elif mode == "gate":
            out = jax.nn.sigmoid(acc + e_ref[:, cs])
        else:
            sq = acc * acc
            hi = sq.astype(BF16)
            lo = (sq - hi.astype(F32)).astype(BF16)
            bw = bd_ref.shape[0]
            ss = jnp.concatenate(
                [jnp.dot(hi[:, k:k + bw], bd_ref[...], preferred_element_type=F32)
                 + jnp.dot(lo[:, k:k + bw], bd_ref[...], preferred_element_type=F32) for k in range(0, tn, bw)],
                axis=1)
            out = acc * lax.rsqrt(ss * (1.0 / HEAD_DIM) + RMS_EPS) * e_ref[:, cs]
        o_ref[:, cs] = out.astype(o_ref.dtype)
        for (col, r), rg_ref in zip(regroup, rg_refs):
            if col // tn == j:
                for k in range(gw // LANES):
                    scr_ref[k] = out[:, col % tn + k * LANES:col % tn + (k + 1) * LANES]
                for c in range(r):
                    for k in range(gw // LANES):
                        rg_ref[:, c * gw + k * LANES:c * gw + (k + 1) * LANES] = (
                            scr_ref[k, pl.ds(c, tm // r, stride=r), :].astype(rg_ref.dtype))


def _proj(h, g, w, e, mode, regroup=()):
    m, d = h.shape
    n = w.shape[1]
    tm, tn = PROJ_TM, PROJ_TN
    gw = DSWA_HPG * HEAD_DIM
    const = lambda shape: pl.BlockSpec(shape, lambda i: (0, 0), pipeline_mode=pl.Buffered(1))
    in_specs = [pl.BlockSpec((tm, d), lambda i: (i, 0)), const((1, d)), const((d, n)), const((1, n))]
    args = [h, g, w, e]
    if mode == "qk":
        r = np.arange(MXU_N) // HEAD_DIM
        bd = jnp.asarray((r[:, None] == r[None, :]).astype(np.float32), dtype=BF16)
        in_specs.append(const((MXU_N, MXU_N)))
        args.append(bd)
    out_specs = [pl.BlockSpec((tm, n), lambda i: (i, 0))]
    out_shape = [jax.ShapeDtypeStruct((m, n), BF16)]
    for _, r in regroup:
        out_specs.append(pl.BlockSpec((tm // r, r * gw), lambda i: (i, 0)))
        out_shape.append(jax.ShapeDtypeStruct((m // r, r * gw), BF16))
    outs = pl.pallas_call(
        functools.partial(_proj_kernel, mode=mode, tn=tn, regroup=tuple(regroup)),
        grid=(m // tm,),
        in_specs=in_specs,
        out_specs=out_specs,
        out_shape=out_shape,
        scratch_shapes=[pltpu.VMEM((gw // LANES, tm, LANES), F32)] if regroup else [],
        compiler_params=_cparams(("parallel",)),
        name="proj_" + mode,
    )(*args)
    return outs if regroup else outs[0]


def _dswa_kernel(q_ref, kp_ref, kc_ref, vp_ref, vc_ref, bias_ref, o_ref, st_ref, *, qb, cb):
    i = pl.program_id(2)
    blk = DSWA_BLOCK
    row = lax.broadcasted_iota(I32, (blk, 2 * blk), 0)
    col = lax.broadcasted_iota(I32, (blk, 2 * blk), 1)
    cur_ok = (col >= blk) & ((col - blk) <= row)
    valid_inner = ((col < blk) & (col >= row)) | cur_ok
    prev_off = jnp.where(i > 0, 0, 4 * blk)
    valid_first = ((col < blk) & (col >= row + prev_off)) | cur_ok
    lane = lax.broadcasted_iota(I32, (blk, LANES), 1)
    gw = DSWA_HPG * HEAD_DIM
    for cc, j in [(cc, j) for cc in range(cb) for j in range(qb)]:
        rs = slice(j * blk, (j + 1) * blk)
        ps = slice((j - 1) * blk, j * blk)
        gs = slice(cc * gw, (cc + 1) * gw)
        valid = valid_first if j == 0 else valid_inner
        q = q_ref[rs, gs]
        k = jnp.concatenate([kp_ref[:, gs] if j == 0 else kc_ref[ps, gs], kc_ref[rs, gs]], axis=0)
        v = jnp.concatenate([vp_ref[:, gs] if j == 0 else vc_ref[ps, gs], vc_ref[rs, gs]], axis=0)
        stats = jnp.zeros((blk, LANES), F32)
        for hp in range(DSWA_HPG // 2):
            ts = slice(hp * LANES, (hp + 1) * LANES)
            q2, k2, v2 = q[:, ts], k[:, ts], v[:, ts]
            halves = []
            for hh in range(2):
                h = 2 * hp + hh
                qm = jnp.where((lane // HEAD_DIM) == hh, q2, jnp.zeros_like(q2))
                s = lax.dot_general(qm, k2, (((1,), (1,)), ((), ())), preferred_element_type=F32)
                s = jnp.where(valid, s + bias_ref[h], -jnp.inf)
                m = jnp.max(s, axis=-1, keepdims=True)
                p = jnp.exp(s - m)
                l = jnp.sum(p, axis=-1, keepdims=True)
                halves.append(jnp.dot(p.astype(BF16), v2, preferred_element_type=F32) / l)
                stats = jnp.where(lane == h, m, stats)
                stats = jnp.where(lane == DSWA_HPG + h, l, stats)
            o_ref[rs, cc * gw + hp * LANES:cc * gw + (hp + 1) * LANES] = jnp.where(
                lane < HEAD_DIM, halves[0], halves[1])
        st_ref[rs, cc * LANES:(cc + 1) * LANES] = stats


def _dswa(p1, p2, rm, bias, g, r, bsz, seq):
    blk = DSWA_BLOCK
    L = seq // r
    nblk = L // blk
    gw = DSWA_HPG * HEAD_DIM
    if r == 1:
        p1v = p1.reshape(bsz, L, W1)
        p2v = p2.reshape(bsz, L, W2)
        qv, kv, vv = p1v, p1v, p2v
        q_col = lambda c: (P1_AQ // gw) + g
        k_col = lambda c: (P1_AK // gw) + g
        v_col = lambda c: (P2_AV // gw) + g
    else:
        qv, kv, vv = (a.reshape(bsz, L, r * gw) for a in rm)
        q_col = k_col = v_col = lambda c: c
    qb = min(DSWA_QB, nblk)
    cb = min(max(DSWA_QB // qb, 1), r)
    prev = lambda i: jnp.maximum(i * qb - 1, 0)
    in_specs = [
        pl.BlockSpec((None, qb * blk, cb * gw), lambda b, c, i: (b, i, q_col(c))),
        pl.BlockSpec((None, blk, cb * gw), lambda b, c, i: (b, prev(i), k_col(c))),
        pl.BlockSpec((None, qb * blk, cb * gw), lambda b, c, i: (b, i, k_col(c))),
        pl.BlockSpec((None, blk, cb * gw), lambda b, c, i: (b, prev(i), v_col(c))),
        pl.BlockSpec((None, qb * blk, cb * gw), lambda b, c, i: (b, i, v_col(c))),
        pl.BlockSpec((DSWA_HPG, blk, 2 * blk), lambda b, c, i: (0, 0, 0)),
    ]
    out_specs = [
        pl.BlockSpec((None, qb * blk, cb * gw), lambda b, c, i: (b, i, c)),
        pl.BlockSpec((None, qb * blk, cb * LANES), lambda b, c, i: (b, i, c)),
    ]
    o, st = pl.pallas_call(
        functools.partial(_dswa_kernel, qb=qb, cb=cb),
        grid=(bsz, r // cb, nblk // qb),
        in_specs=in_specs,
        out_specs=out_specs,
        out_shape=[jax.ShapeDtypeStruct((bsz, L, r * gw), F32),
                   jax.ShapeDtypeStruct((bsz, L, r * LANES), F32)],
        compiler_params=_cparams(("parallel", "parallel", "arbitrary")),
        name="dswa_g%d" % g,
    )(qv, kv, kv, vv, vv, bias)
    return o.reshape(bsz * L, r * gw), st.reshape(bsz * L, r * LANES)


GLA_TC = 256
GLA_GB = 2


def _split3(x):
    a1 = x.astype(BF16)
    r1 = x - a1.astype(F32)
    a2 = r1.astype(BF16)
    a3 = (r1 - a2.astype(F32)).astype(BF16)
    return a1, a2, a3


def _gla_kernel(v_ref, r_ref, q_ref, k_ref, al_ref, wa_ref, ba_ref, gn_ref, o_ref, st_ref, *, gb):
    C = GLA_CHUNK

    @pl.when(pl.program_id(1) == 0)
    def _():
        st_ref[...] = jnp.zeros_like(st_ref)

    TC = GLA_TC
    NC = TC // C
    row = lax.broadcasted_iota(I32, (TC, TC), 0)
    col = lax.broadcasted_iota(I32, (TC, TC), 1)
    tri = ((row // C) == (col // C)) & (row >= col)
    tri_bf = jnp.where(tri, 1.0, 0.0).astype(BF16)
    nt = (((1,), (1,)), ((), ()))
    tn = (((0,), (0,)), ((), ()))
    for bb in range(gb):
        z = jnp.dot(al_ref[bb], wa_ref[...], preferred_element_type=F32) + ba_ref[...]
        la = (jnp.minimum(z, 0.0) - jnp.log(1.0 + jnp.exp(-jnp.abs(z)))) * (1.0 / GLA_TAU)
        a1, a2, a3 = _split3(la)
        bcum = (jnp.dot(tri_bf, a1, preferred_element_type=F32)
                + jnp.dot(tri_bf, a2, preferred_element_type=F32)
                + jnp.dot(tri_bf, a3, preferred_element_type=F32))
        blast = [bcum[(c + 1) * C - 1:(c + 1) * C, :] for c in range(NC)]
        dec = [jnp.exp(b) for b in blast]
        dec_rows = jnp.concatenate([jnp.broadcast_to(d, (C, d.shape[1])) for d in dec], axis=0)
        qf = q_ref[bb].astype(F32) * (GLA_DK ** -0.5)
        k_dec = k_ref[bb].astype(F32) * jnp.exp(-bcum)
        q_in = (qf * jnp.exp(bcum)).astype(BF16)
        k_in = k_dec.astype(BF16)
        k_end = (k_dec * dec_rows).astype(BF16)
        for h in range(GLA_HEADS):
            ks = slice(h * GLA_DK, (h + 1) * GLA_DK)
            vs = slice(h * GLA_DV, (h + 1) * GLA_DV)
            vh = v_ref[bb, :, vs]
            att = lax.dot_general(q_in[:, ks], k_in[:, ks], nt, preferred_element_type=F32)
            att = jnp.where(tri, att, 0.0)
            o = jnp.dot(att.astype(BF16), vh, preferred_element_type=F32)
            st = st_ref[bb, h]
            inter = []
            for c in range(NC):
                rs = slice(c * C, (c + 1) * C)
                inter.append(lax.dot_general(q_in[rs, ks], st.astype(BF16), nt, preferred_element_type=F32))
                st = st * dec[c][:, ks] + lax.dot_general(vh[rs, :], k_end[rs, ks], tn, preferred_element_type=F32)
            st_ref[bb, h] = st
            o = o + jnp.concatenate(inter, axis=0)
            ms = jnp.mean(o * o, axis=-1, keepdims=True)
            y = o * lax.rsqrt(ms + RMS_EPS) * gn_ref[...]
            rg = r_ref[bb, :, vs].astype(F32)
            y = y * (rg * jax.nn.sigmoid(rg))
            o_ref[bb, :, vs] = y.astype(o_ref.dtype)


def _gla(p2, wa, ba, gn, bsz, seq):
    tc = GLA_TC
    gb = GLA_GB if bsz % GLA_GB == 0 else 1
    p2v = p2.reshape(bsz, seq, W2)
    in_specs = [
        pl.BlockSpec((gb, tc, 1024), lambda b, t: (b, t, P2_BV // 1024)),
        pl.BlockSpec((gb, tc, 1024), lambda b, t: (b, t, P2_BR // 1024)),
        pl.BlockSpec((gb, tc, 512), lambda b, t: (b, t, P2_BQ // 512)),
        pl.BlockSpec((gb, tc, 512), lambda b, t: (b, t, P2_BK // 512)),
        pl.BlockSpec((gb, tc, LANES), lambda b, t: (b, t, P2_BAL // LANES)),
        pl.BlockSpec((LANES, 512), lambda b, t: (0, 0)),
        pl.BlockSpec((1, 512), lambda b, t: (0, 0)),
        pl.BlockSpec((1, GLA_DV), lambda b, t: (0, 0)),
    ]
    out = pl.pallas_call(
        functools.partial(_gla_kernel, gb=gb),
        grid=(bsz // gb, seq // tc),
        in_specs=in_specs,
        out_specs=pl.BlockSpec((gb, tc, 1024), lambda b, t: (b, t, 0)),
        out_shape=jax.ShapeDtypeStruct((bsz, seq, 1024), BF16),
        scratch_shapes=[pltpu.VMEM((gb, GLA_HEADS, GLA_DV, GLA_DK), F32)],
        compiler_params=_cparams(("parallel", "arbitrary")),
        name="gla",
    )(p2v, p2v, p2v, p2v, p2v, wa, ba, gn)
    return out.reshape(bsz * seq, 1024)


def _tree_sum(xs):
    xs = list(xs)
    while len(xs) > 1:
        xs = [xs[i] + xs[i + 1] for i in range(0, len(xs) - 1, 2)] + ([xs[-1]] if len(xs) % 2 else [])
    return xs[0]


def _dsa_kernel(q_ref, iq_ref, iw_ref, k_ref, ik_ref, vt_ref, bias_ref, wot_ref, o_ref,
                keys_ref, planes_ref, qaug_ref, iqall_ref, acc_ref, st0_ref, st1_ref, m_ref, cm0_ref, cm1_ref,
                *, topk):
    TQ, CH = DSA_TQ, DSA_CH
    CC = 2 * CH
    qi = pl.program_id(1)
    t0 = qi * TQ
    nch = qi // (CH // TQ) + 1
    npair = (nch + 1) // 2
    nt = (((1,), (1,)), ((), ()))

    eye = (lax.broadcasted_iota(I32, (TQ, TQ), 0) == lax.broadcasted_iota(I32, (TQ, TQ), 1))
    eye = jnp.where(eye, 1.0, 0.0).astype(BF16)
    q_t = q_ref[...].astype(F32).T.astype(BF16)
    iq_t = iq_ref[...].astype(F32).T.astype(BF16)
    for h in range(DSA_HEADS):
        cs = slice(h * TQ, (h + 1) * TQ)
        qaug_ref[0:TQ, cs] = eye
        qaug_ref[TQ:TQ + HEAD_DIM, cs] = q_t[h * HEAD_DIM:(h + 1) * HEAD_DIM, :]
        qaug_ref[TQ + HEAD_DIM:, cs] = jnp.zeros((TQ - HEAD_DIM, TQ), BF16)
    for h in range(IDX_HEADS):
        cs = slice(h * TQ, (h + 1) * TQ)
        iqall_ref[0:IDX_DIM, cs] = iq_t[h * IDX_DIM:(h + 1) * IDX_DIM, :]
        iqall_ref[IDX_DIM:, cs] = jnp.zeros((LANES - IDX_DIM, TQ), BF16)
    idx_scale = (IDX_HEADS ** -0.5) * (IDX_DIM ** -0.5)
    wt = (iw_ref[...].astype(F32) * idx_scale).T

    qpos = t0 + lax.broadcasted_iota(I32, (1, TQ), 1)
    RB = 128
    krow1 = lax.broadcasted_iota(I32, (RB, TQ), 0)
    krow = lax.broadcasted_iota(I32, (CH, TQ), 0)
    krow2 = lax.broadcasted_iota(I32, (CC, TQ), 0)

    def score_block(s0):
        s0 = pl.multiple_of(s0, RB)
        ikc = ik_ref[pl.ds(s0, RB), :]
        s = None
        for hp in range(IDX_HEADS // 2):
            x = jnp.dot(ikc, iqall_ref[:, hp * 2 * TQ:(hp + 1) * 2 * TQ],
                        preferred_element_type=F32)
            t = (jnp.maximum(x[:, :TQ], 0.0) * wt[2 * hp:2 * hp + 1, :]
                 + jnp.maximum(x[:, TQ:], 0.0) * wt[2 * hp + 1:2 * hp + 2, :])
            s = t if s is None else s + t
        s = jnp.where(s == 0.0, 0.0, s)
        bits = pltpu.bitcast(s, I32)
        key = bits ^ ((bits >> 31) & 0x7FFFFFFF)
        key = jnp.where(s0 + krow1 <= qpos, key, INT_MIN)
        keys_ref[pl.ds(s0, RB), :] = key

    def score_body(cp, carry):
        for sub in range(CC // RB):
            score_block(cp * CC + sub * RB)
        return carry

    lax.fori_loop(0, nch // 2, score_body, 0)

    @pl.when(nch % 2 == 1)
    def _():
        for sub in range(CH // RB):
            score_block((nch - 1) * CH + sub * RB)
        keys_ref[pl.ds(pl.multiple_of(nch * CH, CH), CH), :] = jnp.full((CH, TQ), INT_MIN, I32)

    kvec = jnp.minimum(topk, qpos + 1)
    NPL = 32

    def planes_body(c, carry):
        s0 = pl.multiple_of(c * CH, CH)
        a = [keys_ref[pl.ds(s0 + 8 * j, 8), :] ^ INT_MIN for j in range(NPL)]
        j, m = 16, 0x0000FFFF
        while j:
            sh = jnp.full((8, TQ), j, I32)
            k = 0
            while k < NPL:
                t = (a[k] ^ lax.shift_right_logical(a[k + j], sh)) & m
                a[k] = a[k] ^ t
                a[k + j] = a[k + j] ^ lax.shift_left(t, sh)
                k = (k + j + 1) & ~j
            j >>= 1
            m = (m ^ (m << j)) & 0xFFFFFFFF
        for p in range(NPL):
            planes_ref[c, p] = a[p]
        return carry

    def empty_body(c, carry):
        for p in range(NPL):
            planes_ref[c, p] = jnp.zeros((8, TQ), I32)
        return carry

    NCK = keys_ref.shape[0] // CH
    short = nch <= NCK // 2
    lax.fori_loop(0, nch, planes_body, 0)
    lax.fori_loop(nch, jnp.where(short, NCK // 2, NCK), empty_body, 0)

    def radix_select(nc):
        def run():
            def plane_body(p, carry):
                thr_u, n_gt, alive = carry
                hit = [alive[c] & planes_ref[c, p] for c in range(nc)]
                ones = jnp.sum(_tree_sum([lax.population_count(h) for h in hit]), axis=0, keepdims=True)
                take = (n_gt + ones) >= kvec
                thr_u = jnp.where(take, thr_u | (jnp.int32(1) << (31 - p)), thr_u)
                n_gt = jnp.where(take, n_gt, n_gt + ones)
                alive = tuple(jnp.where(take, hit[c], alive[c] ^ hit[c]) for c in range(nc))
                return thr_u, n_gt, alive

            zero = jnp.zeros((1, TQ), I32)
            alive0 = tuple(jnp.full((8, TQ), -1, I32) for _ in range(nc))
            thr_u, n_gt, alive = lax.fori_loop(0, NPL, plane_body, (zero, zero, alive0))
            n_eq = jnp.sum(_tree_sum([lax.population_count(a) for a in alive]), axis=0, keepdims=True)
            return thr_u, n_gt, n_eq
        return run

    thr_u, n_gt, n_eq = lax.cond(short, radix_select(NCK // 2), radix_select(NCK))
    thr = thr_u ^ INT_MIN

    def count(pred_fn):
        def body(c, acc):
            s0 = pl.multiple_of(c * CC, CC)
            kk = keys_ref[pl.ds(s0, CC), :]
            hit = jnp.where(pred_fn(kk, s0), 1, 0).astype(I32)
            return acc + jnp.sum(hit.reshape(CC // 8, 8, TQ), axis=0)
        acc = lax.fori_loop(0, npair, body, jnp.zeros((8, TQ), I32))
        return jnp.sum(acc, axis=0, keepdims=True)

    excess = n_gt + n_eq - kvec
    has_excess = jnp.max(excess) > 0

    @pl.when(has_excess)
    def _():
        need = kvec - n_gt

        def tie_lt(cut):
            return count(lambda kk, s0: (kk == thr) & (s0 + krow2 < cut))

        def cut_body(b, cut):
            cand = cut | (jnp.int32(1) << (12 - b))
            return jnp.where(tie_lt(cand) <= need, cand, cut)

        cut = lax.fori_loop(0, 13, cut_body, jnp.zeros((1, TQ), I32))

        def drop_body(c, carry):
            s0 = pl.multiple_of(c * CH, CH)
            kk = keys_ref[pl.ds(s0, CH), :]
            keys_ref[pl.ds(s0, CH), :] = jnp.where((kk == thr) & (s0 + krow >= cut), INT_MIN, kk)
            return carry

        lax.fori_loop(0, nch, drop_body, 0)

    acc_ref[...] = jnp.zeros_like(acc_ref)

    NP = DSA_HEADS // 2
    PW = 2 * TQ

    AB = DSA_AB
    NB = CH // AB
    last_chunk = keys_ref.shape[0] // CH - 1

    def logits(c, st_ref, cm_ref):
        s0 = pl.multiple_of(c * CH, CH)
        zoff = pl.multiple_of(jnp.clip(s0 - t0 + DSA_Z0, 0, DSA_Z0), 8)
        kaug = []
        for rb in range(NB):
            rs = pl.ds(s0 + rb * AB, AB)
            pen = jnp.where(keys_ref[rs, :] >= thr, 0.0, NEG_BIG).astype(BF16)
            kaug.append(jnp.concatenate([pen, k_ref[rs, :]], axis=1))
        for hp in range(NP):
            ps = slice(hp * PW, (hp + 1) * PW)
            cm = None
            for rb in range(NB):
                st = jnp.dot(kaug[rb], qaug_ref[:, ps], preferred_element_type=F32)
                st = st + bias_ref[pl.ds(zoff + rb * AB, AB), ps]
                st_ref[rb * AB:(rb + 1) * AB, ps] = st
                tm = jnp.max(st.reshape(AB // 8, 8, PW), axis=0)
                cm = tm if cm is None else jnp.maximum(cm, tm)
            cm_ref[0:1, ps] = jnp.max(cm, axis=0, keepdims=True)

    def accumulate(c, st_ref, cm_ref):
        vtc = vt_ref[c]
        for hp in range(NP):
            ps = slice(hp * PW, (hp + 1) * PW)
            m_old = m_ref[0:1, ps]
            m_new = jnp.maximum(m_old, cm_ref[0:1, ps])
            m_ref[0:1, ps] = m_new
            alpha = jnp.exp2(m_old - m_new)
            p = jnp.concatenate(
                [jnp.exp2(st_ref[rb * AB:(rb + 1) * AB, ps] - m_new).astype(BF16) for rb in range(NB)], axis=0)
            acc_ref[:, ps] = acc_ref[:, ps] * alpha + jnp.dot(vtc, p, preferred_element_type=F32)

    def att_body(cp, carry):
        c0 = 2 * cp
        logits(c0 + 1, st1_ref, cm1_ref)
        accumulate(c0, st0_ref, cm0_ref)
        logits(c0 + 2, st0_ref, cm0_ref)
        accumulate(c0 + 1, st1_ref, cm1_ref)
        return carry

    m_ref[...] = jnp.full(m_ref.shape, NEG_BIG, F32)
    logits(0, st0_ref, cm0_ref)
    nfull = (nch - 1) // 2
    lax.fori_loop(0, nfull, att_body, 0)
    c_last = 2 * nfull

    @pl.when(nch % 2 == 1)
    def _():
        accumulate(c_last, st0_ref, cm0_ref)

    @pl.when(nch % 2 == 0)
    def _():
        logits(c_last + 1, st1_ref, cm1_ref)
        accumulate(c_last, st0_ref, cm0_ref)
        accumulate(c_last + 1, st1_ref, cm1_ref)

    ot = jnp.concatenate(
        [acc_ref[0:HEAD_DIM, h * TQ:(h + 1) * TQ] / acc_ref[HEAD_DIM:HEAD_DIM + 1, h * TQ:(h + 1) * TQ]
         for h in range(DSA_HEADS)], axis=0)
    o_ref[...] = jnp.dot(ot.T.astype(BF16), wot_ref[...], preferred_element_type=F32)


def _dsa(p1, p2, vt, bias_t, wot, bsz, seq):
    tq = DSA_TQ
    p1v = p1.reshape(bsz, seq, W1)
    p2v = p2.reshape(bsz, seq, W2)
    nq = DSA_HEADS * tq
    in_specs = [
        pl.BlockSpec((None, tq, 768), lambda b, i: (b, i, P1_CQ // 768)),
        pl.BlockSpec((None, tq, 512), lambda b, i: (b, i, P2_IQ // 512)),
        pl.BlockSpec((None, tq, LANES), lambda b, i: (b, i, P2_IW // LANES)),
        pl.BlockSpec((None, seq, LANES), lambda b, i: (b, 0, P1_CK // LANES)),
        pl.BlockSpec((None, seq, LANES), lambda b, i: (b, 0, P2_IK // LANES)),
        pl.BlockSpec((None, seq // DSA_CH, DSA_VR, DSA_CH), lambda b, i: (b, 0, 0, 0)),
        pl.BlockSpec((DSA_Z, nq), lambda b, i: (0, 0)),
        pl.BlockSpec((DSA_HEADS * HEAD_DIM, D_MODEL), lambda b, i: (0, 0)),
    ]
    out = pl.pallas_call(
        functools.partial(_dsa_kernel, topk=min(IDX_TOPK, seq // 4)),
        grid=(bsz, seq // tq),
        in_specs=in_specs,
        out_specs=pl.BlockSpec((None, tq, D_MODEL), lambda b, i: (b, i, 0)),
        out_shape=jax.ShapeDtypeStruct((bsz, seq, D_MODEL), F32),
        scratch_shapes=[
            pltpu.VMEM((seq, tq), I32),
            pltpu.VMEM((seq // DSA_CH, 32, 8, tq), I32),
            pltpu.VMEM((2 * tq, nq), BF16),
            pltpu.VMEM((LANES, IDX_HEADS * tq), BF16),
            pltpu.VMEM((DSA_VR, nq), F32),
            pltpu.VMEM((DSA_CH, nq), F32),
            pltpu.VMEM((DSA_CH, nq), F32),
            pltpu.VMEM((8, nq), F32),
            pltpu.VMEM((8, nq), F32),
            pltpu.VMEM((8, nq), F32),
        ],
        compiler_params=_cparams(("parallel", "arbitrary")),
        name="dsa",
    )(p1v, p2v, p2v, p1v, p2v, vt, bias_t, wot)
    return out.reshape(bsz * seq, D_MODEL)


MERGE_TM = 512


def _merge_kernel(h_ref, oa0_ref, oa1_ref, oa2_ref, sa0_ref, sa1_ref, sa2_ref, ob_ref, yc_ref, g_ref,
                  woa_ref, wob_ref, wo_ref, o_ref, on1_ref, on2_ref, sn1_ref, sn2_ref):
    H = DSWA_HPG
    tm = h_ref.shape[0]
    gw = H * HEAD_DIM
    for (_, r), src_o, src_s, dst_o, dst_s in zip(DSWA_PATTERNS[1:], (oa1_ref, oa2_ref), (sa1_ref, sa2_ref),
                                                  (on1_ref, on2_ref), (sn1_ref, sn2_ref)):
        for c in range(r):
            for k in range(gw // LANES):
                dst_o[k, pl.ds(c, tm // r, stride=r), :] = src_o[:, c * gw + k * LANES:c * gw + (k + 1) * LANES]
            dst_s[pl.ds(c, tm // r, stride=r), :] = src_s[:, c * LANES:(c + 1) * LANES]
    oa_vals = (oa0_ref[...],) + tuple(
        jnp.concatenate([ref[k] for k in range(gw // LANES)], axis=1) for ref in (on1_ref, on2_ref))
    sts = [sa0_ref[...], sn1_ref[...], sn2_ref[...]]
    lane = lax.broadcasted_iota(I32, sts[0].shape, 1)
    mmax = jnp.maximum(jnp.maximum(sts[0], sts[1]), sts[2])
    wts = [pltpu.roll(s, LANES - H, 1) * jnp.exp(s - mmax) for s in sts]
    tot = wts[0] + wts[1] + wts[2]
    hrow = lax.broadcasted_iota(I32, (LANES, H * HEAD_DIM), 0)
    hcol = lax.broadcasted_iota(I32, (LANES, H * HEAD_DIM), 1) // HEAD_DIM
    expand = jnp.where(hrow == hcol, 1.0, 0.0).astype(BF16)
    oa = None
    for g in range(3):
        w = jnp.where(lane < H, wts[g] / tot, 0.0)
        hi = w.astype(BF16)
        lo = (w - hi.astype(F32)).astype(BF16)
        wfull = (jnp.dot(hi, expand, preferred_element_type=F32)
                 + jnp.dot(lo, expand, preferred_element_type=F32))
        term = wfull * oa_vals[g]
        oa = term if oa is None else oa + term
    oa = oa.astype(BF16)
    y_a = jnp.dot(oa, woa_ref[...], preferred_element_type=F32)
    y_b = jnp.dot(ob_ref[...], wob_ref[...], preferred_element_type=F32)
    y_c = yc_ref[...]
    D = D_MODEL
    mix = (g_ref[:, 0:D].astype(F32) * y_a + g_ref[:, D:2 * D].astype(F32) * y_b
           + g_ref[:, 2 * D:3 * D].astype(F32) * y_c)
    o_ref[...] = h_ref[...] + jnp.dot(mix.astype(BF16), wo_ref[...], preferred_element_type=F32)


def _merge(h, oas, sas, ob, yc, gates, woa, wob, wo):
    m = h.shape[0]
    tm = MERGE_TM
    gw = DSWA_HPG * HEAD_DIM
    row = lambda w: pl.BlockSpec((tm, w), lambda i: (i, 0))
    rmrow = lambda w, r: pl.BlockSpec((tm // r, r * w), lambda i: (i, 0))
    full = lambda a: pl.BlockSpec(a.shape, lambda i: (0, 0), pipeline_mode=pl.Buffered(1))
    rs = [r for _, r in DSWA_PATTERNS]
    in_specs = ([row(D_MODEL)] + [rmrow(gw, r) for r in rs] + [rmrow(LANES, r) for r in rs]
                + [row(1024), row(D_MODEL), row(3 * D_MODEL)] + [full(woa), full(wob), full(wo)])
    return pl.pallas_call(
        _merge_kernel,
        grid=(m // tm,),
        in_specs=in_specs,
        out_specs=row(D_MODEL),
        out_shape=jax.ShapeDtypeStruct((m, D_MODEL), F32),
        scratch_shapes=[pltpu.VMEM((gw // LANES, tm, LANES), F32), pltpu.VMEM((gw // LANES, tm, LANES), F32),
                        pltpu.VMEM((tm, LANES), F32), pltpu.VMEM((tm, LANES), F32)],
        compiler_params=_cparams(("parallel",)),
        name="merge",
    )(h, *oas, *sas, ob, yc, gates, woa, wob, wo)


FFN_TM = 512
FFN_TF = 256


def _rms(x, g):
    ms = jnp.mean(x * x, axis=-1, keepdims=True)
    return (x * lax.rsqrt(ms + RMS_EPS) * g).astype(BF16)


def _ffn_ple_kernel(h_ref, p_ref, gf_ref, wg_ref, wu_ref, wd_ref, gp_ref, wpg_ref, wpp_ref, o_ref, *, tf):
    x = h_ref[...]
    u = _rms(x, gf_ref[...])
    acc = None
    for j in range(wg_ref.shape[1] // tf):
        cs = slice(j * tf, (j + 1) * tf)
        a = jnp.dot(u, wg_ref[:, cs], preferred_element_type=F32)
        b = jnp.dot(u, wu_ref[:, cs], preferred_element_type=F32)
        t = (a * jax.nn.sigmoid(a) * b).astype(BF16)
        d = jnp.dot(t, wd_ref[cs, :], preferred_element_type=F32)
        acc = d if acc is None else acc + d
    h2 = x + acc
    e = _rms(h2, gp_ref[...])
    gate = jax.nn.sigmoid(jnp.dot(e, wpg_ref[...], preferred_element_type=F32))
    proj = jnp.dot(p_ref[...].astype(BF16), wpp_ref[...], preferred_element_type=F32)
    o_ref[...] = h2 + gate * proj


def _ffn_ple(h, p, gf, wg, wu, wd, gp, wpg, wpp):
    m, d = h.shape
    tm, tf = FFN_TM, FFN_TF
    const = lambda a: pl.BlockSpec(a.shape, lambda i: (0, 0), pipeline_mode=pl.Buffered(1))
    return pl.pallas_call(
        functools.partial(_ffn_ple_kernel, tf=tf),
        grid=(m // tm,),
        in_specs=[pl.BlockSpec((tm, d), lambda i: (i, 0)), pl.BlockSpec((tm, PLE_DIM), lambda i: (i, 0)),
                  const(gf), const(wg), const(wu), const(wd), const(gp), const(wpg), const(wpp)],
        out_specs=pl.BlockSpec((tm, d), lambda i: (i, 0)),
        out_shape=jax.ShapeDtypeStruct((m, d), F32),
        compiler_params=_cparams(("parallel",)),
        name="ffn_ple",
    )(h, p, gf, wg, wu, wd, gp, wpg, wpp)


def _rel_bucket(dist):
    max_exact = REL_BUCKETS // 2
    d = jnp.maximum(dist, 0)
    df = jnp.maximum(d, 1).astype(F32)
    large = max_exact + (jnp.log(df / max_exact) / math.log(REL_MAX_DIST / max_exact)
                         * (REL_BUCKETS - max_exact)).astype(I32)
    large = jnp.minimum(large, REL_BUCKETS - 1)
    return jnp.where(d < max_exact, d, large)


def _toeplitz(rev, n_rows, n_cols):
    nh = rev.shape[0]
    lw = rev.shape[1] + 1
    w = jnp.pad(rev, ((0, 0), (0, 1)))
    s = jnp.broadcast_to(w[:, None, :], (nh, n_rows, lw)).reshape(nh, n_rows * lw)
    s = s[:, :n_rows * (lw - 1)].reshape(nh, n_rows, lw - 1)
    return s[:, :, n_rows - 1:n_rows - 1 + n_cols]


def _bias_tables(rel_bias):
    blk = DSWA_BLOCK
    dswa = []
    for g, (_, r) in enumerate(DSWA_PATTERNS):
        delta = np.arange(3 * blk - 1)[::-1] - (blk - 1)
        rev = rel_bias[_rel_bucket(jnp.asarray(delta * r, I32))][:, g * DSWA_HPG:(g + 1) * DSWA_HPG]
        dswa.append(_toeplitz(rev.T, blk, 2 * blk))
    tq = DSA_TQ
    dist = np.arange(DSA_Z + tq - 1)[::-1] - (DSA_Z - 1) + DSA_Z0
    rev = rel_bias[_rel_bucket(jnp.asarray(dist, I32))][:, DSWA_HEADS:] * math.log2(math.e)
    bias_t = jnp.transpose(_toeplitz(rev.T, tq, DSA_Z), (2, 0, 1)).reshape(DSA_Z, DSA_HEADS * tq)
    return dswa, bias_t


def _pad_cols(w, width):
    return jnp.pad(w, ((0, 0), (0, width - w.shape[1])))


def _layer_params(w_in, qn_a, kn_a, qn_c, kn_c, w_alpha2, b_alpha):
    offs = np.cumsum((0,) + IN_WIDTHS)
    parts = [w_in[:, offs[i]:offs[i + 1]] for i in range(len(IN_WIDTHS))]
    a_q, a_k, a_v, b_q, b_k, b_v, b_r, b_al, c_q, c_k, c_v, i_q, i_k, i_w = parts
    w1 = _pad_cols(jnp.concatenate([a_q, a_k, c_q, c_k], axis=1), W1).astype(BF16)
    w2 = _pad_cols(jnp.concatenate(
        [b_v, b_r, b_q, b_k, i_q, a_v, _pad_cols(b_al, LANES), _pad_cols(c_v, LANES),
         _pad_cols(i_k, LANES), _pad_cols(i_w, LANES)], axis=1), W2).astype(BF16)
    scale = HEAD_DIM ** -0.5
    gain1 = jnp.concatenate([jnp.tile(qn_a, DSWA_HEADS) * scale, jnp.tile(kn_a, DSWA_HEADS),
                             jnp.tile(qn_c, DSA_HEADS) * (scale * math.log2(math.e)), kn_c])
    gain1 = jnp.pad(gain1, (0, W1 - gain1.shape[0])).reshape(1, W1)
    wa = jnp.pad(w_alpha2, ((0, LANES - GLA_RANK), (0, 0))).astype(BF16)
    return w1, w2, gain1, wa, b_alpha.reshape(1, -1)


def kernel(x, p, rel_bias, norm_mix, w_in, qn_a, kn_a, qn_c, kn_c, w_alpha2, b_alpha, gla_norm, w_out_a, w_out_b, w_out_c, w_gate, b_gate, w_o, norm_ffn, w_ffn_gate, w_ffn_up, w_ffn_down, norm_ple, w_ple_gate, w_ple_proj):
    bsz, seq, d = x.shape
    depth = p.shape[0]
    m = bsz * seq
    dswa_bias, bias_t = _bias_tables(rel_bias)
    h = x.reshape(m, d)
    zeros_w2 = jnp.zeros((1, W2), F32)
    for i in range(depth):
        w1, w2, gain1, wa, ba = _layer_params(w_in[i], qn_a[i], kn_a[i], qn_c[i], kn_c[i], w_alpha2[i], b_alpha[i])
        gmix = norm_mix[i].reshape(1, d)
        gw = DSWA_HPG * HEAD_DIM
        dil = [(g, r) for g, (_, r) in enumerate(DSWA_PATTERNS) if r > 1]
        p1, *qk_rm = _proj(h, gmix, w1, gain1, "qk",
                           regroup=[(off + g * gw, r) for off in (P1_AQ, P1_AK) for g, r in dil])
        p2, *v_rm = _proj(h, gmix, w2, zeros_w2, "plain", regroup=[(P2_AV + g * gw, r) for g, r in dil])
        gates = _proj(h, gmix, w_gate[i].astype(BF16), b_gate[i].reshape(1, -1), "gate")
        rm = {g: (qk_rm[n], qk_rm[len(dil) + n], v_rm[n]) for n, (g, r) in enumerate(dil)}
        oas, sas = [], []
        for g, (_, r) in enumerate(DSWA_PATTERNS):
            o, st = _dswa(p1, p2, rm.get(g), dswa_bias[g], g, r, bsz, seq)
            oas.append(o)
            sas.append(st)
        ob = _gla(p2, wa, ba, gla_norm[i].reshape(1, -1), bsz, seq)
        cv = p2[:, P2_CV:P2_CV + HEAD_DIM].reshape(bsz, seq // DSA_CH, DSA_CH, HEAD_DIM)
        ones_pad = jnp.zeros((bsz, seq // DSA_CH, DSA_VR - HEAD_DIM, DSA_CH), BF16).at[:, :, 0, :].set(1.0)
        vt = jnp.concatenate([jnp.transpose(cv, (0, 1, 3, 2)), ones_pad], axis=2)
        yc = _dsa(p1, p2, vt, bias_t, w_out_c[i].astype(BF16), bsz, seq)
        h = _merge(h, oas, sas, ob, yc, gates, w_out_a[i].astype(BF16), w_out_b[i].astype(BF16),
                   w_o[i].astype(BF16))
        h = _ffn_ple(h, p[i].reshape(m, PLE_DIM), norm_ffn[i].reshape(1, d), w_ffn_gate[i].astype(BF16),
                     w_ffn_up[i].astype(BF16), w_ffn_down[i].astype(BF16), norm_ple[i].reshape(1, d),
                     w_ple_gate[i].astype(BF16), w_ple_proj[i].astype(BF16))
    return h.reshape(bsz, seq, d)
```

```python
import functools
import math

import numpy as np
import jax
import jax.numpy as jnp
from jax import lax
from jax.experimental import pallas as pl
from jax.experimental.pallas import tpu as pltpu

F32 = jnp.float32
BF16 = jnp.bfloat16
I32 = jnp.int32
I16 = jnp.int16

D_MODEL = 1024
HEAD_DIM = 64
RMS_EPS = 1e-6
DSWA_PATTERNS = ((128, 1), (512, 4), (2048, 16))
DSWA_HPG = 4
DSWA_HEADS = 12
DSWA_BLOCK = 128
GLA_HEADS = 4
GLA_DK = 128
GLA_DV = 256
GLA_RANK = 16
GLA_TAU = 16.0
GLA_CHUNK = 64
DSA_HEADS = 12
IDX_HEADS = 8
IDX_DIM = 64
IDX_TOPK = 256
REL_BUCKETS = 32
REL_MAX_DIST = 2048
D_FF = 2816
PLE_DIM = 256
IN_WIDTHS = (768, 768, 768, 512, 512, 1024, 1024, 16, 768, 64, 64, 512, 64, 8)

LANES = 128
MXU_N = 256
VMEM_LIMIT = 56 * 1024 * 1024

W1 = 2560
P1_AQ, P1_AK, P1_CQ, P1_CK = 0, 768, 1536, 2304
W2 = 5120
P2_BV, P2_BR, P2_BQ, P2_BK, P2_IQ, P2_AV = 0, 1024, 2048, 2560, 3072, 3584
P2_BAL, P2_CV, P2_IK, P2_IW = 4352, 4480, 4608, 4736

PROJ_TM = 512
PROJ_TN = 512

DSWA_QB = 4
DSA_TQ = 128
DSA_CH = 256
DSA_AB = 256
DSA_VR = 80
DSA_BIAS_CONST_FROM = 1512
DSA_Z0 = 1792
DSA_Z = DSA_Z0 + DSA_CH
INT_MIN = -(2 ** 31)
NEG_BIG = -1e30


def _cparams(sem, flags=None):
    return pltpu.CompilerParams(dimension_semantics=sem, vmem_limit_bytes=VMEM_LIMIT, flags=flags)


def _proj_kernel(h_ref, g_ref, w_ref, e_ref, *rest, mode, tn, regroup):
    if mode == "qk":
        bd_ref, rest = rest[0], rest[1:]
    o_ref, rg_refs = rest[0], rest[1:1 + len(regroup)]
    scr_ref = rest[-1] if regroup else None
    tm = h_ref.shape[0]
    gw = DSWA_HPG * HEAD_DIM
    x = h_ref[...]
    ms = jnp.mean(x * x, axis=-1, keepdims=True)
    u = (x * lax.rsqrt(ms + RMS_EPS) * g_ref[...]).astype(BF16)
    for j in range(w_ref.shape[1] // tn):
        cs = slice(j * tn, (j + 1) * tn)
        acc = jnp.dot(u, w_ref[:, cs], preferred_element_type=F32)
        if mode == "plain":
            out = acc
        elif mode == "gate":
            out = jax.nn.sigmoid(acc + e_ref[:, cs])
        else:
            sq = acc * acc
            hi = sq.astype(BF16)
            lo = (sq - hi.astype(F32)).astype(BF16)
            bw = bd_ref.shape[0]
            ss = jnp.concatenate(
                [jnp.dot(hi[:, k:k + bw], bd_ref[...], preferred_element_type=F32)
                 + jnp.dot(lo[:, k:k + bw], bd_ref[...], preferred_element_type=F32) for k in range(0, tn, bw)],
                axis=1)
            out = acc * lax.rsqrt(ss * (1.0 / HEAD_DIM) + RMS_EPS) * e_ref[:, cs]
        o_ref[:, cs] = out.astype(o_ref.dtype)
        for (col, r), rg_ref in zip(regroup, rg_refs):
            if col // tn == j:
                for k in range(gw // LANES):
                    scr_ref[k] = out[:, col % tn + k * LANES:col % tn + (k + 1) * LANES]
                for c in range(r):
                    for k in range(gw // LANES):
                        rg_ref[:, c * gw + k * LANES:c * gw + (k + 1) * LANES] = (
                            scr_ref[k, pl.ds(c, tm // r, stride=r), :].astype(rg_ref.dtype))


def _proj(h, g, w, e, mode, regroup=()):
    m, d = h.shape
    n = w.shape[1]
    tm, tn = PROJ_TM, PROJ_TN
    gw = DSWA_HPG * HEAD_DIM
    const = lambda shape: pl.BlockSpec(shape, lambda i: (0, 0), pipeline_mode=pl.Buffered(1))
    in_specs = [pl.BlockSpec((tm, d), lambda i: (i, 0)), const((1, d)), const((d, n)), const((1, n))]
    args = [h, g, w, e]
    if mode == "qk":
        r = np.arange(MXU_N) // HEAD_DIM
        bd = jnp.asarray((r[:, None] == r[None, :]).astype(np.float32), dtype=BF16)
        in_specs.append(const((MXU_N, MXU_N)))
        args.append(bd)
    out_specs = [pl.BlockSpec((tm, n), lambda i: (i, 0))]
    out_shape = [jax.ShapeDtypeStruct((m, n), BF16)]
    for _, r in regroup:
        out_specs.append(pl.BlockSpec((tm // r, r * gw), lambda i: (i, 0)))
        out_shape.append(jax.ShapeDtypeStruct((m // r, r * gw), BF16))
    outs = pl.pallas_call(
        functools.partial(_proj_kernel, mode=mode, tn=tn, regroup=tuple(regroup)),
        grid=(m // tm,),
        in_specs=in_specs,
        out_specs=out_specs,
        out_shape=out_shape,
        scratch_shapes=[pltpu.VMEM((gw // LANES, tm, LANES), F32)] if regroup else [],
        compiler_params=_cparams(("parallel",)),
        name="proj_" + mode,
    )(*args)
    return outs if regroup else outs[0]


def _dswa_kernel(q_ref, kp_ref, kc_ref, vp_ref, vc_ref, bias_ref, o_ref, st_ref, *, qb, cb):
    i = pl.program_id(2)
    blk = DSWA_BLOCK
    row = lax.broadcasted_iota(I32, (blk, 2 * blk), 0)
    col = lax.broadcasted_iota(I32, (blk, 2 * blk), 1)
    cur_ok = (col >= blk) & ((col - blk) <= row)
    valid_inner = ((col < blk) & (col >= row)) | cur_ok
    prev_off = jnp.where(i > 0, 0, 4 * blk)
    valid_first = ((col < blk) & (col >= row + prev_off)) | cur_ok
    lane = lax.broadcasted_iota(I32, (blk, LANES), 1)
    gw = DSWA_HPG * HEAD_DIM
    for cc, j in [(cc, j) for cc in range(cb) for j in range(qb)]:
        rs = slice(j * blk, (j + 1) * blk)
        ps = slice((j - 1) * blk, j * blk)
        gs = slice(cc * gw, (cc + 1) * gw)
        valid = valid_first if j == 0 else valid_inner
        q = q_ref[rs, gs]
        k = jnp.concatenate([kp_ref[:, gs] if j == 0 else kc_ref[ps, gs], kc_ref[rs, gs]], axis=0)
        v = jnp.concatenate([vp_ref[:, gs] if j == 0 else vc_ref[ps, gs], vc_ref[rs, gs]], axis=0)
        stats = jnp.zeros((blk, LANES), F32)
        for hp in range(DSWA_HPG // 2):
            ts = slice(hp * LANES, (hp + 1) * LANES)
            q2, k2, v2 = q[:, ts], k[:, ts], v[:, ts]
            halves = []
            for hh in range(2):
                h = 2 * hp + hh
                qm = jnp.where((lane // HEAD_DIM) == hh, q2, jnp.zeros_like(q2))
                s = lax.dot_general(qm, k2, (((1,), (1,)), ((), ())), preferred_element_type=F32)
                s = jnp.where(valid, s + bias_ref[h], -jnp.inf)
                m = jnp.max(s, axis=-1, keepdims=True)
                p = jnp.exp(s - m)
                l = jnp.sum(p, axis=-1, keepdims=True)
                halves.append(jnp.dot(p.astype(BF16), v2, preferred_element_type=F32) / l)
                stats = jnp.where(lane == h, m, stats)
                stats = jnp.where(lane == DSWA_HPG + h, l, stats)
            o_ref[rs, cc * gw + hp * LANES:cc * gw + (hp + 1) * LANES] = jnp.where(
                lane < HEAD_DIM, halves[0], halves[1])
        st_ref[rs, cc * LANES:(cc + 1) * LANES] = stats


def _dswa(p1, p2, rm, bias, g, r, bsz, seq):
    blk = DSWA_BLOCK
    L = seq // r
    nblk = L // blk
    gw = DSWA_HPG * HEAD_DIM
    if r == 1:
        p1v = p1.reshape(bsz, L, W1)
        p2v = p2.reshape(bsz, L, W2)
        qv, kv, vv = p1v, p1v, p2v
        q_col = lambda c: (P1_AQ // gw) + g
        k_col = lambda c: (P1_AK // gw) + g
        v_col = lambda c: (P2_AV // gw) + g
    else:
        qv, kv, vv = (a.reshape(bsz, L, r * gw) for a in rm)
        q_col = k_col = v_col = lambda c: c
    qb = min(DSWA_QB, nblk)
    cb = min(max(DSWA_QB // qb, 1), r)
    prev = lambda i: jnp.maximum(i * qb - 1, 0)
    in_specs = [
        pl.BlockSpec((None, qb * blk, cb * gw), lambda b, c, i: (b, i, q_col(c))),
        pl.BlockSpec((None, blk, cb * gw), lambda b, c, i: (b, prev(i), k_col(c))),
        pl.BlockSpec((None, qb * blk, cb * gw), lambda b, c, i: (b, i, k_col(c))),
        pl.BlockSpec((None, blk, cb * gw), lambda b, c, i: (b, prev(i), v_col(c))),
        pl.BlockSpec((None, qb * blk, cb * gw), lambda b, c, i: (b, i, v_col(c))),
        pl.BlockSpec((DSWA_HPG, blk, 2 * blk), lambda b, c, i: (0, 0, 0)),
    ]
    out_specs = [
        pl.BlockSpec((None, qb * blk, cb * gw), lambda b, c, i: (b, i, c)),
        pl.BlockSpec((None, qb * blk, cb * LANES), lambda b, c, i: (b, i, c)),
    ]
    o, st = pl.pallas_call(
        functools.partial(_dswa_kernel, qb=qb, cb=cb),
        grid=(bsz, r // cb, nblk // qb),
        in_specs=in_specs,
        out_specs=out_specs,
        out_shape=[jax.ShapeDtypeStruct((bsz, L, r * gw), F32),
                   jax.ShapeDtypeStruct((bsz, L, r * LANES), F32)],
        compiler_params=_cparams(("parallel", "parallel", "arbitrary")),
        name="dswa_g%d" % g,
    )(qv, kv, kv, vv, vv, bias)
    return o.reshape(bsz * L, r * gw), st.reshape(bsz * L, r * LANES)


GLA_TC = 256
GLA_GB = 2


def _split3(x):
    a1 = x.astype(BF16)
    r1 = x - a1.astype(F32)
    a2 = r1.astype(BF16)
    a3 = (r1 - a2.astype(F32)).astype(BF16)
    return a1, a2, a3


def _gla_kernel(v_ref, r_ref, q_ref, k_ref, al_ref, wa_ref, ba_ref, gn_ref, o_ref, st_ref, *, gb):
    C = GLA_CHUNK

    @pl.when(pl.program_id(1) == 0)
    def _():
        st_ref[...] = jnp.zeros_like(st_ref)

    TC = GLA_TC
    NC = TC // C
    row = lax.broadcasted_iota(I32, (TC, TC), 0)
    col = lax.broadcasted_iota(I32, (TC, TC), 1)
    tri = ((row // C) == (col // C)) & (row >= col)
    tri_bf = jnp.where(tri, 1.0, 0.0).astype(BF16)
    nt = (((1,), (1,)), ((), ()))
    tn = (((0,), (0,)), ((), ()))
    for bb in range(gb):
        z = jnp.dot(al_ref[bb], wa_ref[...], preferred_element_type=F32) + ba_ref[...]
        la = (jnp.minimum(z, 0.0) - jnp.log(1.0 + jnp.exp(-jnp.abs(z)))) * (1.0 / GLA_TAU)
        a1, a2, a3 = _split3(la)
        bcum = (jnp.dot(tri_bf, a1, preferred_element_type=F32)
                + jnp.dot(tri_bf, a2, preferred_element_type=F32)
                + jnp.dot(tri_bf, a3, preferred_element_type=F32))
        blast = [bcum[(c + 1) * C - 1:(c + 1) * C, :] for c in range(NC)]
        dec = [jnp.exp(b) for b in blast]
        dec_rows = jnp.concatenate([jnp.broadcast_to(d, (C, d.shape[1])) for d in dec], axis=0)
        qf = q_ref[bb].astype(F32) * (GLA_DK ** -0.5)
        k_dec = k_ref[bb].astype(F32) * jnp.exp(-bcum)
        q_in = (qf * jnp.exp(bcum)).astype(BF16)
        k_in = k_dec.astype(BF16)
        k_end = (k_dec * dec_rows).astype(BF16)
        for h in range(GLA_HEADS):
            ks = slice(h * GLA_DK, (h + 1) * GLA_DK)
            vs = slice(h * GLA_DV, (h + 1) * GLA_DV)
            vh = v_ref[bb, :, vs]
            att = lax.dot_general(q_in[:, ks], k_in[:, ks], nt, preferred_element_type=F32)
            att = jnp.where(tri, att, 0.0)
            o = jnp.dot(att.astype(BF16), vh, preferred_element_type=F32)
            st = st_ref[bb, h]
            inter = []
            for c in range(NC):
                rs = slice(c * C, (c + 1) * C)
                inter.append(lax.dot_general(q_in[rs, ks], st.astype(BF16), nt, preferred_element_type=F32))
                st = st * dec[c][:, ks] + lax.dot_general(vh[rs, :], k_end[rs, ks], tn, preferred_element_type=F32)
            st_ref[bb, h] = st
            o = o + jnp.concatenate(inter, axis=0)
            ms = jnp.mean(o * o, axis=-1, keepdims=True)
            y = o * lax.rsqrt(ms + RMS_EPS) * gn_ref[...]
            rg = r_ref[bb, :, vs].astype(F32)
            y = y * (rg * jax.nn.sigmoid(rg))
            o_ref[bb, :, vs] = y.astype(o_ref.dtype)


def _gla(p2, wa, ba, gn, bsz, seq):
    tc = GLA_TC
    gb = GLA_GB if bsz % GLA_GB == 0 else 1
    p2v = p2.reshape(bsz, seq, W2)
    in_specs = [
        pl.BlockSpec((gb, tc, 1024), lambda b, t: (b, t, P2_BV // 1024)),
        pl.BlockSpec((gb, tc, 1024), lambda b, t: (b, t, P2_BR // 1024)),
        pl.BlockSpec((gb, tc, 512), lambda b, t: (b, t, P2_BQ // 512)),
        pl.BlockSpec((gb, tc, 512), lambda b, t: (b, t, P2_BK // 512)),
        pl.BlockSpec((gb, tc, LANES), lambda b, t: (b, t, P2_BAL // LANES)),
        pl.BlockSpec((LANES, 512), lambda b, t: (0, 0)),
        pl.BlockSpec((1, 512), lambda b, t: (0, 0)),
        pl.BlockSpec((1, GLA_DV), lambda b, t: (0, 0)),
    ]
    out = pl.pallas_call(
        functools.partial(_gla_kernel, gb=gb),
        grid=(bsz // gb, seq // tc),
        in_specs=in_specs,
        out_specs=pl.BlockSpec((gb, tc, 1024), lambda b, t: (b, t, 0)),
        out_shape=jax.ShapeDtypeStruct((bsz, seq, 1024), BF16),
        scratch_shapes=[pltpu.VMEM((gb, GLA_HEADS, GLA_DV, GLA_DK), F32)],
        compiler_params=_cparams(("parallel", "arbitrary")),
        name="gla",
    )(p2v, p2v, p2v, p2v, p2v, wa, ba, gn)
    return out.reshape(bsz * seq, 1024)


def _tree_sum(xs):
    xs = list(xs)
    while len(xs) > 1:
        xs = [xs[i] + xs[i + 1] for i in range(0, len(xs) - 1, 2)] + ([xs[-1]] if len(xs) % 2 else [])
    return xs[0]


def _dsa_kernel(q_ref, iq_ref, iw_ref, k_ref, ik_ref, vt_ref, bias_ref, wot_ref, o_ref,
                keys_ref, planes_ref, qaug_ref, iqall_ref, acc_ref, st0_ref, st1_ref, m_ref, cm0_ref, cm1_ref,
                *, topk):
    TQ, CH = DSA_TQ, DSA_CH
    CC = 2 * CH
    qi = pl.program_id(1)
    t0 = qi * TQ
    nch = qi // (CH // TQ) + 1
    npair = (nch + 1) // 2
    nt = (((1,), (1,)), ((), ()))

    eye = (lax.broadcasted_iota(I32, (TQ, TQ), 0) == lax.broadcasted_iota(I32, (TQ, TQ), 1))
    eye = jnp.where(eye, 1.0, 0.0).astype(BF16)
    q_t = q_ref[...].astype(F32).T.astype(BF16)
    iq_t = iq_ref[...].astype(F32).T.astype(BF16)
    for h in range(DSA_HEADS):
        cs = slice(h * TQ, (h + 1) * TQ)
        qaug_ref[0:TQ, cs] = eye
        qaug_ref[TQ:TQ + HEAD_DIM, cs] = q_t[h * HEAD_DIM:(h + 1) * HEAD_DIM, :]
        qaug_ref[TQ + HEAD_DIM:, cs] = jnp.zeros((TQ - HEAD_DIM, TQ), BF16)
    for h in range(IDX_HEADS):
        cs = slice(h * TQ, (h + 1) * TQ)
        iqall_ref[0:IDX_DIM, cs] = iq_t[h * IDX_DIM:(h + 1) * IDX_DIM, :]
        iqall_ref[IDX_DIM:, cs] = jnp.zeros((LANES - IDX_DIM, TQ), BF16)
    idx_scale = (IDX_HEADS ** -0.5) * (IDX_DIM ** -0.5)
    wt = (iw_ref[...].astype(F32) * idx_scale).T

    qpos = t0 + lax.broadcasted_iota(I32, (1, TQ), 1)
    RB = 128
    krow1 = lax.broadcasted_iota(I32, (RB, TQ), 0)
    krow = lax.broadcasted_iota(I32, (CH, TQ), 0)
    krow2 = lax.broadcasted_iota(I32, (CC, TQ), 0)

    def score_block(s0):
        s0 = pl.multiple_of(s0, RB)
        ikc = ik_ref[pl.ds(s0, RB), :]
        s = None
        for hp in range(IDX_HEADS // 2):
            x = jnp.dot(ikc, iqall_ref[:, hp * 2 * TQ:(hp + 1) * 2 * TQ],
                        preferred_element_type=F32)
            t = (jnp.maximum(x[:, :TQ], 0.0) * wt[2 * hp:2 * hp + 1, :]
                 + jnp.maximum(x[:, TQ:], 0.0) * wt[2 * hp + 1:2 * hp + 2, :])
            s = t if s is None else s + t
        s = jnp.where(s == 0.0, 0.0, s)
        bits = pltpu.bitcast(s, I32)
        key = bits ^ ((bits >> 31) & 0x7FFFFFFF)
        key = jnp.where(s0 + krow1 <= qpos, key, INT_MIN)
        keys_ref[pl.ds(s0, RB), :] = key

    def score_body(cp, carry):
        for sub in range(CC // RB):
            score_block(cp * CC + sub * RB)
        return carry

    lax.fori_loop(0, nch // 2, score_body, 0)

    @pl.when(nch % 2 == 1)
    def _():
        for sub in range(CH // RB):
            score_block((nch - 1) * CH + sub * RB)
        keys_ref[pl.ds(pl.multiple_of(nch * CH, CH), CH), :] = jnp.full((CH, TQ), INT_MIN, I32)

    kvec = jnp.minimum(topk, qpos + 1)
    NPL = 32

    def planes_body(c, carry):
        s0 = pl.multiple_of(c * CH, CH)
        a = [keys_ref[pl.ds(s0 + 8 * j, 8), :] ^ INT_MIN for j in range(NPL)]
        j, m = 16, 0x0000FFFF
        while j:
            sh = jnp.full((8, TQ), j, I32)
            k = 0
            while k < NPL:
                t = (a[k] ^ lax.shift_right_logical(a[k + j], sh)) & m
                a[k] = a[k] ^ t
                a[k + j] = a[k + j] ^ lax.shift_left(t, sh)
                k = (k + j + 1) & ~j
            j >>= 1
            m = (m ^ (m << j)) & 0xFFFFFFFF
        for p in range(NPL):
            planes_ref[c, p] = a[p]
        return carry

    def empty_body(c, carry):
        for p in range(NPL):
            planes_ref[c, p] = jnp.zeros((8, TQ), I32)
        return carry

    NCK = keys_ref.shape[0] // CH
    short = nch <= NCK // 2
    lax.fori_loop(0, nch, planes_body, 0)
    lax.fori_loop(nch, jnp.where(short, NCK // 2, NCK), empty_body, 0)

    def radix_select(nc):
        def run():
            def lane_count(words):
                return jnp.sum(_tree_sum([lax.population_count(w) for w in words]), axis=0, keepdims=True)

            def plane_body(t, carry):
                thr_u, n_gt, alive = carry
                p = 2 * t
                h1 = [alive[c] & planes_ref[c, p] for c in range(nc)]
                h0 = [alive[c] ^ h1[c] for c in range(nc)]
                h11 = [h1[c] & planes_ref[c, p + 1] for c in range(nc)]
                h01 = [h0[c] & planes_ref[c, p + 1] for c in range(nc)]
                c1, c11, c01 = lane_count(h1), lane_count(h11), lane_count(h01)
                take1 = (n_gt + c1) >= kvec
                n_mid = jnp.where(take1, n_gt, n_gt + c1)
                ones2 = jnp.where(take1, c11, c01)
                take2 = (n_mid + ones2) >= kvec
                n_gt = jnp.where(take2, n_mid, n_mid + ones2)
                thr_u = (thr_u | jnp.where(take1, jnp.int32(1) << (31 - p), 0)
                         | jnp.where(take2, jnp.int32(1) << (30 - p), 0))
                alive = tuple(
                    jnp.where(take1, jnp.where(take2, h11[c], h1[c] ^ h11[c]), jnp.where(take2, h01[c], h0[c] ^ h01[c]))
                    for c in range(nc))
                return thr_u, n_gt, alive

            zero = jnp.zeros((1, TQ), I32)
            alive0 = tuple(jnp.full((8, TQ), -1, I32) for _ in range(nc))
            thr_u, n_gt, alive = lax.fori_loop(0, NPL // 2, plane_body, (zero, zero, alive0))
            n_eq = jnp.sum(_tree_sum([lax.population_count(a) for a in alive]), axis=0, keepdims=True)
            return thr_u, n_gt, n_eq
        return run

    thr_u, n_gt, n_eq = lax.cond(short, radix_select(NCK // 2), radix_select(NCK))
    thr = thr_u ^ INT_MIN

    def count(pred_fn):
        def body(c, acc):
            s0 = pl.multiple_of(c * CC, CC)
            kk = keys_ref[pl.ds(s0, CC), :]
            hit = jnp.where(pred_fn(kk, s0), 1, 0).astype(I32)
            return acc + jnp.sum(hit.reshape(CC // 8, 8, TQ), axis=0)
        acc = lax.fori_loop(0, npair, body, jnp.zeros((8, TQ), I32))
        return jnp.sum(acc, axis=0, keepdims=True)

    excess = n_gt + n_eq - kvec
    has_excess = jnp.max(excess) > 0

    @pl.when(has_excess)
    def _():
        need = kvec - n_gt

        def tie_lt(cut):
            return count(lambda kk, s0: (kk == thr) & (s0 + krow2 < cut))

        def cut_body(b, cut):
            cand = cut | (jnp.int32(1) << (12 - b))
            return jnp.where(tie_lt(cand) <= need, cand, cut)

        cut = lax.fori_loop(0, 13, cut_body, jnp.zeros((1, TQ), I32))

        def drop_body(c, carry):
            s0 = pl.multiple_of(c * CH, CH)
            kk = keys_ref[pl.ds(s0, CH), :]
            keys_ref[pl.ds(s0, CH), :] = jnp.where((kk == thr) & (s0 + krow >= cut), INT_MIN, kk)
            return carry

        lax.fori_loop(0, nch, drop_body, 0)

    acc_ref[...] = jnp.zeros_like(acc_ref)

    NP = DSA_HEADS // 2
    PW = 2 * TQ

    AB = DSA_AB
    NB = CH // AB
    last_chunk = keys_ref.shape[0] // CH - 1

    def logits(c, st_ref, cm_ref):
        s0 = pl.multiple_of(c * CH, CH)
        zoff = pl.multiple_of(jnp.clip(s0 - t0 + DSA_Z0, 0, DSA_Z0), 8)
        kaug = []
        for rb in range(NB):
            rs = pl.ds(s0 + rb * AB, AB)
            pen = jnp.where(keys_ref[rs, :] >= thr, 0.0, NEG_BIG).astype(BF16)
            kaug.append(jnp.concatenate([pen, k_ref[rs, :]], axis=1))
        for hp in range(NP):
            ps = slice(hp * PW, (hp + 1) * PW)
            cm = None
            for rb in range(NB):
                st = jnp.dot(kaug[rb], qaug_ref[:, ps], preferred_element_type=F32)
                st = st + bias_ref[pl.ds(zoff + rb * AB, AB), ps]
                st_ref[rb * AB:(rb + 1) * AB, ps] = st
                tm = jnp.max(st.reshape(AB // 8, 8, PW), axis=0)
                cm = tm if cm is None else jnp.maximum(cm, tm)
            cm_ref[0:1, ps] = jnp.max(cm, axis=0, keepdims=True)

    def accumulate(c, st_ref, cm_ref):
        vtc = vt_ref[c]
        for hp in range(NP):
            ps = slice(hp * PW, (hp + 1) * PW)
            m_old = m_ref[0:1, ps]
            m_new = jnp.maximum(m_old, cm_ref[0:1, ps])
            m_ref[0:1, ps] = m_new
            alpha = jnp.exp2(m_old - m_new)
            p = jnp.concatenate(
                [jnp.exp2(st_ref[rb * AB:(rb + 1) * AB, ps] - m_new).astype(BF16) for rb in range(NB)], axis=0)
            acc_ref[:, ps] = acc_ref[:, ps] * alpha + jnp.dot(vtc, p, preferred_element_type=F32)

    def att_body(cp, carry):
        c0 = 2 * cp
        logits(c0 + 1, st1_ref, cm1_ref)
        accumulate(c0, st0_ref, cm0_ref)
        logits(c0 + 2, st0_ref, cm0_ref)
        accumulate(c0 + 1, st1_ref, cm1_ref)
        return carry

    m_ref[...] = jnp.full(m_ref.shape, NEG_BIG, F32)
    logits(0, st0_ref, cm0_ref)
    nfull = (nch - 1) // 2
    lax.fori_loop(0, nfull, att_body, 0)
    c_last = 2 * nfull

    @pl.when(nch % 2 == 1)
    def _():
        accumulate(c_last, st0_ref, cm0_ref)

    @pl.when(nch % 2 == 0)
    def _():
        logits(c_last + 1, st1_ref, cm1_ref)
        accumulate(c_last, st0_ref, cm0_ref)
        accumulate(c_last + 1, st1_ref, cm1_ref)

    ot = jnp.concatenate(
        [acc_ref[0:HEAD_DIM, h * TQ:(h + 1) * TQ] / acc_ref[HEAD_DIM:HEAD_DIM + 1, h * TQ:(h + 1) * TQ]
         for h in range(DSA_HEADS)], axis=0)
    o_ref[...] = jnp.dot(ot.T.astype(BF16), wot_ref[...], preferred_element_type=F32)


def _dsa(p1, p2, vt, bias_t, wot, bsz, seq):
    tq = DSA_TQ
    p1v = p1.reshape(bsz, seq, W1)
    p2v = p2.reshape(bsz, seq, W2)
    nq = DSA_HEADS * tq
    in_specs = [
        pl.BlockSpec((None, tq, 768), lambda b, i: (b, i, P1_CQ // 768)),
        pl.BlockSpec((None, tq, 512), lambda b, i: (b, i, P2_IQ // 512)),
        pl.BlockSpec((None, tq, LANES), lambda b, i: (b, i, P2_IW // LANES)),
        pl.BlockSpec((None, seq, LANES), lambda b, i: (b, 0, P1_CK // LANES)),
        pl.BlockSpec((None, seq, LANES), lambda b, i: (b, 0, P2_IK // LANES)),
        pl.BlockSpec((None, seq // DSA_CH, DSA_VR, DSA_CH), lambda b, i: (b, 0, 0, 0)),
        pl.BlockSpec((DSA_Z, nq), lambda b, i: (0, 0)),
        pl.BlockSpec((DSA_HEADS * HEAD_DIM, D_MODEL), lambda b, i: (0, 0)),
    ]
    out = pl.pallas_call(
        functools.partial(_dsa_kernel, topk=min(IDX_TOPK, seq // 4)),
        grid=(bsz, seq // tq),
        in_specs=in_specs,
        out_specs=pl.BlockSpec((None, tq, D_MODEL), lambda b, i: (b, i, 0)),
        out_shape=jax.ShapeDtypeStruct((bsz, seq, D_MODEL), F32),
        scratch_shapes=[
            pltpu.VMEM((seq, tq), I32),
            pltpu.VMEM((seq // DSA_CH, 32, 8, tq), I32),
            pltpu.VMEM((2 * tq, nq), BF16),
            pltpu.VMEM((LANES, IDX_HEADS * tq), BF16),
            pltpu.VMEM((DSA_VR, nq), F32),
            pltpu.VMEM((DSA_CH, nq), F32),
            pltpu.VMEM((DSA_CH, nq), F32),
            pltpu.VMEM((8, nq), F32),
            pltpu.VMEM((8, nq), F32),
            pltpu.VMEM((8, nq), F32),
        ],
        compiler_params=_cparams(("parallel", "arbitrary")),
        name="dsa",
    )(p1v, p2v, p2v, p1v, p2v, vt, bias_t, wot)
    return out.reshape(bsz * seq, D_MODEL)


MERGE_TM = 512


def _merge_kernel(h_ref, oa0_ref, oa1_ref, oa2_ref, sa0_ref, sa1_ref, sa2_ref, ob_ref, yc_ref, g_ref,
                  woa_ref, wob_ref, wo_ref, o_ref, on1_ref, on2_ref, sn1_ref, sn2_ref):
    H = DSWA_HPG
    tm = h_ref.shape[0]
    gw = H * HEAD_DIM
    for (_, r), src_o, src_s, dst_o, dst_s in zip(DSWA_PATTERNS[1:], (oa1_ref, oa2_ref), (sa1_ref, sa2_ref),
                                                  (on1_ref, on2_ref), (sn1_ref, sn2_ref)):
        for c in range(r):
            for k in range(gw // LANES):
                dst_o[k, pl.ds(c, tm // r, stride=r), :] = src_o[:, c * gw + k * LANES:c * gw + (k + 1) * LANES]
            dst_s[pl.ds(c, tm // r, stride=r), :] = src_s[:, c * LANES:(c + 1) * LANES]
    oa_vals = (oa0_ref[...],) + tuple(
        jnp.concatenate([ref[k] for k in range(gw // LANES)], axis=1) for ref in (on1_ref, on2_ref))
    sts = [sa0_ref[...], sn1_ref[...], sn2_ref[...]]
    lane = lax.broadcasted_iota(I32, sts[0].shape, 1)
    mmax = jnp.maximum(jnp.maximum(sts[0], sts[1]), sts[2])
    wts = [pltpu.roll(s, LANES - H, 1) * jnp.exp(s - mmax) for s in sts]
    tot = wts[0] + wts[1] + wts[2]
    hrow = lax.broadcasted_iota(I32, (LANES, H * HEAD_DIM), 0)
    hcol = lax.broadcasted_iota(I32, (LANES, H * HEAD_DIM), 1) // HEAD_DIM
    expand = jnp.where(hrow == hcol, 1.0, 0.0).astype(BF16)
    oa = None
    for g in range(3):
        w = jnp.where(lane < H, wts[g] / tot, 0.0)
        hi = w.astype(BF16)
        lo = (w - hi.astype(F32)).astype(BF16)
        wfull = (jnp.dot(hi, expand, preferred_element_type=F32)
                 + jnp.dot(lo, expand, preferred_element_type=F32))
        term = wfull * oa_vals[g]
        oa = term if oa is None else oa + term
    oa = oa.astype(BF16)
    y_a = jnp.dot(oa, woa_ref[...], preferred_element_type=F32)
    y_b = jnp.dot(ob_ref[...], wob_ref[...], preferred_element_type=F32)
    y_c = yc_ref[...]
    D = D_MODEL
    mix = (g_ref[:, 0:D].astype(F32) * y_a + g_ref[:, D:2 * D].astype(F32) * y_b
           + g_ref[:, 2 * D:3 * D].astype(F32) * y_c)
    o_ref[...] = h_ref[...] + jnp.dot(mix.astype(BF16), wo_ref[...], preferred_element_type=F32)


def _merge(h, oas, sas, ob, yc, gates, woa, wob, wo):
    m = h.shape[0]
    tm = MERGE_TM
    gw = DSWA_HPG * HEAD_DIM
    row = lambda w: pl.BlockSpec((tm, w), lambda i: (i, 0))
    rmrow = lambda w, r: pl.BlockSpec((tm // r, r * w), lambda i: (i, 0))
    full = lambda a: pl.BlockSpec(a.shape, lambda i: (0, 0), pipeline_mode=pl.Buffered(1))
    rs = [r for _, r in DSWA_PATTERNS]
    in_specs = ([row(D_MODEL)] + [rmrow(gw, r) for r in rs] + [rmrow(LANES, r) for r in rs]
                + [row(1024), row(D_MODEL), row(3 * D_MODEL)] + [full(woa), full(wob), full(wo)])
    return pl.pallas_call(
        _merge_kernel,
        grid=(m // tm,),
        in_specs=in_specs,
        out_specs=row(D_MODEL),
        out_shape=jax.ShapeDtypeStruct((m, D_MODEL), F32),
        scratch_shapes=[pltpu.VMEM((gw // LANES, tm, LANES), F32), pltpu.VMEM((gw // LANES, tm, LANES), F32),
                        pltpu.VMEM((tm, LANES), F32), pltpu.VMEM((tm, LANES), F32)],
        compiler_params=_cparams(("parallel",)),
        name="merge",
    )(h, *oas, *sas, ob, yc, gates, woa, wob, wo)


FFN_TM = 512
FFN_TF = 256


def _rms(x, g):
    ms = jnp.mean(x * x, axis=-1, keepdims=True)
    return (x * lax.rsqrt(ms + RMS_EPS) * g).astype(BF16)


def _ffn_ple_kernel(h_ref, p_ref, gf_ref, wg_ref, wu_ref, wd_ref, gp_ref, wpg_ref, wpp_ref, o_ref, *, tf):
    x = h_ref[...]
    u = _rms(x, gf_ref[...])
    acc = None
    for j in range(wg_ref.shape[1] // tf):
        cs = slice(j * tf, (j + 1) * tf)
        a = jnp.dot(u, wg_ref[:, cs], preferred_element_type=F32)
        b = jnp.dot(u, wu_ref[:, cs], preferred_element_type=F32)
        t = (a * jax.nn.sigmoid(a) * b).astype(BF16)
        d = jnp.dot(t, wd_ref[cs, :], preferred_element_type=F32)
        acc = d if acc is None else acc + d
    h2 = x + acc
    e = _rms(h2, gp_ref[...])
    gate = jax.nn.sigmoid(jnp.dot(e, wpg_ref[...], preferred_element_type=F32))
    proj = jnp.dot(p_ref[...].astype(BF16), wpp_ref[...], preferred_element_type=F32)
    o_ref[...] = h2 + gate * proj


def _ffn_ple(h, p, gf, wg, wu, wd, gp, wpg, wpp):
    m, d = h.shape
    tm, tf = FFN_TM, FFN_TF
    const = lambda a: pl.BlockSpec(a.shape, lambda i: (0, 0), pipeline_mode=pl.Buffered(1))
    return pl.pallas_call(
        functools.partial(_ffn_ple_kernel, tf=tf),
        grid=(m // tm,),
        in_specs=[pl.BlockSpec((tm, d), lambda i: (i, 0)), pl.BlockSpec((tm, PLE_DIM), lambda i: (i, 0)),
                  const(gf), const(wg), const(wu), const(wd), const(gp), const(wpg), const(wpp)],
        out_specs=pl.BlockSpec((tm, d), lambda i: (i, 0)),
        out_shape=jax.ShapeDtypeStruct((m, d), F32),
        compiler_params=_cparams(("parallel",)),
        name="ffn_ple",
    )(h, p, gf, wg, wu, wd, gp, wpg, wpp)


def _rel_bucket(dist):
    max_exact = REL_BUCKETS // 2
    d = jnp.maximum(dist, 0)
    df = jnp.maximum(d, 1).astype(F32)
    large = max_exact + (jnp.log(df / max_exact) / math.log(REL_MAX_DIST / max_exact)
                         * (REL_BUCKETS - max_exact)).astype(I32)
    large = jnp.minimum(large, REL_BUCKETS - 1)
    return jnp.where(d < max_exact, d, large)


def _toeplitz(rev, n_rows, n_cols):
    nh = rev.shape[0]
    lw = rev.shape[1] + 1
    w = jnp.pad(rev, ((0, 0), (0, 1)))
    s = jnp.broadcast_to(w[:, None, :], (nh, n_rows, lw)).reshape(nh, n_rows * lw)
    s = s[:, :n_rows * (lw - 1)].reshape(nh, n_rows, lw - 1)
    return s[:, :, n_rows - 1:n_rows - 1 + n_cols]


def _bias_tables(rel_bias):
    blk = DSWA_BLOCK
    dswa = []
    for g, (_, r) in enumerate(DSWA_PATTERNS):
        delta = np.arange(3 * blk - 1)[::-1] - (blk - 1)
        rev = rel_bias[_rel_bucket(jnp.asarray(delta * r, I32))][:, g * DSWA_HPG:(g + 1) * DSWA_HPG]
        dswa.append(_toeplitz(rev.T, blk, 2 * blk))
    tq = DSA_TQ
    dist = np.arange(DSA_Z + tq - 1)[::-1] - (DSA_Z - 1) + DSA_Z0
    rev = rel_bias[_rel_bucket(jnp.asarray(dist, I32))][:, DSWA_HEADS:] * math.log2(math.e)
    bias_t = jnp.transpose(_toeplitz(rev.T, tq, DSA_Z), (2, 0, 1)).reshape(DSA_Z, DSA_HEADS * tq)
    return dswa, bias_t


def _pad_cols(w, width):
    return jnp.pad(w, ((0, 0), (0, width - w.shape[1])))


def _layer_params(w_in, qn_a, kn_a, qn_c, kn_c, w_alpha2, b_alpha):
    offs = np.cumsum((0,) + IN_WIDTHS)
    parts = [w_in[:, offs[i]:offs[i + 1]] for i in range(len(IN_WIDTHS))]
    a_q, a_k, a_v, b_q, b_k, b_v, b_r, b_al, c_q, c_k, c_v, i_q, i_k, i_w = parts
    w1 = _pad_cols(jnp.concatenate([a_q, a_k, c_q, c_k], axis=1), W1).astype(BF16)
    w2 = _pad_cols(jnp.concatenate(
        [b_v, b_r, b_q, b_k, i_q, a_v, _pad_cols(b_al, LANES), _pad_cols(c_v, LANES),
         _pad_cols(i_k, LANES), _pad_cols(i_w, LANES)], axis=1), W2).astype(BF16)
    scale = HEAD_DIM ** -0.5
    gain1 = jnp.concatenate([jnp.tile(qn_a, DSWA_HEADS) * scale, jnp.tile(kn_a, DSWA_HEADS),
                             jnp.tile(qn_c, DSA_HEADS) * (scale * math.log2(math.e)), kn_c])
    gain1 = jnp.pad(gain1, (0, W1 - gain1.shape[0])).reshape(1, W1)
    wa = jnp.pad(w_alpha2, ((0, LANES - GLA_RANK), (0, 0))).astype(BF16)
    return w1, w2, gain1, wa, b_alpha.reshape(1, -1)


def kernel(x, p, rel_bias, norm_mix, w_in, qn_a, kn_a, qn_c, kn_c, w_alpha2, b_alpha, gla_norm, w_out_a, w_out_b, w_out_c, w_gate, b_gate, w_o, norm_ffn, w_ffn_gate, w_ffn_up, w_ffn_down, norm_ple, w_ple_gate, w_ple_proj):
    bsz, seq, d = x.shape
    depth = p.shape[0]
    m = bsz * seq
    dswa_bias, bias_t = _bias_tables(rel_bias)
    h = x.reshape(m, d)
    zeros_w2 = jnp.zeros((1, W2), F32)
    for i in range(depth):
        w1, w2, gain1, wa, ba = _layer_params(w_in[i], qn_a[i], kn_a[i], qn_c[i], kn_c[i], w_alpha2[i], b_alpha[i])
        gmix = norm_mix[i].reshape(1, d)
        gw = DSWA_HPG * HEAD_DIM
        dil = [(g, r) for g, (_, r) in enumerate(DSWA_PATTERNS) if r > 1]
        p1, *qk_rm = _proj(h, gmix, w1, gain1, "qk",
                           regroup=[(off + g * gw, r) for off in (P1_AQ, P1_AK) for g, r in dil])
        p2, *v_rm = _proj(h, gmix, w2, zeros_w2, "plain", regroup=[(P2_AV + g * gw, r) for g, r in dil])
        gates = _proj(h, gmix, w_gate[i].astype(BF16), b_gate[i].reshape(1, -1), "gate")
        rm = {g: (qk_rm[n], qk_rm[len(dil) + n], v_rm[n]) for n, (g, r) in enumerate(dil)}
        oas, sas = [], []
        for g, (_, r) in enumerate(DSWA_PATTERNS):
            o, st = _dswa(p1, p2, rm.get(g), dswa_bias[g], g, r, bsz, seq)
            oas.append(o)
            sas.append(st)
        ob = _gla(p2, wa, ba, gla_norm[i].reshape(1, -1), bsz, seq)
        cv = p2[:, P2_CV:P2_CV + HEAD_DIM].reshape(bsz, seq // DSA_CH, DSA_CH, HEAD_DIM)
        ones_pad = jnp.zeros((bsz, seq // DSA_CH, DSA_VR - HEAD_DIM, DSA_CH), BF16).at[:, :, 0, :].set(1.0)
        vt = jnp.concatenate([jnp.transpose(cv, (0, 1, 3, 2)), ones_pad], axis=2)
        yc = _dsa(p1, p2, vt, bias_t, w_out_c[i].astype(BF16), bsz, seq)
        h = _merge(h, oas, sas, ob, yc, gates, w_out_a[i].astype(BF16), w_out_b[i].astype(BF16),
                   w_o[i].astype(BF16))
        h = _ffn_ple(h, p[i].reshape(m, PLE_DIM), norm_ffn[i].reshape(1, d), w_ffn_gate[i].astype(BF16),
                     w_ffn_up[i].astype(BF16), w_ffn_down[i].astype(BF16), norm_ple[i].reshape(1, d),
                     w_ple_gate[i].astype(BF16), w_ple_proj[i].astype(BF16))
    return h.reshape(bsz, seq, d)
```

```python
import functools
import math

import numpy as np
import jax
import jax.numpy as jnp
from jax import lax
from jax.experimental import pallas as pl
from jax.experimental.pallas import tpu as pltpu

F32 = jnp.float32
BF16 = jnp.bfloat16
I32 = jnp.int32
I16 = jnp.int16

D_MODEL = 1024
HEAD_DIM = 64
RMS_EPS = 1e-6
DSWA_PATTERNS = ((128, 1), (512, 4), (2048, 16))
DSWA_HPG = 4
DSWA_HEADS = 12
DSWA_BLOCK = 128
GLA_HEADS = 4
GLA_DK = 128
GLA_DV = 256
GLA_RANK = 16
GLA_TAU = 16.0
GLA_CHUNK = 64
DSA_HEADS = 12
IDX_HEADS = 8
IDX_DIM = 64
IDX_TOPK = 256
REL_BUCKETS = 32
REL_MAX_DIST = 2048
D_FF = 2816
PLE_DIM = 256
IN_WIDTHS = (768, 768, 768, 512, 512, 1024, 1024, 16, 768, 64, 64, 512, 64, 8)

LANES = 128
MXU_N = 256
VMEM_LIMIT = 56 * 1024 * 1024

W1 = 2560
P1_AQ, P1_AK, P1_CQ, P1_CK = 0, 768, 1536, 2304
W2 = 5120
P2_BV, P2_BR, P2_BQ, P2_BK, P2_IQ, P2_AV = 0, 1024, 2048, 2560, 3072, 3584
P2_BAL, P2_CV, P2_IK, P2_IW = 4352, 4480, 4608, 4736

PROJ_TM = 1024
PROJ_TN = 512

DSWA_QB = 8
DSA_TQ = 128
DSA_CH = 256
DSA_AB = 256
DSA_VR = 80
DSA_BIAS_CONST_FROM = 1512
DSA_Z0 = 1792
DSA_Z = DSA_Z0 + DSA_CH
INT_MIN = -(2 ** 31)
NEG_BIG = -1e30


def _cparams(sem, flags=None):
    return pltpu.CompilerParams(dimension_semantics=sem, vmem_limit_bytes=VMEM_LIMIT, flags=flags)


def _proj_kernel(h_ref, g_ref, w_ref, e_ref, *rest, mode, tn, regroup):
    if mode == "qk":
        bd_ref, rest = rest[0], rest[1:]
    o_ref, rg_refs = rest[0], rest[1:1 + len(regroup)]
    scr_ref = rest[-1] if regroup else None
    tm = h_ref.shape[0]
    gw = DSWA_HPG * HEAD_DIM
    x = h_ref[...]
    ms = jnp.mean(x * x, axis=-1, keepdims=True)
    u = (x * lax.rsqrt(ms + RMS_EPS) * g_ref[...]).astype(BF16)
    for j in range(w_ref.shape[1] // tn):
        cs = slice(j * tn, (j + 1) * tn)
        acc = jnp.dot(u, w_ref[:, cs], preferred_element_type=F32)
        if mode == "plain":
            out = acc
        elif mode == "gate":
            out = jax.nn.sigmoid(acc + e_ref[:, cs])
        else:
            sq = acc * acc
            hi = sq.astype(BF16)
            lo = (sq - hi.astype(F32)).astype(BF16)
            bw = bd_ref.shape[0]
            ss = jnp.concatenate(
                [jnp.dot(hi[:, k:k + bw], bd_ref[...], preferred_element_type=F32)
                 + jnp.dot(lo[:, k:k + bw], bd_ref[...], preferred_element_type=F32) for k in range(0, tn, bw)],
                axis=1)
            out = acc * lax.rsqrt(ss * (1.0 / HEAD_DIM) + RMS_EPS) * e_ref[:, cs]
        o_ref[:, cs] = out.astype(o_ref.dtype)
        for (col, r), rg_ref in zip(regroup, rg_refs):
            if col // tn == j:
                for k in range(gw // LANES):
                    scr_ref[k] = out[:, col % tn + k * LANES:col % tn + (k + 1) * LANES]
                for c in range(r):
                    for k in range(gw // LANES):
                        rg_ref[:, c * gw + k * LANES:c * gw + (k + 1) * LANES] = (
                            scr_ref[k, pl.ds(c, tm // r, stride=r), :].astype(rg_ref.dtype))


def _proj(h, g, w, e, mode, regroup=()):
    m, d = h.shape
    n = w.shape[1]
    tm, tn = PROJ_TM, PROJ_TN
    gw = DSWA_HPG * HEAD_DIM
    const = lambda shape: pl.BlockSpec(shape, lambda i: (0, 0), pipeline_mode=pl.Buffered(1))
    in_specs = [pl.BlockSpec((tm, d), lambda i: (i, 0)), const((1, d)), const((d, n)), const((1, n))]
    args = [h, g, w, e]
    if mode == "qk":
        r = np.arange(MXU_N) // HEAD_DIM
        bd = jnp.asarray((r[:, None] == r[None, :]).astype(np.float32), dtype=BF16)
        in_specs.append(const((MXU_N, MXU_N)))
        args.append(bd)
    out_specs = [pl.BlockSpec((tm, n), lambda i: (i, 0))]
    out_shape = [jax.ShapeDtypeStruct((m, n), BF16)]
    for _, r in regroup:
        out_specs.append(pl.BlockSpec((tm // r, r * gw), lambda i: (i, 0)))
        out_shape.append(jax.ShapeDtypeStruct((m // r, r * gw), BF16))
    outs = pl.pallas_call(
        functools.partial(_proj_kernel, mode=mode, tn=tn, regroup=tuple(regroup)),
        grid=(m // tm,),
        in_specs=in_specs,
        out_specs=out_specs,
        out_shape=out_shape,
        scratch_shapes=[pltpu.VMEM((gw // LANES, tm, LANES), F32)] if regroup else [],
        compiler_params=_cparams(("parallel",)),
        name="proj_" + mode,
    )(*args)
    return outs if regroup else outs[0]


def _dswa_kernel(q_ref, kp_ref, kc_ref, vp_ref, vc_ref, bias_ref, o_ref, st_ref, *, qb, cb):
    i = pl.program_id(2)
    blk = DSWA_BLOCK
    row = lax.broadcasted_iota(I32, (blk, 2 * blk), 0)
    col = lax.broadcasted_iota(I32, (blk, 2 * blk), 1)
    cur_ok = (col >= blk) & ((col - blk) <= row)
    valid_inner = ((col < blk) & (col >= row)) | cur_ok
    prev_off = jnp.where(i > 0, 0, 4 * blk)
    valid_first = ((col < blk) & (col >= row + prev_off)) | cur_ok
    lane = lax.broadcasted_iota(I32, (blk, LANES), 1)
    gw = DSWA_HPG * HEAD_DIM
    for cc, j in [(cc, j) for cc in range(cb) for j in range(qb)]:
        rs = slice(j * blk, (j + 1) * blk)
        ps = slice((j - 1) * blk, j * blk)
        gs = slice(cc * gw, (cc + 1) * gw)
        valid = valid_first if j == 0 else valid_inner
        q = q_ref[rs, gs]
        k = jnp.concatenate([kp_ref[:, gs] if j == 0 else kc_ref[ps, gs], kc_ref[rs, gs]], axis=0)
        v = jnp.concatenate([vp_ref[:, gs] if j == 0 else vc_ref[ps, gs], vc_ref[rs, gs]], axis=0)
        stats = jnp.zeros((blk, LANES), F32)
        for hp in range(DSWA_HPG // 2):
            ts = slice(hp * LANES, (hp + 1) * LANES)
            q2, k2, v2 = q[:, ts], k[:, ts], v[:, ts]
            halves = []
            for hh in range(2):
                h = 2 * hp + hh
                qm = jnp.where((lane // HEAD_DIM) == hh, q2, jnp.zeros_like(q2))
                s = lax.dot_general(qm, k2, (((1,), (1,)), ((), ())), preferred_element_type=F32)
                s = jnp.where(valid, s + bias_ref[h], -jnp.inf)
                m = jnp.max(s, axis=-1, keepdims=True)
                p = jnp.exp(s - m)
                l = jnp.sum(p, axis=-1, keepdims=True)
                halves.append(jnp.dot(p.astype(BF16), v2, preferred_element_type=F32) / l)
                stats = jnp.where(lane == h, m, stats)
                stats = jnp.where(lane == DSWA_HPG + h, l, stats)
            o_ref[rs, cc * gw + hp * LANES:cc * gw + (hp + 1) * LANES] = jnp.where(
                lane < HEAD_DIM, halves[0], halves[1])
        st_ref[rs, cc * LANES:(cc + 1) * LANES] = stats


def _dswa(p1, p2, rm, bias, g, r, bsz, seq):
    blk = DSWA_BLOCK
    L = seq // r
    nblk = L // blk
    gw = DSWA_HPG * HEAD_DIM
    if r == 1:
        p1v = p1.reshape(bsz, L, W1)
        p2v = p2.reshape(bsz, L, W2)
        qv, kv, vv = p1v, p1v, p2v
        q_col = lambda c: (P1_AQ // gw) + g
        k_col = lambda c: (P1_AK // gw) + g
        v_col = lambda c: (P2_AV // gw) + g
    else:
        qv, kv, vv = (a.reshape(bsz, L, r * gw) for a in rm)
        q_col = k_col = v_col = lambda c: c
    qb = min(DSWA_QB, nblk)
    cb = min(max(DSWA_QB // qb, 1), r)
    prev = lambda i: jnp.maximum(i * qb - 1, 0)
    in_specs = [
        pl.BlockSpec((None, qb * blk, cb * gw), lambda b, c, i: (b, i, q_col(c))),
        pl.BlockSpec((None, blk, cb * gw), lambda b, c, i: (b, prev(i), k_col(c))),
        pl.BlockSpec((None, qb * blk, cb * gw), lambda b, c, i: (b, i, k_col(c))),
        pl.BlockSpec((None, blk, cb * gw), lambda b, c, i: (b, prev(i), v_col(c))),
        pl.BlockSpec((None, qb * blk, cb * gw), lambda b, c, i: (b, i, v_col(c))),
        pl.BlockSpec((DSWA_HPG, blk, 2 * blk), lambda b, c, i: (0, 0, 0)),
    ]
    out_specs = [
        pl.BlockSpec((None, qb * blk, cb * gw), lambda b, c, i: (b, i, c)),
        pl.BlockSpec((None, qb * blk, cb * LANES), lambda b, c, i: (b, i, c)),
    ]
    o, st = pl.pallas_call(
        functools.partial(_dswa_kernel, qb=qb, cb=cb),
        grid=(bsz, r // cb, nblk // qb),
        in_specs=in_specs,
        out_specs=out_specs,
        out_shape=[jax.ShapeDtypeStruct((bsz, L, r * gw), F32),
                   jax.ShapeDtypeStruct((bsz, L, r * LANES), F32)],
        compiler_params=_cparams(("parallel", "parallel", "arbitrary")),
        name="dswa_g%d" % g,
    )(qv, kv, kv, vv, vv, bias)
    return o.reshape(bsz * L, r * gw), st.reshape(bsz * L, r * LANES)


GLA_TC = 256
GLA_GB = 2


def _split3(x):
    a1 = x.astype(BF16)
    r1 = x - a1.astype(F32)
    a2 = r1.astype(BF16)
    a3 = (r1 - a2.astype(F32)).astype(BF16)
    return a1, a2, a3


def _gla_kernel(v_ref, r_ref, q_ref, k_ref, al_ref, wa_ref, ba_ref, gn_ref, o_ref, st_ref, *, gb):
    C = GLA_CHUNK

    @pl.when(pl.program_id(1) == 0)
    def _():
        st_ref[...] = jnp.zeros_like(st_ref)

    TC = GLA_TC
    NC = TC // C
    row = lax.broadcasted_iota(I32, (TC, TC), 0)
    col = lax.broadcasted_iota(I32, (TC, TC), 1)
    tri = ((row // C) == (col // C)) & (row >= col)
    tri_bf = jnp.where(tri, 1.0, 0.0).astype(BF16)
    nt = (((1,), (1,)), ((), ()))
    tn = (((0,), (0,)), ((), ()))
    for bb in range(gb):
        z = jnp.dot(al_ref[bb], wa_ref[...], preferred_element_type=F32) + ba_ref[...]
        la = (jnp.minimum(z, 0.0) - jnp.log(1.0 + jnp.exp(-jnp.abs(z)))) * (1.0 / GLA_TAU)
        a1, a2, a3 = _split3(la)
        bcum = (jnp.dot(tri_bf, a1, preferred_element_type=F32)
                + jnp.dot(tri_bf, a2, preferred_element_type=F32)
                + jnp.dot(tri_bf, a3, preferred_element_type=F32))
        blast = [bcum[(c + 1) * C - 1:(c + 1) * C, :] for c in range(NC)]
        dec = [jnp.exp(b) for b in blast]
        dec_rows = jnp.concatenate([jnp.broadcast_to(d, (C, d.shape[1])) for d in dec], axis=0)
        qf = q_ref[bb].astype(F32) * (GLA_DK ** -0.5)
        k_dec = k_ref[bb].astype(F32) * jnp.exp(-bcum)
        q_in = (qf * jnp.exp(bcum)).astype(BF16)
        k_in = k_dec.astype(BF16)
        k_end = (k_dec * dec_rows).astype(BF16)
        for h in range(GLA_HEADS):
            ks = slice(h * GLA_DK, (h + 1) * GLA_DK)
            vs = slice(h * GLA_DV, (h + 1) * GLA_DV)
            vh = v_ref[bb, :, vs]
            att = lax.dot_general(q_in[:, ks], k_in[:, ks], nt, preferred_element_type=F32)
            att = jnp.where(tri, att, 0.0)
            o = jnp.dot(att.astype(BF16), vh, preferred_element_type=F32)
            st = st_ref[bb, h]
            inter = []
            for c in range(NC):
                rs = slice(c * C, (c + 1) * C)
                inter.append(lax.dot_general(q_in[rs, ks], st.astype(BF16), nt, preferred_element_type=F32))
                st = st * dec[c][:, ks] + lax.dot_general(vh[rs, :], k_end[rs, ks], tn, preferred_element_type=F32)
            st_ref[bb, h] = st
            o = o + jnp.concatenate(inter, axis=0)
            ms = jnp.mean(o * o, axis=-1, keepdims=True)
            y = o * lax.rsqrt(ms + RMS_EPS) * gn_ref[...]
            rg = r_ref[bb, :, vs].astype(F32)
            y = y * (rg * jax.nn.sigmoid(rg))
            o_ref[bb, :, vs] = y.astype(o_ref.dtype)


def _gla(p2, wa, ba, gn, bsz, seq):
    tc = GLA_TC
    gb = GLA_GB if bsz % GLA_GB == 0 else 1
    p2v = p2.reshape(bsz, seq, W2)
    in_specs = [
        pl.BlockSpec((gb, tc, 1024), lambda b, t: (b, t, P2_BV // 1024)),
        pl.BlockSpec((gb, tc, 1024), lambda b, t: (b, t, P2_BR // 1024)),
        pl.BlockSpec((gb, tc, 512), lambda b, t: (b, t, P2_BQ // 512)),
        pl.BlockSpec((gb, tc, 512), lambda b, t: (b, t, P2_BK // 512)),
        pl.BlockSpec((gb, tc, LANES), lambda b, t: (b, t, P2_BAL // LANES)),
        pl.BlockSpec((LANES, 512), lambda b, t: (0, 0)),
        pl.BlockSpec((1, 512), lambda b, t: (0, 0)),
        pl.BlockSpec((1, GLA_DV), lambda b, t: (0, 0)),
    ]
    out = pl.pallas_call(
        functools.partial(_gla_kernel, gb=gb),
        grid=(bsz // gb, seq // tc),
        in_specs=in_specs,
        out_specs=pl.BlockSpec((gb, tc, 1024), lambda b, t: (b, t, 0)),
        out_shape=jax.ShapeDtypeStruct((bsz, seq, 1024), BF16),
        scratch_shapes=[pltpu.VMEM((gb, GLA_HEADS, GLA_DV, GLA_DK), F32)],
        compiler_params=_cparams(("parallel", "arbitrary")),
        name="gla",
    )(p2v, p2v, p2v, p2v, p2v, wa, ba, gn)
    return out.reshape(bsz * seq, 1024)


def _tree_sum(xs):
    xs = list(xs)
    while len(xs) > 1:
        xs = [xs[i] + xs[i + 1] for i in range(0, len(xs) - 1, 2)] + ([xs[-1]] if len(xs) % 2 else [])
    return xs[0]


def _dsa_kernel(q_ref, iq_ref, iw_ref, k_ref, ik_ref, vt_ref, bias_ref, wot_ref, o_ref,
                keys_ref, planes_ref, qaug_ref, iqall_ref, acc_ref, st0_ref, st1_ref, m_ref, cm0_ref, cm1_ref,
                *, topk):
    TQ, CH = DSA_TQ, DSA_CH
    CC = 2 * CH
    qi = pl.program_id(1)
    t0 = qi * TQ
    nch = qi // (CH // TQ) + 1
    npair = (nch + 1) // 2
    nt = (((1,), (1,)), ((), ()))

    eye = (lax.broadcasted_iota(I32, (TQ, TQ), 0) == lax.broadcasted_iota(I32, (TQ, TQ), 1))
    eye = jnp.where(eye, 1.0, 0.0).astype(BF16)
    q_t = q_ref[...].astype(F32).T.astype(BF16)
    iq_t = iq_ref[...].astype(F32).T.astype(BF16)
    for h in range(DSA_HEADS):
        cs = slice(h * TQ, (h + 1) * TQ)
        qaug_ref[0:TQ, cs] = eye
        qaug_ref[TQ:TQ + HEAD_DIM, cs] = q_t[h * HEAD_DIM:(h + 1) * HEAD_DIM, :]
        qaug_ref[TQ + HEAD_DIM:, cs] = jnp.zeros((TQ - HEAD_DIM, TQ), BF16)
    for h in range(IDX_HEADS):
        cs = slice(h * TQ, (h + 1) * TQ)
        iqall_ref[0:IDX_DIM, cs] = iq_t[h * IDX_DIM:(h + 1) * IDX_DIM, :]
        iqall_ref[IDX_DIM:, cs] = jnp.zeros((LANES - IDX_DIM, TQ), BF16)
    idx_scale = (IDX_HEADS ** -0.5) * (IDX_DIM ** -0.5)
    wt = (iw_ref[...].astype(F32) * idx_scale).T

    qpos = t0 + lax.broadcasted_iota(I32, (1, TQ), 1)
    RB = 128
    krow1 = lax.broadcasted_iota(I32, (RB, TQ), 0)
    krow = lax.broadcasted_iota(I32, (CH, TQ), 0)
    krow2 = lax.broadcasted_iota(I32, (CC, TQ), 0)

    def score_block(s0):
        s0 = pl.multiple_of(s0, RB)
        ikc = ik_ref[pl.ds(s0, RB), :]
        s = None
        for hp in range(IDX_HEADS // 2):
            x = jnp.dot(ikc, iqall_ref[:, hp * 2 * TQ:(hp + 1) * 2 * TQ],
                        preferred_element_type=F32)
            t = (jnp.maximum(x[:, :TQ], 0.0) * wt[2 * hp:2 * hp + 1, :]
                 + jnp.maximum(x[:, TQ:], 0.0) * wt[2 * hp + 1:2 * hp + 2, :])
            s = t if s is None else s + t
        s = jnp.where(s == 0.0, 0.0, s)
        bits = pltpu.bitcast(s, I32)
        key = bits ^ ((bits >> 31) & 0x7FFFFFFF)
        key = jnp.where(s0 + krow1 <= qpos, key, INT_MIN)
        keys_ref[pl.ds(s0, RB), :] = key

    def score_body(cp, carry):
        for sub in range(CC // RB):
            score_block(cp * CC + sub * RB)
        return carry

    lax.fori_loop(0, nch // 2, score_body, 0)

    @pl.when(nch % 2 == 1)
    def _():
        for sub in range(CH // RB):
            score_block((nch - 1) * CH + sub * RB)
        keys_ref[pl.ds(pl.multiple_of(nch * CH, CH), CH), :] = jnp.full((CH, TQ), INT_MIN, I32)

    kvec = jnp.minimum(topk, qpos + 1)
    NPL = 32

    def planes_body(c, carry):
        s0 = pl.multiple_of(c * CH, CH)
        a = [keys_ref[pl.ds(s0 + 8 * j, 8), :] ^ INT_MIN for j in range(NPL)]
        j, m = 16, 0x0000FFFF
        while j:
            sh = jnp.full((8, TQ), j, I32)
            k = 0
            while k < NPL:
                t = (a[k] ^ lax.shift_right_logical(a[k + j], sh)) & m
                a[k] = a[k] ^ t
                a[k + j] = a[k + j] ^ lax.shift_left(t, sh)
                k = (k + j + 1) & ~j
            j >>= 1
            m = (m ^ (m << j)) & 0xFFFFFFFF
        for p in range(NPL):
            planes_ref[c, p] = a[p]
        return carry

    def empty_body(c, carry):
        for p in range(NPL):
            planes_ref[c, p] = jnp.zeros((8, TQ), I32)
        return carry

    NCK = keys_ref.shape[0] // CH
    short = nch <= NCK // 2
    lax.fori_loop(0, nch, planes_body, 0)
    lax.fori_loop(nch, jnp.where(short, NCK // 2, NCK), empty_body, 0)

    def radix_select(nc):
        def run():
            def lane_count(words):
                return jnp.sum(_tree_sum([lax.population_count(w) for w in words]), axis=0, keepdims=True)

            def plane_body(t, carry):
                thr_u, n_gt, alive = carry
                p = 2 * t
                h1 = [alive[c] & planes_ref[c, p] for c in range(nc)]
                h0 = [alive[c] ^ h1[c] for c in range(nc)]
                h11 = [h1[c] & planes_ref[c, p + 1] for c in range(nc)]
                h01 = [h0[c] & planes_ref[c, p + 1] for c in range(nc)]
                c1, c11, c01 = lane_count(h1), lane_count(h11), lane_count(h01)
                take1 = (n_gt + c1) >= kvec
                n_mid = jnp.where(take1, n_gt, n_gt + c1)
                ones2 = jnp.where(take1, c11, c01)
                take2 = (n_mid + ones2) >= kvec
                n_gt = jnp.where(take2, n_mid, n_mid + ones2)
                thr_u = (thr_u | jnp.where(take1, jnp.int32(1) << (31 - p), 0)
                         | jnp.where(take2, jnp.int32(1) << (30 - p), 0))
                alive = tuple(
                    jnp.where(take1, jnp.where(take2, h11[c], h1[c] ^ h11[c]), jnp.where(take2, h01[c], h0[c] ^ h01[c]))
                    for c in range(nc))
                return thr_u, n_gt, alive

            zero = jnp.zeros((1, TQ), I32)
            alive0 = tuple(jnp.full((8, TQ), -1, I32) for _ in range(nc))
            thr_u, n_gt, alive = lax.fori_loop(0, NPL // 2, plane_body, (zero, zero, alive0))
            n_eq = jnp.sum(_tree_sum([lax.population_count(a) for a in alive]), axis=0, keepdims=True)
            return thr_u, n_gt, n_eq
        return run

    thr_u, n_gt, n_eq = lax.cond(short, radix_select(NCK // 2), radix_select(NCK))
    thr = thr_u ^ INT_MIN

    def count(pred_fn):
        def body(c, acc):
            s0 = pl.multiple_of(c * CC, CC)
            kk = keys_ref[pl.ds(s0, CC), :]
            hit = jnp.where(pred_fn(kk, s0), 1, 0).astype(I32)
            return acc + jnp.sum(hit.reshape(CC // 8, 8, TQ), axis=0)
        acc = lax.fori_loop(0, npair, body, jnp.zeros((8, TQ), I32))
        return jnp.sum(acc, axis=0, keepdims=True)

    excess = n_gt + n_eq - kvec
    has_excess = jnp.max(excess) > 0

    @pl.when(has_excess)
    def _():
        need = kvec - n_gt

        def tie_lt(cut):
            return count(lambda kk, s0: (kk == thr) & (s0 + krow2 < cut))

        def cut_body(b, cut):
            cand = cut | (jnp.int32(1) << (12 - b))
            return jnp.where(tie_lt(cand) <= need, cand, cut)

        cut = lax.fori_loop(0, 13, cut_body, jnp.zeros((1, TQ), I32))

        def drop_body(c, carry):
            s0 = pl.multiple_of(c * CH, CH)
            kk = keys_ref[pl.ds(s0, CH), :]
            keys_ref[pl.ds(s0, CH), :] = jnp.where((kk == thr) & (s0 + krow >= cut), INT_MIN, kk)
            return carry

        lax.fori_loop(0, nch, drop_body, 0)

    acc_ref[...] = jnp.zeros_like(acc_ref)

    NP = DSA_HEADS // 2
    PW = 2 * TQ

    AB = DSA_AB
    NB = CH // AB
    last_chunk = keys_ref.shape[0] // CH - 1

    def logits(c, st_ref, cm_ref):
        s0 = pl.multiple_of(c * CH, CH)
        zoff = pl.multiple_of(jnp.clip(s0 - t0 + DSA_Z0, 0, DSA_Z0), 8)
        kaug = []
        for rb in range(NB):
            rs = pl.ds(s0 + rb * AB, AB)
            pen = jnp.where(keys_ref[rs, :] >= thr, 0.0, NEG_BIG).astype(BF16)
            kaug.append(jnp.concatenate([pen, k_ref[rs, :]], axis=1))
        for hp in range(NP):
            ps = slice(hp * PW, (hp + 1) * PW)
            cm = None
            for rb in range(NB):
                st = jnp.dot(kaug[rb], qaug_ref[:, ps], preferred_element_type=F32)
                st = st + bias_ref[pl.ds(zoff + rb * AB, AB), ps]
                st_ref[rb * AB:(rb + 1) * AB, ps] = st
                tm = jnp.max(st.reshape(AB // 8, 8, PW), axis=0)
                cm = tm if cm is None else jnp.maximum(cm, tm)
            cm_ref[0:1, ps] = jnp.max(cm, axis=0, keepdims=True)

    def accumulate(c, st_ref, cm_ref):
        vtc = vt_ref[c]
        for hp in range(NP):
            ps = slice(hp * PW, (hp + 1) * PW)
            m_old = m_ref[0:1, ps]
            m_new = jnp.maximum(m_old, cm_ref[0:1, ps])
            m_ref[0:1, ps] = m_new
            alpha = jnp.exp2(m_old - m_new)
            p = jnp.concatenate(
                [jnp.exp2(st_ref[rb * AB:(rb + 1) * AB, ps] - m_new).astype(BF16) for rb in range(NB)], axis=0)
            acc_ref[:, ps] = acc_ref[:, ps] * alpha + jnp.dot(vtc, p, preferred_element_type=F32)

    def att_body(cp, carry):
        c0 = 2 * cp
        logits(c0 + 1, st1_ref, cm1_ref)
        accumulate(c0, st0_ref, cm0_ref)
        logits(c0 + 2, st0_ref, cm0_ref)
        accumulate(c0 + 1, st1_ref, cm1_ref)
        return carry

    m_ref[...] = jnp.full(m_ref.shape, NEG_BIG, F32)
    logits(0, st0_ref, cm0_ref)
    nfull = (nch - 1) // 2
    lax.fori_loop(0, nfull, att_body, 0)
    c_last = 2 * nfull

    @pl.when(nch % 2 == 1)
    def _():
        accumulate(c_last, st0_ref, cm0_ref)

    @pl.when(nch % 2 == 0)
    def _():
        logits(c_last + 1, st1_ref, cm1_ref)
        accumulate(c_last, st0_ref, cm0_ref)
        accumulate(c_last + 1, st1_ref, cm1_ref)

    ot = jnp.concatenate(
        [acc_ref[0:HEAD_DIM, h * TQ:(h + 1) * TQ] / acc_ref[HEAD_DIM:HEAD_DIM + 1, h * TQ:(h + 1) * TQ]
         for h in range(DSA_HEADS)], axis=0)
    o_ref[...] = jnp.dot(ot.T.astype(BF16), wot_ref[...], preferred_element_type=F32)


def _dsa(p1, p2, vt, bias_t, wot, bsz, seq):
    tq = DSA_TQ
    p1v = p1.reshape(bsz, seq, W1)
    p2v = p2.reshape(bsz, seq, W2)
    nq = DSA_HEADS * tq
    in_specs = [
        pl.BlockSpec((None, tq, 768), lambda b, i: (b, i, P1_CQ // 768)),
        pl.BlockSpec((None, tq, 512), lambda b, i: (b, i, P2_IQ // 512)),
        pl.BlockSpec((None, tq, LANES), lambda b, i: (b, i, P2_IW // LANES)),
        pl.BlockSpec((None, seq, LANES), lambda b, i: (b, 0, P1_CK // LANES)),
        pl.BlockSpec((None, seq, LANES), lambda b, i: (b, 0, P2_IK // LANES)),
        pl.BlockSpec((None, seq // DSA_CH, DSA_VR, DSA_CH), lambda b, i: (b, 0, 0, 0)),
        pl.BlockSpec((DSA_Z, nq), lambda b, i: (0, 0)),
        pl.BlockSpec((DSA_HEADS * HEAD_DIM, D_MODEL), lambda b, i: (0, 0)),
    ]
    out = pl.pallas_call(
        functools.partial(_dsa_kernel, topk=min(IDX_TOPK, seq // 4)),
        grid=(bsz, seq // tq),
        in_specs=in_specs,
        out_specs=pl.BlockSpec((None, tq, D_MODEL), lambda b, i: (b, i, 0)),
        out_shape=jax.ShapeDtypeStruct((bsz, seq, D_MODEL), F32),
        scratch_shapes=[
            pltpu.VMEM((seq, tq), I32),
            pltpu.VMEM((seq // DSA_CH, 32, 8, tq), I32),
            pltpu.VMEM((2 * tq, nq), BF16),
            pltpu.VMEM((LANES, IDX_HEADS * tq), BF16),
            pltpu.VMEM((DSA_VR, nq), F32),
            pltpu.VMEM((DSA_CH, nq), F32),
            pltpu.VMEM((DSA_CH, nq), F32),
            pltpu.VMEM((8, nq), F32),
            pltpu.VMEM((8, nq), F32),
            pltpu.VMEM((8, nq), F32),
        ],
        compiler_params=_cparams(("parallel", "arbitrary")),
        name="dsa",
    )(p1v, p2v, p2v, p1v, p2v, vt, bias_t, wot)
    return out.reshape(bsz * seq, D_MODEL)


MERGE_TM = 512


def _merge_kernel(h_ref, oa0_ref, oa1_ref, oa2_ref, sa0_ref, sa1_ref, sa2_ref, ob_ref, yc_ref, g_ref,
                  woa_ref, wob_ref, wo_ref, o_ref, on1_ref, on2_ref, sn1_ref, sn2_ref):
    H = DSWA_HPG
    tm = h_ref.shape[0]
    gw = H * HEAD_DIM
    for (_, r), src_o, src_s, dst_o, dst_s in zip(DSWA_PATTERNS[1:], (oa1_ref, oa2_ref), (sa1_ref, sa2_ref),
                                                  (on1_ref, on2_ref), (sn1_ref, sn2_ref)):
        for c in range(r):
            for k in range(gw // LANES):
                dst_o[k, pl.ds(c, tm // r, stride=r), :] = src_o[:, c * gw + k * LANES:c * gw + (k + 1) * LANES]
            dst_s[pl.ds(c, tm // r, stride=r), :] = src_s[:, c * LANES:(c + 1) * LANES]
    oa_vals = (oa0_ref[...],) + tuple(
        jnp.concatenate([ref[k] for k in range(gw // LANES)], axis=1) for ref in (on1_ref, on2_ref))
    sts = [sa0_ref[...], sn1_ref[...], sn2_ref[...]]
    lane = lax.broadcasted_iota(I32, sts[0].shape, 1)
    mmax = jnp.maximum(jnp.maximum(sts[0], sts[1]), sts[2])
    wts = [pltpu.roll(s, LANES - H, 1) * jnp.exp(s - mmax) for s in sts]
    tot = wts[0] + wts[1] + wts[2]
    hrow = lax.broadcasted_iota(I32, (LANES, H * HEAD_DIM), 0)
    hcol = lax.broadcasted_iota(I32, (LANES, H * HEAD_DIM), 1) // HEAD_DIM
    expand = jnp.where(hrow == hcol, 1.0, 0.0).astype(BF16)
    oa = None
    for g in range(3):
        w = jnp.where(lane < H, wts[g] / tot, 0.0)
        hi = w.astype(BF16)
        lo = (w - hi.astype(F32)).astype(BF16)
        wfull = (jnp.dot(hi, expand, preferred_element_type=F32)
                 + jnp.dot(lo, expand, preferred_element_type=F32))
        term = wfull * oa_vals[g]
        oa = term if oa is None else oa + term
    oa = oa.astype(BF16)
    y_a = jnp.dot(oa, woa_ref[...], preferred_element_type=F32)
    y_b = jnp.dot(ob_ref[...], wob_ref[...], preferred_element_type=F32)
    y_c = yc_ref[...]
    D = D_MODEL
    mix = (g_ref[:, 0:D].astype(F32) * y_a + g_ref[:, D:2 * D].astype(F32) * y_b
           + g_ref[:, 2 * D:3 * D].astype(F32) * y_c)
    o_ref[...] = h_ref[...] + jnp.dot(mix.astype(BF16), wo_ref[...], preferred_element_type=F32)


def _merge(h, oas, sas, ob, yc, gates, woa, wob, wo):
    m = h.shape[0]
    tm = MERGE_TM
    gw = DSWA_HPG * HEAD_DIM
    row = lambda w: pl.BlockSpec((tm, w), lambda i: (i, 0))
    rmrow = lambda w, r: pl.BlockSpec((tm // r, r * w), lambda i: (i, 0))
    full = lambda a: pl.BlockSpec(a.shape, lambda i: (0, 0), pipeline_mode=pl.Buffered(1))
    rs = [r for _, r in DSWA_PATTERNS]
    in_specs = ([row(D_MODEL)] + [rmrow(gw, r) for r in rs] + [rmrow(LANES, r) for r in rs]
                + [row(1024), row(D_MODEL), row(3 * D_MODEL)] + [full(woa), full(wob), full(wo)])
    return pl.pallas_call(
        _merge_kernel,
        grid=(m // tm,),
        in_specs=in_specs,
        out_specs=row(D_MODEL),
        out_shape=jax.ShapeDtypeStruct((m, D_MODEL), F32),
        scratch_shapes=[pltpu.VMEM((gw // LANES, tm, LANES), F32), pltpu.VMEM((gw // LANES, tm, LANES), F32),
                        pltpu.VMEM((tm, LANES), F32), pltpu.VMEM((tm, LANES), F32)],
        compiler_params=_cparams(("parallel",)),
        name="merge",
    )(h, *oas, *sas, ob, yc, gates, woa, wob, wo)


FFN_TM = 512
FFN_TF = 256


def _rms(x, g):
    ms = jnp.mean(x * x, axis=-1, keepdims=True)
    return (x * lax.rsqrt(ms + RMS_EPS) * g).astype(BF16)


def _ffn_ple_kernel(h_ref, p_ref, gf_ref, wg_ref, wu_ref, wd_ref, gp_ref, wpg_ref, wpp_ref, o_ref, *, tf):
    x = h_ref[...]
    u = _rms(x, gf_ref[...])
    acc = None
    for j in range(wg_ref.shape[1] // tf):
        cs = slice(j * tf, (j + 1) * tf)
        a = jnp.dot(u, wg_ref[:, cs], preferred_element_type=F32)
        b = jnp.dot(u, wu_ref[:, cs], preferred_element_type=F32)
        t = (a * jax.nn.sigmoid(a) * b).astype(BF16)
        d = jnp.dot(t, wd_ref[cs, :], preferred_element_type=F32)
        acc = d if acc is None else acc + d
    h2 = x + acc
    e = _rms(h2, gp_ref[...])
    gate = jax.nn.sigmoid(jnp.dot(e, wpg_ref[...], preferred_element_type=F32))
    proj = jnp.dot(p_ref[...].astype(BF16), wpp_ref[...], preferred_element_type=F32)
    o_ref[...] = h2 + gate * proj


def _ffn_ple(h, p, gf, wg, wu, wd, gp, wpg, wpp):
    m, d = h.shape
    tm, tf = FFN_TM, FFN_TF
    const = lambda a: pl.BlockSpec(a.shape, lambda i: (0, 0), pipeline_mode=pl.Buffered(1))
    return pl.pallas_call(
        functools.partial(_ffn_ple_kernel, tf=tf),
        grid=(m // tm,),
        in_specs=[pl.BlockSpec((tm, d), lambda i: (i, 0)), pl.BlockSpec((tm, PLE_DIM), lambda i: (i, 0)),
                  const(gf), const(wg), const(wu), const(wd), const(gp), const(wpg), const(wpp)],
        out_specs=pl.BlockSpec((tm, d), lambda i: (i, 0)),
        out_shape=jax.ShapeDtypeStruct((m, d), F32),
        compiler_params=_cparams(("parallel",)),
        name="ffn_ple",
    )(h, p, gf, wg, wu, wd, gp, wpg, wpp)


def _rel_bucket(dist):
    max_exact = REL_BUCKETS // 2
    d = jnp.maximum(dist, 0)
    df = jnp.maximum(d, 1).astype(F32)
    large = max_exact + (jnp.log(df / max_exact) / math.log(REL_MAX_DIST / max_exact)
                         * (REL_BUCKETS - max_exact)).astype(I32)
    large = jnp.minimum(large, REL_BUCKETS - 1)
    return jnp.where(d < max_exact, d, large)


def _toeplitz(rev, n_rows, n_cols):
    nh = rev.shape[0]
    lw = rev.shape[1] + 1
    w = jnp.pad(rev, ((0, 0), (0, 1)))
    s = jnp.broadcast_to(w[:, None, :], (nh, n_rows, lw)).reshape(nh, n_rows * lw)
    s = s[:, :n_rows * (lw - 1)].reshape(nh, n_rows, lw - 1)
    return s[:, :, n_rows - 1:n_rows - 1 + n_cols]


def _bias_tables(rel_bias):
    blk = DSWA_BLOCK
    dswa = []
    for g, (_, r) in enumerate(DSWA_PATTERNS):
        delta = np.arange(3 * blk - 1)[::-1] - (blk - 1)
        rev = rel_bias[_rel_bucket(jnp.asarray(delta * r, I32))][:, g * DSWA_HPG:(g + 1) * DSWA_HPG]
        dswa.append(_toeplitz(rev.T, blk, 2 * blk))
    tq = DSA_TQ
    dist = np.arange(DSA_Z + tq - 1)[::-1] - (DSA_Z - 1) + DSA_Z0
    rev = rel_bias[_rel_bucket(jnp.asarray(dist, I32))][:, DSWA_HEADS:] * math.log2(math.e)
    bias_t = jnp.transpose(_toeplitz(rev.T, tq, DSA_Z), (2, 0, 1)).reshape(DSA_Z, DSA_HEADS * tq)
    return dswa, bias_t


def _pad_cols(w, width):
    return jnp.pad(w, ((0, 0), (0, width - w.shape[1])))


def _layer_params(w_in, qn_a, kn_a, qn_c, kn_c, w_alpha2, b_alpha):
    offs = np.cumsum((0,) + IN_WIDTHS)
    parts = [w_in[:, offs[i]:offs[i + 1]] for i in range(len(IN_WIDTHS))]
    a_q, a_k, a_v, b_q, b_k, b_v, b_r, b_al, c_q, c_k, c_v, i_q, i_k, i_w = parts
    w1 = _pad_cols(jnp.concatenate([a_q, a_k, c_q, c_k], axis=1), W1).astype(BF16)
    w2 = _pad_cols(jnp.concatenate(
        [b_v, b_r, b_q, b_k, i_q, a_v, _pad_cols(b_al, LANES), _pad_cols(c_v, LANES),
         _pad_cols(i_k, LANES), _pad_cols(i_w, LANES)], axis=1), W2).astype(BF16)
    scale = HEAD_DIM ** -0.5
    gain1 = jnp.concatenate([jnp.tile(qn_a, DSWA_HEADS) * scale, jnp.tile(kn_a, DSWA_HEADS),
                             jnp.tile(qn_c, DSA_HEADS) * (scale * math.log2(math.e)), kn_c])
    gain1 = jnp.pad(gain1, (0, W1 - gain1.shape[0])).reshape(1, W1)
    wa = jnp.pad(w_alpha2, ((0, LANES - GLA_RANK), (0, 0))).astype(BF16)
    return w1, w2, gain1, wa, b_alpha.reshape(1, -1)


def kernel(x, p, rel_bias, norm_mix, w_in, qn_a, kn_a, qn_c, kn_c, w_alpha2, b_alpha, gla_norm, w_out_a, w_out_b, w_out_c, w_gate, b_gate, w_o, norm_ffn, w_ffn_gate, w_ffn_up, w_ffn_down, norm_ple, w_ple_gate, w_ple_proj):
    bsz, seq, d = x.shape
    depth = p.shape[0]
    m = bsz * seq
    dswa_bias, bias_t = _bias_tables(rel_bias)
    h = x.reshape(m, d)
    zeros_w2 = jnp.zeros((1, W2), F32)
    for i in range(depth):
        w1, w2, gain1, wa, ba = _layer_params(w_in[i], qn_a[i], kn_a[i], qn_c[i], kn_c[i], w_alpha2[i], b_alpha[i])
        gmix = norm_mix[i].reshape(1, d)
        gw = DSWA_HPG * HEAD_DIM
        dil = [(g, r) for g, (_, r) in enumerate(DSWA_PATTERNS) if r > 1]
        p1, *qk_rm = _proj(h, gmix, w1, gain1, "qk",
                           regroup=[(off + g * gw, r) for off in (P1_AQ, P1_AK) for g, r in dil])
        p2, *v_rm = _proj(h, gmix, w2, zeros_w2, "plain", regroup=[(P2_AV + g * gw, r) for g, r in dil])
        gates = _proj(h, gmix, w_gate[i].astype(BF16), b_gate[i].reshape(1, -1), "gate")
        rm = {g: (qk_rm[n], qk_rm[len(dil) + n], v_rm[n]) for n, (g, r) in enumerate(dil)}
        oas, sas = [], []
        for g, (_, r) in enumerate(DSWA_PATTERNS):
            o, st = _dswa(p1, p2, rm.get(g), dswa_bias[g], g, r, bsz, seq)
            oas.append(o)
            sas.append(st)
        ob = _gla(p2, wa, ba, gla_norm[i].reshape(1, -1), bsz, seq)
        cv = p2[:, P2_CV:P2_CV + HEAD_DIM].reshape(bsz, seq // DSA_CH, DSA_CH, HEAD_DIM)
        ones_pad = jnp.zeros((bsz, seq // DSA_CH, DSA_VR - HEAD_DIM, DSA_CH), BF16).at[:, :, 0, :].set(1.0)
        vt = jnp.concatenate([jnp.transpose(cv, (0, 1, 3, 2)), ones_pad], axis=2)
        yc = _dsa(p1, p2, vt, bias_t, w_out_c[i].astype(BF16), bsz, seq)
        h = _merge(h, oas, sas, ob, yc, gates, w_out_a[i].astype(BF16), w_out_b[i].astype(BF16),
                   w_o[i].astype(BF16))
        h = _ffn_ple(h, p[i].reshape(m, PLE_DIM), norm_ffn[i].reshape(1, d), w_ffn_gate[i].astype(BF16),
                     w_ffn_up[i].astype(BF16), w_ffn_down[i].astype(BF16), norm_ple[i].reshape(1, d),
                     w_ple_gate[i].astype(BF16), w_ple_proj[i].astype(BF16))
    return h.reshape(bsz, seq, d)
```

```python
import functools
import math

import numpy as np
import jax
import jax.numpy as jnp
from jax import lax
from jax.experimental import pallas as pl
from jax.experimental.pallas import tpu as pltpu

F32 = jnp.float32
BF16 = jnp.bfloat16
I32 = jnp.int32
I16 = jnp.int16

D_MODEL = 1024
HEAD_DIM = 64
RMS_EPS = 1e-6
DSWA_PATTERNS = ((128, 1), (512, 4), (2048, 16))
DSWA_HPG = 4
DSWA_HEADS = 12
DSWA_BLOCK = 128
GLA_HEADS = 4
GLA_DK = 128
GLA_DV = 256
GLA_RANK = 16
GLA_TAU = 16.0
GLA_CHUNK = 64
DSA_HEADS = 12
IDX_HEADS = 8
IDX_DIM = 64
IDX_TOPK = 256
REL_BUCKETS = 32
REL_MAX_DIST = 2048
D_FF = 2816
PLE_DIM = 256
IN_WIDTHS = (768, 768, 768, 512, 512, 1024, 1024, 16, 768, 64, 64, 512, 64, 8)

LANES = 128
MXU_N = 256
VMEM_LIMIT = 56 * 1024 * 1024

W1 = 2560
P1_AQ, P1_AK, P1_CQ, P1_CK = 0, 768, 1536, 2304
W2 = 5120
P2_BV, P2_BR, P2_BQ, P2_BK, P2_IQ, P2_AV = 0, 1024, 2048, 2560, 3072, 3584
P2_BAL, P2_CV, P2_IK, P2_IW = 4352, 4480, 4608, 4736

PROJ_TM = 1024
PROJ_TN = 512

DSWA_QB = 8
DSA_TQ = 128
DSA_CH = 256
DSA_AB = 256
DSA_SRB = 128
DSA_VR = 80
DSA_BIAS_CONST_FROM = 1512
DSA_Z0 = 1792
DSA_Z = DSA_Z0 + DSA_CH
INT_MIN = -(2 ** 31)
NEG_BIG = -1e30


def _cparams(sem, flags=None):
    return pltpu.CompilerParams(dimension_semantics=sem, vmem_limit_bytes=VMEM_LIMIT, flags=flags)


def _proj_kernel(h_ref, g_ref, w_ref, e_ref, *rest, mode, tn, regroup):
    if mode == "qk":
        bd_ref, rest = rest[0], rest[1:]
    o_ref, rg_refs = rest[0], rest[1:1 + len(regroup)]
    scr_ref = rest[-1] if regroup else None
    tm = h_ref.shape[0]
    gw = DSWA_HPG * HEAD_DIM
    x = h_ref[...]
    ms = jnp.mean(x * x, axis=-1, keepdims=True)
    u = (x * lax.rsqrt(ms + RMS_EPS) * g_ref[...]).astype(BF16)
    for j in range(w_ref.shape[1] // tn):
        cs = slice(j * tn, (j + 1) * tn)
        acc = jnp.dot(u, w_ref[:, cs], preferred_element_type=F32)
        if mode == "plain":
            out = acc
        elif mode == "gate":
            out = jax.nn.sigmoid(acc + e_ref[:, cs])
        else:
            sq = acc * acc
            hi = sq.astype(BF16)
            lo = (sq - hi.astype(F32)).astype(BF16)
            bw = bd_ref.shape[0]
            ss = jnp.concatenate(
                [jnp.dot(hi[:, k:k + bw], bd_ref[...], preferred_element_type=F32)
                 + jnp.dot(lo[:, k:k + bw], bd_ref[...], preferred_element_type=F32) for k in range(0, tn, bw)],
                axis=1)
            out = acc * lax.rsqrt(ss * (1.0 / HEAD_DIM) + RMS_EPS) * e_ref[:, cs]
        o_ref[:, cs] = out.astype(o_ref.dtype)
        for (col, r), rg_ref in zip(regroup, rg_refs):
            if col // tn == j:
                for k in range(gw // LANES):
                    scr_ref[k] = out[:, col % tn + k * LANES:col % tn + (k + 1) * LANES]
                for c in range(r):
                    for k in range(gw // LANES):
                        rg_ref[:, c * gw + k * LANES:c * gw + (k + 1) * LANES] = (
                            scr_ref[k, pl.ds(c, tm // r, stride=r), :].astype(rg_ref.dtype))


def _proj(h, g, w, e, mode, regroup=()):
    m, d = h.shape
    n = w.shape[1]
    tm, tn = PROJ_TM, PROJ_TN
    gw = DSWA_HPG * HEAD_DIM
    const = lambda shape: pl.BlockSpec(shape, lambda i: (0, 0), pipeline_mode=pl.Buffered(1))
    in_specs = [pl.BlockSpec((tm, d), lambda i: (i, 0)), const((1, d)), const((d, n)), const((1, n))]
    args = [h, g, w, e]
    if mode == "qk":
        r = np.arange(MXU_N) // HEAD_DIM
        bd = jnp.asarray((r[:, None] == r[None, :]).astype(np.float32), dtype=BF16)
        in_specs.append(const((MXU_N, MXU_N)))
        args.append(bd)
    out_specs = [pl.BlockSpec((tm, n), lambda i: (i, 0))]
    out_shape = [jax.ShapeDtypeStruct((m, n), BF16)]
    for _, r in regroup:
        out_specs.append(pl.BlockSpec((tm // r, r * gw), lambda i: (i, 0)))
        out_shape.append(jax.ShapeDtypeStruct((m // r, r * gw), BF16))
    outs = pl.pallas_call(
        functools.partial(_proj_kernel, mode=mode, tn=tn, regroup=tuple(regroup)),
        grid=(m // tm,),
        in_specs=in_specs,
        out_specs=out_specs,
        out_shape=out_shape,
        scratch_shapes=[pltpu.VMEM((gw // LANES, tm, LANES), F32)] if regroup else [],
        compiler_params=_cparams(("parallel",)),
        name="proj_" + mode,
    )(*args)
    return outs if regroup else outs[0]


def _dswa_kernel(q_ref, kp_ref, kc_ref, vp_ref, vc_ref, bias_ref, o_ref, st_ref, *, qb, cb):
    i = pl.program_id(2)
    blk = DSWA_BLOCK
    row = lax.broadcasted_iota(I32, (blk, 2 * blk), 0)
    col = lax.broadcasted_iota(I32, (blk, 2 * blk), 1)
    cur_ok = (col >= blk) & ((col - blk) <= row)
    valid_inner = ((col < blk) & (col >= row)) | cur_ok
    prev_off = jnp.where(i > 0, 0, 4 * blk)
    valid_first = ((col < blk) & (col >= row + prev_off)) | cur_ok
    lane = lax.broadcasted_iota(I32, (blk, LANES), 1)
    gw = DSWA_HPG * HEAD_DIM
    for cc, j in [(cc, j) for cc in range(cb) for j in range(qb)]:
        rs = slice(j * blk, (j + 1) * blk)
        ps = slice((j - 1) * blk, j * blk)
        gs = slice(cc * gw, (cc + 1) * gw)
        valid = valid_first if j == 0 else valid_inner
        q = q_ref[rs, gs]
        k = jnp.concatenate([kp_ref[:, gs] if j == 0 else kc_ref[ps, gs], kc_ref[rs, gs]], axis=0)
        v = jnp.concatenate([vp_ref[:, gs] if j == 0 else vc_ref[ps, gs], vc_ref[rs, gs]], axis=0)
        stats = jnp.zeros((blk, LANES), F32)
        for hp in range(DSWA_HPG // 2):
            ts = slice(hp * LANES, (hp + 1) * LANES)
            q2, k2, v2 = q[:, ts], k[:, ts], v[:, ts]
            halves = []
            for hh in range(2):
                h = 2 * hp + hh
                qm = jnp.where((lane // HEAD_DIM) == hh, q2, jnp.zeros_like(q2))
                s = lax.dot_general(qm, k2, (((1,), (1,)), ((), ())), preferred_element_type=F32)
                s = jnp.where(valid, s + bias_ref[h], -jnp.inf)
                m = jnp.max(s, axis=-1, keepdims=True)
                p = jnp.exp(s - m)
                l = jnp.sum(p, axis=-1, keepdims=True)
                halves.append(jnp.dot(p.astype(BF16), v2, preferred_element_type=F32) / l)
                stats = jnp.where(lane == h, m, stats)
                stats = jnp.where(lane == DSWA_HPG + h, l, stats)
            o_ref[rs, cc * gw + hp * LANES:cc * gw + (hp + 1) * LANES] = jnp.where(
                lane < HEAD_DIM, halves[0], halves[1])
        st_ref[rs, cc * LANES:(cc + 1) * LANES] = stats


def _dswa(p1, p2, rm, bias, g, r, bsz, seq):
    blk = DSWA_BLOCK
    L = seq // r
    nblk = L // blk
    gw = DSWA_HPG * HEAD_DIM
    if r == 1:
        p1v = p1.reshape(bsz, L, W1)
        p2v = p2.reshape(bsz, L, W2)
        qv, kv, vv = p1v, p1v, p2v
        q_col = lambda c: (P1_AQ // gw) + g
        k_col = lambda c: (P1_AK // gw) + g
        v_col = lambda c: (P2_AV // gw) + g
    else:
        qv, kv, vv = (a.reshape(bsz, L, r * gw) for a in rm)
        q_col = k_col = v_col = lambda c: c
    qb = min(DSWA_QB, nblk)
    cb = min(max(DSWA_QB // qb, 1), r)
    prev = lambda i: jnp.maximum(i * qb - 1, 0)
    in_specs = [
        pl.BlockSpec((None, qb * blk, cb * gw), lambda b, c, i: (b, i, q_col(c))),
        pl.BlockSpec((None, blk, cb * gw), lambda b, c, i: (b, prev(i), k_col(c))),
        pl.BlockSpec((None, qb * blk, cb * gw), lambda b, c, i: (b, i, k_col(c))),
        pl.BlockSpec((None, blk, cb * gw), lambda b, c, i: (b, prev(i), v_col(c))),
        pl.BlockSpec((None, qb * blk, cb * gw), lambda b, c, i: (b, i, v_col(c))),
        pl.BlockSpec((DSWA_HPG, blk, 2 * blk), lambda b, c, i: (0, 0, 0)),
    ]
    out_specs = [
        pl.BlockSpec((None, qb * blk, cb * gw), lambda b, c, i: (b, i, c)),
        pl.BlockSpec((None, qb * blk, cb * LANES), lambda b, c, i: (b, i, c)),
    ]
    o, st = pl.pallas_call(
        functools.partial(_dswa_kernel, qb=qb, cb=cb),
        grid=(bsz, r // cb, nblk // qb),
        in_specs=in_specs,
        out_specs=out_specs,
        out_shape=[jax.ShapeDtypeStruct((bsz, L, r * gw), F32),
                   jax.ShapeDtypeStruct((bsz, L, r * LANES), F32)],
        compiler_params=_cparams(("parallel", "parallel", "arbitrary")),
        name="dswa_g%d" % g,
    )(qv, kv, kv, vv, vv, bias)
    return o.reshape(bsz * L, r * gw), st.reshape(bsz * L, r * LANES)


GLA_TC = 256
GLA_GB = 4


def _split3(x):
    a1 = x.astype(BF16)
    r1 = x - a1.astype(F32)
    a2 = r1.astype(BF16)
    a3 = (r1 - a2.astype(F32)).astype(BF16)
    return a1, a2, a3


def _gla_kernel(v_ref, r_ref, q_ref, k_ref, al_ref, wa_ref, ba_ref, gn_ref, o_ref, st_ref, *, gb):
    C = GLA_CHUNK

    @pl.when(pl.program_id(1) == 0)
    def _():
        st_ref[...] = jnp.zeros_like(st_ref)

    TC = GLA_TC
    NC = TC // C
    row = lax.broadcasted_iota(I32, (TC, TC), 0)
    col = lax.broadcasted_iota(I32, (TC, TC), 1)
    tri = ((row // C) == (col // C)) & (row >= col)
    tri_bf = jnp.where(tri, 1.0, 0.0).astype(BF16)
    nt = (((1,), (1,)), ((), ()))
    tn = (((0,), (0,)), ((), ()))
    for bb in range(gb):
        z = jnp.dot(al_ref[bb], wa_ref[...], preferred_element_type=F32) + ba_ref[...]
        la = (jnp.minimum(z, 0.0) - jnp.log(1.0 + jnp.exp(-jnp.abs(z)))) * (1.0 / GLA_TAU)
        a1, a2, a3 = _split3(la)
        bcum = (jnp.dot(tri_bf, a1, preferred_element_type=F32)
                + jnp.dot(tri_bf, a2, preferred_element_type=F32)
                + jnp.dot(tri_bf, a3, preferred_element_type=F32))
        blast = [bcum[(c + 1) * C - 1:(c + 1) * C, :] for c in range(NC)]
        dec = [jnp.exp(b) for b in blast]
        dec_rows = jnp.concatenate([jnp.broadcast_to(d, (C, d.shape[1])) for d in dec], axis=0)
        qf = q_ref[bb].astype(F32) * (GLA_DK ** -0.5)
        k_dec = k_ref[bb].astype(F32) * jnp.exp(-bcum)
        q_in = (qf * jnp.exp(bcum)).astype(BF16)
        k_in = k_dec.astype(BF16)
        k_end = (k_dec * dec_rows).astype(BF16)
        for h in range(GLA_HEADS):
            ks = slice(h * GLA_DK, (h + 1) * GLA_DK)
            vs = slice(h * GLA_DV, (h + 1) * GLA_DV)
            vh = v_ref[bb, :, vs]
            att = lax.dot_general(q_in[:, ks], k_in[:, ks], nt, preferred_element_type=F32)
            att = jnp.where(tri, att, 0.0)
            o = jnp.dot(att.astype(BF16), vh, preferred_element_type=F32)
            st = st_ref[bb, h]
            inter = []
            for c in range(NC):
                rs = slice(c * C, (c + 1) * C)
                inter.append(lax.dot_general(q_in[rs, ks], st.astype(BF16), nt, preferred_element_type=F32))
                st = st * dec[c][:, ks] + lax.dot_general(vh[rs, :], k_end[rs, ks], tn, preferred_element_type=F32)
            st_ref[bb, h] = st
            o = o + jnp.concatenate(inter, axis=0)
            ms = jnp.mean(o * o, axis=-1, keepdims=True)
            y = o * lax.rsqrt(ms + RMS_EPS) * gn_ref[...]
            rg = r_ref[bb, :, vs].astype(F32)
            y = y * (rg * jax.nn.sigmoid(rg))
            o_ref[bb, :, vs] = y.astype(o_ref.dtype)


def _gla(p2, wa, ba, gn, bsz, seq):
    tc = GLA_TC
    gb = GLA_GB if bsz % GLA_GB == 0 else 1
    p2v = p2.reshape(bsz, seq, W2)
    in_specs = [
        pl.BlockSpec((gb, tc, 1024), lambda b, t: (b, t, P2_BV // 1024)),
        pl.BlockSpec((gb, tc, 1024), lambda b, t: (b, t, P2_BR // 1024)),
        pl.BlockSpec((gb, tc, 512), lambda b, t: (b, t, P2_BQ // 512)),
        pl.BlockSpec((gb, tc, 512), lambda b, t: (b, t, P2_BK // 512)),
        pl.BlockSpec((gb, tc, LANES), lambda b, t: (b, t, P2_BAL // LANES)),
        pl.BlockSpec((LANES, 512), lambda b, t: (0, 0)),
        pl.BlockSpec((1, 512), lambda b, t: (0, 0)),
        pl.BlockSpec((1, GLA_DV), lambda b, t: (0, 0)),
    ]
    out = pl.pallas_call(
        functools.partial(_gla_kernel, gb=gb),
        grid=(bsz // gb, seq // tc),
        in_specs=in_specs,
        out_specs=pl.BlockSpec((gb, tc, 1024), lambda b, t: (b, t, 0)),
        out_shape=jax.ShapeDtypeStruct((bsz, seq, 1024), BF16),
        scratch_shapes=[pltpu.VMEM((gb, GLA_HEADS, GLA_DV, GLA_DK), F32)],
        compiler_params=_cparams(("parallel", "arbitrary")),
        name="gla",
    )(p2v, p2v, p2v, p2v, p2v, wa, ba, gn)
    return out.reshape(bsz * seq, 1024)


def _tree_sum(xs):
    xs = list(xs)
    while len(xs) > 1:
        xs = [xs[i] + xs[i + 1] for i in range(0, len(xs) - 1, 2)] + ([xs[-1]] if len(xs) % 2 else [])
    return xs[0]


def _dsa_kernel(q_ref, iq_ref, iw_ref, k_ref, ik_ref, vt_ref, bias_ref, wot_ref, o_ref,
                keys_ref, planes_ref, qaug_ref, iqall_ref, acc_ref, st0_ref, st1_ref, m_ref, cm0_ref, cm1_ref,
                *, topk):
    TQ, CH = DSA_TQ, DSA_CH
    CC = 2 * CH
    qi = pl.program_id(1)
    t0 = qi * TQ
    nch = qi // (CH // TQ) + 1
    npair = (nch + 1) // 2
    nt = (((1,), (1,)), ((), ()))

    eye = (lax.broadcasted_iota(I32, (TQ, TQ), 0) == lax.broadcasted_iota(I32, (TQ, TQ), 1))
    eye = jnp.where(eye, 1.0, 0.0).astype(BF16)
    q_t = q_ref[...].astype(F32).T.astype(BF16)
    iq_t = iq_ref[...].astype(F32).T.astype(BF16)
    for h in range(DSA_HEADS):
        cs = slice(h * TQ, (h + 1) * TQ)
        qaug_ref[0:TQ, cs] = eye
        qaug_ref[TQ:TQ + HEAD_DIM, cs] = q_t[h * HEAD_DIM:(h + 1) * HEAD_DIM, :]
        qaug_ref[TQ + HEAD_DIM:, cs] = jnp.zeros((TQ - HEAD_DIM, TQ), BF16)
    for h in range(IDX_HEADS):
        cs = slice(h * TQ, (h + 1) * TQ)
        iqall_ref[0:IDX_DIM, cs] = iq_t[h * IDX_DIM:(h + 1) * IDX_DIM, :]
        iqall_ref[IDX_DIM:, cs] = jnp.zeros((LANES - IDX_DIM, TQ), BF16)
    idx_scale = (IDX_HEADS ** -0.5) * (IDX_DIM ** -0.5)
    wt = (iw_ref[...].astype(F32) * idx_scale).T

    qpos = t0 + lax.broadcasted_iota(I32, (1, TQ), 1)
    RB = 128
    krow1 = lax.broadcasted_iota(I32, (RB, TQ), 0)
    krow = lax.broadcasted_iota(I32, (CH, TQ), 0)
    krow2 = lax.broadcasted_iota(I32, (CC, TQ), 0)

    SRB = DSA_SRB

    def score_block(s0, RB=SRB):
        s0 = pl.multiple_of(s0, RB)
        krow1 = lax.broadcasted_iota(I32, (RB, TQ), 0)
        ikc = ik_ref[pl.ds(s0, RB), :]
        s = None
        for hp in range(IDX_HEADS // 2):
            x = jnp.dot(ikc, iqall_ref[:, hp * 2 * TQ:(hp + 1) * 2 * TQ],
                        preferred_element_type=F32)
            t = (jnp.maximum(x[:, :TQ], 0.0) * wt[2 * hp:2 * hp + 1, :]
                 + jnp.maximum(x[:, TQ:], 0.0) * wt[2 * hp + 1:2 * hp + 2, :])
            s = t if s is None else s + t
        s = jnp.where(s == 0.0, 0.0, s)
        bits = pltpu.bitcast(s, I32)
        key = bits ^ ((bits >> 31) & 0x7FFFFFFF)
        key = jnp.where(s0 + krow1 <= qpos, key, INT_MIN)
        keys_ref[pl.ds(s0, RB), :] = key

    def score_trips(first, count, width):
        def body(t, carry):
            for sub in range(width * CH // SRB):
                score_block((first + t * width) * CH + sub * SRB)
            return carry
        lax.fori_loop(0, count, body, 0)

    n4 = nch // 4
    n2 = (nch - 4 * n4) // 2
    score_trips(0, n4, 4)
    score_trips(4 * n4, n2, 2)
    score_trips(4 * n4 + 2 * n2, nch - 4 * n4 - 2 * n2, 1)

    @pl.when(nch % 2 == 1)
    def _():
        keys_ref[pl.ds(pl.multiple_of(nch * CH, CH), CH), :] = jnp.full((CH, TQ), INT_MIN, I32)

    kvec = jnp.minimum(topk, qpos + 1)
    NPL = 32

    def planes_body(c, carry):
        s0 = pl.multiple_of(c * CH, CH)
        a = [keys_ref[pl.ds(s0 + 8 * j, 8), :] ^ INT_MIN for j in range(NPL)]
        j, m = 16, 0x0000FFFF
        while j:
            sh = jnp.full((8, TQ), j, I32)
            k = 0
            while k < NPL:
                t = (a[k] ^ lax.shift_right_logical(a[k + j], sh)) & m
                a[k] = a[k] ^ t
                a[k + j] = a[k + j] ^ lax.shift_left(t, sh)
                k = (k + j + 1) & ~j
            j >>= 1
            m = (m ^ (m << j)) & 0xFFFFFFFF
        for p in range(NPL):
            planes_ref[c, p] = a[p]
        return carry

    def empty_body(c, carry):
        for p in range(NPL):
            planes_ref[c, p] = jnp.zeros((8, TQ), I32)
        return carry

    NCK = keys_ref.shape[0] // CH
    short = nch <= NCK // 2
    lax.fori_loop(0, nch, planes_body, 0)
    lax.fori_loop(nch, jnp.where(short, NCK // 2, NCK), empty_body, 0)

    def radix_select(nc):
        def run():
            def lane_count(words):
                return jnp.sum(_tree_sum([lax.population_count(w) for w in words]), axis=0, keepdims=True)

            def plane_body(t, carry):
                thr_u, n_gt, alive = carry
                p = 2 * t
                h1 = [alive[c] & planes_ref[c, p] for c in range(nc)]
                h0 = [alive[c] ^ h1[c] for c in range(nc)]
                h11 = [h1[c] & planes_ref[c, p + 1] for c in range(nc)]
                h01 = [h0[c] & planes_ref[c, p + 1] for c in range(nc)]
                c1, c11, c01 = lane_count(h1), lane_count(h11), lane_count(h01)
                take1 = (n_gt + c1) >= kvec
                n_mid = jnp.where(take1, n_gt, n_gt + c1)
                ones2 = jnp.where(take1, c11, c01)
                take2 = (n_mid + ones2) >= kvec
                n_gt = jnp.where(take2, n_mid, n_mid + ones2)
                thr_u = (thr_u | jnp.where(take1, jnp.int32(1) << (31 - p), 0)
                         | jnp.where(take2, jnp.int32(1) << (30 - p), 0))
                alive = tuple(
                    jnp.where(take1, jnp.where(take2, h11[c], h1[c] ^ h11[c]), jnp.where(take2, h01[c], h0[c] ^ h01[c]))
                    for c in range(nc))
                return thr_u, n_gt, alive

            zero = jnp.zeros((1, TQ), I32)
            alive0 = tuple(jnp.full((8, TQ), -1, I32) for _ in range(nc))
            thr_u, n_gt, alive = lax.fori_loop(0, NPL // 2, plane_body, (zero, zero, alive0))
            n_eq = jnp.sum(_tree_sum([lax.population_count(a) for a in alive]), axis=0, keepdims=True)
            return thr_u, n_gt, n_eq
        return run

    thr_u, n_gt, n_eq = lax.cond(short, radix_select(NCK // 2), radix_select(NCK))
    thr = thr_u ^ INT_MIN

    def count(pred_fn):
        def body(c, acc):
            s0 = pl.multiple_of(c * CC, CC)
            kk = keys_ref[pl.ds(s0, CC), :]
            hit = jnp.where(pred_fn(kk, s0), 1, 0).astype(I32)
            return acc + jnp.sum(hit.reshape(CC // 8, 8, TQ), axis=0)
        acc = lax.fori_loop(0, npair, body, jnp.zeros((8, TQ), I32))
        return jnp.sum(acc, axis=0, keepdims=True)

    excess = n_gt + n_eq - kvec
    has_excess = jnp.max(excess) > 0

    @pl.when(has_excess)
    def _():
        need = kvec - n_gt

        def tie_lt(cut):
            return count(lambda kk, s0: (kk == thr) & (s0 + krow2 < cut))

        def cut_body(b, cut):
            cand = cut | (jnp.int32(1) << (12 - b))
            return jnp.where(tie_lt(cand) <= need, cand, cut)

        cut = lax.fori_loop(0, 13, cut_body, jnp.zeros((1, TQ), I32))

        def drop_body(c, carry):
            s0 = pl.multiple_of(c * CH, CH)
            kk = keys_ref[pl.ds(s0, CH), :]
            keys_ref[pl.ds(s0, CH), :] = jnp.where((kk == thr) & (s0 + krow >= cut), INT_MIN, kk)
            return carry

        lax.fori_loop(0, nch, drop_body, 0)

    acc_ref[...] = jnp.zeros_like(acc_ref)

    NP = DSA_HEADS // 2
    PW = 2 * TQ

    AB = DSA_AB
    NB = CH // AB
    last_chunk = keys_ref.shape[0] // CH - 1

    def logits(c, st_ref, cm_ref):
        s0 = pl.multiple_of(c * CH, CH)
        zoff = pl.multiple_of(jnp.clip(s0 - t0 + DSA_Z0, 0, DSA_Z0), 8)
        kaug = []
        for rb in range(NB):
            rs = pl.ds(s0 + rb * AB, AB)
            pen = jnp.where(keys_ref[rs, :] >= thr, 0.0, NEG_BIG).astype(BF16)
            kaug.append(jnp.concatenate([pen, k_ref[rs, :]], axis=1))
        for hp in range(NP):
            ps = slice(hp * PW, (hp + 1) * PW)
            cm = None
            for rb in range(NB):
                st = jnp.dot(kaug[rb], qaug_ref[:, ps], preferred_element_type=F32)
                st = st + bias_ref[pl.ds(zoff + rb * AB, AB), ps]
                st_ref[rb * AB:(rb + 1) * AB, ps] = st
                tm = jnp.max(st.reshape(AB // 8, 8, PW), axis=0)
                cm = tm if cm is None else jnp.maximum(cm, tm)
            cm_ref[0:1, ps] = jnp.max(cm, axis=0, keepdims=True)

    def accumulate(c, st_ref, cm_ref):
        vtc = vt_ref[c]
        for hp in range(NP):
            ps = slice(hp * PW, (hp + 1) * PW)
            m_old = m_ref[0:1, ps]
            m_new = jnp.maximum(m_old, cm_ref[0:1, ps])
            m_ref[0:1, ps] = m_new
            alpha = jnp.exp2(m_old - m_new)
            p = jnp.concatenate(
                [jnp.exp2(st_ref[rb * AB:(rb + 1) * AB, ps] - m_new).astype(BF16) for rb in range(NB)], axis=0)
            acc_ref[:, ps] = acc_ref[:, ps] * alpha + jnp.dot(vtc, p, preferred_element_type=F32)

    def att_body(cp, carry):
        c0 = 2 * cp
        logits(c0 + 1, st1_ref, cm1_ref)
        accumulate(c0, st0_ref, cm0_ref)
        logits(c0 + 2, st0_ref, cm0_ref)
        accumulate(c0 + 1, st1_ref, cm1_ref)
        return carry

    m_ref[...] = jnp.full(m_ref.shape, NEG_BIG, F32)
    logits(0, st0_ref, cm0_ref)
    nfull = (nch - 1) // 2
    lax.fori_loop(0, nfull, att_body, 0)
    c_last = 2 * nfull

    @pl.when(nch % 2 == 1)
    def _():
        accumulate(c_last, st0_ref, cm0_ref)

    @pl.when(nch % 2 == 0)
    def _():
        logits(c_last + 1, st1_ref, cm1_ref)
        accumulate(c_last, st0_ref, cm0_ref)
        accumulate(c_last + 1, st1_ref, cm1_ref)

    ot = jnp.concatenate(
        [acc_ref[0:HEAD_DIM, h * TQ:(h + 1) * TQ] / acc_ref[HEAD_DIM:HEAD_DIM + 1, h * TQ:(h + 1) * TQ]
         for h in range(DSA_HEADS)], axis=0)
    o_ref[...] = jnp.dot(ot.T.astype(BF16), wot_ref[...], preferred_element_type=F32)


def _dsa(p1, p2, vt, bias_t, wot, bsz, seq):
    tq = DSA_TQ
    p1v = p1.reshape(bsz, seq, W1)
    p2v = p2.reshape(bsz, seq, W2)
    nq = DSA_HEADS * tq
    in_specs = [
        pl.BlockSpec((None, tq, 768), lambda b, i: (b, i, P1_CQ // 768)),
        pl.BlockSpec((None, tq, 512), lambda b, i: (b, i, P2_IQ // 512)),
        pl.BlockSpec((None, tq, LANES), lambda b, i: (b, i, P2_IW // LANES)),
        pl.BlockSpec((None, seq, LANES), lambda b, i: (b, 0, P1_CK // LANES)),
        pl.BlockSpec((None, seq, LANES), lambda b, i: (b, 0, P2_IK // LANES)),
        pl.BlockSpec((None, seq // DSA_CH, DSA_VR, DSA_CH), lambda b, i: (b, 0, 0, 0)),
        pl.BlockSpec((DSA_Z, nq), lambda b, i: (0, 0)),
        pl.BlockSpec((DSA_HEADS * HEAD_DIM, D_MODEL), lambda b, i: (0, 0)),
    ]
    out = pl.pallas_call(
        functools.partial(_dsa_kernel, topk=min(IDX_TOPK, seq // 4)),
        grid=(bsz, seq // tq),
        in_specs=in_specs,
        out_specs=pl.BlockSpec((None, tq, D_MODEL), lambda b, i: (b, i, 0)),
        out_shape=jax.ShapeDtypeStruct((bsz, seq, D_MODEL), F32),
        scratch_shapes=[
            pltpu.VMEM((seq, tq), I32),
            pltpu.VMEM((seq // DSA_CH, 32, 8, tq), I32),
            pltpu.VMEM((2 * tq, nq), BF16),
            pltpu.VMEM((LANES, IDX_HEADS * tq), BF16),
            pltpu.VMEM((DSA_VR, nq), F32),
            pltpu.VMEM((DSA_CH, nq), F32),
            pltpu.VMEM((DSA_CH, nq), F32),
            pltpu.VMEM((8, nq), F32),
            pltpu.VMEM((8, nq), F32),
            pltpu.VMEM((8, nq), F32),
        ],
        compiler_params=_cparams(("parallel", "arbitrary")),
        name="dsa",
    )(p1v, p2v, p2v, p1v, p2v, vt, bias_t, wot)
    return out.reshape(bsz * seq, D_MODEL)


MERGE_TM = 512


def _merge_kernel(h_ref, oa0_ref, oa1_ref, oa2_ref, sa0_ref, sa1_ref, sa2_ref, ob_ref, yc_ref, g_ref,
                  woa_ref, wob_ref, wo_ref, o_ref, on1_ref, on2_ref, sn1_ref, sn2_ref):
    H = DSWA_HPG
    tm = h_ref.shape[0]
    gw = H * HEAD_DIM
    for (_, r), src_o, src_s, dst_o, dst_s in zip(DSWA_PATTERNS[1:], (oa1_ref, oa2_ref), (sa1_ref, sa2_ref),
                                                  (on1_ref, on2_ref), (sn1_ref, sn2_ref)):
        for c in range(r):
            for k in range(gw // LANES):
                dst_o[k, pl.ds(c, tm // r, stride=r), :] = src_o[:, c * gw + k * LANES:c * gw + (k + 1) * LANES]
            dst_s[pl.ds(c, tm // r, stride=r), :] = src_s[:, c * LANES:(c + 1) * LANES]
    oa_vals = (oa0_ref[...],) + tuple(
        jnp.concatenate([ref[k] for k in range(gw // LANES)], axis=1) for ref in (on1_ref, on2_ref))
    sts = [sa0_ref[...], sn1_ref[...], sn2_ref[...]]
    lane = lax.broadcasted_iota(I32, sts[0].shape, 1)
    mmax = jnp.maximum(jnp.maximum(sts[0], sts[1]), sts[2])
    wts = [pltpu.roll(s, LANES - H, 1) * jnp.exp(s - mmax) for s in sts]
    tot = wts[0] + wts[1] + wts[2]
    hrow = lax.broadcasted_iota(I32, (LANES, H * HEAD_DIM), 0)
    hcol = lax.broadcasted_iota(I32, (LANES, H * HEAD_DIM), 1) // HEAD_DIM
    expand = jnp.where(hrow == hcol, 1.0, 0.0).astype(BF16)
    oa = None
    for g in range(3):
        w = jnp.where(lane < H, wts[g] / tot, 0.0)
        hi = w.astype(BF16)
        lo = (w - hi.astype(F32)).astype(BF16)
        wfull = (jnp.dot(hi, expand, preferred_element_type=F32)
                 + jnp.dot(lo, expand, preferred_element_type=F32))
        term = wfull * oa_vals[g]
        oa = term if oa is None else oa + term
    oa = oa.astype(BF16)
    y_a = jnp.dot(oa, woa_ref[...], preferred_element_type=F32)
    y_b = jnp.dot(ob_ref[...], wob_ref[...], preferred_element_type=F32)
    y_c = yc_ref[...]
    D = D_MODEL
    mix = (g_ref[:, 0:D].astype(F32) * y_a + g_ref[:, D:2 * D].astype(F32) * y_b
           + g_ref[:, 2 * D:3 * D].astype(F32) * y_c)
    o_ref[...] = h_ref[...] + jnp.dot(mix.astype(BF16), wo_ref[...], preferred_element_type=F32)


def _merge(h, oas, sas, ob, yc, gates, woa, wob, wo):
    m = h.shape[0]
    tm = MERGE_TM
    gw = DSWA_HPG * HEAD_DIM
    row = lambda w: pl.BlockSpec((tm, w), lambda i: (i, 0))
    rmrow = lambda w, r: pl.BlockSpec((tm // r, r * w), lambda i: (i, 0))
    full = lambda a: pl.BlockSpec(a.shape, lambda i: (0, 0), pipeline_mode=pl.Buffered(1))
    rs = [r for _, r in DSWA_PATTERNS]
    in_specs = ([row(D_MODEL)] + [rmrow(gw, r) for r in rs] + [rmrow(LANES, r) for r in rs]
                + [row(1024), row(D_MODEL), row(3 * D_MODEL)] + [full(woa), full(wob), full(wo)])
    return pl.pallas_call(
        _merge_kernel,
        grid=(m // tm,),
        in_specs=in_specs,
        out_specs=row(D_MODEL),
        out_shape=jax.ShapeDtypeStruct((m, D_MODEL), F32),
        scratch_shapes=[pltpu.VMEM((gw // LANES, tm, LANES), F32), pltpu.VMEM((gw // LANES, tm, LANES), F32),
                        pltpu.VMEM((tm, LANES), F32), pltpu.VMEM((tm, LANES), F32)],
        compiler_params=_cparams(("parallel",)),
        name="merge",
    )(h, *oas, *sas, ob, yc, gates, woa, wob, wo)


FFN_TM = 512
FFN_TF = 256


def _rms(x, g):
    ms = jnp.mean(x * x, axis=-1, keepdims=True)
    return (x * lax.rsqrt(ms + RMS_EPS) * g).astype(BF16)


def _ffn_ple_kernel(h_ref, p_ref, gf_ref, wg_ref, wu_ref, wd_ref, gp_ref, wpg_ref, wpp_ref, o_ref, *, tf):
    x = h_ref[...]
    u = _rms(x, gf_ref[...])
    acc = None
    for j in range(wg_ref.shape[1] // tf):
        cs = slice(j * tf, (j + 1) * tf)
        a = jnp.dot(u, wg_ref[:, cs], preferred_element_type=F32)
        b = jnp.dot(u, wu_ref[:, cs], preferred_element_type=F32)
        t = (a * jax.nn.sigmoid(a) * b).astype(BF16)
        d = jnp.dot(t, wd_ref[cs, :], preferred_element_type=F32)
        acc = d if acc is None else acc + d
    h2 = x + acc
    e = _rms(h2, gp_ref[...])
    gate = jax.nn.sigmoid(jnp.dot(e, wpg_ref[...], preferred_element_type=F32))
    proj = jnp.dot(p_ref[...].astype(BF16), wpp_ref[...], preferred_element_type=F32)
    o_ref[...] = h2 + gate * proj


def _ffn_ple(h, p, gf, wg, wu, wd, gp, wpg, wpp):
    m, d = h.shape
    tm, tf = FFN_TM, FFN_TF
    const = lambda a: pl.BlockSpec(a.shape, lambda i: (0, 0), pipeline_mode=pl.Buffered(1))
    return pl.pallas_call(
        functools.partial(_ffn_ple_kernel, tf=tf),
        grid=(m // tm,),
        in_specs=[pl.BlockSpec((tm, d), lambda i: (i, 0)), pl.BlockSpec((tm, PLE_DIM), lambda i: (i, 0)),
                  const(gf), const(wg), const(wu), const(wd), const(gp), const(wpg), const(wpp)],
        out_specs=pl.BlockSpec((tm, d), lambda i: (i, 0)),
        out_shape=jax.ShapeDtypeStruct((m, d), F32),
        compiler_params=_cparams(("parallel",)),
        name="ffn_ple",
    )(h, p, gf, wg, wu, wd, gp, wpg, wpp)


def _rel_bucket(dist):
    max_exact = REL_BUCKETS // 2
    d = jnp.maximum(dist, 0)
    df = jnp.maximum(d, 1).astype(F32)
    large = max_exact + (jnp.log(df / max_exact) / math.log(REL_MAX_DIST / max_exact)
                         * (REL_BUCKETS - max_exact)).astype(I32)
    large = jnp.minimum(large, REL_BUCKETS - 1)
    return jnp.where(d < max_exact, d, large)


def _toeplitz(rev, n_rows, n_cols):
    nh = rev.shape[0]
    lw = rev.shape[1] + 1
    w = jnp.pad(rev, ((0, 0), (0, 1)))
    s = jnp.broadcast_to(w[:, None, :], (nh, n_rows, lw)).reshape(nh, n_rows * lw)
    s = s[:, :n_rows * (lw - 1)].reshape(nh, n_rows, lw - 1)
    return s[:, :, n_rows - 1:n_rows - 1 + n_cols]


def _bias_tables(rel_bias):
    blk = DSWA_BLOCK
    dswa = []
    for g, (_, r) in enumerate(DSWA_PATTERNS):
        delta = np.arange(3 * blk - 1)[::-1] - (blk - 1)
        rev = rel_bias[_rel_bucket(jnp.asarray(delta * r, I32))][:, g * DSWA_HPG:(g + 1) * DSWA_HPG]
        dswa.append(_toeplitz(rev.T, blk, 2 * blk))
    tq = DSA_TQ
    dist = np.arange(DSA_Z + tq - 1)[::-1] - (DSA_Z - 1) + DSA_Z0
    rev = rel_bias[_rel_bucket(jnp.asarray(dist, I32))][:, DSWA_HEADS:] * math.log2(math.e)
    bias_t = jnp.transpose(_toeplitz(rev.T, tq, DSA_Z), (2, 0, 1)).reshape(DSA_Z, DSA_HEADS * tq)
    return dswa, bias_t


def _pad_cols(w, width):
    return jnp.pad(w, ((0, 0), (0, width - w.shape[1])))


def _layer_params(w_in, qn_a, kn_a, qn_c, kn_c, w_alpha2, b_alpha):
    offs = np.cumsum((0,) + IN_WIDTHS)
    parts = [w_in[:, offs[i]:offs[i + 1]] for i in range(len(IN_WIDTHS))]
    a_q, a_k, a_v, b_q, b_k, b_v, b_r, b_al, c_q, c_k, c_v, i_q, i_k, i_w = parts
    w1 = _pad_cols(jnp.concatenate([a_q, a_k, c_q, c_k], axis=1), W1).astype(BF16)
    w2 = _pad_cols(jnp.concatenate(
        [b_v, b_r, b_q, b_k, i_q, a_v, _pad_cols(b_al, LANES), _pad_cols(c_v, LANES),
         _pad_cols(i_k, LANES), _pad_cols(i_w, LANES)], axis=1), W2).astype(BF16)
    scale = HEAD_DIM ** -0.5
    gain1 = jnp.concatenate([jnp.tile(qn_a, DSWA_HEADS) * scale, jnp.tile(kn_a, DSWA_HEADS),
                             jnp.tile(qn_c, DSA_HEADS) * (scale * math.log2(math.e)), kn_c])
    gain1 = jnp.pad(gain1, (0, W1 - gain1.shape[0])).reshape(1, W1)
    wa = jnp.pad(w_alpha2, ((0, LANES - GLA_RANK), (0, 0))).astype(BF16)
    return w1, w2, gain1, wa, b_alpha.reshape(1, -1)


def kernel(x, p, rel_bias, norm_mix, w_in, qn_a, kn_a, qn_c, kn_c, w_alpha2, b_alpha, gla_norm, w_out_a, w_out_b, w_out_c, w_gate, b_gate, w_o, norm_ffn, w_ffn_gate, w_ffn_up, w_ffn_down, norm_ple, w_ple_gate, w_ple_proj):
    bsz, seq, d = x.shape
    depth = p.shape[0]
    m = bsz * seq
    dswa_bias, bias_t = _bias_tables(rel_bias)
    h = x.reshape(m, d)
    zeros_w2 = jnp.zeros((1, W2), F32)
    for i in range(depth):
        w1, w2, gain1, wa, ba = _layer_params(w_in[i], qn_a[i], kn_a[i], qn_c[i], kn_c[i], w_alpha2[i], b_alpha[i])
        gmix = norm_mix[i].reshape(1, d)
        gw = DSWA_HPG * HEAD_DIM
        dil = [(g, r) for g, (_, r) in enumerate(DSWA_PATTERNS) if r > 1]
        p1, *qk_rm = _proj(h, gmix, w1, gain1, "qk",
                           regroup=[(off + g * gw, r) for off in (P1_AQ, P1_AK) for g, r in dil])
        p2, *v_rm = _proj(h, gmix, w2, zeros_w2, "plain", regroup=[(P2_AV + g * gw, r) for g, r in dil])
        gates = _proj(h, gmix, w_gate[i].astype(BF16), b_gate[i].reshape(1, -1), "gate")
        rm = {g: (qk_rm[n], qk_rm[len(dil) + n], v_rm[n]) for n, (g, r) in enumerate(dil)}
        oas, sas = [], []
        for g, (_, r) in enumerate(DSWA_PATTERNS):
            o, st = _dswa(p1, p2, rm.get(g), dswa_bias[g], g, r, bsz, seq)
            oas.append(o)
            sas.append(st)
        ob = _gla(p2, wa, ba, gla_norm[i].reshape(1, -1), bsz, seq)
        cv = p2[:, P2_CV:P2_CV + HEAD_DIM].reshape(bsz, seq // DSA_CH, DSA_CH, HEAD_DIM)
        ones_pad = jnp.zeros((bsz, seq // DSA_CH, DSA_VR - HEAD_DIM, DSA_CH), BF16).at[:, :, 0, :].set(1.0)
        vt = jnp.concatenate([jnp.transpose(cv, (0, 1, 3, 2)), ones_pad], axis=2)
        yc = _dsa(p1, p2, vt, bias_t, w_out_c[i].astype(BF16), bsz, seq)
        h = _merge(h, oas, sas, ob, yc, gates, w_out_a[i].astype(BF16), w_out_b[i].astype(BF16),
                   w_o[i].astype(BF16))
        h = _ffn_ple(h, p[i].reshape(m, PLE_DIM), norm_ffn[i].reshape(1, d), w_ffn_gate[i].astype(BF16),
                     w_ffn_up[i].astype(BF16), w_ffn_down[i].astype(BF16), norm_ple[i].reshape(1, d),
                     w_ple_gate[i].astype(BF16), w_ple_proj[i].astype(BF16))
    return h.reshape(bsz, seq, d)
```

```python
import functools
import math

import numpy as np
import jax
import jax.numpy as jnp
from jax import lax
from jax.experimental import pallas as pl
from jax.experimental.pallas import tpu as pltpu

F32 = jnp.float32
BF16 = jnp.bfloat16
I32 = jnp.int32
I16 = jnp.int16

D_MODEL = 1024
HEAD_DIM = 64
RMS_EPS = 1e-6
DSWA_PATTERNS = ((128, 1), (512, 4), (2048, 16))
DSWA_HPG = 4
DSWA_HEADS = 12
DSWA_BLOCK = 128
GLA_HEADS = 4
GLA_DK = 128
GLA_DV = 256
GLA_RANK = 16
GLA_TAU = 16.0
GLA_CHUNK = 64
DSA_HEADS = 12
IDX_HEADS = 8
IDX_DIM = 64
IDX_TOPK = 256
REL_BUCKETS = 32
REL_MAX_DIST = 2048
D_FF = 2816
PLE_DIM = 256
IN_WIDTHS = (768, 768, 768, 512, 512, 1024, 1024, 16, 768, 64, 64, 512, 64, 8)

LANES = 128
MXU_N = 256
VMEM_LIMIT = 56 * 1024 * 1024

W1 = 2560
P1_AQ, P1_AK, P1_CQ, P1_CK = 0, 768, 1536, 2304
W2 = 5120
P2_BV, P2_BR, P2_BQ, P2_BK, P2_IQ, P2_AV = 0, 1024, 2048, 2560, 3072, 3584
P2_BAL, P2_CV, P2_IK, P2_IW = 4352, 4480, 4608, 4736

PROJ_TM = 1024
PROJ_TN = 512

DSWA_QB = 8
DSA_TQ = 128
DSA_CH = 256
DSA_AB = 256
DSA_SRB = 128
DSA_VR = 80
DSA_BIAS_CONST_FROM = 1512
DSA_Z0 = 1792
DSA_Z = DSA_Z0 + DSA_CH
INT_MIN = -(2 ** 31)
NEG_BIG = -1e30


def _cparams(sem, flags=None):
    return pltpu.CompilerParams(dimension_semantics=sem, vmem_limit_bytes=VMEM_LIMIT, flags=flags)


def _proj_kernel(h_ref, g_ref, w_ref, e_ref, *rest, mode, tn, regroup):
    if mode == "qk":
        bd_ref, rest = rest[0], rest[1:]
    o_ref, rg_refs = rest[0], rest[1:1 + len(regroup)]
    scr_ref = rest[-1] if regroup else None
    tm = h_ref.shape[0]
    gw = DSWA_HPG * HEAD_DIM
    x = h_ref[...]
    ms = jnp.mean(x * x, axis=-1, keepdims=True)
    u = (x * lax.rsqrt(ms + RMS_EPS) * g_ref[...]).astype(BF16)
    for j in range(w_ref.shape[1] // tn):
        cs = slice(j * tn, (j + 1) * tn)
        acc = jnp.dot(u, w_ref[:, cs], preferred_element_type=F32)
        if mode == "plain":
            out = acc
        elif mode == "gate":
            out = jax.nn.sigmoid(acc + e_ref[:, cs])
        else:
            sq = acc * acc
            hi = sq.astype(BF16)
            lo = (sq - hi.astype(F32)).astype(BF16)
            bw = bd_ref.shape[0]
            ss = jnp.concatenate(
                [jnp.dot(hi[:, k:k + bw], bd_ref[...], preferred_element_type=F32)
                 + jnp.dot(lo[:, k:k + bw], bd_ref[...], preferred_element_type=F32) for k in range(0, tn, bw)],
                axis=1)
            out = acc * lax.rsqrt(ss * (1.0 / HEAD_DIM) + RMS_EPS) * e_ref[:, cs]
        o_ref[:, cs] = out.astype(o_ref.dtype)
        for (col, r), rg_ref in zip(regroup, rg_refs):
            if col // tn == j:
                for k in range(gw // LANES):
                    scr_ref[k] = out[:, col % tn + k * LANES:col % tn + (k + 1) * LANES]
                for c in range(r):
                    for k in range(gw // LANES):
                        rg_ref[:, c * gw + k * LANES:c * gw + (k + 1) * LANES] = (
                            scr_ref[k, pl.ds(c, tm // r, stride=r), :].astype(rg_ref.dtype))


def _proj(h, g, w, e, mode, regroup=()):
    m, d = h.shape
    n = w.shape[1]
    tm, tn = PROJ_TM, PROJ_TN
    gw = DSWA_HPG * HEAD_DIM
    const = lambda shape: pl.BlockSpec(shape, lambda i: (0, 0), pipeline_mode=pl.Buffered(1))
    in_specs = [pl.BlockSpec((tm, d), lambda i: (i, 0)), const((1, d)), const((d, n)), const((1, n))]
    args = [h, g, w, e]
    if mode == "qk":
        r = np.arange(MXU_N) // HEAD_DIM
        bd = jnp.asarray((r[:, None] == r[None, :]).astype(np.float32), dtype=BF16)
        in_specs.append(const((MXU_N, MXU_N)))
        args.append(bd)
    out_specs = [pl.BlockSpec((tm, n), lambda i: (i, 0))]
    out_shape = [jax.ShapeDtypeStruct((m, n), BF16)]
    for _, r in regroup:
        out_specs.append(pl.BlockSpec((tm // r, r * gw), lambda i: (i, 0)))
        out_shape.append(jax.ShapeDtypeStruct((m // r, r * gw), BF16))
    outs = pl.pallas_call(
        functools.partial(_proj_kernel, mode=mode, tn=tn, regroup=tuple(regroup)),
        grid=(m // tm,),
        in_specs=in_specs,
        out_specs=out_specs,
        out_shape=out_shape,
        scratch_shapes=[pltpu.VMEM((gw // LANES, tm, LANES), F32)] if regroup else [],
        compiler_params=_cparams(("parallel",)),
        name="proj_" + mode,
    )(*args)
    return outs if regroup else outs[0]


def _dswa_kernel(q_ref, kp_ref, kc_ref, vp_ref, vc_ref, bias_ref, o_ref, st_ref, *, qb, cb):
    i = pl.program_id(2)
    blk = DSWA_BLOCK
    row = lax.broadcasted_iota(I32, (blk, 2 * blk), 0)
    col = lax.broadcasted_iota(I32, (blk, 2 * blk), 1)
    cur_ok = (col >= blk) & ((col - blk) <= row)
    valid_inner = ((col < blk) & (col >= row)) | cur_ok
    prev_off = jnp.where(i > 0, 0, 4 * blk)
    valid_first = ((col < blk) & (col >= row + prev_off)) | cur_ok
    lane = lax.broadcasted_iota(I32, (blk, LANES), 1)
    gw = DSWA_HPG * HEAD_DIM
    for cc, j in [(cc, j) for cc in range(cb) for j in range(qb)]:
        rs = slice(j * blk, (j + 1) * blk)
        ps = slice((j - 1) * blk, j * blk)
        gs = slice(cc * gw, (cc + 1) * gw)
        valid = valid_first if j == 0 else valid_inner
        q = q_ref[rs, gs]
        k = jnp.concatenate([kp_ref[:, gs] if j == 0 else kc_ref[ps, gs], kc_ref[rs, gs]], axis=0)
        v = jnp.concatenate([vp_ref[:, gs] if j == 0 else vc_ref[ps, gs], vc_ref[rs, gs]], axis=0)
        stats = jnp.zeros((blk, LANES), F32)
        for hp in range(DSWA_HPG // 2):
            ts = slice(hp * LANES, (hp + 1) * LANES)
            q2, k2, v2 = q[:, ts], k[:, ts], v[:, ts]
            halves = []
            for hh in range(2):
                h = 2 * hp + hh
                qm = jnp.where((lane // HEAD_DIM) == hh, q2, jnp.zeros_like(q2))
                s = lax.dot_general(qm, k2, (((1,), (1,)), ((), ())), preferred_element_type=F32)
                s = jnp.where(valid, s + bias_ref[h], -jnp.inf)
                m = jnp.max(s, axis=-1, keepdims=True)
                p = jnp.exp(s - m)
                l = jnp.sum(p, axis=-1, keepdims=True)
                halves.append(jnp.dot(p.astype(BF16), v2, preferred_element_type=F32) / l)
                stats = jnp.where(lane == h, m, stats)
                stats = jnp.where(lane == DSWA_HPG + h, l, stats)
            o_ref[rs, cc * gw + hp * LANES:cc * gw + (hp + 1) * LANES] = jnp.where(
                lane < HEAD_DIM, halves[0], halves[1])
        st_ref[rs, cc * LANES:(cc + 1) * LANES] = stats


def _dswa(p1, p2, rm, bias, g, r, bsz, seq):
    blk = DSWA_BLOCK
    L = seq // r
    nblk = L // blk
    gw = DSWA_HPG * HEAD_DIM
    if r == 1:
        p1v = p1.reshape(bsz, L, W1)
        p2v = p2.reshape(bsz, L, W2)
        qv, kv, vv = p1v, p1v, p2v
        q_col = lambda c: (P1_AQ // gw) + g
        k_col = lambda c: (P1_AK // gw) + g
        v_col = lambda c: (P2_AV // gw) + g
    else:
        qv, kv, vv = (a.reshape(bsz, L, r * gw) for a in rm)
        q_col = k_col = v_col = lambda c: c
    qb = min(DSWA_QB, nblk)
    cb = min(max(DSWA_QB // qb, 1), r)
    prev = lambda i: jnp.maximum(i * qb - 1, 0)
    in_specs = [
        pl.BlockSpec((None, qb * blk, cb * gw), lambda b, c, i: (b, i, q_col(c))),
        pl.BlockSpec((None, blk, cb * gw), lambda b, c, i: (b, prev(i), k_col(c))),
        pl.BlockSpec((None, qb * blk, cb * gw), lambda b, c, i: (b, i, k_col(c))),
        pl.BlockSpec((None, blk, cb * gw), lambda b, c, i: (b, prev(i), v_col(c))),
        pl.BlockSpec((None, qb * blk, cb * gw), lambda b, c, i: (b, i, v_col(c))),
        pl.BlockSpec((DSWA_HPG, blk, 2 * blk), lambda b, c, i: (0, 0, 0)),
    ]
    out_specs = [
        pl.BlockSpec((None, qb * blk, cb * gw), lambda b, c, i: (b, i, c)),
        pl.BlockSpec((None, qb * blk, cb * LANES), lambda b, c, i: (b, i, c)),
    ]
    o, st = pl.pallas_call(
        functools.partial(_dswa_kernel, qb=qb, cb=cb),
        grid=(bsz, r // cb, nblk // qb),
        in_specs=in_specs,
        out_specs=out_specs,
        out_shape=[jax.ShapeDtypeStruct((bsz, L, r * gw), F32),
                   jax.ShapeDtypeStruct((bsz, L, r * LANES), F32)],
        compiler_params=_cparams(("parallel", "parallel", "arbitrary")),
        name="dswa_g%d" % g,
    )(qv, kv, kv, vv, vv, bias)
    return o.reshape(bsz * L, r * gw), st.reshape(bsz * L, r * LANES)


GLA_TC = 256
GLA_GB = 4


def _split3(x):
    a1 = x.astype(BF16)
    r1 = x - a1.astype(F32)
    a2 = r1.astype(BF16)
    a3 = (r1 - a2.astype(F32)).astype(BF16)
    return a1, a2, a3


def _gla_kernel(v_ref, r_ref, q_ref, k_ref, al_ref, wa_ref, ba_ref, gn_ref, o_ref, st_ref, *, gb):
    C = GLA_CHUNK

    @pl.when(pl.program_id(1) == 0)
    def _():
        st_ref[...] = jnp.zeros_like(st_ref)

    TC = GLA_TC
    NC = TC // C
    row = lax.broadcasted_iota(I32, (TC, TC), 0)
    col = lax.broadcasted_iota(I32, (TC, TC), 1)
    tri = ((row // C) == (col // C)) & (row >= col)
    tri_bf = jnp.where(tri, 1.0, 0.0).astype(BF16)
    nt = (((1,), (1,)), ((), ()))
    tn = (((0,), (0,)), ((), ()))
    for bb in range(gb):
        z = jnp.dot(al_ref[bb], wa_ref[...], preferred_element_type=F32) + ba_ref[...]
        la = (jnp.minimum(z, 0.0) - jnp.log(1.0 + jnp.exp(-jnp.abs(z)))) * (1.0 / GLA_TAU)
        a1, a2, a3 = _split3(la)
        bcum = (jnp.dot(tri_bf, a1, preferred_element_type=F32)
                + jnp.dot(tri_bf, a2, preferred_element_type=F32)
                + jnp.dot(tri_bf, a3, preferred_element_type=F32))
        blast = [bcum[(c + 1) * C - 1:(c + 1) * C, :] for c in range(NC)]
        dec = [jnp.exp(b) for b in blast]
        dec_rows = jnp.concatenate([jnp.broadcast_to(d, (C, d.shape[1])) for d in dec], axis=0)
        qf = q_ref[bb].astype(F32) * (GLA_DK ** -0.5)
        k_dec = k_ref[bb].astype(F32) * jnp.exp(-bcum)
        q_in = (qf * jnp.exp(bcum)).astype(BF16)
        k_in = k_dec.astype(BF16)
        k_end = (k_dec * dec_rows).astype(BF16)
        for h in range(GLA_HEADS):
            ks = slice(h * GLA_DK, (h + 1) * GLA_DK)
            vs = slice(h * GLA_DV, (h + 1) * GLA_DV)
            vh = v_ref[bb, :, vs]
            att = lax.dot_general(q_in[:, ks], k_in[:, ks], nt, preferred_element_type=F32)
            att = jnp.where(tri, att, 0.0)
            o = jnp.dot(att.astype(BF16), vh, preferred_element_type=F32)
            st = st_ref[bb, h]
            inter = []
            for c in range(NC):
                rs = slice(c * C, (c + 1) * C)
                inter.append(lax.dot_general(q_in[rs, ks], st.astype(BF16), nt, preferred_element_type=F32))
                st = st * dec[c][:, ks] + lax.dot_general(vh[rs, :], k_end[rs, ks], tn, preferred_element_type=F32)
            st_ref[bb, h] = st
            o = o + jnp.concatenate(inter, axis=0)
            ms = jnp.mean(o * o, axis=-1, keepdims=True)
            y = o * lax.rsqrt(ms + RMS_EPS) * gn_ref[...]
            rg = r_ref[bb, :, vs].astype(F32)
            y = y * (rg * jax.nn.sigmoid(rg))
            o_ref[bb, :, vs] = y.astype(o_ref.dtype)


def _gla(p2, wa, ba, gn, bsz, seq):
    tc = GLA_TC
    gb = GLA_GB if bsz % GLA_GB == 0 else 1
    p2v = p2.reshape(bsz, seq, W2)
    in_specs = [
        pl.BlockSpec((gb, tc, 1024), lambda b, t: (b, t, P2_BV // 1024)),
        pl.BlockSpec((gb, tc, 1024), lambda b, t: (b, t, P2_BR // 1024)),
        pl.BlockSpec((gb, tc, 512), lambda b, t: (b, t, P2_BQ // 512)),
        pl.BlockSpec((gb, tc, 512), lambda b, t: (b, t, P2_BK // 512)),
        pl.BlockSpec((gb, tc, LANES), lambda b, t: (b, t, P2_BAL // LANES)),
        pl.BlockSpec((LANES, 512), lambda b, t: (0, 0)),
        pl.BlockSpec((1, 512), lambda b, t: (0, 0)),
        pl.BlockSpec((1, GLA_DV), lambda b, t: (0, 0)),
    ]
    out = pl.pallas_call(
        functools.partial(_gla_kernel, gb=gb),
        grid=(bsz // gb, seq // tc),
        in_specs=in_specs,
        out_specs=pl.BlockSpec((gb, tc, 1024), lambda b, t: (b, t, 0)),
        out_shape=jax.ShapeDtypeStruct((bsz, seq, 1024), BF16),
        scratch_shapes=[pltpu.VMEM((gb, GLA_HEADS, GLA_DV, GLA_DK), F32)],
        compiler_params=_cparams(("parallel", "arbitrary")),
        name="gla",
    )(p2v, p2v, p2v, p2v, p2v, wa, ba, gn)
    return out.reshape(bsz * seq, 1024)


def _tree_sum(xs):
    xs = list(xs)
    while len(xs) > 1:
        xs = [xs[i] + xs[i + 1] for i in range(0, len(xs) - 1, 2)] + ([xs[-1]] if len(xs) % 2 else [])
    return xs[0]


def _dsa_kernel(q_ref, iq_ref, iw_ref, k_ref, ik_ref, vt_ref, bias_ref, wot_ref, o_ref,
                keys_ref, planes_ref, qaug_ref, iqall_ref, acc_ref, st0_ref, st1_ref, m_ref, cm0_ref, cm1_ref,
                *, topk):
    TQ, CH = DSA_TQ, DSA_CH
    CC = 2 * CH
    qi = pl.program_id(1)
    t0 = qi * TQ
    nch = qi // (CH // TQ) + 1
    npair = (nch + 1) // 2
    nt = (((1,), (1,)), ((), ()))

    eye = (lax.broadcasted_iota(I32, (TQ, TQ), 0) == lax.broadcasted_iota(I32, (TQ, TQ), 1))
    eye = jnp.where(eye, 1.0, 0.0).astype(BF16)
    q_t = q_ref[...].astype(F32).T.astype(BF16)
    iq_t = iq_ref[...].astype(F32).T.astype(BF16)
    for h in range(DSA_HEADS):
        cs = slice(h * TQ, (h + 1) * TQ)
        qaug_ref[0:TQ, cs] = eye
        qaug_ref[TQ:TQ + HEAD_DIM, cs] = q_t[h * HEAD_DIM:(h + 1) * HEAD_DIM, :]
        qaug_ref[TQ + HEAD_DIM:, cs] = jnp.zeros((TQ - HEAD_DIM, TQ), BF16)
    for h in range(IDX_HEADS):
        cs = slice(h * TQ, (h + 1) * TQ)
        iqall_ref[0:IDX_DIM, cs] = iq_t[h * IDX_DIM:(h + 1) * IDX_DIM, :]
        iqall_ref[IDX_DIM:, cs] = jnp.zeros((LANES - IDX_DIM, TQ), BF16)
    idx_scale = (IDX_HEADS ** -0.5) * (IDX_DIM ** -0.5)
    wt = (iw_ref[...].astype(F32) * idx_scale).T

    qpos = t0 + lax.broadcasted_iota(I32, (1, TQ), 1)
    RB = 128
    krow1 = lax.broadcasted_iota(I32, (RB, TQ), 0)
    krow = lax.broadcasted_iota(I32, (CH, TQ), 0)
    krow2 = lax.broadcasted_iota(I32, (CC, TQ), 0)

    SRB = DSA_SRB

    def score_block(s0, RB=SRB):
        s0 = pl.multiple_of(s0, RB)
        krow1 = lax.broadcasted_iota(I32, (RB, TQ), 0)
        ikc = ik_ref[pl.ds(s0, RB), :]
        s = None
        for hp in range(IDX_HEADS // 2):
            x = jnp.dot(ikc, iqall_ref[:, hp * 2 * TQ:(hp + 1) * 2 * TQ],
                        preferred_element_type=F32)
            t = (jnp.maximum(x[:, :TQ], 0.0) * wt[2 * hp:2 * hp + 1, :]
                 + jnp.maximum(x[:, TQ:], 0.0) * wt[2 * hp + 1:2 * hp + 2, :])
            s = t if s is None else s + t
        s = jnp.where(s == 0.0, 0.0, s)
        bits = pltpu.bitcast(s, I32)
        key = bits ^ ((bits >> 31) & 0x7FFFFFFF)
        key = jnp.where(s0 + krow1 <= qpos, key, INT_MIN)
        keys_ref[pl.ds(s0, RB), :] = key

    def score_trips(first, count, width):
        def body(t, carry):
            for sub in range(width * CH // SRB):
                score_block((first + t * width) * CH + sub * SRB)
            return carry
        lax.fori_loop(0, count, body, 0)

    n4 = nch // 4
    n2 = (nch - 4 * n4) // 2
    score_trips(0, n4, 4)
    score_trips(4 * n4, n2, 2)
    score_trips(4 * n4 + 2 * n2, nch - 4 * n4 - 2 * n2, 1)

    @pl.when(nch % 2 == 1)
    def _():
        keys_ref[pl.ds(pl.multiple_of(nch * CH, CH), CH), :] = jnp.full((CH, TQ), INT_MIN, I32)

    kvec = jnp.minimum(topk, qpos + 1)
    NPL = 32

    def planes_body(c, carry):
        s0 = pl.multiple_of(c * CH, CH)
        a = [keys_ref[pl.ds(s0 + 8 * j, 8), :] ^ INT_MIN for j in range(NPL)]
        j, m = 16, 0x0000FFFF
        while j:
            sh = jnp.full((8, TQ), j, I32)
            k = 0
            while k < NPL:
                t = (a[k] ^ lax.shift_right_logical(a[k + j], sh)) & m
                a[k] = a[k] ^ t
                a[k + j] = a[k + j] ^ lax.shift_left(t, sh)
                k = (k + j + 1) & ~j
            j >>= 1
            m = (m ^ (m << j)) & 0xFFFFFFFF
        for p in range(NPL):
            planes_ref[c, p] = a[p]
        return carry

    def empty_body(c, carry):
        for p in range(NPL):
            planes_ref[c, p] = jnp.zeros((8, TQ), I32)
        return carry

    NCK = keys_ref.shape[0] // CH
    short = nch <= NCK // 2
    lax.fori_loop(0, nch, planes_body, 0)
    lax.fori_loop(nch, jnp.where(short, NCK // 2, NCK), empty_body, 0)

    def radix_select(nc):
        def run():
            def lane_count(words):
                return jnp.sum(_tree_sum([lax.population_count(w) for w in words]), axis=0, keepdims=True)

            def plane_body(t, carry):
                thr_u, n_gt, alive = carry
                p = 2 * t
                h1 = [alive[c] & planes_ref[c, p] for c in range(nc)]
                h0 = [alive[c] ^ h1[c] for c in range(nc)]
                h11 = [h1[c] & planes_ref[c, p + 1] for c in range(nc)]
                h01 = [h0[c] & planes_ref[c, p + 1] for c in range(nc)]
                c1, c11, c01 = lane_count(h1), lane_count(h11), lane_count(h01)
                take1 = (n_gt + c1) >= kvec
                n_mid = jnp.where(take1, n_gt, n_gt + c1)
                ones2 = jnp.where(take1, c11, c01)
                take2 = (n_mid + ones2) >= kvec
                n_gt = jnp.where(take2, n_mid, n_mid + ones2)
                thr_u = (thr_u | jnp.where(take1, jnp.int32(1) << (31 - p), 0)
                         | jnp.where(take2, jnp.int32(1) << (30 - p), 0))
                alive = tuple(
                    jnp.where(take1, jnp.where(take2, h11[c], h1[c] ^ h11[c]), jnp.where(take2, h01[c], h0[c] ^ h01[c]))
                    for c in range(nc))
                return thr_u, n_gt, alive

            zero = jnp.zeros((1, TQ), I32)
            alive0 = tuple(jnp.full((8, TQ), -1, I32) for _ in range(nc))
            thr_u, n_gt, alive = lax.fori_loop(0, NPL // 2, plane_body, (zero, zero, alive0))
            n_eq = jnp.sum(_tree_sum([lax.population_count(a) for a in alive]), axis=0, keepdims=True)
            return thr_u, n_gt, n_eq
        return run

    thr_u, n_gt, n_eq = lax.cond(short, radix_select(NCK // 2), radix_select(NCK))
    thr = thr_u ^ INT_MIN

    def count(pred_fn):
        def body(c, acc):
            s0 = pl.multiple_of(c * CC, CC)
            kk = keys_ref[pl.ds(s0, CC), :]
            hit = jnp.where(pred_fn(kk, s0), 1, 0).astype(I32)
            return acc + jnp.sum(hit.reshape(CC // 8, 8, TQ), axis=0)
        acc = lax.fori_loop(0, npair, body, jnp.zeros((8, TQ), I32))
        return jnp.sum(acc, axis=0, keepdims=True)

    excess = n_gt + n_eq - kvec
    has_excess = jnp.max(excess) > 0

    @pl.when(has_excess)
    def _():
        need = kvec - n_gt

        def tie_lt(cut):
            return count(lambda kk, s0: (kk == thr) & (s0 + krow2 < cut))

        def cut_body(b, cut):
            cand = cut | (jnp.int32(1) << (12 - b))
            return jnp.where(tie_lt(cand) <= need, cand, cut)

        cut = lax.fori_loop(0, 13, cut_body, jnp.zeros((1, TQ), I32))

        def drop_body(c, carry):
            s0 = pl.multiple_of(c * CH, CH)
            kk = keys_ref[pl.ds(s0, CH), :]
            keys_ref[pl.ds(s0, CH), :] = jnp.where((kk == thr) & (s0 + krow >= cut), INT_MIN, kk)
            return carry

        lax.fori_loop(0, nch, drop_body, 0)

    acc_ref[...] = jnp.zeros_like(acc_ref)

    NP = DSA_HEADS // 2
    PW = 2 * TQ

    AB = DSA_AB
    NB = CH // AB
    last_chunk = keys_ref.shape[0] // CH - 1

    def logits(c, st_ref, cm_ref):
        s0 = pl.multiple_of(c * CH, CH)
        zoff = pl.multiple_of(jnp.clip(s0 - t0 + DSA_Z0, 0, DSA_Z0), 8)
        kaug = []
        for rb in range(NB):
            rs = pl.ds(s0 + rb * AB, AB)
            pen = jnp.where(keys_ref[rs, :] >= thr, 0.0, NEG_BIG).astype(BF16)
            kaug.append(jnp.concatenate([pen, k_ref[rs, :]], axis=1))
        for hp in range(NP):
            ps = slice(hp * PW, (hp + 1) * PW)
            cm = None
            for rb in range(NB):
                st = jnp.dot(kaug[rb], qaug_ref[:, ps], preferred_element_type=F32)
                st = st + bias_ref[pl.ds(zoff + rb * AB, AB), ps]
                st_ref[rb * AB:(rb + 1) * AB, ps] = st
                tm = jnp.max(st.reshape(AB // 8, 8, PW), axis=0)
                cm = tm if cm is None else jnp.maximum(cm, tm)
            cm_ref[0:1, ps] = jnp.max(cm, axis=0, keepdims=True)

    def accumulate(c, st_ref, cm_ref):
        vtc = vt_ref[c]
        for hp in range(NP):
            ps = slice(hp * PW, (hp + 1) * PW)
            m_old = m_ref[0:1, ps]
            m_new = jnp.maximum(m_old, cm_ref[0:1, ps])
            m_ref[0:1, ps] = m_new
            alpha = jnp.exp2(m_old - m_new)
            p = jnp.concatenate(
                [jnp.exp2(st_ref[rb * AB:(rb + 1) * AB, ps] - m_new).astype(BF16) for rb in range(NB)], axis=0)
            acc_ref[:, ps] = acc_ref[:, ps] * alpha + jnp.dot(vtc, p, preferred_element_type=F32)

    def att_pairs(first, count, pairs):
        def body(t, carry):
            c0 = first + 2 * pairs * t
            for k in range(pairs):
                logits(c0 + 2 * k + 1, st1_ref, cm1_ref)
                accumulate(c0 + 2 * k, st0_ref, cm0_ref)
                logits(c0 + 2 * k + 2, st0_ref, cm0_ref)
                accumulate(c0 + 2 * k + 1, st1_ref, cm1_ref)
            return carry
        lax.fori_loop(0, count, body, 0)

    m_ref[...] = jnp.full(m_ref.shape, NEG_BIG, F32)
    logits(0, st0_ref, cm0_ref)
    nfull = (nch - 1) // 2
    att_pairs(0, nfull // 2, 2)
    att_pairs(4 * (nfull // 2), nfull % 2, 1)
    c_last = 2 * nfull

    @pl.when(nch % 2 == 1)
    def _():
        accumulate(c_last, st0_ref, cm0_ref)

    @pl.when(nch % 2 == 0)
    def _():
        logits(c_last + 1, st1_ref, cm1_ref)
        accumulate(c_last, st0_ref, cm0_ref)
        accumulate(c_last + 1, st1_ref, cm1_ref)

    ot = jnp.concatenate(
        [acc_ref[0:HEAD_DIM, h * TQ:(h + 1) * TQ] / acc_ref[HEAD_DIM:HEAD_DIM + 1, h * TQ:(h + 1) * TQ]
         for h in range(DSA_HEADS)], axis=0)
    o_ref[...] = jnp.dot(ot.T.astype(BF16), wot_ref[...], preferred_element_type=F32)


def _dsa(p1, p2, vt, bias_t, wot, bsz, seq):
    tq = DSA_TQ
    p1v = p1.reshape(bsz, seq, W1)
    p2v = p2.reshape(bsz, seq, W2)
    nq = DSA_HEADS * tq
    in_specs = [
        pl.BlockSpec((None, tq, 768), lambda b, i: (b, i, P1_CQ // 768)),
        pl.BlockSpec((None, tq, 512), lambda b, i: (b, i, P2_IQ // 512)),
        pl.BlockSpec((None, tq, LANES), lambda b, i: (b, i, P2_IW // LANES)),
        pl.BlockSpec((None, seq, LANES), lambda b, i: (b, 0, P1_CK // LANES)),
        pl.BlockSpec((None, seq, LANES), lambda b, i: (b, 0, P2_IK // LANES)),
        pl.BlockSpec((None, seq // DSA_CH, DSA_VR, DSA_CH), lambda b, i: (b, 0, 0, 0)),
        pl.BlockSpec((DSA_Z, nq), lambda b, i: (0, 0)),
        pl.BlockSpec((DSA_HEADS * HEAD_DIM, D_MODEL), lambda b, i: (0, 0)),
    ]
    out = pl.pallas_call(
        functools.partial(_dsa_kernel, topk=min(IDX_TOPK, seq // 4)),
        grid=(bsz, seq // tq),
        in_specs=in_specs,
        out_specs=pl.BlockSpec((None, tq, D_MODEL), lambda b, i: (b, i, 0)),
        out_shape=jax.ShapeDtypeStruct((bsz, seq, D_MODEL), F32),
        scratch_shapes=[
            pltpu.VMEM((seq, tq), I32),
            pltpu.VMEM((seq // DSA_CH, 32, 8, tq), I32),
            pltpu.VMEM((2 * tq, nq), BF16),
            pltpu.VMEM((LANES, IDX_HEADS * tq), BF16),
            pltpu.VMEM((DSA_VR, nq), F32),
            pltpu.VMEM((DSA_CH, nq), F32),
            pltpu.VMEM((DSA_CH, nq), F32),
            pltpu.VMEM((8, nq), F32),
            pltpu.VMEM((8, nq), F32),
            pltpu.VMEM((8, nq), F32),
        ],
        compiler_params=_cparams(("parallel", "arbitrary")),
        name="dsa",
    )(p1v, p2v, p2v, p1v, p2v, vt, bias_t, wot)
    return out.reshape(bsz * seq, D_MODEL)


MERGE_TM = 512


def _merge_kernel(h_ref, oa0_ref, oa1_ref, oa2_ref, sa0_ref, sa1_ref, sa2_ref, ob_ref, yc_ref, g_ref,
                  woa_ref, wob_ref, wo_ref, o_ref, on1_ref, on2_ref, sn1_ref, sn2_ref):
    H = DSWA_HPG
    tm = h_ref.shape[0]
    gw = H * HEAD_DIM
    for (_, r), src_o, src_s, dst_o, dst_s in zip(DSWA_PATTERNS[1:], (oa1_ref, oa2_ref), (sa1_ref, sa2_ref),
                                                  (on1_ref, on2_ref), (sn1_ref, sn2_ref)):
        for c in range(r):
            for k in range(gw // LANES):
                dst_o[k, pl.ds(c, tm // r, stride=r), :] = src_o[:, c * gw + k * LANES:c * gw + (k + 1) * LANES]
            dst_s[pl.ds(c, tm // r, stride=r), :] = src_s[:, c * LANES:(c + 1) * LANES]
    oa_vals = (oa0_ref[...],) + tuple(
        jnp.concatenate([ref[k] for k in range(gw // LANES)], axis=1) for ref in (on1_ref, on2_ref))
    sts = [sa0_ref[...], sn1_ref[...], sn2_ref[...]]
    lane = lax.broadcasted_iota(I32, sts[0].shape, 1)
    mmax = jnp.maximum(jnp.maximum(sts[0], sts[1]), sts[2])
    wts = [pltpu.roll(s, LANES - H, 1) * jnp.exp(s - mmax) for s in sts]
    tot = wts[0] + wts[1] + wts[2]
    hrow = lax.broadcasted_iota(I32, (LANES, H * HEAD_DIM), 0)
    hcol = lax.broadcasted_iota(I32, (LANES, H * HEAD_DIM), 1) // HEAD_DIM
    expand = jnp.where(hrow == hcol, 1.0, 0.0).astype(BF16)
    oa = None
    for g in range(3):
        w = jnp.where(lane < H, wts[g] / tot, 0.0)
        hi = w.astype(BF16)
        lo = (w - hi.astype(F32)).astype(BF16)
        wfull = (jnp.dot(hi, expand, preferred_element_type=F32)
                 + jnp.dot(lo, expand, preferred_element_type=F32))
        term = wfull * oa_vals[g]
        oa = term if oa is None else oa + term
    oa = oa.astype(BF16)
    y_a = jnp.dot(oa, woa_ref[...], preferred_element_type=F32)
    y_b = jnp.dot(ob_ref[...], wob_ref[...], preferred_element_type=F32)
    y_c = yc_ref[...]
    D = D_MODEL
    mix = (g_ref[:, 0:D].astype(F32) * y_a + g_ref[:, D:2 * D].astype(F32) * y_b
           + g_ref[:, 2 * D:3 * D].astype(F32) * y_c)
    o_ref[...] = h_ref[...] + jnp.dot(mix.astype(BF16), wo_ref[...], preferred_element_type=F32)


def _merge(h, oas, sas, ob, yc, gates, woa, wob, wo):
    m = h.shape[0]
    tm = MERGE_TM
    gw = DSWA_HPG * HEAD_DIM
    row = lambda w: pl.BlockSpec((tm, w), lambda i: (i, 0))
    rmrow = lambda w, r: pl.BlockSpec((tm // r, r * w), lambda i: (i, 0))
    full = lambda a: pl.BlockSpec(a.shape, lambda i: (0, 0), pipeline_mode=pl.Buffered(1))
    rs = [r for _, r in DSWA_PATTERNS]
    in_specs = ([row(D_MODEL)] + [rmrow(gw, r) for r in rs] + [rmrow(LANES, r) for r in rs]
                + [row(1024), row(D_MODEL), row(3 * D_MODEL)] + [full(woa), full(wob), full(wo)])
    return pl.pallas_call(
        _merge_kernel,
        grid=(m // tm,),
        in_specs=in_specs,
        out_specs=row(D_MODEL),
        out_shape=jax.ShapeDtypeStruct((m, D_MODEL), F32),
        scratch_shapes=[pltpu.VMEM((gw // LANES, tm, LANES), F32), pltpu.VMEM((gw // LANES, tm, LANES), F32),
                        pltpu.VMEM((tm, LANES), F32), pltpu.VMEM((tm, LANES), F32)],
        compiler_params=_cparams(("parallel",)),
        name="merge",
    )(h, *oas, *sas, ob, yc, gates, woa, wob, wo)


FFN_TM = 512
FFN_TF = 256


def _rms(x, g):
    ms = jnp.mean(x * x, axis=-1, keepdims=True)
    return (x * lax.rsqrt(ms + RMS_EPS) * g).astype(BF16)


def _ffn_ple_kernel(h_ref, p_ref, gf_ref, wg_ref, wu_ref, wd_ref, gp_ref, wpg_ref, wpp_ref, o_ref, *, tf):
    x = h_ref[...]
    u = _rms(x, gf_ref[...])
    acc = None
    for j in range(wg_ref.shape[1] // tf):
        cs = slice(j * tf, (j + 1) * tf)
        a = jnp.dot(u, wg_ref[:, cs], preferred_element_type=F32)
        b = jnp.dot(u, wu_ref[:, cs], preferred_element_type=F32)
        t = (a * jax.nn.sigmoid(a) * b).astype(BF16)
        d = jnp.dot(t, wd_ref[cs, :], preferred_element_type=F32)
        acc = d if acc is None else acc + d
    h2 = x + acc
    e = _rms(h2, gp_ref[...])
    gate = jax.nn.sigmoid(jnp.dot(e, wpg_ref[...], preferred_element_type=F32))
    proj = jnp.dot(p_ref[...].astype(BF16), wpp_ref[...], preferred_element_type=F32)
    o_ref[...] = h2 + gate * proj


def _ffn_ple(h, p, gf, wg, wu, wd, gp, wpg, wpp):
    m, d = h.shape
    tm, tf = FFN_TM, FFN_TF
    const = lambda a: pl.BlockSpec(a.shape, lambda i: (0, 0), pipeline_mode=pl.Buffered(1))
    return pl.pallas_call(
        functools.partial(_ffn_ple_kernel, tf=tf),
        grid=(m // tm,),
        in_specs=[pl.BlockSpec((tm, d), lambda i: (i, 0)), pl.BlockSpec((tm, PLE_DIM), lambda i: (i, 0)),
                  const(gf), const(wg), const(wu), const(wd), const(gp), const(wpg), const(wpp)],
        out_specs=pl.BlockSpec((tm, d), lambda i: (i, 0)),
        out_shape=jax.ShapeDtypeStruct((m, d), F32),
        compiler_params=_cparams(("parallel",)),
        name="ffn_ple",
    )(h, p, gf, wg, wu, wd, gp, wpg, wpp)


def _rel_bucket(dist):
    max_exact = REL_BUCKETS // 2
    d = jnp.maximum(dist, 0)
    df = jnp.maximum(d, 1).astype(F32)
    large = max_exact + (jnp.log(df / max_exact) / math.log(REL_MAX_DIST / max_exact)
                         * (REL_BUCKETS - max_exact)).astype(I32)
    large = jnp.minimum(large, REL_BUCKETS - 1)
    return jnp.where(d < max_exact, d, large)


def _toeplitz(rev, n_rows, n_cols):
    nh = rev.shape[0]
    lw = rev.shape[1] + 1
    w = jnp.pad(rev, ((0, 0), (0, 1)))
    s = jnp.broadcast_to(w[:, None, :], (nh, n_rows, lw)).reshape(nh, n_rows * lw)
    s = s[:, :n_rows * (lw - 1)].reshape(nh, n_rows, lw - 1)
    return s[:, :, n_rows - 1:n_rows - 1 + n_cols]


def _bias_tables(rel_bias):
    blk = DSWA_BLOCK
    dswa = []
    for g, (_, r) in enumerate(DSWA_PATTERNS):
        delta = np.arange(3 * blk - 1)[::-1] - (blk - 1)
        rev = rel_bias[_rel_bucket(jnp.asarray(delta * r, I32))][:, g * DSWA_HPG:(g + 1) * DSWA_HPG]
        dswa.append(_toeplitz(rev.T, blk, 2 * blk))
    tq = DSA_TQ
    dist = np.arange(DSA_Z + tq - 1)[::-1] - (DSA_Z - 1) + DSA_Z0
    rev = rel_bias[_rel_bucket(jnp.asarray(dist, I32))][:, DSWA_HEADS:] * math.log2(math.e)
    bias_t = jnp.transpose(_toeplitz(rev.T, tq, DSA_Z), (2, 0, 1)).reshape(DSA_Z, DSA_HEADS * tq)
    return dswa, bias_t


def _pad_cols(w, width):
    return jnp.pad(w, ((0, 0), (0, width - w.shape[1])))


def _layer_params(w_in, qn_a, kn_a, qn_c, kn_c, w_alpha2, b_alpha):
    offs = np.cumsum((0,) + IN_WIDTHS)
    parts = [w_in[:, offs[i]:offs[i + 1]] for i in range(len(IN_WIDTHS))]
    a_q, a_k, a_v, b_q, b_k, b_v, b_r, b_al, c_q, c_k, c_v, i_q, i_k, i_w = parts
    w1 = _pad_cols(jnp.concatenate([a_q, a_k, c_q, c_k], axis=1), W1).astype(BF16)
    w2 = _pad_cols(jnp.concatenate(
        [b_v, b_r, b_q, b_k, i_q, a_v, _pad_cols(b_al, LANES), _pad_cols(c_v, LANES),
         _pad_cols(i_k, LANES), _pad_cols(i_w, LANES)], axis=1), W2).astype(BF16)
    scale = HEAD_DIM ** -0.5
    gain1 = jnp.concatenate([jnp.tile(qn_a, DSWA_HEADS) * scale, jnp.tile(kn_a, DSWA_HEADS),
                             jnp.tile(qn_c, DSA_HEADS) * (scale * math.log2(math.e)), kn_c])
    gain1 = jnp.pad(gain1, (0, W1 - gain1.shape[0])).reshape(1, W1)
    wa = jnp.pad(w_alpha2, ((0, LANES - GLA_RANK), (0, 0))).astype(BF16)
    return w1, w2, gain1, wa, b_alpha.reshape(1, -1)


def kernel(x, p, rel_bias, norm_mix, w_in, qn_a, kn_a, qn_c, kn_c, w_alpha2, b_alpha, gla_norm, w_out_a, w_out_b, w_out_c, w_gate, b_gate, w_o, norm_ffn, w_ffn_gate, w_ffn_up, w_ffn_down, norm_ple, w_ple_gate, w_ple_proj):
    bsz, seq, d = x.shape
    depth = p.shape[0]
    m = bsz * seq
    dswa_bias, bias_t = _bias_tables(rel_bias)
    h = x.reshape(m, d)
    zeros_w2 = jnp.zeros((1, W2), F32)
    for i in range(depth):
        w1, w2, gain1, wa, ba = _layer_params(w_in[i], qn_a[i], kn_a[i], qn_c[i], kn_c[i], w_alpha2[i], b_alpha[i])
        gmix = norm_mix[i].reshape(1, d)
        gw = DSWA_HPG * HEAD_DIM
        dil = [(g, r) for g, (_, r) in enumerate(DSWA_PATTERNS) if r > 1]
        p1, *qk_rm = _proj(h, gmix, w1, gain1, "qk",
                           regroup=[(off + g * gw, r) for off in (P1_AQ, P1_AK) for g, r in dil])
        p2, *v_rm = _proj(h, gmix, w2, zeros_w2, "plain", regroup=[(P2_AV + g * gw, r) for g, r in dil])
        gates = _proj(h, gmix, w_gate[i].astype(BF16), b_gate[i].reshape(1, -1), "gate")
        rm = {g: (qk_rm[n], qk_rm[len(dil) + n], v_rm[n]) for n, (g, r) in enumerate(dil)}
        oas, sas = [], []
        for g, (_, r) in enumerate(DSWA_PATTERNS):
            o, st = _dswa(p1, p2, rm.get(g), dswa_bias[g], g, r, bsz, seq)
            oas.append(o)
            sas.append(st)
        ob = _gla(p2, wa, ba, gla_norm[i].reshape(1, -1), bsz, seq)
        cv = p2[:, P2_CV:P2_CV + HEAD_DIM].reshape(bsz, seq // DSA_CH, DSA_CH, HEAD_DIM)
        ones_pad = jnp.zeros((bsz, seq // DSA_CH, DSA_VR - HEAD_DIM, DSA_CH), BF16).at[:, :, 0, :].set(1.0)
        vt = jnp.concatenate([jnp.transpose(cv, (0, 1, 3, 2)), ones_pad], axis=2)
        yc = _dsa(p1, p2, vt, bias_t, w_out_c[i].astype(BF16), bsz, seq)
        h = _merge(h, oas, sas, ob, yc, gates, w_out_a[i].astype(BF16), w_out_b[i].astype(BF16),
                   w_o[i].astype(BF16))
        h = _ffn_ple(h, p[i].reshape(m, PLE_DIM), norm_ffn[i].reshape(1, d), w_ffn_gate[i].astype(BF16),
                     w_ffn_up[i].astype(BF16), w_ffn_down[i].astype(BF16), norm_ple[i].reshape(1, d),
                     w_ple_gate[i].astype(BF16), w_ple_proj[i].astype(BF16))
    return h.reshape(bsz, seq, d)
```

```python
import functools
import math

import numpy as np
import jax
import jax.numpy as jnp
from jax import lax
from jax.experimental import pallas as pl
from jax.experimental.pallas import tpu as pltpu

F32 = jnp.float32
BF16 = jnp.bfloat16
I32 = jnp.int32
I16 = jnp.int16

D_MODEL = 1024
HEAD_DIM = 64
RMS_EPS = 1e-6
DSWA_PATTERNS = ((128, 1), (512, 4), (2048, 16))
DSWA_HPG = 4
DSWA_HEADS = 12
DSWA_BLOCK = 128
GLA_HEADS = 4
GLA_DK = 128
GLA_DV = 256
GLA_RANK = 16
GLA_TAU = 16.0
GLA_CHUNK = 64
DSA_HEADS = 12
IDX_HEADS = 8
IDX_DIM = 64
IDX_TOPK = 256
REL_BUCKETS = 32
REL_MAX_DIST = 2048
D_FF = 2816
PLE_DIM = 256
IN_WIDTHS = (768, 768, 768, 512, 512, 1024, 1024, 16, 768, 64, 64, 512, 64, 8)

LANES = 128
MXU_N = 256
VMEM_LIMIT = 56 * 1024 * 1024

W1 = 2560
P1_AQ, P1_AK, P1_CQ, P1_CK = 0, 768, 1536, 2304
W2 = 5120
P2_BV, P2_BR, P2_BQ, P2_BK, P2_IQ, P2_AV = 0, 1024, 2048, 2560, 3072, 3584
P2_BAL, P2_CV, P2_IK, P2_IW = 4352, 4480, 4608, 4736

PROJ_TM = 1024
PROJ_TN = 512

DSWA_QB = 8
DSA_TQ = 128
DSA_CH = 256
DSA_AB = 256
DSA_SRB = 128
DSA_VR = 80
DSA_BIAS_CONST_FROM = 1512
DSA_Z0 = 1792
DSA_Z = DSA_Z0 + DSA_CH
INT_MIN = -(2 ** 31)
NEG_BIG = -1e30


def _cparams(sem, flags=None):
    return pltpu.CompilerParams(dimension_semantics=sem, vmem_limit_bytes=VMEM_LIMIT, flags=flags)


def _proj_kernel(h_ref, g_ref, w_ref, e_ref, *rest, mode, tn, regroup):
    if mode == "qk":
        bd_ref, rest = rest[0], rest[1:]
    o_ref, rg_refs = rest[0], rest[1:1 + len(regroup)]
    scr_ref = rest[-1] if regroup else None
    tm = h_ref.shape[0]
    gw = DSWA_HPG * HEAD_DIM
    x = h_ref[...]
    ms = jnp.mean(x * x, axis=-1, keepdims=True)
    u = (x * lax.rsqrt(ms + RMS_EPS) * g_ref[...]).astype(BF16)
    for j in range(w_ref.shape[1] // tn):
        cs = slice(j * tn, (j + 1) * tn)
        acc = jnp.dot(u, w_ref[:, cs], preferred_element_type=F32)
        if mode == "plain":
            out = acc
        elif mode == "gate":
            out = jax.nn.sigmoid(acc + e_ref[:, cs])
        else:
            sq = acc * acc
            hi = sq.astype(BF16)
            lo = (sq - hi.astype(F32)).astype(BF16)
            bw = bd_ref.shape[0]
            ss = jnp.concatenate(
                [jnp.dot(hi[:, k:k + bw], bd_ref[...], preferred_element_type=F32)
                 + jnp.dot(lo[:, k:k + bw], bd_ref[...], preferred_element_type=F32) for k in range(0, tn, bw)],
                axis=1)
            out = acc * lax.rsqrt(ss * (1.0 / HEAD_DIM) + RMS_EPS) * e_ref[:, cs]
        o_ref[:, cs] = out.astype(o_ref.dtype)
        for (col, r), rg_ref in zip(regroup, rg_refs):
            if col // tn == j:
                for k in range(gw // LANES):
                    scr_ref[k] = out[:, col % tn + k * LANES:col % tn + (k + 1) * LANES]
                for c in range(r):
                    for k in range(gw // LANES):
                        rg_ref[:, c * gw + k * LANES:c * gw + (k + 1) * LANES] = (
                            scr_ref[k, pl.ds(c, tm // r, stride=r), :].astype(rg_ref.dtype))


def _proj(h, g, w, e, mode, regroup=()):
    m, d = h.shape
    n = w.shape[1]
    tm, tn = PROJ_TM, PROJ_TN
    gw = DSWA_HPG * HEAD_DIM
    const = lambda shape: pl.BlockSpec(shape, lambda i: (0, 0), pipeline_mode=pl.Buffered(1))
    in_specs = [pl.BlockSpec((tm, d), lambda i: (i, 0)), const((1, d)), const((d, n)), const((1, n))]
    args = [h, g, w, e]
    if mode == "qk":
        r = np.arange(MXU_N) // HEAD_DIM
        bd = jnp.asarray((r[:, None] == r[None, :]).astype(np.float32), dtype=BF16)
        in_specs.append(const((MXU_N, MXU_N)))
        args.append(bd)
    out_specs = [pl.BlockSpec((tm, n), lambda i: (i, 0))]
    out_shape = [jax.ShapeDtypeStruct((m, n), BF16)]
    for _, r in regroup:
        out_specs.append(pl.BlockSpec((tm // r, r * gw), lambda i: (i, 0)))
        out_shape.append(jax.ShapeDtypeStruct((m // r, r * gw), BF16))
    outs = pl.pallas_call(
        functools.partial(_proj_kernel, mode=mode, tn=tn, regroup=tuple(regroup)),
        grid=(m // tm,),
        in_specs=in_specs,
        out_specs=out_specs,
        out_shape=out_shape,
        scratch_shapes=[pltpu.VMEM((gw // LANES, tm, LANES), F32)] if regroup else [],
        compiler_params=_cparams(("parallel",)),
        name="proj_" + mode,
    )(*args)
    return outs if regroup else outs[0]


def _dswa_kernel(q_ref, kp_ref, kc_ref, vp_ref, vc_ref, bias_ref, o_ref, st_ref, *, qb, cb):
    i = pl.program_id(2)
    blk = DSWA_BLOCK
    row = lax.broadcasted_iota(I32, (blk, 2 * blk), 0)
    col = lax.broadcasted_iota(I32, (blk, 2 * blk), 1)
    cur_ok = (col >= blk) & ((col - blk) <= row)
    valid_inner = ((col < blk) & (col >= row)) | cur_ok
    prev_off = jnp.where(i > 0, 0, 4 * blk)
    valid_first = ((col < blk) & (col >= row + prev_off)) | cur_ok
    lane = lax.broadcasted_iota(I32, (blk, LANES), 1)
    gw = DSWA_HPG * HEAD_DIM
    for cc, j in [(cc, j) for cc in range(cb) for j in range(qb)]:
        rs = slice(j * blk, (j + 1) * blk)
        ps = slice((j - 1) * blk, j * blk)
        gs = slice(cc * gw, (cc + 1) * gw)
        valid = valid_first if j == 0 else valid_inner
        q = q_ref[rs, gs]
        k = jnp.concatenate([kp_ref[:, gs] if j == 0 else kc_ref[ps, gs], kc_ref[rs, gs]], axis=0)
        v = jnp.concatenate([vp_ref[:, gs] if j == 0 else vc_ref[ps, gs], vc_ref[rs, gs]], axis=0)
        stats = jnp.zeros((blk, LANES), F32)
        for hp in range(DSWA_HPG // 2):
            ts = slice(hp * LANES, (hp + 1) * LANES)
            q2, k2, v2 = q[:, ts], k[:, ts], v[:, ts]
            halves = []
            for hh in range(2):
                h = 2 * hp + hh
                qm = jnp.where((lane // HEAD_DIM) == hh, q2, jnp.zeros_like(q2))
                s = lax.dot_general(qm, k2, (((1,), (1,)), ((), ())), preferred_element_type=F32)
                s = jnp.where(valid, s + bias_ref[h], -jnp.inf)
                m = jnp.max(s, axis=-1, keepdims=True)
                p = jnp.exp(s - m)
                l = jnp.sum(p, axis=-1, keepdims=True)
                halves.append(jnp.dot(p.astype(BF16), v2, preferred_element_type=F32) / l)
                stats = jnp.where(lane == h, m, stats)
                stats = jnp.where(lane == DSWA_HPG + h, l, stats)
            o_ref[rs, cc * gw + hp * LANES:cc * gw + (hp + 1) * LANES] = jnp.where(
                lane < HEAD_DIM, halves[0], halves[1])
        st_ref[rs, cc * LANES:(cc + 1) * LANES] = stats


def _dswa(p1, p2, rm, bias, g, r, bsz, seq):
    blk = DSWA_BLOCK
    L = seq // r
    nblk = L // blk
    gw = DSWA_HPG * HEAD_DIM
    if r == 1:
        p1v = p1.reshape(bsz, L, W1)
        p2v = p2.reshape(bsz, L, W2)
        qv, kv, vv = p1v, p1v, p2v
        q_col = lambda c: (P1_AQ // gw) + g
        k_col = lambda c: (P1_AK // gw) + g
        v_col = lambda c: (P2_AV // gw) + g
    else:
        qv, kv, vv = (a.reshape(bsz, L, r * gw) for a in rm)
        q_col = k_col = v_col = lambda c: c
    qb = min(DSWA_QB, nblk)
    cb = min(max(DSWA_QB // qb, 1), r)
    prev = lambda i: jnp.maximum(i * qb - 1, 0)
    in_specs = [
        pl.BlockSpec((None, qb * blk, cb * gw), lambda b, c, i: (b, i, q_col(c))),
        pl.BlockSpec((None, blk, cb * gw), lambda b, c, i: (b, prev(i), k_col(c))),
        pl.BlockSpec((None, qb * blk, cb * gw), lambda b, c, i: (b, i, k_col(c))),
        pl.BlockSpec((None, blk, cb * gw), lambda b, c, i: (b, prev(i), v_col(c))),
        pl.BlockSpec((None, qb * blk, cb * gw), lambda b, c, i: (b, i, v_col(c))),
        pl.BlockSpec((DSWA_HPG, blk, 2 * blk), lambda b, c, i: (0, 0, 0)),
    ]
    out_specs = [
        pl.BlockSpec((None, qb * blk, cb * gw), lambda b, c, i: (b, i, c)),
        pl.BlockSpec((None, qb * blk, cb * LANES), lambda b, c, i: (b, i, c)),
    ]
    o, st = pl.pallas_call(
        functools.partial(_dswa_kernel, qb=qb, cb=cb),
        grid=(bsz, r // cb, nblk // qb),
        in_specs=in_specs,
        out_specs=out_specs,
        out_shape=[jax.ShapeDtypeStruct((bsz, L, r * gw), F32),
                   jax.ShapeDtypeStruct((bsz, L, r * LANES), F32)],
        compiler_params=_cparams(("parallel", "parallel", "arbitrary")),
        name="dswa_g%d" % g,
    )(qv, kv, kv, vv, vv, bias)
    return o.reshape(bsz * L, r * gw), st.reshape(bsz * L, r * LANES)


GLA_TC = 256
GLA_GB = 4


def _split3(x):
    a1 = x.astype(BF16)
    r1 = x - a1.astype(F32)
    a2 = r1.astype(BF16)
    a3 = (r1 - a2.astype(F32)).astype(BF16)
    return a1, a2, a3


def _gla_kernel(v_ref, r_ref, q_ref, k_ref, al_ref, wa_ref, ba_ref, gn_ref, o_ref, st_ref, *, gb):
    C = GLA_CHUNK

    @pl.when(pl.program_id(1) == 0)
    def _():
        st_ref[...] = jnp.zeros_like(st_ref)

    TC = GLA_TC
    NC = TC // C
    row = lax.broadcasted_iota(I32, (TC, TC), 0)
    col = lax.broadcasted_iota(I32, (TC, TC), 1)
    tri = ((row // C) == (col // C)) & (row >= col)
    tri_bf = jnp.where(tri, 1.0, 0.0).astype(BF16)
    nt = (((1,), (1,)), ((), ()))
    tn = (((0,), (0,)), ((), ()))
    for bb in range(gb):
        z = jnp.dot(al_ref[bb], wa_ref[...], preferred_element_type=F32) + ba_ref[...]
        la = (jnp.minimum(z, 0.0) - jnp.log(1.0 + jnp.exp(-jnp.abs(z)))) * (1.0 / GLA_TAU)
        a1, a2, a3 = _split3(la)
        bcum = (jnp.dot(tri_bf, a1, preferred_element_type=F32)
                + jnp.dot(tri_bf, a2, preferred_element_type=F32)
                + jnp.dot(tri_bf, a3, preferred_element_type=F32))
        blast = [bcum[(c + 1) * C - 1:(c + 1) * C, :] for c in range(NC)]
        dec = [jnp.exp(b) for b in blast]
        dec_rows = jnp.concatenate([jnp.broadcast_to(d, (C, d.shape[1])) for d in dec], axis=0)
        qf = q_ref[bb].astype(F32) * (GLA_DK ** -0.5)
        k_dec = k_ref[bb].astype(F32) * jnp.exp(-bcum)
        q_in = (qf * jnp.exp(bcum)).astype(BF16)
        k_in = k_dec.astype(BF16)
        k_end = (k_dec * dec_rows).astype(BF16)
        for h in range(GLA_HEADS):
            ks = slice(h * GLA_DK, (h + 1) * GLA_DK)
            vs = slice(h * GLA_DV, (h + 1) * GLA_DV)
            vh = v_ref[bb, :, vs]
            att = lax.dot_general(q_in[:, ks], k_in[:, ks], nt, preferred_element_type=F32)
            att = jnp.where(tri, att, 0.0)
            o = jnp.dot(att.astype(BF16), vh, preferred_element_type=F32)
            st = st_ref[bb, h]
            inter = []
            for c in range(NC):
                rs = slice(c * C, (c + 1) * C)
                inter.append(lax.dot_general(q_in[rs, ks], st.astype(BF16), nt, preferred_element_type=F32))
                st = st * dec[c][:, ks] + lax.dot_general(vh[rs, :], k_end[rs, ks], tn, preferred_element_type=F32)
            st_ref[bb, h] = st
            o = o + jnp.concatenate(inter, axis=0)
            ms = jnp.mean(o * o, axis=-1, keepdims=True)
            y = o * lax.rsqrt(ms + RMS_EPS) * gn_ref[...]
            rg = r_ref[bb, :, vs].astype(F32)
            y = y * (rg * jax.nn.sigmoid(rg))
            o_ref[bb, :, vs] = y.astype(o_ref.dtype)


def _gla(p2, wa, ba, gn, bsz, seq):
    tc = GLA_TC
    gb = GLA_GB if bsz % GLA_GB == 0 else 1
    p2v = p2.reshape(bsz, seq, W2)
    in_specs = [
        pl.BlockSpec((gb, tc, 1024), lambda b, t: (b, t, P2_BV // 1024)),
        pl.BlockSpec((gb, tc, 1024), lambda b, t: (b, t, P2_BR // 1024)),
        pl.BlockSpec((gb, tc, 512), lambda b, t: (b, t, P2_BQ // 512)),
        pl.BlockSpec((gb, tc, 512), lambda b, t: (b, t, P2_BK // 512)),
        pl.BlockSpec((gb, tc, LANES), lambda b, t: (b, t, P2_BAL // LANES)),
        pl.BlockSpec((LANES, 512), lambda b, t: (0, 0)),
        pl.BlockSpec((1, 512), lambda b, t: (0, 0)),
        pl.BlockSpec((1, GLA_DV), lambda b, t: (0, 0)),
    ]
    out = pl.pallas_call(
        functools.partial(_gla_kernel, gb=gb),
        grid=(bsz // gb, seq // tc),
        in_specs=in_specs,
        out_specs=pl.BlockSpec((gb, tc, 1024), lambda b, t: (b, t, 0)),
        out_shape=jax.ShapeDtypeStruct((bsz, seq, 1024), BF16),
        scratch_shapes=[pltpu.VMEM((gb, GLA_HEADS, GLA_DV, GLA_DK), F32)],
        compiler_params=_cparams(("parallel", "arbitrary")),
        name="gla",
    )(p2v, p2v, p2v, p2v, p2v, wa, ba, gn)
    return out.reshape(bsz * seq, 1024)


def _tree_sum(xs):
    xs = list(xs)
    while len(xs) > 1:
        xs = [xs[i] + xs[i + 1] for i in range(0, len(xs) - 1, 2)] + ([xs[-1]] if len(xs) % 2 else [])
    return xs[0]


def _dsa_kernel(q_ref, iq_ref, iw_ref, k_ref, ik_ref, vt_ref, bias_ref, wot_ref, o_ref,
                keys_ref, planes_ref, qaug_ref, iqall_ref, acc_ref, st0_ref, st1_ref, m_ref, cm0_ref, cm1_ref,
                *, topk):
    TQ, CH = DSA_TQ, DSA_CH
    CC = 2 * CH
    qi = pl.program_id(1)
    t0 = qi * TQ
    nch = qi // (CH // TQ) + 1
    npair = (nch + 1) // 2
    nt = (((1,), (1,)), ((), ()))

    eye = (lax.broadcasted_iota(I32, (TQ, TQ), 0) == lax.broadcasted_iota(I32, (TQ, TQ), 1))
    eye = jnp.where(eye, 1.0, 0.0).astype(BF16)
    q_t = q_ref[...].astype(F32).T.astype(BF16)
    iq_t = iq_ref[...].astype(F32).T.astype(BF16)
    for h in range(DSA_HEADS):
        cs = slice(h * TQ, (h + 1) * TQ)
        qaug_ref[0:TQ, cs] = eye
        qaug_ref[TQ:TQ + HEAD_DIM, cs] = q_t[h * HEAD_DIM:(h + 1) * HEAD_DIM, :]
        qaug_ref[TQ + HEAD_DIM:, cs] = jnp.zeros((TQ - HEAD_DIM, TQ), BF16)
    for h in range(IDX_HEADS):
        cs = slice(h * TQ, (h + 1) * TQ)
        iqall_ref[0:IDX_DIM, cs] = iq_t[h * IDX_DIM:(h + 1) * IDX_DIM, :]
        iqall_ref[IDX_DIM:, cs] = jnp.zeros((LANES - IDX_DIM, TQ), BF16)
    idx_scale = (IDX_HEADS ** -0.5) * (IDX_DIM ** -0.5)
    wt = (iw_ref[...].astype(F32) * idx_scale).T

    qpos = t0 + lax.broadcasted_iota(I32, (1, TQ), 1)
    krow = lax.broadcasted_iota(I32, (CH, TQ), 0)
    krow2 = lax.broadcasted_iota(I32, (CC, TQ), 0)

    SRB = DSA_SRB

    def score_block(s0, RB=SRB):
        s0 = pl.multiple_of(s0, RB)
        krow1 = lax.broadcasted_iota(I32, (RB, TQ), 0)
        ikc = ik_ref[pl.ds(s0, RB), :]
        s = None
        for hp in range(IDX_HEADS // 2):
            x = jnp.dot(ikc, iqall_ref[:, hp * 2 * TQ:(hp + 1) * 2 * TQ],
                        preferred_element_type=F32)
            t = (jnp.maximum(x[:, :TQ], 0.0) * wt[2 * hp:2 * hp + 1, :]
                 + jnp.maximum(x[:, TQ:], 0.0) * wt[2 * hp + 1:2 * hp + 2, :])
            s = t if s is None else s + t
        s = jnp.where(s == 0.0, 0.0, s)
        bits = pltpu.bitcast(s, I32)
        key = bits ^ ((bits >> 31) & 0x7FFFFFFF)
        key = jnp.where(s0 + krow1 <= qpos, key, INT_MIN)
        keys_ref[pl.ds(s0, RB), :] = key

    def score_trips(first, count, width):
        def body(t, carry):
            for sub in range(width * CH // SRB):
                score_block((first + t * width) * CH + sub * SRB)
            return carry
        lax.fori_loop(0, count, body, 0)

    n4 = nch // 4
    n2 = (nch - 4 * n4) // 2
    score_trips(0, n4, 4)
    score_trips(4 * n4, n2, 2)
    score_trips(4 * n4 + 2 * n2, nch - 4 * n4 - 2 * n2, 1)

    @pl.when(nch % 2 == 1)
    def _():
        keys_ref[pl.ds(pl.multiple_of(nch * CH, CH), CH), :] = jnp.full((CH, TQ), INT_MIN, I32)

    kvec = jnp.minimum(topk, qpos + 1)
    NPL = 32

    def planes_body(c, carry):
        s0 = pl.multiple_of(c * CH, CH)
        a = [keys_ref[pl.ds(s0 + 8 * j, 8), :] ^ INT_MIN for j in range(NPL)]
        j, m = 16, 0x0000FFFF
        while j:
            sh = jnp.full((8, TQ), j, I32)
            k = 0
            while k < NPL:
                t = (a[k] ^ lax.shift_right_logical(a[k + j], sh)) & m
                a[k] = a[k] ^ t
                a[k + j] = a[k + j] ^ lax.shift_left(t, sh)
                k = (k + j + 1) & ~j
            j >>= 1
            m = (m ^ (m << j)) & 0xFFFFFFFF
        for p in range(NPL):
            planes_ref[c, p] = a[p]
        return carry

    def empty_body(c, carry):
        for p in range(NPL):
            planes_ref[c, p] = jnp.zeros((8, TQ), I32)
        return carry

    NCK = keys_ref.shape[0] // CH
    short = nch <= NCK // 2
    lax.fori_loop(0, nch, planes_body, 0)
    lax.fori_loop(nch, jnp.where(short, NCK // 2, NCK), empty_body, 0)

    def radix_select(nc):
        def run():
            def lane_count(words):
                return jnp.sum(_tree_sum([lax.population_count(w) for w in words]), axis=0, keepdims=True)

            def plane_body(t, carry):
                thr_u, n_gt, alive = carry
                p = 2 * t
                h1 = [alive[c] & planes_ref[c, p] for c in range(nc)]
                h0 = [alive[c] ^ h1[c] for c in range(nc)]
                h11 = [h1[c] & planes_ref[c, p + 1] for c in range(nc)]
                h01 = [h0[c] & planes_ref[c, p + 1] for c in range(nc)]
                c1, c11, c01 = lane_count(h1), lane_count(h11), lane_count(h01)
                take1 = (n_gt + c1) >= kvec
                n_mid = jnp.where(take1, n_gt, n_gt + c1)
                ones2 = jnp.where(take1, c11, c01)
                take2 = (n_mid + ones2) >= kvec
                n_gt = jnp.where(take2, n_mid, n_mid + ones2)
                thr_u = (thr_u | jnp.where(take1, jnp.int32(1) << (31 - p), 0)
                         | jnp.where(take2, jnp.int32(1) << (30 - p), 0))
                alive = tuple(
                    jnp.where(take1, jnp.where(take2, h11[c], h1[c] ^ h11[c]), jnp.where(take2, h01[c], h0[c] ^ h01[c]))
                    for c in range(nc))
                return thr_u, n_gt, alive

            zero = jnp.zeros((1, TQ), I32)
            alive0 = tuple(jnp.full((8, TQ), -1, I32) for _ in range(nc))
            thr_u, n_gt, alive = lax.fori_loop(0, NPL // 2, plane_body, (zero, zero, alive0))
            n_eq = jnp.sum(_tree_sum([lax.population_count(a) for a in alive]), axis=0, keepdims=True)
            return thr_u, n_gt, n_eq
        return run

    thr_u, n_gt, n_eq = lax.cond(short, radix_select(NCK // 2), radix_select(NCK))
    thr = thr_u ^ INT_MIN

    def count(pred_fn):
        def body(c, acc):
            s0 = pl.multiple_of(c * CC, CC)
            kk = keys_ref[pl.ds(s0, CC), :]
            hit = jnp.where(pred_fn(kk, s0), 1, 0).astype(I32)
            return acc + jnp.sum(hit.reshape(CC // 8, 8, TQ), axis=0)
        acc = lax.fori_loop(0, npair, body, jnp.zeros((8, TQ), I32))
        return jnp.sum(acc, axis=0, keepdims=True)

    excess = n_gt + n_eq - kvec
    has_excess = jnp.max(excess) > 0

    @pl.when(has_excess)
    def _():
        need = kvec - n_gt

        def tie_lt(cut):
            return count(lambda kk, s0: (kk == thr) & (s0 + krow2 < cut))

        def cut_body(b, cut):
            cand = cut | (jnp.int32(1) << (12 - b))
            return jnp.where(tie_lt(cand) <= need, cand, cut)

        cut = lax.fori_loop(0, 13, cut_body, jnp.zeros((1, TQ), I32))

        def drop_body(c, carry):
            s0 = pl.multiple_of(c * CH, CH)
            kk = keys_ref[pl.ds(s0, CH), :]
            keys_ref[pl.ds(s0, CH), :] = jnp.where((kk == thr) & (s0 + krow >= cut), INT_MIN, kk)
            return carry

        lax.fori_loop(0, nch, drop_body, 0)

    acc_ref[...] = jnp.zeros_like(acc_ref)

    NP = DSA_HEADS // 2
    PW = 2 * TQ

    AB = DSA_AB
    NB = CH // AB

    def logits(c, st_ref, cm_ref):
        s0 = pl.multiple_of(c * CH, CH)
        zoff = pl.multiple_of(jnp.clip(s0 - t0 + DSA_Z0, 0, DSA_Z0), 8)
        kaug = []
        for rb in range(NB):
            rs = pl.ds(s0 + rb * AB, AB)
            pen = jnp.where(keys_ref[rs, :] >= thr, 0.0, NEG_BIG).astype(BF16)
            kaug.append(jnp.concatenate([pen, k_ref[rs, :]], axis=1))
        for hp in range(NP):
            ps = slice(hp * PW, (hp + 1) * PW)
            cm = None
            for rb in range(NB):
                st = jnp.dot(kaug[rb], qaug_ref[:, ps], preferred_element_type=F32)
                st = st + bias_ref[pl.ds(zoff + rb * AB, AB), ps]
                st_ref[rb * AB:(rb + 1) * AB, ps] = st
                tm = jnp.max(st.reshape(AB // 8, 8, PW), axis=0)
                cm = tm if cm is None else jnp.maximum(cm, tm)
            cm_ref[0:1, ps] = jnp.max(cm, axis=0, keepdims=True)

    def accumulate(c, st_ref, cm_ref):
        vtc = vt_ref[c]
        for hp in range(NP):
            ps = slice(hp * PW, (hp + 1) * PW)
            m_old = m_ref[0:1, ps]
            m_new = jnp.maximum(m_old, cm_ref[0:1, ps])
            m_ref[0:1, ps] = m_new
            alpha = jnp.exp2(m_old - m_new)
            p = jnp.concatenate(
                [jnp.exp2(st_ref[rb * AB:(rb + 1) * AB, ps] - m_new).astype(BF16) for rb in range(NB)], axis=0)
            acc_ref[:, ps] = acc_ref[:, ps] * alpha + jnp.dot(vtc, p, preferred_element_type=F32)

    def att_pairs(first, count, pairs):
        def body(t, carry):
            c0 = first + 2 * pairs * t
            for k in range(pairs):
                logits(c0 + 2 * k + 1, st1_ref, cm1_ref)
                accumulate(c0 + 2 * k, st0_ref, cm0_ref)
                logits(c0 + 2 * k + 2, st0_ref, cm0_ref)
                accumulate(c0 + 2 * k + 1, st1_ref, cm1_ref)
            return carry
        lax.fori_loop(0, count, body, 0)

    m_ref[...] = jnp.full(m_ref.shape, NEG_BIG, F32)
    logits(0, st0_ref, cm0_ref)
    nfull = (nch - 1) // 2
    att_pairs(0, nfull // 2, 2)
    att_pairs(4 * (nfull // 2), nfull % 2, 1)
    c_last = 2 * nfull

    @pl.when(nch % 2 == 1)
    def _():
        accumulate(c_last, st0_ref, cm0_ref)

    @pl.when(nch % 2 == 0)
    def _():
        logits(c_last + 1, st1_ref, cm1_ref)
        accumulate(c_last, st0_ref, cm0_ref)
        accumulate(c_last + 1, st1_ref, cm1_ref)

    ot = jnp.concatenate(
        [acc_ref[0:HEAD_DIM, h * TQ:(h + 1) * TQ] / acc_ref[HEAD_DIM:HEAD_DIM + 1, h * TQ:(h + 1) * TQ]
         for h in range(DSA_HEADS)], axis=0)
    o_ref[...] = jnp.dot(ot.T.astype(BF16), wot_ref[...], preferred_element_type=F32)


def _dsa(p1, p2, vt, bias_t, wot, bsz, seq):
    tq = DSA_TQ
    p1v = p1.reshape(bsz, seq, W1)
    p2v = p2.reshape(bsz, seq, W2)
    nq = DSA_HEADS * tq
    in_specs = [
        pl.BlockSpec((None, tq, 768), lambda b, i: (b, i, P1_CQ // 768)),
        pl.BlockSpec((None, tq, 512), lambda b, i: (b, i, P2_IQ // 512)),
        pl.BlockSpec((None, tq, LANES), lambda b, i: (b, i, P2_IW // LANES)),
        pl.BlockSpec((None, seq, LANES), lambda b, i: (b, 0, P1_CK // LANES)),
        pl.BlockSpec((None, seq, LANES), lambda b, i: (b, 0, P2_IK // LANES)),
        pl.BlockSpec((None, seq // DSA_CH, DSA_VR, DSA_CH), lambda b, i: (b, 0, 0, 0)),
        pl.BlockSpec((DSA_Z, nq), lambda b, i: (0, 0)),
        pl.BlockSpec((DSA_HEADS * HEAD_DIM, D_MODEL), lambda b, i: (0, 0)),
    ]
    out = pl.pallas_call(
        functools.partial(_dsa_kernel, topk=min(IDX_TOPK, seq // 4)),
        grid=(bsz, seq // tq),
        in_specs=in_specs,
        out_specs=pl.BlockSpec((None, tq, D_MODEL), lambda b, i: (b, i, 0)),
        out_shape=jax.ShapeDtypeStruct((bsz, seq, D_MODEL), F32),
        scratch_shapes=[
            pltpu.VMEM((seq, tq), I32),
            pltpu.VMEM((seq // DSA_CH, 32, 8, tq), I32),
            pltpu.VMEM((2 * tq, nq), BF16),
            pltpu.VMEM((LANES, IDX_HEADS * tq), BF16),
            pltpu.VMEM((DSA_VR, nq), F32),
            pltpu.VMEM((DSA_CH, nq), F32),
            pltpu.VMEM((DSA_CH, nq), F32),
            pltpu.VMEM((8, nq), F32),
            pltpu.VMEM((8, nq), F32),
            pltpu.VMEM((8, nq), F32),
        ],
        compiler_params=_cparams(("parallel", "arbitrary")),
        name="dsa",
    )(p1v, p2v, p2v, p1v, p2v, vt, bias_t, wot)
    return out.reshape(bsz * seq, D_MODEL)


MERGE_TM = 512


def _merge_kernel(h_ref, oa0_ref, oa1_ref, oa2_ref, sa0_ref, sa1_ref, sa2_ref, ob_ref, yc_ref, g_ref,
                  woa_ref, wob_ref, wo_ref, o_ref, on1_ref, on2_ref, sn1_ref, sn2_ref):
    H = DSWA_HPG
    tm = h_ref.shape[0]
    gw = H * HEAD_DIM
    for (_, r), src_o, src_s, dst_o, dst_s in zip(DSWA_PATTERNS[1:], (oa1_ref, oa2_ref), (sa1_ref, sa2_ref),
                                                  (on1_ref, on2_ref), (sn1_ref, sn2_ref)):
        for c in range(r):
            for k in range(gw // LANES):
                dst_o[k, pl.ds(c, tm // r, stride=r), :] = src_o[:, c * gw + k * LANES:c * gw + (k + 1) * LANES]
            dst_s[pl.ds(c, tm // r, stride=r), :] = src_s[:, c * LANES:(c + 1) * LANES]
    oa_vals = (oa0_ref[...],) + tuple(
        jnp.concatenate([ref[k] for k in range(gw // LANES)], axis=1) for ref in (on1_ref, on2_ref))
    sts = [sa0_ref[...], sn1_ref[...], sn2_ref[...]]
    lane = lax.broadcasted_iota(I32, sts[0].shape, 1)
    mmax = jnp.maximum(jnp.maximum(sts[0], sts[1]), sts[2])
    wts = [pltpu.roll(s, LANES - H, 1) * jnp.exp(s - mmax) for s in sts]
    tot = wts[0] + wts[1] + wts[2]
    hrow = lax.broadcasted_iota(I32, (LANES, H * HEAD_DIM), 0)
    hcol = lax.broadcasted_iota(I32, (LANES, H * HEAD_DIM), 1) // HEAD_DIM
    expand = jnp.where(hrow == hcol, 1.0, 0.0).astype(BF16)
    oa = None
    for g in range(3):
        w = jnp.where(lane < H, wts[g] / tot, 0.0)
        hi = w.astype(BF16)
        lo = (w - hi.astype(F32)).astype(BF16)
        wfull = (jnp.dot(hi, expand, preferred_element_type=F32)
                 + jnp.dot(lo, expand, preferred_element_type=F32))
        term = wfull * oa_vals[g]
        oa = term if oa is None else oa + term
    oa = oa.astype(BF16)
    y_a = jnp.dot(oa, woa_ref[...], preferred_element_type=F32)
    y_b = jnp.dot(ob_ref[...], wob_ref[...], preferred_element_type=F32)
    y_c = yc_ref[...]
    D = D_MODEL
    mix = (g_ref[:, 0:D].astype(F32) * y_a + g_ref[:, D:2 * D].astype(F32) * y_b
           + g_ref[:, 2 * D:3 * D].astype(F32) * y_c)
    o_ref[...] = h_ref[...] + jnp.dot(mix.astype(BF16), wo_ref[...], preferred_element_type=F32)


def _merge(h, oas, sas, ob, yc, gates, woa, wob, wo):
    m = h.shape[0]
    tm = MERGE_TM
    gw = DSWA_HPG * HEAD_DIM
    row = lambda w: pl.BlockSpec((tm, w), lambda i: (i, 0))
    rmrow = lambda w, r: pl.BlockSpec((tm // r, r * w), lambda i: (i, 0))
    full = lambda a: pl.BlockSpec(a.shape, lambda i: (0, 0), pipeline_mode=pl.Buffered(1))
    rs = [r for _, r in DSWA_PATTERNS]
    in_specs = ([row(D_MODEL)] + [rmrow(gw, r) for r in rs] + [rmrow(LANES, r) for r in rs]
                + [row(1024), row(D_MODEL), row(3 * D_MODEL)] + [full(woa), full(wob), full(wo)])
    return pl.pallas_call(
        _merge_kernel,
        grid=(m // tm,),
        in_specs=in_specs,
        out_specs=row(D_MODEL),
        out_shape=jax.ShapeDtypeStruct((m, D_MODEL), F32),
        scratch_shapes=[pltpu.VMEM((gw // LANES, tm, LANES), F32), pltpu.VMEM((gw // LANES, tm, LANES), F32),
                        pltpu.VMEM((tm, LANES), F32), pltpu.VMEM((tm, LANES), F32)],
        compiler_params=_cparams(("parallel",)),
        name="merge",
    )(h, *oas, *sas, ob, yc, gates, woa, wob, wo)


FFN_TM = 512
FFN_TF = 256


def _rms(x, g):
    ms = jnp.mean(x * x, axis=-1, keepdims=True)
    return (x * lax.rsqrt(ms + RMS_EPS) * g).astype(BF16)


def _ffn_ple_kernel(h_ref, p_ref, gf_ref, wg_ref, wu_ref, wd_ref, gp_ref, wpg_ref, wpp_ref, o_ref, *, tf):
    x = h_ref[...]
    u = _rms(x, gf_ref[...])
    acc = None
    for j in range(wg_ref.shape[1] // tf):
        cs = slice(j * tf, (j + 1) * tf)
        a = jnp.dot(u, wg_ref[:, cs], preferred_element_type=F32)
        b = jnp.dot(u, wu_ref[:, cs], preferred_element_type=F32)
        t = (a * jax.nn.sigmoid(a) * b).astype(BF16)
        d = jnp.dot(t, wd_ref[cs, :], preferred_element_type=F32)
        acc = d if acc is None else acc + d
    h2 = x + acc
    e = _rms(h2, gp_ref[...])
    gate = jax.nn.sigmoid(jnp.dot(e, wpg_ref[...], preferred_element_type=F32))
    proj = jnp.dot(p_ref[...].astype(BF16), wpp_ref[...], preferred_element_type=F32)
    o_ref[...] = h2 + gate * proj


def _ffn_ple(h, p, gf, wg, wu, wd, gp, wpg, wpp):
    m, d = h.shape
    tm, tf = FFN_TM, FFN_TF
    const = lambda a: pl.BlockSpec(a.shape, lambda i: (0, 0), pipeline_mode=pl.Buffered(1))
    return pl.pallas_call(
        functools.partial(_ffn_ple_kernel, tf=tf),
        grid=(m // tm,),
        in_specs=[pl.BlockSpec((tm, d), lambda i: (i, 0)), pl.BlockSpec((tm, PLE_DIM), lambda i: (i, 0)),
                  const(gf), const(wg), const(wu), const(wd), const(gp), const(wpg), const(wpp)],
        out_specs=pl.BlockSpec((tm, d), lambda i: (i, 0)),
        out_shape=jax.ShapeDtypeStruct((m, d), F32),
        compiler_params=_cparams(("parallel",)),
        name="ffn_ple",
    )(h, p, gf, wg, wu, wd, gp, wpg, wpp)


def _rel_bucket(dist):
    max_exact = REL_BUCKETS // 2
    d = jnp.maximum(dist, 0)
    df = jnp.maximum(d, 1).astype(F32)
    large = max_exact + (jnp.log(df / max_exact) / math.log(REL_MAX_DIST / max_exact)
                         * (REL_BUCKETS - max_exact)).astype(I32)
    large = jnp.minimum(large, REL_BUCKETS - 1)
    return jnp.where(d < max_exact, d, large)


def _toeplitz(rev, n_rows, n_cols):
    nh = rev.shape[0]
    lw = rev.shape[1] + 1
    w = jnp.pad(rev, ((0, 0), (0, 1)))
    s = jnp.broadcast_to(w[:, None, :], (nh, n_rows, lw)).reshape(nh, n_rows * lw)
    s = s[:, :n_rows * (lw - 1)].reshape(nh, n_rows, lw - 1)
    return s[:, :, n_rows - 1:n_rows - 1 + n_cols]


def _bias_tables(rel_bias):
    blk = DSWA_BLOCK
    dswa = []
    for g, (_, r) in enumerate(DSWA_PATTERNS):
        delta = np.arange(3 * blk - 1)[::-1] - (blk - 1)
        rev = rel_bias[_rel_bucket(jnp.asarray(delta * r, I32))][:, g * DSWA_HPG:(g + 1) * DSWA_HPG]
        dswa.append(_toeplitz(rev.T, blk, 2 * blk))
    tq = DSA_TQ
    dist = np.arange(DSA_Z + tq - 1)[::-1] - (DSA_Z - 1) + DSA_Z0
    rev = rel_bias[_rel_bucket(jnp.asarray(dist, I32))][:, DSWA_HEADS:] * math.log2(math.e)
    bias_t = jnp.transpose(_toeplitz(rev.T, tq, DSA_Z), (2, 0, 1)).reshape(DSA_Z, DSA_HEADS * tq)
    return dswa, bias_t


def _pad_cols(w, width):
    return jnp.pad(w, ((0, 0), (0, width - w.shape[1])))


def _layer_params(w_in, qn_a, kn_a, qn_c, kn_c, w_alpha2, b_alpha):
    offs = np.cumsum((0,) + IN_WIDTHS)
    parts = [w_in[:, offs[i]:offs[i + 1]] for i in range(len(IN_WIDTHS))]
    a_q, a_k, a_v, b_q, b_k, b_v, b_r, b_al, c_q, c_k, c_v, i_q, i_k, i_w = parts
    w1 = _pad_cols(jnp.concatenate([a_q, a_k, c_q, c_k], axis=1), W1).astype(BF16)
    w2 = _pad_cols(jnp.concatenate(
        [b_v, b_r, b_q, b_k, i_q, a_v, _pad_cols(b_al, LANES), _pad_cols(c_v, LANES),
         _pad_cols(i_k, LANES), _pad_cols(i_w, LANES)], axis=1), W2).astype(BF16)
    scale = HEAD_DIM ** -0.5
    gain1 = jnp.concatenate([jnp.tile(qn_a, DSWA_HEADS) * scale, jnp.tile(kn_a, DSWA_HEADS),
                             jnp.tile(qn_c, DSA_HEADS) * (scale * math.log2(math.e)), kn_c])
    gain1 = jnp.pad(gain1, (0, W1 - gain1.shape[0])).reshape(1, W1)
    wa = jnp.pad(w_alpha2, ((0, LANES - GLA_RANK), (0, 0))).astype(BF16)
    return w1, w2, gain1, wa, b_alpha.reshape(1, -1)


def kernel(x, p, rel_bias, norm_mix, w_in, qn_a, kn_a, qn_c, kn_c, w_alpha2, b_alpha, gla_norm, w_out_a, w_out_b, w_out_c, w_gate, b_gate, w_o, norm_ffn, w_ffn_gate, w_ffn_up, w_ffn_down, norm_ple, w_ple_gate, w_ple_proj):
    bsz, seq, d = x.shape
    depth = p.shape[0]
    m = bsz * seq
    dswa_bias, bias_t = _bias_tables(rel_bias)
    h = x.reshape(m, d)
    zeros_w2 = jnp.zeros((1, W2), F32)
    for i in range(depth):
        w1, w2, gain1, wa, ba = _layer_params(w_in[i], qn_a[i], kn_a[i], qn_c[i], kn_c[i], w_alpha2[i], b_alpha[i])
        gmix = norm_mix[i].reshape(1, d)
        gw = DSWA_HPG * HEAD_DIM
        dil = [(g, r) for g, (_, r) in enumerate(DSWA_PATTERNS) if r > 1]
        p1, *qk_rm = _proj(h, gmix, w1, gain1, "qk",
                           regroup=[(off + g * gw, r) for off in (P1_AQ, P1_AK) for g, r in dil])
        p2, *v_rm = _proj(h, gmix, w2, zeros_w2, "plain", regroup=[(P2_AV + g * gw, r) for g, r in dil])
        gates = _proj(h, gmix, w_gate[i].astype(BF16), b_gate[i].reshape(1, -1), "gate")
        rm = {g: (qk_rm[n], qk_rm[len(dil) + n], v_rm[n]) for n, (g, r) in enumerate(dil)}
        oas, sas = [], []
        for g, (_, r) in enumerate(DSWA_PATTERNS):
            o, st = _dswa(p1, p2, rm.get(g), dswa_bias[g], g, r, bsz, seq)
            oas.append(o)
            sas.append(st)
        ob = _gla(p2, wa, ba, gla_norm[i].reshape(1, -1), bsz, seq)
        cv = p2[:, P2_CV:P2_CV + HEAD_DIM].reshape(bsz, seq // DSA_CH, DSA_CH, HEAD_DIM)
        ones_pad = jnp.zeros((bsz, seq // DSA_CH, DSA_VR - HEAD_DIM, DSA_CH), BF16).at[:, :, 0, :].set(1.0)
        vt = jnp.concatenate([jnp.transpose(cv, (0, 1, 3, 2)), ones_pad], axis=2)
        yc = _dsa(p1, p2, vt, bias_t, w_out_c[i].astype(BF16), bsz, seq)
        h = _merge(h, oas, sas, ob, yc, gates, w_out_a[i].astype(BF16), w_out_b[i].astype(BF16),
                   w_o[i].astype(BF16))
        h = _ffn_ple(h, p[i].reshape(m, PLE_DIM), norm_ffn[i].reshape(1, d), w_ffn_gate[i].astype(BF16),
                     w_ffn_up[i].astype(BF16), w_ffn_down[i].astype(BF16), norm_ple[i].reshape(1, d),
                     w_ple_gate[i].astype(BF16), w_ple_proj[i].astype(BF16))
    return h.reshape(bsz, seq, d)
```
